```python
import jax
import jax.numpy as jnp
from jax import lax
import numpy as np

D_MODEL = 1024
BATCH = 32
SEQ = 256
DEPTH = 1
DEC_BATCH = 4
DEC_SEQ = 2048
PAST_LEN = 512

GRID_W = 64
QBLK = 128
WINDOW = 128
ROPE_BASE = 10000.0
EPS = 1e-6
NEG = -1e30
A_HEADS = 8
A_KV_HEADS = 2
A_GROUP = A_HEADS // A_KV_HEADS
A_HEAD_DIM = 64
B_HEADS = 8
B_NOPE = 64
B_ROPE = 32
B_V = 64
KV_LORA = 256
Q_LORA = 384
MLA_SCALE = (B_NOPE + B_ROPE) ** -0.5
N_EXPERTS = 256
TOP_K = 8
N_GROUPS = 8
TOPK_GROUPS = 4
D_EXPERT = 256
D_SHARED = 256
ROUTED_SCALE = 2.5
MOE_BLK = 128
P_QA = A_HEADS * A_HEAD_DIM
P_KA = A_KV_HEADS * A_HEAD_DIM
P_VA = A_KV_HEADS * A_HEAD_DIM
P_QL = Q_LORA
P_CKV = KV_LORA
P_KR = B_ROPE
P_GATE = 2 * D_MODEL
SPLIT_POINTS = (P_QA, P_QA + P_KA, P_QA + P_KA + P_VA, P_QA + P_KA + P_VA + P_QL,
                P_QA + P_KA + P_VA + P_QL + P_CKV, P_QA + P_KA + P_VA + P_QL + P_CKV + P_KR)
D_IN = P_QA + P_KA + P_VA + P_QL + P_CKV + P_KR + P_GATE

kernel_name = 'hybrid_dit_swa_mla_moe_step'


def rmsnorm(x, g):
    xf = x.astype(jnp.float32)
    y = xf * lax.rsqrt(jnp.mean(xf * xf, axis=-1, keepdims=True) + EPS)
    return (y * g.astype(jnp.float32)).astype(x.dtype)


def _rot(x, ang):
    x1, x2 = jnp.split(x, 2, axis=-1)
    cos, sin = jnp.cos(ang), jnp.sin(ang)
    return jnp.concatenate([x1 * cos - x2 * sin, x1 * sin + x2 * cos], axis=-1)


def axial_rope(x):
    n, d_rot = x.shape[1], x.shape[-1]
    rows = n // GRID_W
    nf = d_rot // 4
    inv = ROPE_BASE ** (-jnp.arange(nf, dtype=jnp.float32) / nf)
    row = jnp.repeat(jnp.arange(rows, dtype=jnp.float32), GRID_W)
    col = jnp.tile(jnp.arange(GRID_W, dtype=jnp.float32), rows)
    ang_r = (row[:, None] * inv)[None, :, None, :]
    ang_c = (col[:, None] * inv)[None, :, None, :]
    xr, xc = jnp.split(x.astype(jnp.float32), 2, axis=-1)
    return jnp.concatenate([_rot(xr, ang_r), _rot(xc, ang_c)], axis=-1).astype(x.dtype)


def sink_softmax(s, sink):
    m = jnp.maximum(jnp.max(s, axis=-1, keepdims=True), sink)
    p = jnp.exp(s - m)
    return p / (jnp.sum(p, axis=-1, keepdims=True) + jnp.exp(sink - m))


def block_attention(q, k, v, scale, sink):
    b, s, g, r, dk = q.shape
    nq = s // QBLK
    kf = k.astype(jnp.float32)
    vf = v.astype(jnp.float32)
    qb = jnp.moveaxis(q.reshape(b, nq, QBLK, g, r, dk), 1, 0)

    def one(qi):
        sc = jnp.einsum('bqgrd,bkgd->bgrqk', qi.astype(jnp.float32), kf) * scale
        if sink is None:
            p = jax.nn.softmax(sc, axis=-1)
        else:
            p = sink_softmax(sc, sink.astype(jnp.float32)[None, :, :, None, None])
        return jnp.einsum('bgrqk,bkge->bqgre', p, vf)

    out = lax.map(one, qb)
    return jnp.moveaxis(out, 0, 1).reshape(b, s, g, r, -1).astype(q.dtype)


def windowed_attention(q, k, v, k_ctx, v_ctx, scale, sink):
    b, n, g, r, d = q.shape
    nb = n // QBLK

    def windows(t):
        tp = jnp.pad(t, ((0, 0), (QBLK, QBLK), (0, 0), (0, 0)))
        parts = [tp[:, i * QBLK:i * QBLK + n].reshape(b, nb, QBLK, g, -1) for i in range(3)]
        return jnp.concatenate(parts, axis=2)

    qpos = jnp.arange(n).reshape(nb, QBLK, 1)
    kpos = (jnp.arange(nb)[:, None, None] - 1) * QBLK + jnp.arange(3 * QBLK)[None, None, :]
    valid = (jnp.abs(qpos - kpos) <= WINDOW) & (kpos >= 0) & (kpos < n)
    qb = q.reshape(b, nb, QBLK, g, r, d).astype(jnp.float32)
    kw = windows(k).astype(jnp.float32)
    vw = windows(v).astype(jnp.float32)
    s_w = jnp.einsum('bnqgrd,bnkgd->bngrqk', qb, kw) * scale
    s_w = jnp.where(valid[None, :, None, None, :, :], s_w, NEG)
    s_c = jnp.einsum('bnqgrd,bkgd->bngrqk', qb, k_ctx.astype(jnp.float32)) * scale
    p = sink_softmax(jnp.concatenate([s_w, s_c], axis=-1),
                     sink.astype(jnp.float32)[None, None, :, :, None, None])
    o = (jnp.einsum('bngrqk,bnkge->bnqgre', p[..., :3 * QBLK], vw)
         + jnp.einsum('bngrqk,bkge->bnqgre', p[..., 3 * QBLK:], v_ctx.astype(jnp.float32)))
    return o.reshape(b, n, g, r, -1).astype(q.dtype)


def split_projection(h, w_in):
    return jnp.split(h @ w_in, SPLIT_POINTS, axis=-1)


def merge_branches(out_a, out_b, gates, lp):
    g_a, g_b = jnp.split(gates, 2, axis=-1)
    y = jax.nn.sigmoid(g_a) * (out_a @ lp['w_o_swa']) + jax.nn.sigmoid(g_b) * (out_b @ lp['w_o_mla'])
    return y @ lp['w_out']


def mix_context(h, lp):
    b, l, _ = h.shape
    qa, ka, va, ql, ckv, kr, gates = split_projection(h, lp['w_in'])
    q = qa.reshape(b, l, A_KV_HEADS, A_GROUP, A_HEAD_DIM)
    k = ka.reshape(b, l, A_KV_HEADS, A_HEAD_DIM)
    v = va.reshape(b, l, A_KV_HEADS, A_HEAD_DIM)
    sink = lp['sink_swa'].reshape(A_KV_HEADS, A_GROUP)
    out_a = block_attention(q, k, v, A_HEAD_DIM ** -0.5, sink).reshape(b, l, -1)
    ckv_n = rmsnorm(ckv, lp['kv_norm_mla'])
    qb = (rmsnorm(ql, lp['q_norm_mla']) @ lp['w_uq_mla']).reshape(b, l, B_HEADS, 1, B_NOPE + B_ROPE)
    kv = (ckv_n @ lp['w_ukv_mla']).reshape(b, l, B_HEADS, B_NOPE + B_V)
    k_nope, vb = jnp.split(kv, [B_NOPE], axis=-1)
    kb = jnp.concatenate([k_nope, jnp.broadcast_to(kr[:, :, None, :], (b, l, B_HEADS, B_ROPE))], axis=-1)
    out_b = block_attention(qb, kb, vb, MLA_SCALE, None).reshape(b, l, -1)
    return merge_branches(out_a, out_b, gates, lp), k, v, ckv_n, kr


def mix_latent(h, k_c, v_c, ckv_c, kr_c, lp):
    b, n, _ = h.shape
    lc = ckv_c.shape[1]
    qa, ka, va, ql, ckv, kr, gates = split_projection(h, lp['w_in'])
    q = axial_rope(qa.reshape(b, n, A_HEADS, A_HEAD_DIM)).reshape(b, n, A_KV_HEADS, A_GROUP, A_HEAD_DIM)
    k = axial_rope(ka.reshape(b, n, A_KV_HEADS, A_HEAD_DIM))
    v = va.reshape(b, n, A_KV_HEADS, A_HEAD_DIM)
    sink = lp['sink_swa'].reshape(A_KV_HEADS, A_GROUP)
    out_a = windowed_attention(q, k, v, k_c, v_c, A_HEAD_DIM ** -0.5, sink).reshape(b, n, -1)
    qb = (rmsnorm(ql, lp['q_norm_mla']) @ lp['w_uq_mla']).reshape(b, n, B_HEADS, B_NOPE + B_ROPE)
    q_nope, q_rope = jnp.split(qb, [B_NOPE], axis=-1)
    qb = jnp.concatenate([q_nope, axial_rope(q_rope)], axis=-1)[:, :, :, None, :]
    kv_lat = (rmsnorm(ckv, lp['kv_norm_mla']) @ lp['w_ukv_mla']).reshape(b, n, B_HEADS, B_NOPE + B_V)
    kv_ctx = (ckv_c @ lp['w_ukv_mla']).reshape(b, lc, B_HEADS, B_NOPE + B_V)
    kr_lat = axial_rope(kr[:, :, None, :])
    k_lat = jnp.concatenate([kv_lat[..., :B_NOPE], jnp.broadcast_to(kr_lat, (b, n, B_HEADS, B_ROPE))], axis=-1)
    k_ctx = jnp.concatenate([kv_ctx[..., :B_NOPE],
                             jnp.broadcast_to(kr_c[:, :, None, :], (b, lc, B_HEADS, B_ROPE))], axis=-1)
    kb = jnp.concatenate([k_ctx, k_lat], axis=1)
    vb = jnp.concatenate([kv_ctx[..., B_NOPE:], kv_lat[..., B_NOPE:]], axis=1)
    out_b = block_attention(qb, kb, vb, MLA_SCALE, None).reshape(b, n, -1)
    return merge_branches(out_a, out_b, gates, lp)


def swiglu(x, wg, wu, wd):
    return (jax.nn.silu(x @ wg) * (x @ wu)) @ wd


def routed_experts(xt, topi, topw, w_gate, w_up, w_down):
    t, d = xt.shape
    a = t * TOP_K
    e_flat = topi.reshape(a)
    tok_flat = jnp.arange(a, dtype=jnp.int32) // TOP_K
    w_flat = topw.reshape(a)
    order = jnp.argsort(e_flat)
    e_sorted = e_flat[order]
    counts = jax.ops.segment_sum(jnp.ones((a,), jnp.int32), e_flat, num_segments=N_EXPERTS)
    start = jnp.cumsum(counts) - counts
    padded = (counts + MOE_BLK - 1) // MOE_BLK * MOE_BLK
    pend = jnp.cumsum(padded)
    pstart = pend - padded
    dest = pstart[e_sorted] + (jnp.arange(a, dtype=jnp.int32) - start[e_sorted])
    nblk = (a + MOE_BLK - 1) // MOE_BLK + N_EXPERTS
    slot_tok = jnp.full((nblk * MOE_BLK,), t, jnp.int32).at[dest].set(tok_flat[order])
    slot_w = jnp.zeros((nblk * MOE_BLK,), jnp.float32).at[dest].set(w_flat[order])
    blk_start = jnp.arange(nblk, dtype=jnp.int32) * MOE_BLK
    blk_expert = jnp.minimum(jnp.sum(blk_start[:, None] >= pend[None, :], axis=1), N_EXPERTS - 1).astype(jnp.int32)
    x_pad = jnp.concatenate([xt, jnp.zeros((1, d), xt.dtype)], axis=0)

    def one_block(args):
        toks, wts, e = args
        y = swiglu(x_pad[toks], w_gate[e], w_up[e], w_down[e])
        return y.astype(jnp.float32) * wts[:, None]

    yb = lax.map(one_block, (slot_tok.reshape(nblk, MOE_BLK), slot_w.reshape(nblk, MOE_BLK), blk_expert))
    y = jax.ops.segment_sum(yb.reshape(-1, d), slot_tok, num_segments=t + 1)
    return y[:t]


def moe_ffn(h, lp):
    b, s, d = h.shape
    xt = h.reshape(-1, d)
    t = xt.shape[0]
    n_per = N_EXPERTS // N_GROUPS
    scores = jax.nn.sigmoid((xt @ lp['router_w']).astype(jnp.float32))
    sel = scores + lp['router_bias'].astype(jnp.float32)
    grp_score = jnp.sum(lax.top_k(sel.reshape(t, N_GROUPS, n_per), 2)[0], axis=-1)
    _, gidx = lax.top_k(grp_score, TOPK_GROUPS)
    gmask = jnp.any(gidx[:, :, None] == jnp.arange(N_GROUPS)[None, None, :], axis=1)
    sel = jnp.where(jnp.repeat(gmask, n_per, axis=1), sel, NEG)
    _, topi = lax.top_k(sel, TOP_K)
    topw = jnp.take_along_axis(scores, topi, axis=-1)
    topw = topw / jnp.sum(topw, axis=-1, keepdims=True) * ROUTED_SCALE
    routed = routed_experts(xt, topi, topw, lp['w_gate_exp'], lp['w_up_exp'], lp['w_down_exp'])
    shared = swiglu(xt, lp['w_gate_sh'], lp['w_up_sh'], lp['w_down_sh']).astype(jnp.float32)
    return (routed + shared).astype(h.dtype).reshape(b, s, d)


def modulation(cond, lp):
    m = jax.nn.silu(cond) @ lp['w_ada'] + lp['b_ada']
    return jnp.split(m[:, None, :], 6, axis=-1)


def modulate(x, g, shift, scale):
    return rmsnorm(x, g) * (1 + scale) + shift


def ffn_sublayer(x, shift, scale, gate, lp):
    h = modulate(x, lp['norm_pre_ffn'], shift, scale)
    return x + gate * rmsnorm(moe_ffn(h, lp), lp['norm_post_ffn'])


def context_layer(x, c_ctx, lp):
    sh1, sc1, g1, sh2, sc2, g2 = modulation(c_ctx[None, :], lp)
    h = modulate(x, lp['norm_pre_mix'], sh1, sc1)
    mix, k, v, ckv, kr = mix_context(h, lp)
    x = x + g1 * rmsnorm(mix, lp['norm_post_mix'])
    return ffn_sublayer(x, sh2, sc2, g2, lp), k, v, ckv, kr


def latent_layer(x, c, k_c, v_c, ckv_c, kr_c, lp):
    sh1, sc1, g1, sh2, sc2, g2 = modulation(c, lp)
    h = modulate(x, lp['norm_pre_mix'], sh1, sc1)
    x = x + g1 * rmsnorm(mix_latent(h, k_c, v_c, ckv_c, kr_c, lp), lp['norm_post_mix'])
    return ffn_sublayer(x, sh2, sc2, g2, lp)


def setup_inputs(seed: int = 0) -> dict:
    key = jax.random.key(seed)
    ks = jax.random.split(key, 32)
    nrm = jax.random.normal
    D = D_MODEL
    return {
        'x_prompt': nrm(ks[0], (BATCH, SEQ, D), jnp.float32),
        'x_sample': nrm(ks[1], (DEC_BATCH, DEC_SEQ, D), jnp.float32),
        'cache_k_swa': nrm(ks[2], (DEC_BATCH, DEPTH, PAST_LEN, A_KV_HEADS, A_HEAD_DIM), jnp.float32),
        'cache_v_swa': nrm(ks[3], (DEC_BATCH, DEPTH, PAST_LEN, A_KV_HEADS, A_HEAD_DIM), jnp.float32),
        'cache_ckv_mla': nrm(ks[4], (DEC_BATCH, DEPTH, PAST_LEN, KV_LORA), jnp.float32),
        'cache_krope_mla': nrm(ks[5], (DEC_BATCH, DEPTH, PAST_LEN, B_ROPE), jnp.float32),
        'c': nrm(ks[6], (DEC_BATCH, D), jnp.float32),
        'c_ctx': nrm(ks[7], (D,), jnp.float32),
        'w_ada': nrm(ks[8], (DEPTH, D, 6 * D), jnp.float32) * (0.5 * D ** -0.5),
        'b_ada': nrm(ks[9], (DEPTH, 6 * D), jnp.float32) * 0.02,
        'norm_pre_mix': 1.0 + 0.05 * nrm(ks[10], (DEPTH, D), jnp.float32),
        'norm_post_mix': 1.0 + 0.05 * nrm(ks[11], (DEPTH, D), jnp.float32),
        'norm_pre_ffn': 1.0 + 0.05 * nrm(ks[12], (DEPTH, D), jnp.float32),
        'norm_post_ffn': 1.0 + 0.05 * nrm(ks[13], (DEPTH, D), jnp.float32),
        'w_in': nrm(ks[14], (DEPTH, D, D_IN), jnp.float32) * D ** -0.5,
        'sink_swa': nrm(ks[15], (DEPTH, A_HEADS), jnp.float32) * 0.5,
        'q_norm_mla': 1.0 + 0.05 * nrm(ks[16], (DEPTH, Q_LORA), jnp.float32),
        'kv_norm_mla': 1.0 + 0.05 * nrm(ks[17], (DEPTH, KV_LORA), jnp.float32),
        'w_uq_mla': nrm(ks[18], (DEPTH, Q_LORA, B_HEADS * (B_NOPE + B_ROPE)), jnp.float32) * Q_LORA ** -0.5,
        'w_ukv_mla': nrm(ks[19], (DEPTH, KV_LORA, B_HEADS * (B_NOPE + B_V)), jnp.float32) * KV_LORA ** -0.5,
        'w_o_swa': nrm(ks[20], (DEPTH, A_HEADS * A_HEAD_DIM, D), jnp.float32) * (A_HEADS * A_HEAD_DIM) ** -0.5,
        'w_o_mla': nrm(ks[21], (DEPTH, B_HEADS * B_V, D), jnp.float32) * (B_HEADS * B_V) ** -0.5,
        'w_out': nrm(ks[22], (DEPTH, D, D), jnp.float32) * D ** -0.5,
        'router_w': nrm(ks[23], (DEPTH, D, N_EXPERTS), jnp.float32) * D ** -0.5,
        'router_bias': nrm(ks[24], (DEPTH, N_EXPERTS), jnp.float32) * 0.01,
        'w_gate_exp': nrm(ks[25], (DEPTH, N_EXPERTS, D, D_EXPERT), jnp.float32) * D ** -0.5,
        'w_up_exp': nrm(ks[26], (DEPTH, N_EXPERTS, D, D_EXPERT), jnp.float32) * D ** -0.5,
        'w_down_exp': nrm(ks[27], (DEPTH, N_EXPERTS, D_EXPERT, D), jnp.float32) * D_EXPERT ** -0.5,
        'w_gate_sh': nrm(ks[28], (DEPTH, D, D_SHARED), jnp.float32) * D ** -0.5,
        'w_up_sh': nrm(ks[29], (DEPTH, D, D_SHARED), jnp.float32) * D ** -0.5,
        'w_down_sh': nrm(ks[30], (DEPTH, D_SHARED, D), jnp.float32) * D_SHARED ** -0.5,
    }


def reference(x_prompt, x_sample, cache_k_swa, cache_v_swa, cache_ckv_mla, cache_krope_mla, c, c_ctx,
              w_ada, b_ada, norm_pre_mix, norm_post_mix, norm_pre_ffn, norm_post_ffn, w_in, sink_swa,
              q_norm_mla, kv_norm_mla, w_uq_mla, w_ukv_mla, w_o_swa, w_o_mla, w_out, router_w, router_bias,
              w_gate_exp, w_up_exp, w_down_exp, w_gate_sh, w_up_sh, w_down_sh):
    xp = x_prompt
    xs = x_sample
    new_k, new_v, new_ckv, new_kr = [], [], [], []
    for l in range(DEPTH):
        lp = {
            'w_ada': w_ada[l], 'b_ada': b_ada[l],
            'norm_pre_mix': norm_pre_mix[l], 'norm_post_mix': norm_post_mix[l],
            'norm_pre_ffn': norm_pre_ffn[l], 'norm_post_ffn': norm_post_ffn[l],
            'w_in': w_in[l], 'sink_swa': sink_swa[l],
            'q_norm_mla': q_norm_mla[l], 'kv_norm_mla': kv_norm_mla[l],
            'w_uq_mla': w_uq_mla[l], 'w_ukv_mla': w_ukv_mla[l],
            'w_o_swa': w_o_swa[l], 'w_o_mla': w_o_mla[l], 'w_out': w_out[l],
            'router_w': router_w[l], 'router_bias': router_bias[l],
            'w_gate_exp': w_gate_exp[l], 'w_up_exp': w_up_exp[l], 'w_down_exp': w_down_exp[l],
            'w_gate_sh': w_gate_sh[l], 'w_up_sh': w_up_sh[l], 'w_down_sh': w_down_sh[l],
        }
        xp, k_l, v_l, ckv_l, kr_l = context_layer(xp, c_ctx, lp)
        new_k.append(k_l)
        new_v.append(v_l)
        new_ckv.append(ckv_l)
        new_kr.append(kr_l)
        xs = latent_layer(xs, c, cache_k_swa[:, l], cache_v_swa[:, l], cache_ckv_mla[:, l],
                          cache_krope_mla[:, l], lp)
    new_k_swa = jnp.stack(new_k, axis=1)
    new_v_swa = jnp.stack(new_v, axis=1)
    new_ckv_mla = jnp.stack(new_ckv, axis=1)
    new_krope_mla = jnp.stack(new_kr, axis=1)
    return (xp, xs, new_k_swa, new_v_swa, new_ckv_mla, new_krope_mla)
```

```python
import functools

import jax
import jax.numpy as jnp
import numpy as np
from jax import lax
from jax.experimental import pallas as pl
from jax.experimental.pallas import tpu as pltpu

D_MODEL = 1024
BATCH = 32
SEQ = 256
DEC_BATCH = 4
DEC_SEQ = 2048
PAST_LEN = 512
GRID_W = 64
WINDOW = 128
ROPE_BASE = 10000.0
EPS = 1e-6
NEG = -1e30
A_HEADS = 8
A_KV_HEADS = 2
A_GROUP = A_HEADS // A_KV_HEADS
A_HEAD_DIM = 64
B_HEADS = 8
B_NOPE = 64
B_ROPE = 32
B_V = 64
KV_LORA = 256
Q_LORA = 384
MLA_SCALE = (B_NOPE + B_ROPE) ** -0.5
N_EXPERTS = 256
TOP_K = 8
N_GROUPS = 8
TOPK_GROUPS = 4
D_EXPERT = 256
D_SHARED = 256
ROUTED_SCALE = 2.5

T_CTX = BATCH * SEQ
T_LAT = DEC_BATCH * DEC_SEQ
T_ALL = T_CTX + T_LAT

LANES = 128
TM = 512
TQ = 256
MOE_BLK = 256
N_SLOT_BLK = (T_ALL * TOP_K) // MOE_BLK + N_EXPERTS
N_SLOTS = N_SLOT_BLK * MOE_BLK
VMEM_LIMIT = 48 * 1024 * 1024

F32 = jnp.float32
BF16 = jnp.bfloat16

_O_QA = 0
_O_KA = _O_QA + A_HEADS * A_HEAD_DIM
_O_VA = _O_KA + A_KV_HEADS * A_HEAD_DIM
_O_QL = _O_VA + A_KV_HEADS * A_HEAD_DIM
_O_CKV = _O_QL + Q_LORA
_O_KR = _O_CKV + KV_LORA
_O_GATE = _O_KR + B_ROPE


def _params(*sem):
    return pltpu.CompilerParams(dimension_semantics=sem, vmem_limit_bytes=VMEM_LIMIT)


def _dot(a, b):
    return jnp.dot(a, b, preferred_element_type=F32)


def _dot_nt(a, b):
    return lax.dot_general(a, b, (((1,), (1,)), ((), ())), preferred_element_type=F32)


def _rms(x, g):
    return x * lax.rsqrt(jnp.mean(x * x, axis=-1, keepdims=True) + EPS) * g


def _sigmoid(x):
    return 1.0 / (1.0 + jnp.exp(-x))


def _silu(x):
    return x * _sigmoid(x)


def _mod_kernel(cond_ref, w_ref, b_ref, o_ref):
    s = _silu(cond_ref[...]).astype(BF16)
    o_ref[...] = _dot(s, w_ref[...].astype(BF16)) + b_ref[...]


def _modulation(cond8, w_ada, b_ada):
    tn = 768
    n = w_ada.shape[1]
    return pl.pallas_call(
        _mod_kernel,
        grid=(n // tn,),
        in_specs=[pl.BlockSpec((8, D_MODEL), lambda j: (0, 0)),
                  pl.BlockSpec((D_MODEL, tn), lambda j: (0, j)),
                  pl.BlockSpec((1, tn), lambda j: (0, j))],
        out_specs=pl.BlockSpec((8, tn), lambda j: (0, j)),
        out_shape=jax.ShapeDtypeStruct((8, n), F32),
        compiler_params=_params("arbitrary"),
        name="modulation",
    )(cond8, w_ada, b_ada.reshape(1, n))


def _rope(x, cos, sin, half):
    lane = lax.broadcasted_iota(jnp.int32, x.shape, 1)
    first = (lane & (2 * half - 1)) < half
    partner = jnp.where(first, pltpu.roll(x, LANES - half, 1), pltpu.roll(x, half, 1))
    return x * cos + partner * sin


def _rope_tables():
    t = jnp.arange(DEC_SEQ, dtype=jnp.int32)
    row = (t // GRID_W).astype(F32)
    col = (t % GRID_W).astype(F32)

    def tables(d_rot):
        nf = d_rot // 4
        inv = ROPE_BASE ** (-jnp.arange(nf, dtype=F32) / nf)
        ar = row[:, None] * inv
        ac = col[:, None] * inv
        cos = jnp.concatenate([jnp.cos(ar), jnp.cos(ar), jnp.cos(ac), jnp.cos(ac)], axis=1)
        sin = jnp.concatenate([-jnp.sin(ar), jnp.sin(ar), -jnp.sin(ac), jnp.sin(ac)], axis=1)
        return cos, sin

    cos_a, sin_a = tables(A_HEAD_DIM)
    cos_a = jnp.concatenate([cos_a, cos_a], axis=1)
    sin_a = jnp.concatenate([sin_a, sin_a], axis=1)
    cos_b, sin_b = tables(B_ROPE)
    one = jnp.ones((DEC_SEQ, B_NOPE), F32)
    zero = jnp.zeros((DEC_SEQ, B_NOPE), F32)
    pad = LANES - B_NOPE - B_ROPE
    cos_b = jnp.concatenate([one, cos_b, one[:, :pad]], axis=1)
    sin_b = jnp.concatenate([zero, sin_b, zero[:, :pad]], axis=1)
    return cos_a, sin_a, cos_b, sin_b


_N1 = 1536


def _inproj_kernel(latent, *refs):
    if latent:
        (x_ref, mod_ref, g_ref, w1_ref, qn_ref, kvn_ref, wuq_ref, wk_ref, wv_ref,
         ca_ref, sa_ref, cb_ref, sb_ref,
         qa_o, ka_o, va_o, qb_o, kb_o, vb_o) = refs
    else:
        (x_ref, mod_ref, g_ref, w1_ref, qn_ref, kvn_ref, wuq_ref, wk_ref, wv_ref,
         qa_o, ka_o, va_o, ckv_o, kr_o, qb_o, kb_o, vb_o) = refs
    x = x_ref[...]
    shift = mod_ref[:, 0:D_MODEL]
    scale = mod_ref[:, D_MODEL:2 * D_MODEL]
    h = _rms(x, g_ref[...]) * (1.0 + scale) + shift
    p = _dot(h.astype(BF16), w1_ref[...])
    qa = p[:, 0:512]
    ka = p[:, 512:640]
    va = p[:, 640:768]
    ql = p[:, 768:1152]
    ckv = p[:, 1152:1408]
    krp = p[:, 1408:1536]
    qn = _rms(ql, qn_ref[...])
    qb = _dot(qn.astype(BF16), wuq_ref[...])
    cn = _rms(ckv, kvn_ref[...])
    cnb = cn.astype(BF16)
    kn = _dot(cnb, wk_ref[...])
    vb = _dot(cnb, wv_ref[...])
    if latent:
        ca, sa, cb, sb = ca_ref[...], sa_ref[...], cb_ref[...], sb_ref[...]
        ka = _rope(ka, ca, sa, A_HEAD_DIM // 4)
        krp = _rope(krp, cb, sb, B_ROPE // 4)
        for j in range(4):
            blk = _rope(qa[:, j * LANES:(j + 1) * LANES], ca, sa, A_HEAD_DIM // 4)
            qa_o[:, j * LANES:(j + 1) * LANES] = (blk * (A_HEAD_DIM ** -0.5)).astype(BF16)
        for hd in range(B_HEADS):
            blk = _rope(qb[:, hd * LANES:(hd + 1) * LANES], cb, sb, B_ROPE // 4)
            qb_o[:, hd * LANES:(hd + 1) * LANES] = (blk * MLA_SCALE).astype(BF16)
        ka_o[...] = ka.astype(BF16)
        va_o[...] = va.astype(BF16)
    else:
        qa_o[...] = (qa * (A_HEAD_DIM ** -0.5)).astype(BF16)
        qb_o[...] = (qb * MLA_SCALE).astype(BF16)
        ka_o[...] = ka
        va_o[...] = va
        ckv_o[...] = cn
        kr_o[...] = krp
    for hd in range(B_HEADS):
        kb_o[:, hd * LANES:(hd + 1) * LANES] = (kn[:, hd * LANES:(hd + 1) * LANES] + krp).astype(BF16)
    vb_o[...] = vb.astype(BF16)


def _inproj(x, mod3, mod_row, g, w1, qn, kvn, wuq, wk, wv, tables):
    latent = tables is not None
    t = x.shape[0]
    n_tiles = t // TM
    row = lambda i: (i, 0)
    const = lambda i: (0, 0)
    in_specs = [pl.BlockSpec((TM, D_MODEL), row),
                pl.BlockSpec((None, 1, 6 * D_MODEL), lambda i: (mod_row(i), 0, 0)),
                pl.BlockSpec((1, D_MODEL), const),
                pl.BlockSpec((D_MODEL, _N1), const),
                pl.BlockSpec((1, Q_LORA), const),
                pl.BlockSpec((1, KV_LORA), const),
                pl.BlockSpec((Q_LORA, B_HEADS * LANES), const),
                pl.BlockSpec((KV_LORA, B_HEADS * LANES), const),
                pl.BlockSpec((KV_LORA, B_HEADS * B_V), const)]
    args = [x, mod3, g, w1, qn, kvn, wuq, wk, wv]
    sds = jax.ShapeDtypeStruct
    if latent:
        per = DEC_SEQ // TM
        tab = lambda i: (i % per, 0)
        in_specs += [pl.BlockSpec((TM, LANES), tab)] * 4
        args += list(tables)
        out_shape = [sds((t, 512), BF16), sds((t, LANES), BF16), sds((t, LANES), BF16),
                     sds((t, B_HEADS * LANES), BF16), sds((t, B_HEADS * LANES), BF16),
                     sds((t, B_HEADS * B_V), BF16)]
        widths = [512, LANES, LANES, B_HEADS * LANES, B_HEADS * LANES, B_HEADS * B_V]
    else:
        out_shape = [sds((t, 512), BF16), sds((t, LANES), F32), sds((t, LANES), F32),
                     sds((t, KV_LORA), F32), sds((t, LANES), F32),
                     sds((t, B_HEADS * LANES), BF16), sds((t, B_HEADS * LANES), BF16),
                     sds((t, B_HEADS * B_V), BF16)]
        widths = [512, LANES, LANES, KV_LORA, LANES, B_HEADS * LANES, B_HEADS * LANES, B_HEADS * B_V]
    out_specs = [pl.BlockSpec((TM, w), row) for w in widths]
    return pl.pallas_call(
        functools.partial(_inproj_kernel, latent),
        grid=(n_tiles,),
        in_specs=in_specs,
        out_specs=out_specs,
        out_shape=out_shape,
        compiler_params=_params("arbitrary"),
        name="inproj_latent" if latent else "inproj_ctx",
    )(*args)


def _kvexp_kernel(ckv_ref, krp_ref, wk_ref, wv_ref, kb_o, vb_o):
    cb = ckv_ref[...].astype(BF16)
    kn = _dot(cb, wk_ref[...])
    krp = krp_ref[...]
    for hd in range(B_HEADS):
        kb_o[:, hd * LANES:(hd + 1) * LANES] = (kn[:, hd * LANES:(hd + 1) * LANES] + krp).astype(BF16)
    vb_o[...] = _dot(cb, wv_ref[...]).astype(BF16)


def _kv_expand(ckv, krp, wk, wv):
    t = ckv.shape[0]
    row = lambda i: (i, 0)
    const = lambda i: (0, 0)
    return pl.pallas_call(
        _kvexp_kernel,
        grid=(t // TM,),
        in_specs=[pl.BlockSpec((TM, KV_LORA), row), pl.BlockSpec((TM, LANES), row),
                  pl.BlockSpec((KV_LORA, B_HEADS * LANES), const),
                  pl.BlockSpec((KV_LORA, B_HEADS * B_V), const)],
        out_specs=[pl.BlockSpec((TM, B_HEADS * LANES), row), pl.BlockSpec((TM, B_HEADS * B_V), row)],
        out_shape=[jax.ShapeDtypeStruct((t, B_HEADS * LANES), BF16),
                   jax.ShapeDtypeStruct((t, B_HEADS * B_V), BF16)],
        compiler_params=_params("arbitrary"),
        name="kv_expand",
    )(ckv, krp, wk, wv)


def _lo_mask(rows):
    return lax.broadcasted_iota(jnp.int32, (rows, LANES), 1) < (LANES // 2)


def _softmax_pv(parts, sink):
    m = None
    for s, _ in parts:
        mi = jnp.max(s, axis=-1, keepdims=True)
        m = mi if m is None else jnp.maximum(m, mi)
    if sink is not None:
        m = jnp.maximum(m, sink)
    den = None
    acc = None
    for s, v in parts:
        p = jnp.exp(s - m)
        di = jnp.sum(p, axis=-1, keepdims=True)
        den = di if den is None else den + di
        oi = _dot(p.astype(BF16), v)
        acc = oi if acc is None else acc + oi
    if sink is not None:
        den = den + jnp.exp(sink - m)
    return acc / den


def _ctx_attn_kernel(sink_ref, qa_ref, ka_ref, va_ref, qb_ref, kb_ref, vb_ref, oa_ref, ob_ref):
    lo = _lo_mask(SEQ)
    k = ka_ref[...].astype(BF16)
    v = va_ref[...].astype(BF16)
    for j in range(A_GROUP):
        q = qa_ref[:, j * LANES:(j + 1) * LANES]
        zero = jnp.zeros_like(q)
        o0 = _softmax_pv([(_dot_nt(jnp.where(lo, q, zero), k), v)], sink_ref[j])
        o1 = _softmax_pv([(_dot_nt(jnp.where(lo, zero, q), k), v)], sink_ref[A_GROUP + j])
        oa_ref[:, j * LANES:(j + 1) * LANES] = jnp.where(lo, o0, o1).astype(BF16)
    for j in range(B_HEADS // 2):
        vp = vb_ref[:, j * LANES:(j + 1) * LANES]
        outs = []
        for hd in (2 * j, 2 * j + 1):
            s = _dot_nt(qb_ref[:, hd * LANES:(hd + 1) * LANES], kb_ref[:, hd * LANES:(hd + 1) * LANES])
            outs.append(_softmax_pv([(s, vp)], None))
        ob_ref[:, j * LANES:(j + 1) * LANES] = jnp.where(lo, outs[0], outs[1]).astype(BF16)


def _ctx_attention(sink, qa, ka, va, qb, kb, vb):
    blk = lambda w: pl.BlockSpec((SEQ, w), lambda b: (b, 0))
    return pl.pallas_call(
        _ctx_attn_kernel,
        grid=(BATCH,),
        in_specs=[pl.BlockSpec(memory_space=pltpu.SMEM),
                  blk(512), blk(LANES), blk(LANES), blk(B_HEADS * LANES), blk(B_HEADS * LANES),
                  blk(B_HEADS * B_V)],
        out_specs=[blk(512), blk(512)],
        out_shape=[jax.ShapeDtypeStruct((T_CTX, 512), BF16), jax.ShapeDtypeStruct((T_CTX, 512), BF16)],
        compiler_params=_params("arbitrary"),
        name="ctx_attention",
    )(sink, qa, ka, va, qb, kb, vb)


_WIN = TQ + 2 * WINDOW


def _lat_swa_kernel(sink_ref, q_ref, k_ref, v_ref, kc_ref, vc_ref, o_ref):
    qi = pl.program_id(1)
    q0 = qi * TQ
    start = pl.multiple_of(jnp.clip(q0 - WINDOW, 0, DEC_SEQ - _WIN), WINDOW)
    kw = k_ref[pl.ds(start, _WIN), :]
    vw = v_ref[pl.ds(start, _WIN), :]
    kc = kc_ref[...].astype(BF16)
    vc = vc_ref[...].astype(BF16)
    qpos = q0 + lax.broadcasted_iota(jnp.int32, (TQ, _WIN), 0)
    kpos = start + lax.broadcasted_iota(jnp.int32, (TQ, _WIN), 1)
    valid = jnp.abs(qpos - kpos) <= WINDOW
    lo = _lo_mask(TQ)
    for j in range(A_GROUP):
        q = q_ref[:, j * LANES:(j + 1) * LANES]
        zero = jnp.zeros_like(q)
        outs = []
        for g, qh in ((0, jnp.where(lo, q, zero)), (1, jnp.where(lo, zero, q))):
            sw = jnp.where(valid, _dot_nt(qh, kw), NEG)
            sc = _dot_nt(qh, kc)
            outs.append(_softmax_pv([(sw, vw), (sc, vc)], sink_ref[g * A_GROUP + j]))
        o_ref[:, j * LANES:(j + 1) * LANES] = jnp.where(lo, outs[0], outs[1]).astype(BF16)


def _lat_swa(sink, qa, ka, va, kc, vc):
    nq = DEC_SEQ // TQ
    return pl.pallas_call(
        _lat_swa_kernel,
        grid=(DEC_BATCH, nq),
        in_specs=[pl.BlockSpec(memory_space=pltpu.SMEM),
                  pl.BlockSpec((TQ, 512), lambda b, i: (b * nq + i, 0)),
                  pl.BlockSpec((None, DEC_SEQ, LANES), lambda b, i: (b, 0, 0)),
                  pl.BlockSpec((None, DEC_SEQ, LANES), lambda b, i: (b, 0, 0)),
                  pl.BlockSpec((None, PAST_LEN, LANES), lambda b, i: (b, 0, 0)),
                  pl.BlockSpec((None, PAST_LEN, LANES), lambda b, i: (b, 0, 0))],
        out_specs=pl.BlockSpec((TQ, 512), lambda b, i: (b * nq + i, 0)),
        out_shape=jax.ShapeDtypeStruct((T_LAT, 512), BF16),
        compiler_params=_params("arbitrary", "arbitrary"),
        name="latent_swa",
    )(sink, qa, ka.reshape(DEC_BATCH, DEC_SEQ, LANES), va.reshape(DEC_BATCH, DEC_SEQ, LANES), kc, vc)


def _lat_mla_kernel(q_ref, kl_ref, vl_ref, kc_ref, vc_ref, o_ref):
    lo = _lo_mask(TQ)
    vl = vl_ref[...]
    vc = vc_ref[...]
    outs = []
    for hh in range(2):
        q = q_ref[:, hh * LANES:(hh + 1) * LANES]
        sc = _dot_nt(q, kc_ref[:, hh * LANES:(hh + 1) * LANES])
        sl = _dot_nt(q, kl_ref[:, hh * LANES:(hh + 1) * LANES])
        outs.append(_softmax_pv([(sc, vc), (sl, vl)], None))
    o_ref[...] = jnp.where(lo, outs[0], outs[1]).astype(BF16)


def _lat_mla(qb, kb, vb, kc, vc):
    nq = DEC_SEQ // TQ
    npair = B_HEADS // 2
    return pl.pallas_call(
        _lat_mla_kernel,
        grid=(DEC_BATCH, npair, nq),
        in_specs=[pl.BlockSpec((TQ, 2 * LANES), lambda b, p, i: (b * nq + i, p)),
                  pl.BlockSpec((None, DEC_SEQ, 2 * LANES), lambda b, p, i: (b, 0, p)),
                  pl.BlockSpec((None, DEC_SEQ, LANES), lambda b, p, i: (b, 0, p)),
                  pl.BlockSpec((None, PAST_LEN, 2 * LANES), lambda b, p, i: (b, 0, p)),
                  pl.BlockSpec((None, PAST_LEN, LANES), lambda b, p, i: (b, 0, p))],
        out_specs=pl.BlockSpec((TQ, LANES), lambda b, p, i: (b * nq + i, p)),
        out_shape=jax.ShapeDtypeStruct((T_LAT, 512), BF16),
        compiler_params=_params("arbitrary", "arbitrary", "arbitrary"),
        name="latent_mla",
    )(qb, kb.reshape(DEC_BATCH, DEC_SEQ, B_HEADS * LANES), vb.reshape(DEC_BATCH, DEC_SEQ, B_HEADS * B_V),
      kc.reshape(DEC_BATCH, PAST_LEN, B_HEADS * LANES), vc.reshape(DEC_BATCH, PAST_LEN, B_HEADS * B_V))


_N_CTX_TILES = T_CTX // TM


def _merge_kernel(xp_ref, xs_ref, oac_ref, oal_ref, obc_ref, obl_ref, mod_ref,
                  gpre_ref, gpost_ref, gffn_ref, woa_ref, wob_ref, wout_ref, wg_ref, wr_ref,
                  x1_o, h2_o, lg_o):
    def body(x_ref, oa_ref, ob_ref):
        x = x_ref[...]
        sh1 = mod_ref[:, 0:D_MODEL]
        sc1 = mod_ref[:, D_MODEL:2 * D_MODEL]
        g1 = mod_ref[:, 2 * D_MODEL:3 * D_MODEL]
        sh2 = mod_ref[:, 3 * D_MODEL:4 * D_MODEL]
        sc2 = mod_ref[:, 4 * D_MODEL:5 * D_MODEL]
        h = (_rms(x, gpre_ref[...]) * (1.0 + sc1) + sh1).astype(BF16)
        gates = _dot(h, wg_ref[...])
        ya = _dot(oa_ref[...], woa_ref[...])
        yb = _dot(ob_ref[...], wob_ref[...])
        y = _sigmoid(gates[:, :D_MODEL]) * ya + _sigmoid(gates[:, D_MODEL:]) * yb
        mix = _dot(y.astype(BF16), wout_ref[...])
        x1 = x + g1 * _rms(mix, gpost_ref[...])
        x1_o[...] = x1
        h2 = (_rms(x1, gffn_ref[...]) * (1.0 + sc2) + sh2).astype(BF16)
        h2_o[...] = h2
        lg_o[...] = _dot(h2, wr_ref[...])

    is_ctx = pl.program_id(0) < _N_CTX_TILES

    @pl.when(is_ctx)
    def _():
        body(xp_ref, oac_ref, obc_ref)

    @pl.when(jnp.logical_not(is_ctx))
    def _():
        body(xs_ref, oal_ref, obl_ref)


def _mod_row_all(i):
    per = DEC_SEQ // TM
    return jnp.where(i < _N_CTX_TILES, 0, 1 + (jnp.maximum(i - _N_CTX_TILES, 0)) // per)


def _merge(xp, xs, oac, oal, obc, obl, mod3, gpre, gpost, gffn, woa, wob, wout, wg, wr):
    ctx = lambda i: (jnp.minimum(i, _N_CTX_TILES - 1), 0)
    lat = lambda i: (jnp.maximum(i - _N_CTX_TILES, 0), 0)
    row = lambda i: (i, 0)
    const = lambda i: (0, 0)
    return pl.pallas_call(
        _merge_kernel,
        grid=(T_ALL // TM,),
        in_specs=[pl.BlockSpec((TM, D_MODEL), ctx), pl.BlockSpec((TM, D_MODEL), lat),
                  pl.BlockSpec((TM, 512), ctx), pl.BlockSpec((TM, 512), lat),
                  pl.BlockSpec((TM, 512), ctx), pl.BlockSpec((TM, 512), lat),
                  pl.BlockSpec((None, 1, 6 * D_MODEL), lambda i: (_mod_row_all(i), 0, 0)),
                  pl.BlockSpec((1, D_MODEL), const), pl.BlockSpec((1, D_MODEL), const),
                  pl.BlockSpec((1, D_MODEL), const),
                  pl.BlockSpec((512, D_MODEL), const), pl.BlockSpec((512, D_MODEL), const),
                  pl.BlockSpec((D_MODEL, D_MODEL), const), pl.BlockSpec((D_MODEL, 2 * D_MODEL), const),
                  pl.BlockSpec((D_MODEL, N_EXPERTS), const)],
        out_specs=[pl.BlockSpec((TM, D_MODEL), row), pl.BlockSpec((TM, D_MODEL), row),
                   pl.BlockSpec((TM, N_EXPERTS), row)],
        out_shape=[jax.ShapeDtypeStruct((T_ALL, D_MODEL), F32), jax.ShapeDtypeStruct((T_ALL, D_MODEL), BF16),
                   jax.ShapeDtypeStruct((T_ALL, N_EXPERTS), F32)],
        compiler_params=_params("arbitrary"),
        name="merge",
    )(xp, xs, oac, oal, obc, obl, mod3, gpre, gpost, gffn, woa, wob, wout, wg, wr)


_PER_GROUP = N_EXPERTS // N_GROUPS


def _router_kernel(lg_ref, bias_ref, topi_o, rank_o, topw_o, cnt_o, cnt_s):
    @pl.when(pl.program_id(0) == 0)
    def _():
        cnt_s[...] = jnp.zeros_like(cnt_s)

    scores = _sigmoid(lg_ref[...])
    sel = scores + bias_ref[...]
    lane_i = lax.broadcasted_iota(jnp.int32, (TM, N_EXPERTS), 1)
    lane = lane_i.astype(F32)
    grp = lax.shift_right_logical(lane_i, int(np.log2(_PER_GROUP)))
    ninf = -jnp.inf
    big = float(N_EXPERTS)
    gs = []
    for g in range(N_GROUPS):
        sg = jnp.where(grp == g, sel, ninf)
        m1 = jnp.max(sg, axis=-1, keepdims=True)
        i1 = jnp.min(jnp.where(sg == m1, lane, big), axis=-1, keepdims=True)
        m2 = jnp.max(jnp.where(lane == i1, ninf, sg), axis=-1, keepdims=True)
        gs.append(m1 + m2)
    keep = jnp.zeros((TM, N_EXPERTS), F32)
    for g in range(N_GROUPS):
        beat = jnp.zeros((TM, 1), F32)
        for g2 in range(N_GROUPS):
            if g2 == g:
                continue
            better = (gs[g2] >= gs[g]) if g2 < g else (gs[g2] > gs[g])
            beat = beat + jnp.where(better, 1.0, 0.0)
        keep = jnp.where(grp == g, jnp.where(beat < TOPK_GROUPS, 1.0, 0.0), keep)
    cur = jnp.where(keep > 0.5, sel, NEG)
    lane8 = lax.broadcasted_iota(jnp.int32, (TM, LANES), 1)
    topi = jnp.zeros((TM, LANES), F32)
    topw = jnp.zeros((TM, LANES), F32)
    chosen = jnp.zeros((TM, N_EXPERTS), F32)
    idxs = []
    for k in range(TOP_K):
        m = jnp.max(cur, axis=-1, keepdims=True)
        idx = jnp.min(jnp.where(cur == m, lane, big), axis=-1, keepdims=True)
        hit = lane == idx
        w = jnp.sum(jnp.where(hit, scores, 0.0), axis=-1, keepdims=True)
        cur = jnp.where(hit, ninf, cur)
        chosen = jnp.where(hit, 1.0, chosen)
        topi = jnp.where(lane8 == k, idx, topi)
        topw = jnp.where(lane8 == k, w, topw)
        idxs.append(idx)
    wsum = jnp.sum(topw, axis=-1, keepdims=True)
    topw_o[...] = topw / wsum * ROUTED_SCALE
    topi_o[...] = topi.astype(jnp.int32)
    r_i = lax.broadcasted_iota(jnp.int32, (TM, TM), 0)
    c_i = lax.broadcasted_iota(jnp.int32, (TM, TM), 1)
    tri = jnp.where(c_i < r_i, 1.0, 0.0).astype(BF16)
    prefix = _dot(tri, chosen.astype(BF16)) + cnt_s[0:1, :]
    rank = jnp.zeros((TM, LANES), jnp.int32)
    for k in range(TOP_K):
        r = jnp.sum(jnp.where(lane == idxs[k], prefix, 0.0), axis=-1, keepdims=True)
        rank = jnp.where(lane8 == k, r.astype(jnp.int32), rank)
    rank_o[...] = rank
    total = cnt_s[...] + jnp.sum(chosen, axis=0, keepdims=True)
    cnt_s[...] = total
    cnt_o[...] = total


def _router(logits, bias):
    row = lambda i: (i, 0)
    const = lambda i: (0, 0)
    return pl.pallas_call(
        _router_kernel,
        grid=(T_ALL // TM,),
        in_specs=[pl.BlockSpec((TM, N_EXPERTS), row), pl.BlockSpec((1, N_EXPERTS), const)],
        out_specs=[pl.BlockSpec((TM, LANES), row), pl.BlockSpec((TM, LANES), row),
                   pl.BlockSpec((TM, LANES), row), pl.BlockSpec((8, N_EXPERTS), const)],
        out_shape=[jax.ShapeDtypeStruct((T_ALL, LANES), jnp.int32),
                   jax.ShapeDtypeStruct((T_ALL, LANES), jnp.int32),
                   jax.ShapeDtypeStruct((T_ALL, LANES), F32),
                   jax.ShapeDtypeStruct((8, N_EXPERTS), F32)],
        scratch_shapes=[pltpu.VMEM((8, N_EXPERTS), F32)],
        compiler_params=_params("arbitrary"),
        name="router",
    )(logits, bias)


def _expert_kernel(be_ref, nu_ref, x_ref, wg_ref, wu_ref, wd_ref, o_ref, wgu_s, wd_s):
    i = pl.program_id(0)
    active = i < nu_ref[0]
    prev = be_ref[jnp.maximum(i - 1, 0)]
    fresh = (i == 0) | (be_ref[i] != prev)

    @pl.when(active & fresh)
    def _():
        wgu_s[:, 0:D_EXPERT] = wg_ref[...].astype(BF16)
        wgu_s[:, D_EXPERT:2 * D_EXPERT] = wu_ref[...].astype(BF16)
        wd_s[...] = wd_ref[...].astype(BF16)

    @pl.when(active)
    def _():
        gu = _dot(x_ref[...], wgu_s[...])
        a = _silu(gu[:, :D_EXPERT]) * gu[:, D_EXPERT:]
        o_ref[...] = _dot(a.astype(BF16), wd_s[...])


def _experts(blk_expert, n_used, x_sorted, w_gate, w_up, w_down):
    def rows(i, be, nu):
        return (jnp.minimum(i, nu[0] - 1), 0)

    def wsel(i, be, nu):
        return (be[jnp.minimum(i, nu[0] - 1)], 0, 0)

    grid_spec = pltpu.PrefetchScalarGridSpec(
        num_scalar_prefetch=2,
        grid=(N_SLOT_BLK,),
        in_specs=[pl.BlockSpec((MOE_BLK, D_MODEL), rows),
                  pl.BlockSpec((None, D_MODEL, D_EXPERT), wsel),
                  pl.BlockSpec((None, D_MODEL, D_EXPERT), wsel),
                  pl.BlockSpec((None, D_EXPERT, D_MODEL), wsel)],
        out_specs=pl.BlockSpec((MOE_BLK, D_MODEL), rows),
        scratch_shapes=[pltpu.VMEM((D_MODEL, 2 * D_EXPERT), BF16), pltpu.VMEM((D_EXPERT, D_MODEL), BF16)],
    )
    return pl.pallas_call(
        _expert_kernel,
        grid_spec=grid_spec,
        out_shape=jax.ShapeDtypeStruct((N_SLOTS, D_MODEL), F32),
        compiler_params=_params("arbitrary"),
        name="experts",
    )(blk_expert, n_used, x_sorted, w_gate, w_up, w_down)


def _final_kernel(x1_ref, h2_ref, r_ref, mod_ref, g_ref, wgu_ref, wd_ref, o_ref):
    gate = mod_ref[:, 5 * D_MODEL:6 * D_MODEL]
    gu = _dot(h2_ref[...], wgu_ref[...])
    a = _silu(gu[:, :D_SHARED]) * gu[:, D_SHARED:]
    shared = _dot(a.astype(BF16), wd_ref[...])
    moe = r_ref[...] + shared
    o_ref[...] = x1_ref[...] + gate * _rms(moe, g_ref[...])


def _final(x1, h2, routed, mod3, mod_row, g, wgu, wd, tile0, n_tiles):
    row = lambda i: (tile0 + i, 0)
    const = lambda i: (0, 0)
    return pl.pallas_call(
        _final_kernel,
        grid=(n_tiles,),
        in_specs=[pl.BlockSpec((TM, D_MODEL), row), pl.BlockSpec((TM, D_MODEL), row),
                  pl.BlockSpec((TM, D_MODEL), row),
                  pl.BlockSpec((None, 1, 6 * D_MODEL), lambda i: (mod_row(i), 0, 0)),
                  pl.BlockSpec((1, D_MODEL), const),
                  pl.BlockSpec((D_MODEL, 2 * D_SHARED), const), pl.BlockSpec((D_SHARED, D_MODEL), const)],
        out_specs=pl.BlockSpec((TM, D_MODEL), lambda i: (i, 0)),
        out_shape=jax.ShapeDtypeStruct((n_tiles * TM, D_MODEL), F32),
        compiler_params=_params("arbitrary"),
        name="final",
    )(x1, h2, routed, mod3, g, wgu, wd)


def _qa_perm():
    cols = []
    for j in range(A_GROUP):
        for g in range(A_KV_HEADS):
            hd = g * A_GROUP + j
            cols.extend(range(hd * A_HEAD_DIM, (hd + 1) * A_HEAD_DIM))
    return np.asarray(cols, np.int32)


def _pad_heads(w, width, n_heads, offset, total=LANES):
    k = w.shape[0]
    stride = w.shape[1] // n_heads
    w3 = w.reshape(k, n_heads, stride)[:, :, offset:offset + width]
    w3 = jnp.pad(w3, ((0, 0), (0, 0), (0, total - width)))
    return w3.reshape(k, n_heads * total)


def kernel(x_prompt, x_sample, cache_k_swa, cache_v_swa, cache_ckv_mla, cache_krope_mla, c, c_ctx, w_ada, b_ada, norm_pre_mix, norm_post_mix, norm_pre_ffn, norm_post_ffn, w_in, sink_swa, q_norm_mla, kv_norm_mla, w_uq_mla, w_ukv_mla, w_o_swa, w_o_mla, w_out, router_w, router_bias, w_gate_exp, w_up_exp, w_down_exp, w_gate_sh, w_up_sh, w_down_sh):
    l = 0
    perm = _qa_perm()
    wi = w_in[l]
    kr_blk = jnp.pad(wi[:, _O_KR:_O_GATE], ((0, 0), (B_NOPE, LANES - B_NOPE - B_ROPE)))
    w1 = jnp.concatenate([wi[:, _O_QA:_O_KA][:, perm], wi[:, _O_KA:_O_KR], kr_blk], axis=1).astype(BF16)
    wg = wi[:, _O_GATE:].astype(BF16)
    wuq = _pad_heads(w_uq_mla[l], B_NOPE + B_ROPE, B_HEADS, 0).astype(BF16)
    wk = _pad_heads(w_ukv_mla[l], B_NOPE, B_HEADS, 0).astype(BF16)
    wv = w_ukv_mla[l].reshape(KV_LORA, B_HEADS, B_NOPE + B_V)[:, :, B_NOPE:].reshape(KV_LORA, B_HEADS * B_V).astype(BF16)
    woa = w_o_swa[l][perm, :].astype(BF16)
    wob = w_o_mla[l].astype(BF16)
    wout = w_out[l].astype(BF16)
    wr = router_w[l].astype(BF16)
    wgu_sh = jnp.concatenate([w_gate_sh[l], w_up_sh[l]], axis=1).astype(BF16)
    wd_sh = w_down_sh[l].astype(BF16)
    sink = sink_swa[l]
    row2 = lambda v: v.reshape(1, -1)

    cond8 = jnp.concatenate([c_ctx[None, :], c, jnp.zeros((8 - 1 - DEC_BATCH, D_MODEL), F32)], axis=0)
    mod3 = _modulation(cond8, w_ada[l], b_ada[l]).reshape(8, 1, 6 * D_MODEL)

    xp = x_prompt.reshape(T_CTX, D_MODEL)
    xs = x_sample.reshape(T_LAT, D_MODEL)
    per = DEC_SEQ // TM
    shared_in = (row2(norm_pre_mix[l]), w1, row2(q_norm_mla[l]), row2(kv_norm_mla[l]), wuq, wk, wv)
    qa_c, ka_c, va_c, ckv_c, kr_c, qb_c, kb_c, vb_c = _inproj(xp, mod3, lambda i: 0, *shared_in, None)
    qa_l, ka_l, va_l, qb_l, kb_l, vb_l = _inproj(xs, mod3, lambda i: 1 + i // per, *shared_in, _rope_tables())

    oa_c, ob_c = _ctx_attention(sink, qa_c, ka_c, va_c, qb_c, kb_c, vb_c)

    krp_cache = jnp.pad(cache_krope_mla[:, l].reshape(DEC_BATCH * PAST_LEN, B_ROPE),
                        ((0, 0), (B_NOPE, LANES - B_NOPE - B_ROPE)))
    kc_b, vc_b = _kv_expand(cache_ckv_mla[:, l].reshape(DEC_BATCH * PAST_LEN, KV_LORA), krp_cache, wk, wv)
    oa_l = _lat_swa(sink, qa_l, ka_l, va_l,
                    cache_k_swa[:, l].reshape(DEC_BATCH, PAST_LEN, LANES),
                    cache_v_swa[:, l].reshape(DEC_BATCH, PAST_LEN, LANES))
    ob_l = _lat_mla(qb_l, kb_l, vb_l, kc_b, vc_b)

    x1, h2, logits = _merge(xp, xs, oa_c, oa_l, ob_c, ob_l, mod3,
                            row2(norm_pre_mix[l]), row2(norm_post_mix[l]), row2(norm_pre_ffn[l]),
                            woa, wob, wout, wg, wr)

    topi_p, rank_p, topw_p, counts8 = _router(logits, row2(router_bias[l]))
    topi = topi_p[:, :TOP_K]
    rank = rank_p[:, :TOP_K]
    topw = topw_p[:, :TOP_K]
    counts = counts8[0].astype(jnp.int32)

    padded = (counts + MOE_BLK - 1) // MOE_BLK * MOE_BLK
    pend = jnp.cumsum(padded)
    pstart = pend - padded
    dest = (pstart[topi] + rank).reshape(-1)
    tok = jnp.arange(T_ALL * TOP_K, dtype=jnp.int32) // TOP_K
    slot_tok = jnp.zeros((N_SLOTS,), jnp.int32).at[dest].set(tok)
    blk_start = jnp.arange(N_SLOT_BLK, dtype=jnp.int32) * MOE_BLK
    blk_expert = jnp.minimum(jnp.sum(blk_start[:, None] >= pend[None, :], axis=1), N_EXPERTS - 1).astype(jnp.int32)
    n_used = (pend[-1] // MOE_BLK).astype(jnp.int32).reshape(1)

    x_sorted = jnp.take(h2, slot_tok, axis=0)
    eo = _experts(blk_expert, n_used, x_sorted, w_gate_exp[l], w_up_exp[l], w_down_exp[l])
    routed = jnp.sum(jnp.take(eo, dest, axis=0).reshape(T_ALL, TOP_K, D_MODEL) * topw[:, :, None], axis=1)

    g_post = row2(norm_post_ffn[l])
    y_p = _final(x1, h2, routed, mod3, lambda i: 0, g_post, wgu_sh, wd_sh, 0, _N_CTX_TILES)
    y_s = _final(x1, h2, routed, mod3, lambda i: 1 + i // per, g_post, wgu_sh, wd_sh, _N_CTX_TILES, T_LAT // TM)

    return (y_p.reshape(BATCH, SEQ, D_MODEL), y_s.reshape(DEC_BATCH, DEC_SEQ, D_MODEL),
            ka_c.reshape(BATCH, 1, SEQ, A_KV_HEADS, A_HEAD_DIM),
            va_c.reshape(BATCH, 1, SEQ, A_KV_HEADS, A_HEAD_DIM),
            ckv_c.reshape(BATCH, 1, SEQ, KV_LORA),
            kr_c[:, B_NOPE:B_NOPE + B_ROPE].reshape(BATCH, 1, SEQ, B_ROPE))
```

```python
import functools

import jax
import jax.numpy as jnp
import numpy as np
from jax import lax
from jax.experimental import pallas as pl
from jax.experimental.pallas import tpu as pltpu

D_MODEL = 1024
BATCH = 32
SEQ = 256
DEC_BATCH = 4
DEC_SEQ = 2048
PAST_LEN = 512
GRID_W = 64
WINDOW = 128
ROPE_BASE = 10000.0
EPS = 1e-6
NEG = -1e30
A_HEADS = 8
A_KV_HEADS = 2
A_GROUP = A_HEADS // A_KV_HEADS
A_HEAD_DIM = 64
B_HEADS = 8
B_NOPE = 64
B_ROPE = 32
B_V = 64
KV_LORA = 256
Q_LORA = 384
MLA_SCALE = (B_NOPE + B_ROPE) ** -0.5
N_EXPERTS = 256
TOP_K = 8
N_GROUPS = 8
TOPK_GROUPS = 4
D_EXPERT = 256
D_SHARED = 256
ROUTED_SCALE = 2.5

T_CTX = BATCH * SEQ
T_LAT = DEC_BATCH * DEC_SEQ
T_ALL = T_CTX + T_LAT

LANES = 128
TM = 512
TQ = 256
MOE_BLK = 256
N_SLOT_BLK = (T_ALL * TOP_K) // MOE_BLK + N_EXPERTS
N_SLOTS = N_SLOT_BLK * MOE_BLK
VMEM_LIMIT = 48 * 1024 * 1024

F32 = jnp.float32
BF16 = jnp.bfloat16

_O_QA = 0
_O_KA = _O_QA + A_HEADS * A_HEAD_DIM
_O_VA = _O_KA + A_KV_HEADS * A_HEAD_DIM
_O_QL = _O_VA + A_KV_HEADS * A_HEAD_DIM
_O_CKV = _O_QL + Q_LORA
_O_KR = _O_CKV + KV_LORA
_O_GATE = _O_KR + B_ROPE


def _params(*sem):
    return pltpu.CompilerParams(dimension_semantics=sem, vmem_limit_bytes=VMEM_LIMIT)


def _dot(a, b):
    return jnp.dot(a, b, preferred_element_type=F32)


def _dot_nt(a, b):
    return lax.dot_general(a, b, (((1,), (1,)), ((), ())), preferred_element_type=F32)


def _rms(x, g):
    return x * lax.rsqrt(jnp.mean(x * x, axis=-1, keepdims=True) + EPS) * g


def _sigmoid(x):
    return 1.0 / (1.0 + jnp.exp(-x))


def _silu(x):
    return x * _sigmoid(x)


def _mod_kernel(cond_ref, w_ref, b_ref, o_ref):
    s = _silu(cond_ref[...]).astype(BF16)
    o_ref[...] = _dot(s, w_ref[...].astype(BF16)) + b_ref[...]


def _modulation(cond8, w_ada, b_ada):
    tn = 768
    n = w_ada.shape[1]
    return pl.pallas_call(
        _mod_kernel,
        grid=(n // tn,),
        in_specs=[pl.BlockSpec((8, D_MODEL), lambda j: (0, 0)),
                  pl.BlockSpec((D_MODEL, tn), lambda j: (0, j)),
                  pl.BlockSpec((1, tn), lambda j: (0, j))],
        out_specs=pl.BlockSpec((8, tn), lambda j: (0, j)),
        out_shape=jax.ShapeDtypeStruct((8, n), F32),
        compiler_params=_params("arbitrary"),
        name="modulation",
    )(cond8, w_ada, b_ada.reshape(1, n))


def _rope(x, cos, sin, half):
    lane = lax.broadcasted_iota(jnp.int32, x.shape, 1)
    first = (lane & (2 * half - 1)) < half
    partner = jnp.where(first, pltpu.roll(x, LANES - half, 1), pltpu.roll(x, half, 1))
    return x * cos + partner * sin


def _rope_tables():
    t = jnp.arange(DEC_SEQ, dtype=jnp.int32)
    row = (t // GRID_W).astype(F32)
    col = (t % GRID_W).astype(F32)

    def tables(d_rot):
        nf = d_rot // 4
        inv = ROPE_BASE ** (-jnp.arange(nf, dtype=F32) / nf)
        ar = row[:, None] * inv
        ac = col[:, None] * inv
        cos = jnp.concatenate([jnp.cos(ar), jnp.cos(ar), jnp.cos(ac), jnp.cos(ac)], axis=1)
        sin = jnp.concatenate([-jnp.sin(ar), jnp.sin(ar), -jnp.sin(ac), jnp.sin(ac)], axis=1)
        return cos, sin

    cos_a, sin_a = tables(A_HEAD_DIM)
    cos_a = jnp.concatenate([cos_a, cos_a], axis=1)
    sin_a = jnp.concatenate([sin_a, sin_a], axis=1)
    cos_b, sin_b = tables(B_ROPE)
    one = jnp.ones((DEC_SEQ, B_NOPE), F32)
    zero = jnp.zeros((DEC_SEQ, B_NOPE), F32)
    pad = LANES - B_NOPE - B_ROPE
    cos_b = jnp.concatenate([one, cos_b, one[:, :pad]], axis=1)
    sin_b = jnp.concatenate([zero, sin_b, zero[:, :pad]], axis=1)
    return cos_a, sin_a, cos_b, sin_b


_N1 = 1536


def _inproj_kernel(latent, *refs):
    if latent:
        (x_ref, mod_ref, g_ref, w1_ref, qn_ref, kvn_ref, wuq_ref, wk_ref, wv_ref,
         ca_ref, sa_ref, cb_ref, sb_ref,
         qa_o, ka_o, va_o, qb_o, kb_o, vb_o) = refs
    else:
        (x_ref, mod_ref, g_ref, w1_ref, qn_ref, kvn_ref, wuq_ref, wk_ref, wv_ref,
         qa_o, ka_o, va_o, ckv_o, kr_o, qb_o, kb_o, vb_o) = refs
    x = x_ref[...]
    shift = mod_ref[:, 0:D_MODEL]
    scale = mod_ref[:, D_MODEL:2 * D_MODEL]
    h = _rms(x, g_ref[...]) * (1.0 + scale) + shift
    p = _dot(h.astype(BF16), w1_ref[...])
    qa = p[:, 0:512]
    ka = p[:, 512:640]
    va = p[:, 640:768]
    ql = p[:, 768:1152]
    ckv = p[:, 1152:1408]
    krp = p[:, 1408:1536]
    qn = _rms(ql, qn_ref[...])
    qb = _dot(qn.astype(BF16), wuq_ref[...])
    cn = _rms(ckv, kvn_ref[...])
    cnb = cn.astype(BF16)
    kn = _dot(cnb, wk_ref[...])
    vb = _dot(cnb, wv_ref[...])
    if latent:
        ca, sa, cb, sb = ca_ref[...], sa_ref[...], cb_ref[...], sb_ref[...]
        ka = _rope(ka, ca, sa, A_HEAD_DIM // 4)
        krp = _rope(krp, cb, sb, B_ROPE // 4)
        for j in range(4):
            blk = _rope(qa[:, j * LANES:(j + 1) * LANES], ca, sa, A_HEAD_DIM // 4)
            qa_o[:, j * LANES:(j + 1) * LANES] = (blk * (A_HEAD_DIM ** -0.5)).astype(BF16)
        for hd in range(B_HEADS):
            blk = _rope(qb[:, hd * LANES:(hd + 1) * LANES], cb, sb, B_ROPE // 4)
            qb_o[:, hd * LANES:(hd + 1) * LANES] = (blk * MLA_SCALE).astype(BF16)
        ka_o[...] = ka.astype(BF16)
        va_o[...] = va.astype(BF16)
    else:
        qa_o[...] = (qa * (A_HEAD_DIM ** -0.5)).astype(BF16)
        qb_o[...] = (qb * MLA_SCALE).astype(BF16)
        ka_o[...] = ka
        va_o[...] = va
        ckv_o[...] = cn
        kr_o[...] = krp
    for hd in range(B_HEADS):
        kb_o[:, hd * LANES:(hd + 1) * LANES] = (kn[:, hd * LANES:(hd + 1) * LANES] + krp).astype(BF16)
    vb_o[...] = vb.astype(BF16)


def _inproj(x, mod3, mod_row, g, w1, qn, kvn, wuq, wk, wv, tables):
    latent = tables is not None
    t = x.shape[0]
    n_tiles = t // TM
    row = lambda i: (i, 0)
    const = lambda i: (0, 0)
    in_specs = [pl.BlockSpec((TM, D_MODEL), row),
                pl.BlockSpec((None, 1, 6 * D_MODEL), lambda i: (mod_row(i), 0, 0)),
                pl.BlockSpec((1, D_MODEL), const),
                pl.BlockSpec((D_MODEL, _N1), const),
                pl.BlockSpec((1, Q_LORA), const),
                pl.BlockSpec((1, KV_LORA), const),
                pl.BlockSpec((Q_LORA, B_HEADS * LANES), const),
                pl.BlockSpec((KV_LORA, B_HEADS * LANES), const),
                pl.BlockSpec((KV_LORA, B_HEADS * B_V), const)]
    args = [x, mod3, g, w1, qn, kvn, wuq, wk, wv]
    sds = jax.ShapeDtypeStruct
    if latent:
        per = DEC_SEQ // TM
        tab = lambda i: (i % per, 0)
        in_specs += [pl.BlockSpec((TM, LANES), tab)] * 4
        args += list(tables)
        out_shape = [sds((t, 512), BF16), sds((t, LANES), BF16), sds((t, LANES), BF16),
                     sds((t, B_HEADS * LANES), BF16), sds((t, B_HEADS * LANES), BF16),
                     sds((t, B_HEADS * B_V), BF16)]
        widths = [512, LANES, LANES, B_HEADS * LANES, B_HEADS * LANES, B_HEADS * B_V]
    else:
        out_shape = [sds((t, 512), BF16), sds((t, LANES), F32), sds((t, LANES), F32),
                     sds((t, KV_LORA), F32), sds((t, LANES), F32),
                     sds((t, B_HEADS * LANES), BF16), sds((t, B_HEADS * LANES), BF16),
                     sds((t, B_HEADS * B_V), BF16)]
        widths = [512, LANES, LANES, KV_LORA, LANES, B_HEADS * LANES, B_HEADS * LANES, B_HEADS * B_V]
    out_specs = [pl.BlockSpec((TM, w), row) for w in widths]
    return pl.pallas_call(
        functools.partial(_inproj_kernel, latent),
        grid=(n_tiles,),
        in_specs=in_specs,
        out_specs=out_specs,
        out_shape=out_shape,
        compiler_params=_params("arbitrary"),
        name="inproj_latent" if latent else "inproj_ctx",
    )(*args)


def _kvexp_kernel(ckv_ref, krp_ref, wk_ref, wv_ref, kb_o, vb_o):
    cb = ckv_ref[...].astype(BF16)
    kn = _dot(cb, wk_ref[...])
    krp = krp_ref[...]
    for hd in range(B_HEADS):
        kb_o[:, hd * LANES:(hd + 1) * LANES] = (kn[:, hd * LANES:(hd + 1) * LANES] + krp).astype(BF16)
    vb_o[...] = _dot(cb, wv_ref[...]).astype(BF16)


def _kv_expand(ckv, krp, wk, wv):
    t = ckv.shape[0]
    row = lambda i: (i, 0)
    const = lambda i: (0, 0)
    return pl.pallas_call(
        _kvexp_kernel,
        grid=(t // TM,),
        in_specs=[pl.BlockSpec((TM, KV_LORA), row), pl.BlockSpec((TM, LANES), row),
                  pl.BlockSpec((KV_LORA, B_HEADS * LANES), const),
                  pl.BlockSpec((KV_LORA, B_HEADS * B_V), const)],
        out_specs=[pl.BlockSpec((TM, B_HEADS * LANES), row), pl.BlockSpec((TM, B_HEADS * B_V), row)],
        out_shape=[jax.ShapeDtypeStruct((t, B_HEADS * LANES), BF16),
                   jax.ShapeDtypeStruct((t, B_HEADS * B_V), BF16)],
        compiler_params=_params("arbitrary"),
        name="kv_expand",
    )(ckv, krp, wk, wv)


def _lo_mask(rows):
    return lax.broadcasted_iota(jnp.int32, (rows, LANES), 1) < (LANES // 2)


def _softmax_pv(parts, sink):
    m = None
    for s, _ in parts:
        mi = jnp.max(s, axis=-1, keepdims=True)
        m = mi if m is None else jnp.maximum(m, mi)
    if sink is not None:
        m = jnp.maximum(m, sink)
    den = None
    acc = None
    for s, v in parts:
        p = jnp.exp(s - m)
        di = jnp.sum(p, axis=-1, keepdims=True)
        den = di if den is None else den + di
        oi = _dot(p.astype(BF16), v)
        acc = oi if acc is None else acc + oi
    if sink is not None:
        den = den + jnp.exp(sink - m)
    return acc / den


def _ctx_attn_kernel(sink_ref, qa_ref, ka_ref, va_ref, qb_ref, kb_ref, vb_ref, oa_ref, ob_ref):
    lo = _lo_mask(SEQ)
    k = ka_ref[...].astype(BF16)
    v = va_ref[...].astype(BF16)
    for j in range(A_GROUP):
        q = qa_ref[:, j * LANES:(j + 1) * LANES]
        zero = jnp.zeros_like(q)
        o0 = _softmax_pv([(_dot_nt(jnp.where(lo, q, zero), k), v)], sink_ref[j])
        o1 = _softmax_pv([(_dot_nt(jnp.where(lo, zero, q), k), v)], sink_ref[A_GROUP + j])
        oa_ref[:, j * LANES:(j + 1) * LANES] = jnp.where(lo, o0, o1).astype(BF16)
    for j in range(B_HEADS // 2):
        vp = vb_ref[:, j * LANES:(j + 1) * LANES]
        outs = []
        for hd in (2 * j, 2 * j + 1):
            s = _dot_nt(qb_ref[:, hd * LANES:(hd + 1) * LANES], kb_ref[:, hd * LANES:(hd + 1) * LANES])
            outs.append(_softmax_pv([(s, vp)], None))
        ob_ref[:, j * LANES:(j + 1) * LANES] = jnp.where(lo, outs[0], outs[1]).astype(BF16)


def _ctx_attention(sink, qa, ka, va, qb, kb, vb):
    blk = lambda w: pl.BlockSpec((SEQ, w), lambda b: (b, 0))
    return pl.pallas_call(
        _ctx_attn_kernel,
        grid=(BATCH,),
        in_specs=[pl.BlockSpec(memory_space=pltpu.SMEM),
                  blk(512), blk(LANES), blk(LANES), blk(B_HEADS * LANES), blk(B_HEADS * LANES),
                  blk(B_HEADS * B_V)],
        out_specs=[blk(512), blk(512)],
        out_shape=[jax.ShapeDtypeStruct((T_CTX, 512), BF16), jax.ShapeDtypeStruct((T_CTX, 512), BF16)],
        compiler_params=_params("arbitrary"),
        name="ctx_attention",
    )(sink, qa, ka, va, qb, kb, vb)


_WIN = TQ + 2 * WINDOW


def _lat_swa_kernel(sink_ref, q_ref, k_ref, v_ref, kc_ref, vc_ref, o_ref):
    qi = pl.program_id(1)
    q0 = qi * TQ
    start = pl.multiple_of(jnp.clip(q0 - WINDOW, 0, DEC_SEQ - _WIN), WINDOW)
    kw = k_ref[pl.ds(start, _WIN), :]
    vw = v_ref[pl.ds(start, _WIN), :]
    kc = kc_ref[...].astype(BF16)
    vc = vc_ref[...].astype(BF16)
    qpos = q0 + lax.broadcasted_iota(jnp.int32, (TQ, _WIN), 0)
    kpos = start + lax.broadcasted_iota(jnp.int32, (TQ, _WIN), 1)
    valid = jnp.abs(qpos - kpos) <= WINDOW
    lo = _lo_mask(TQ)
    for j in range(A_GROUP):
        q = q_ref[:, j * LANES:(j + 1) * LANES]
        zero = jnp.zeros_like(q)
        outs = []
        for g, qh in ((0, jnp.where(lo, q, zero)), (1, jnp.where(lo, zero, q))):
            sw = jnp.where(valid, _dot_nt(qh, kw), NEG)
            sc = _dot_nt(qh, kc)
            outs.append(_softmax_pv([(sw, vw), (sc, vc)], sink_ref[g * A_GROUP + j]))
        o_ref[:, j * LANES:(j + 1) * LANES] = jnp.where(lo, outs[0], outs[1]).astype(BF16)


def _lat_swa(sink, qa, ka, va, kc, vc):
    nq = DEC_SEQ // TQ
    return pl.pallas_call(
        _lat_swa_kernel,
        grid=(DEC_BATCH, nq),
        in_specs=[pl.BlockSpec(memory_space=pltpu.SMEM),
                  pl.BlockSpec((TQ, 512), lambda b, i: (b * nq + i, 0)),
                  pl.BlockSpec((None, DEC_SEQ, LANES), lambda b, i: (b, 0, 0)),
                  pl.BlockSpec((None, DEC_SEQ, LANES), lambda b, i: (b, 0, 0)),
                  pl.BlockSpec((None, PAST_LEN, LANES), lambda b, i: (b, 0, 0)),
                  pl.BlockSpec((None, PAST_LEN, LANES), lambda b, i: (b, 0, 0))],
        out_specs=pl.BlockSpec((TQ, 512), lambda b, i: (b * nq + i, 0)),
        out_shape=jax.ShapeDtypeStruct((T_LAT, 512), BF16),
        compiler_params=_params("arbitrary", "arbitrary"),
        name="latent_swa",
    )(sink, qa, ka.reshape(DEC_BATCH, DEC_SEQ, LANES), va.reshape(DEC_BATCH, DEC_SEQ, LANES), kc, vc)


def _lat_mla_kernel(q_ref, kl_ref, vl_ref, kc_ref, vc_ref, o_ref):
    lo = _lo_mask(TQ)
    vl = vl_ref[...]
    vc = vc_ref[...]
    outs = []
    for hh in range(2):
        q = q_ref[:, hh * LANES:(hh + 1) * LANES]
        sc = _dot_nt(q, kc_ref[:, hh * LANES:(hh + 1) * LANES])
        sl = _dot_nt(q, kl_ref[:, hh * LANES:(hh + 1) * LANES])
        outs.append(_softmax_pv([(sc, vc), (sl, vl)], None))
    o_ref[...] = jnp.where(lo, outs[0], outs[1]).astype(BF16)


def _lat_mla(qb, kb, vb, kc, vc):
    nq = DEC_SEQ // TQ
    npair = B_HEADS // 2
    return pl.pallas_call(
        _lat_mla_kernel,
        grid=(DEC_BATCH, npair, nq),
        in_specs=[pl.BlockSpec((TQ, 2 * LANES), lambda b, p, i: (b * nq + i, p)),
                  pl.BlockSpec((None, DEC_SEQ, 2 * LANES), lambda b, p, i: (b, 0, p)),
                  pl.BlockSpec((None, DEC_SEQ, LANES), lambda b, p, i: (b, 0, p)),
                  pl.BlockSpec((None, PAST_LEN, 2 * LANES), lambda b, p, i: (b, 0, p)),
                  pl.BlockSpec((None, PAST_LEN, LANES), lambda b, p, i: (b, 0, p))],
        out_specs=pl.BlockSpec((TQ, LANES), lambda b, p, i: (b * nq + i, p)),
        out_shape=jax.ShapeDtypeStruct((T_LAT, 512), BF16),
        compiler_params=_params("arbitrary", "arbitrary", "arbitrary"),
        name="latent_mla",
    )(qb, kb.reshape(DEC_BATCH, DEC_SEQ, B_HEADS * LANES), vb.reshape(DEC_BATCH, DEC_SEQ, B_HEADS * B_V),
      kc.reshape(DEC_BATCH, PAST_LEN, B_HEADS * LANES), vc.reshape(DEC_BATCH, PAST_LEN, B_HEADS * B_V))


_N_CTX_TILES = T_CTX // TM


def _merge_kernel(xp_ref, xs_ref, oac_ref, oal_ref, obc_ref, obl_ref, mod_ref,
                  gpre_ref, gpost_ref, gffn_ref, woa_ref, wob_ref, wout_ref, wg_ref, wr_ref,
                  x1_o, h2_o, lg_o):
    def body(x_ref, oa_ref, ob_ref):
        x = x_ref[...]
        sh1 = mod_ref[:, 0:D_MODEL]
        sc1 = mod_ref[:, D_MODEL:2 * D_MODEL]
        g1 = mod_ref[:, 2 * D_MODEL:3 * D_MODEL]
        sh2 = mod_ref[:, 3 * D_MODEL:4 * D_MODEL]
        sc2 = mod_ref[:, 4 * D_MODEL:5 * D_MODEL]
        h = (_rms(x, gpre_ref[...]) * (1.0 + sc1) + sh1).astype(BF16)
        gates = _dot(h, wg_ref[...])
        ya = _dot(oa_ref[...], woa_ref[...])
        yb = _dot(ob_ref[...], wob_ref[...])
        y = _sigmoid(gates[:, :D_MODEL]) * ya + _sigmoid(gates[:, D_MODEL:]) * yb
        mix = _dot(y.astype(BF16), wout_ref[...])
        x1 = x + g1 * _rms(mix, gpost_ref[...])
        x1_o[...] = x1
        h2 = _rms(x1, gffn_ref[...]) * (1.0 + sc2) + sh2
        h2_o[...] = h2
        lg_o[...] = _dot(h2.astype(BF16), wr_ref[...])

    is_ctx = pl.program_id(0) < _N_CTX_TILES

    @pl.when(is_ctx)
    def _():
        body(xp_ref, oac_ref, obc_ref)

    @pl.when(jnp.logical_not(is_ctx))
    def _():
        body(xs_ref, oal_ref, obl_ref)


def _mod_row_all(i):
    per = DEC_SEQ // TM
    return jnp.where(i < _N_CTX_TILES, 0, 1 + (jnp.maximum(i - _N_CTX_TILES, 0)) // per)


def _merge(xp, xs, oac, oal, obc, obl, mod3, gpre, gpost, gffn, woa, wob, wout, wg, wr):
    ctx = lambda i: (jnp.minimum(i, _N_CTX_TILES - 1), 0)
    lat = lambda i: (jnp.maximum(i - _N_CTX_TILES, 0), 0)
    row = lambda i: (i, 0)
    const = lambda i: (0, 0)
    return pl.pallas_call(
        _merge_kernel,
        grid=(T_ALL // TM,),
        in_specs=[pl.BlockSpec((TM, D_MODEL), ctx), pl.BlockSpec((TM, D_MODEL), lat),
                  pl.BlockSpec((TM, 512), ctx), pl.BlockSpec((TM, 512), lat),
                  pl.BlockSpec((TM, 512), ctx), pl.BlockSpec((TM, 512), lat),
                  pl.BlockSpec((None, 1, 6 * D_MODEL), lambda i: (_mod_row_all(i), 0, 0)),
                  pl.BlockSpec((1, D_MODEL), const), pl.BlockSpec((1, D_MODEL), const),
                  pl.BlockSpec((1, D_MODEL), const),
                  pl.BlockSpec((512, D_MODEL), const), pl.BlockSpec((512, D_MODEL), const),
                  pl.BlockSpec((D_MODEL, D_MODEL), const), pl.BlockSpec((D_MODEL, 2 * D_MODEL), const),
                  pl.BlockSpec((D_MODEL, N_EXPERTS), const)],
        out_specs=[pl.BlockSpec((TM, D_MODEL), row), pl.BlockSpec((TM, D_MODEL), row),
                   pl.BlockSpec((TM, N_EXPERTS), row)],
        out_shape=[jax.ShapeDtypeStruct((T_ALL, D_MODEL), F32), jax.ShapeDtypeStruct((T_ALL, D_MODEL), F32),
                   jax.ShapeDtypeStruct((T_ALL, N_EXPERTS), F32)],
        compiler_params=_params("arbitrary"),
        name="merge",
    )(xp, xs, oac, oal, obc, obl, mod3, gpre, gpost, gffn, woa, wob, wout, wg, wr)


_PER_GROUP = N_EXPERTS // N_GROUPS


def _router_kernel(lg_ref, bias_ref, topi_o, rank_o, topw_o, cnt_o, cnt_s):
    @pl.when(pl.program_id(0) == 0)
    def _():
        cnt_s[...] = jnp.zeros_like(cnt_s)

    scores = _sigmoid(lg_ref[...])
    sel = scores + bias_ref[...]
    lane_i = lax.broadcasted_iota(jnp.int32, (TM, N_EXPERTS), 1)
    lane = lane_i.astype(F32)
    grp = lax.shift_right_logical(lane_i, int(np.log2(_PER_GROUP)))
    ninf = -jnp.inf
    big = float(N_EXPERTS)
    gs = []
    for g in range(N_GROUPS):
        sg = jnp.where(grp == g, sel, ninf)
        m1 = jnp.max(sg, axis=-1, keepdims=True)
        i1 = jnp.min(jnp.where(sg == m1, lane, big), axis=-1, keepdims=True)
        m2 = jnp.max(jnp.where(lane == i1, ninf, sg), axis=-1, keepdims=True)
        gs.append(m1 + m2)
    keep = jnp.zeros((TM, N_EXPERTS), F32)
    for g in range(N_GROUPS):
        beat = jnp.zeros((TM, 1), F32)
        for g2 in range(N_GROUPS):
            if g2 == g:
                continue
            better = (gs[g2] >= gs[g]) if g2 < g else (gs[g2] > gs[g])
            beat = beat + jnp.where(better, 1.0, 0.0)
        keep = jnp.where(grp == g, jnp.where(beat < TOPK_GROUPS, 1.0, 0.0), keep)
    cur = jnp.where(keep > 0.5, sel, NEG)
    lane8 = lax.broadcasted_iota(jnp.int32, (TM, LANES), 1)
    topi = jnp.zeros((TM, LANES), F32)
    topw = jnp.zeros((TM, LANES), F32)
    chosen = jnp.zeros((TM, N_EXPERTS), F32)
    idxs = []
    for k in range(TOP_K):
        m = jnp.max(cur, axis=-1, keepdims=True)
        idx = jnp.min(jnp.where(cur == m, lane, big), axis=-1, keepdims=True)
        hit = lane == idx
        w = jnp.sum(jnp.where(hit, scores, 0.0), axis=-1, keepdims=True)
        cur = jnp.where(hit, ninf, cur)
        chosen = jnp.where(hit, 1.0, chosen)
        topi = jnp.where(lane8 == k, idx, topi)
        topw = jnp.where(lane8 == k, w, topw)
        idxs.append(idx)
    wsum = jnp.sum(topw, axis=-1, keepdims=True)
    topw_o[...] = topw / wsum * ROUTED_SCALE
    topi_o[...] = topi.astype(jnp.int32)
    r_i = lax.broadcasted_iota(jnp.int32, (TM, TM), 0)
    c_i = lax.broadcasted_iota(jnp.int32, (TM, TM), 1)
    tri = jnp.where(c_i < r_i, 1.0, 0.0).astype(BF16)
    prefix = _dot(tri, chosen.astype(BF16)) + cnt_s[0:1, :]
    rank = jnp.zeros((TM, LANES), jnp.int32)
    for k in range(TOP_K):
        r = jnp.sum(jnp.where(lane == idxs[k], prefix, 0.0), axis=-1, keepdims=True)
        rank = jnp.where(lane8 == k, r.astype(jnp.int32), rank)
    rank_o[...] = rank
    total = cnt_s[...] + jnp.sum(chosen, axis=0, keepdims=True)
    cnt_s[...] = total
    cnt_o[...] = total


def _router(logits, bias):
    row = lambda i: (i, 0)
    const = lambda i: (0, 0)
    return pl.pallas_call(
        _router_kernel,
        grid=(T_ALL // TM,),
        in_specs=[pl.BlockSpec((TM, N_EXPERTS), row), pl.BlockSpec((1, N_EXPERTS), const)],
        out_specs=[pl.BlockSpec((TM, LANES), row), pl.BlockSpec((TM, LANES), row),
                   pl.BlockSpec((TM, LANES), row), pl.BlockSpec((8, N_EXPERTS), const)],
        out_shape=[jax.ShapeDtypeStruct((T_ALL, LANES), jnp.int32),
                   jax.ShapeDtypeStruct((T_ALL, LANES), jnp.int32),
                   jax.ShapeDtypeStruct((T_ALL, LANES), F32),
                   jax.ShapeDtypeStruct((8, N_EXPERTS), F32)],
        scratch_shapes=[pltpu.VMEM((8, N_EXPERTS), F32)],
        compiler_params=_params("arbitrary"),
        name="router",
    )(logits, bias)


TD = 512
ROW_UNROLL = 4


def _dispatch_kernel(dest_ref, h_hbm, xs_hbm, sem):
    base = pl.program_id(0) * TD

    def row(t, carry):
        src = h_hbm.at[pl.ds(base + t, 1)]
        for k in range(TOP_K):
            d = dest_ref[t * TOP_K + k]
            pltpu.make_async_copy(src, xs_hbm.at[pl.ds(d, 1)], sem).start(priority=k % 2)
        return carry

    lax.fori_loop(0, TD, row, 0, unroll=ROW_UNROLL)
    for _ in range(TOP_K):
        pltpu.make_async_copy(h_hbm.at[pl.ds(0, TD)], xs_hbm.at[pl.ds(0, TD)], sem).wait()


def _dispatch(dest_flat, h2):
    return pl.pallas_call(
        _dispatch_kernel,
        grid=(T_ALL // TD,),
        in_specs=[pl.BlockSpec((TD * TOP_K,), lambda i: (i,), memory_space=pltpu.SMEM),
                  pl.BlockSpec(memory_space=pl.ANY)],
        out_specs=pl.BlockSpec(memory_space=pl.ANY),
        out_shape=jax.ShapeDtypeStruct((N_SLOTS, D_MODEL), F32),
        scratch_shapes=[pltpu.SemaphoreType.DMA(())],
        compiler_params=_params("arbitrary"),
        name="dispatch",
    )(dest_flat, h2)


def _expert_kernel(be_ref, nu_ref, x_ref, wg_ref, wu_ref, wd_ref, o_ref, wgu_s, wd_s):
    i = pl.program_id(0)
    active = i < nu_ref[0]
    prev = be_ref[jnp.maximum(i - 1, 0)]
    fresh = (i == 0) | (be_ref[i] != prev)

    @pl.when(active & fresh)
    def _():
        wgu_s[:, 0:D_EXPERT] = wg_ref[...].astype(BF16)
        wgu_s[:, D_EXPERT:2 * D_EXPERT] = wu_ref[...].astype(BF16)
        wd_s[...] = wd_ref[...].astype(BF16)

    @pl.when(active)
    def _():
        gu = _dot(x_ref[...].astype(BF16), wgu_s[...])
        a = _silu(gu[:, :D_EXPERT]) * gu[:, D_EXPERT:]
        o_ref[...] = _dot(a.astype(BF16), wd_s[...])


def _experts(blk_expert, n_used, x_sorted, w_gate, w_up, w_down):
    def rows(i, be, nu):
        return (jnp.minimum(i, nu[0] - 1), 0)

    def wsel(i, be, nu):
        return (be[jnp.minimum(i, nu[0] - 1)], 0, 0)

    grid_spec = pltpu.PrefetchScalarGridSpec(
        num_scalar_prefetch=2,
        grid=(N_SLOT_BLK,),
        in_specs=[pl.BlockSpec((MOE_BLK, D_MODEL), rows),
                  pl.BlockSpec((None, D_MODEL, D_EXPERT), wsel),
                  pl.BlockSpec((None, D_MODEL, D_EXPERT), wsel),
                  pl.BlockSpec((None, D_EXPERT, D_MODEL), wsel)],
        out_specs=pl.BlockSpec((MOE_BLK, D_MODEL), rows),
        scratch_shapes=[pltpu.VMEM((D_MODEL, 2 * D_EXPERT), BF16), pltpu.VMEM((D_EXPERT, D_MODEL), BF16)],
    )
    return pl.pallas_call(
        _expert_kernel,
        grid_spec=grid_spec,
        out_shape=jax.ShapeDtypeStruct((N_SLOTS, D_MODEL), F32),
        compiler_params=_params("arbitrary"),
        name="experts",
    )(blk_expert, n_used, x_sorted, w_gate, w_up, w_down)


TC = 256


def _combine_kernel(dcur_ref, dnext_ref, eo_hbm, x1_ref, h2_ref, w_ref, mod_ref, g_ref, wgu_ref, wd_ref,
                    o_ref, buf, sem):
    i = pl.program_id(0)
    n = pl.num_programs(0)
    slot = lax.rem(i, 2)

    def gather(d_ref, s):
        def row(t, carry):
            for k in range(TOP_K):
                d = d_ref[t * TOP_K + k]
                pltpu.make_async_copy(eo_hbm.at[pl.ds(d, 1)], buf.at[s, k, pl.ds(t, 1)],
                                      sem.at[s]).start(priority=k % 2)
            return carry
        lax.fori_loop(0, TC, row, 0, unroll=ROW_UNROLL)

    @pl.when(i == 0)
    def _():
        gather(dcur_ref, 0)

    @pl.when(i + 1 < n)
    def _():
        gather(dnext_ref, 1 - slot)

    gate = mod_ref[:, 5 * D_MODEL:6 * D_MODEL]
    gu = _dot(h2_ref[...].astype(BF16), wgu_ref[...])
    a = _silu(gu[:, :D_SHARED]) * gu[:, D_SHARED:]
    moe = _dot(a.astype(BF16), wd_ref[...])
    for k in range(TOP_K):
        pltpu.make_async_copy(eo_hbm.at[pl.ds(0, TC)], buf.at[slot, k], sem.at[slot]).wait()
    w = w_ref[...]
    for k in range(TOP_K):
        moe = moe + buf[slot, k] * w[:, k:k + 1]
    o_ref[...] = x1_ref[...] + gate * _rms(moe, g_ref[...])


def _combine(dest_flat, eo, x1, h2, topw, mod3, mod_row, g, wgu, wd, tile0, n_tiles):
    row = lambda i: (tile0 + i, 0)
    const = lambda i: (0, 0)
    blk = TC * TOP_K
    return pl.pallas_call(
        _combine_kernel,
        grid=(n_tiles,),
        in_specs=[pl.BlockSpec((blk,), lambda i: (tile0 + i,), memory_space=pltpu.SMEM),
                  pl.BlockSpec((blk,), lambda i: (tile0 + jnp.minimum(i + 1, n_tiles - 1),),
                               memory_space=pltpu.SMEM),
                  pl.BlockSpec(memory_space=pl.ANY),
                  pl.BlockSpec((TC, D_MODEL), row), pl.BlockSpec((TC, D_MODEL), row),
                  pl.BlockSpec((TC, LANES), row),
                  pl.BlockSpec((None, 1, 6 * D_MODEL), lambda i: (mod_row(i), 0, 0)),
                  pl.BlockSpec((1, D_MODEL), const),
                  pl.BlockSpec((D_MODEL, 2 * D_SHARED), const), pl.BlockSpec((D_SHARED, D_MODEL), const)],
        out_specs=pl.BlockSpec((TC, D_MODEL), lambda i: (i, 0)),
        out_shape=jax.ShapeDtypeStruct((n_tiles * TC, D_MODEL), F32),
        scratch_shapes=[pltpu.VMEM((2, TOP_K, TC, D_MODEL), F32), pltpu.SemaphoreType.DMA((2,))],
        compiler_params=_params("arbitrary"),
        name="combine",
    )(dest_flat, dest_flat, eo, x1, h2, topw, mod3, g, wgu, wd)


def _qa_perm():
    cols = []
    for j in range(A_GROUP):
        for g in range(A_KV_HEADS):
            hd = g * A_GROUP + j
            cols.extend(range(hd * A_HEAD_DIM, (hd + 1) * A_HEAD_DIM))
    return np.asarray(cols, np.int32)


def _pad_heads(w, width, n_heads, offset, total=LANES):
    k = w.shape[0]
    stride = w.shape[1] // n_heads
    w3 = w.reshape(k, n_heads, stride)[:, :, offset:offset + width]
    w3 = jnp.pad(w3, ((0, 0), (0, 0), (0, total - width)))
    return w3.reshape(k, n_heads * total)


def kernel(x_prompt, x_sample, cache_k_swa, cache_v_swa, cache_ckv_mla, cache_krope_mla, c, c_ctx, w_ada, b_ada, norm_pre_mix, norm_post_mix, norm_pre_ffn, norm_post_ffn, w_in, sink_swa, q_norm_mla, kv_norm_mla, w_uq_mla, w_ukv_mla, w_o_swa, w_o_mla, w_out, router_w, router_bias, w_gate_exp, w_up_exp, w_down_exp, w_gate_sh, w_up_sh, w_down_sh):
    l = 0
    perm = _qa_perm()
    wi = w_in[l]
    kr_blk = jnp.pad(wi[:, _O_KR:_O_GATE], ((0, 0), (B_NOPE, LANES - B_NOPE - B_ROPE)))
    w1 = jnp.concatenate([wi[:, _O_QA:_O_KA][:, perm], wi[:, _O_KA:_O_KR], kr_blk], axis=1).astype(BF16)
    wg = wi[:, _O_GATE:].astype(BF16)
    wuq = _pad_heads(w_uq_mla[l], B_NOPE + B_ROPE, B_HEADS, 0).astype(BF16)
    wk = _pad_heads(w_ukv_mla[l], B_NOPE, B_HEADS, 0).astype(BF16)
    wv = w_ukv_mla[l].reshape(KV_LORA, B_HEADS, B_NOPE + B_V)[:, :, B_NOPE:].reshape(KV_LORA, B_HEADS * B_V).astype(BF16)
    woa = w_o_swa[l][perm, :].astype(BF16)
    wob = w_o_mla[l].astype(BF16)
    wout = w_out[l].astype(BF16)
    wr = router_w[l].astype(BF16)
    wgu_sh = jnp.concatenate([w_gate_sh[l], w_up_sh[l]], axis=1).astype(BF16)
    wd_sh = w_down_sh[l].astype(BF16)
    sink = sink_swa[l]
    row2 = lambda v: v.reshape(1, -1)

    cond8 = jnp.concatenate([c_ctx[None, :], c, jnp.zeros((8 - 1 - DEC_BATCH, D_MODEL), F32)], axis=0)
    mod3 = _modulation(cond8, w_ada[l], b_ada[l]).reshape(8, 1, 6 * D_MODEL)

    xp = x_prompt.reshape(T_CTX, D_MODEL)
    xs = x_sample.reshape(T_LAT, D_MODEL)
    per = DEC_SEQ // TM
    shared_in = (row2(norm_pre_mix[l]), w1, row2(q_norm_mla[l]), row2(kv_norm_mla[l]), wuq, wk, wv)
    qa_c, ka_c, va_c, ckv_c, kr_c, qb_c, kb_c, vb_c = _inproj(xp, mod3, lambda i: 0, *shared_in, None)
    qa_l, ka_l, va_l, qb_l, kb_l, vb_l = _inproj(xs, mod3, lambda i: 1 + i // per, *shared_in, _rope_tables())

    oa_c, ob_c = _ctx_attention(sink, qa_c, ka_c, va_c, qb_c, kb_c, vb_c)

    krp_cache = jnp.pad(cache_krope_mla[:, l].reshape(DEC_BATCH * PAST_LEN, B_ROPE),
                        ((0, 0), (B_NOPE, LANES - B_NOPE - B_ROPE)))
    kc_b, vc_b = _kv_expand(cache_ckv_mla[:, l].reshape(DEC_BATCH * PAST_LEN, KV_LORA), krp_cache, wk, wv)
    oa_l = _lat_swa(sink, qa_l, ka_l, va_l,
                    cache_k_swa[:, l].reshape(DEC_BATCH, PAST_LEN, LANES),
                    cache_v_swa[:, l].reshape(DEC_BATCH, PAST_LEN, LANES))
    ob_l = _lat_mla(qb_l, kb_l, vb_l, kc_b, vc_b)

    x1, h2, logits = _merge(xp, xs, oa_c, oa_l, ob_c, ob_l, mod3,
                            row2(norm_pre_mix[l]), row2(norm_post_mix[l]), row2(norm_pre_ffn[l]),
                            woa, wob, wout, wg, wr)

    topi_p, rank_p, topw_p, counts8 = _router(logits, row2(router_bias[l]))
    topi = topi_p[:, :TOP_K]
    rank = rank_p[:, :TOP_K]
    counts = counts8[0].astype(jnp.int32)

    padded = (counts + MOE_BLK - 1) // MOE_BLK * MOE_BLK
    pend = jnp.cumsum(padded)
    pstart = pend - padded
    dest = (pstart[topi] + rank).reshape(-1)
    blk_start = jnp.arange(N_SLOT_BLK, dtype=jnp.int32) * MOE_BLK
    blk_expert = jnp.minimum(jnp.sum(blk_start[:, None] >= pend[None, :], axis=1), N_EXPERTS - 1).astype(jnp.int32)
    n_used = (pend[-1] // MOE_BLK).astype(jnp.int32).reshape(1)

    x_sorted = _dispatch(dest, h2)
    eo = _experts(blk_expert, n_used, x_sorted, w_gate_exp[l], w_up_exp[l], w_down_exp[l])

    g_post = row2(norm_post_ffn[l])
    per_c = DEC_SEQ // TC
    y_p = _combine(dest, eo, x1, h2, topw_p, mod3, lambda i: 0, g_post, wgu_sh, wd_sh, 0, T_CTX // TC)
    y_s = _combine(dest, eo, x1, h2, topw_p, mod3, lambda i: 1 + i // per_c, g_post, wgu_sh, wd_sh,
                   T_CTX // TC, T_LAT // TC)

    return (y_p.reshape(BATCH, SEQ, D_MODEL), y_s.reshape(DEC_BATCH, DEC_SEQ, D_MODEL),
            ka_c.reshape(BATCH, 1, SEQ, A_KV_HEADS, A_HEAD_DIM),
            va_c.reshape(BATCH, 1, SEQ, A_KV_HEADS, A_HEAD_DIM),
            ckv_c.reshape(BATCH, 1, SEQ, KV_LORA),
            kr_c[:, B_NOPE:B_NOPE + B_ROPE].reshape(BATCH, 1, SEQ, B_ROPE))
```

```python
import functools

import jax
import jax.numpy as jnp
import numpy as np
from jax import lax
from jax.experimental import pallas as pl
from jax.experimental.pallas import tpu as pltpu

D_MODEL = 1024
BATCH = 32
SEQ = 256
DEC_BATCH = 4
DEC_SEQ = 2048
PAST_LEN = 512
GRID_W = 64
WINDOW = 128
ROPE_BASE = 10000.0
EPS = 1e-6
NEG = -1e30
A_HEADS = 8
A_KV_HEADS = 2
A_GROUP = A_HEADS // A_KV_HEADS
A_HEAD_DIM = 64
B_HEADS = 8
B_NOPE = 64
B_ROPE = 32
B_V = 64
KV_LORA = 256
Q_LORA = 384
MLA_SCALE = (B_NOPE + B_ROPE) ** -0.5
N_EXPERTS = 256
TOP_K = 8
N_GROUPS = 8
TOPK_GROUPS = 4
D_EXPERT = 256
D_SHARED = 256
ROUTED_SCALE = 2.5

T_CTX = BATCH * SEQ
T_LAT = DEC_BATCH * DEC_SEQ
T_ALL = T_CTX + T_LAT

LANES = 128
TM = 512
TQ = 256
MOE_BLK = 256
N_SLOT_BLK = (T_ALL * TOP_K) // MOE_BLK + N_EXPERTS
N_SLOTS = N_SLOT_BLK * MOE_BLK
VMEM_LIMIT = 48 * 1024 * 1024

F32 = jnp.float32
BF16 = jnp.bfloat16

_O_QA = 0
_O_KA = _O_QA + A_HEADS * A_HEAD_DIM
_O_VA = _O_KA + A_KV_HEADS * A_HEAD_DIM
_O_QL = _O_VA + A_KV_HEADS * A_HEAD_DIM
_O_CKV = _O_QL + Q_LORA
_O_KR = _O_CKV + KV_LORA
_O_GATE = _O_KR + B_ROPE


def _params(*sem):
    return pltpu.CompilerParams(dimension_semantics=sem, vmem_limit_bytes=VMEM_LIMIT)


def _dot(a, b):
    return jnp.dot(a, b, preferred_element_type=F32)


def _dot_nt(a, b):
    return lax.dot_general(a, b, (((1,), (1,)), ((), ())), preferred_element_type=F32)


def _rms(x, g):
    return x * lax.rsqrt(jnp.mean(x * x, axis=-1, keepdims=True) + EPS) * g


def _sigmoid(x):
    return 1.0 / (1.0 + jnp.exp(-x))


def _silu(x):
    return x * _sigmoid(x)


def _mod_kernel(cond_ref, w_ref, b_ref, o_ref):
    s = _silu(cond_ref[...]).astype(BF16)
    o_ref[...] = _dot(s, w_ref[...].astype(BF16)) + b_ref[...]


def _modulation(cond8, w_ada, b_ada):
    tn = 768
    n = w_ada.shape[1]
    return pl.pallas_call(
        _mod_kernel,
        grid=(n // tn,),
        in_specs=[pl.BlockSpec((8, D_MODEL), lambda j: (0, 0)),
                  pl.BlockSpec((D_MODEL, tn), lambda j: (0, j)),
                  pl.BlockSpec((1, tn), lambda j: (0, j))],
        out_specs=pl.BlockSpec((8, tn), lambda j: (0, j)),
        out_shape=jax.ShapeDtypeStruct((8, n), F32),
        compiler_params=_params("arbitrary"),
        name="modulation",
    )(cond8, w_ada, b_ada.reshape(1, n))


def _rope(x, cos, sin, half):
    lane = lax.broadcasted_iota(jnp.int32, x.shape, 1)
    first = (lane & (2 * half - 1)) < half
    partner = jnp.where(first, pltpu.roll(x, LANES - half, 1), pltpu.roll(x, half, 1))
    return x * cos + partner * sin


def _rope_tables():
    t = jnp.arange(DEC_SEQ, dtype=jnp.int32)
    row = (t // GRID_W).astype(F32)
    col = (t % GRID_W).astype(F32)

    def tables(d_rot):
        nf = d_rot // 4
        inv = ROPE_BASE ** (-jnp.arange(nf, dtype=F32) / nf)
        ar = row[:, None] * inv
        ac = col[:, None] * inv
        cos = jnp.concatenate([jnp.cos(ar), jnp.cos(ar), jnp.cos(ac), jnp.cos(ac)], axis=1)
        sin = jnp.concatenate([-jnp.sin(ar), jnp.sin(ar), -jnp.sin(ac), jnp.sin(ac)], axis=1)
        return cos, sin

    cos_a, sin_a = tables(A_HEAD_DIM)
    cos_a = jnp.concatenate([cos_a, cos_a], axis=1)
    sin_a = jnp.concatenate([sin_a, sin_a], axis=1)
    cos_b, sin_b = tables(B_ROPE)
    one = jnp.ones((DEC_SEQ, B_NOPE), F32)
    zero = jnp.zeros((DEC_SEQ, B_NOPE), F32)
    pad = LANES - B_NOPE - B_ROPE
    cos_b = jnp.concatenate([one, cos_b, one[:, :pad]], axis=1)
    sin_b = jnp.concatenate([zero, sin_b, zero[:, :pad]], axis=1)
    return cos_a, sin_a, cos_b, sin_b


_N1 = 1536


def _inproj_kernel(latent, *refs):
    if latent:
        (x_ref, mod_ref, g_ref, w1_ref, qn_ref, kvn_ref, wuq_ref, wk_ref, wv_ref,
         ca_ref, sa_ref, cb_ref, sb_ref,
         qa_o, ka_o, va_o, qb_o, kb_o, vb_o) = refs
    else:
        (x_ref, mod_ref, g_ref, w1_ref, qn_ref, kvn_ref, wuq_ref, wk_ref, wv_ref,
         qa_o, ka_o, va_o, ckv_o, kr_o, qb_o, kb_o, vb_o) = refs
    x = x_ref[...]
    shift = mod_ref[:, 0:D_MODEL]
    scale = mod_ref[:, D_MODEL:2 * D_MODEL]
    h = _rms(x, g_ref[...]) * (1.0 + scale) + shift
    p = _dot(h.astype(BF16), w1_ref[...])
    qa = p[:, 0:512]
    ka = p[:, 512:640]
    va = p[:, 640:768]
    ql = p[:, 768:1152]
    ckv = p[:, 1152:1408]
    krp = p[:, 1408:1536]
    qn = _rms(ql, qn_ref[...])
    qb = _dot(qn.astype(BF16), wuq_ref[...])
    cn = _rms(ckv, kvn_ref[...])
    cnb = cn.astype(BF16)
    kn = _dot(cnb, wk_ref[...])
    vb = _dot(cnb, wv_ref[...])
    if latent:
        ca, sa, cb, sb = ca_ref[...], sa_ref[...], cb_ref[...], sb_ref[...]
        ka = _rope(ka, ca, sa, A_HEAD_DIM // 4)
        krp = _rope(krp, cb, sb, B_ROPE // 4)
        for j in range(4):
            blk = _rope(qa[:, j * LANES:(j + 1) * LANES], ca, sa, A_HEAD_DIM // 4)
            qa_o[:, j * LANES:(j + 1) * LANES] = (blk * (A_HEAD_DIM ** -0.5)).astype(BF16)
        for hd in range(B_HEADS):
            blk = _rope(qb[:, hd * LANES:(hd + 1) * LANES], cb, sb, B_ROPE // 4)
            qb_o[:, hd * LANES:(hd + 1) * LANES] = (blk * MLA_SCALE).astype(BF16)
        ka_o[...] = ka.astype(BF16)
        va_o[...] = va.astype(BF16)
    else:
        qa_o[...] = (qa * (A_HEAD_DIM ** -0.5)).astype(BF16)
        qb_o[...] = (qb * MLA_SCALE).astype(BF16)
        ka_o[...] = ka
        va_o[...] = va
        ckv_o[...] = cn
        kr_o[...] = krp
    for hd in range(B_HEADS):
        kb_o[:, hd * LANES:(hd + 1) * LANES] = (kn[:, hd * LANES:(hd + 1) * LANES] + krp).astype(BF16)
    vb_o[...] = vb.astype(BF16)


def _inproj(x, mod3, mod_row, g, w1, qn, kvn, wuq, wk, wv, tables):
    latent = tables is not None
    t = x.shape[0]
    n_tiles = t // TM
    row = lambda i: (i, 0)
    const = lambda i: (0, 0)
    in_specs = [pl.BlockSpec((TM, D_MODEL), row),
                pl.BlockSpec((None, 1, 6 * D_MODEL), lambda i: (mod_row(i), 0, 0)),
                pl.BlockSpec((1, D_MODEL), const),
                pl.BlockSpec((D_MODEL, _N1), const),
                pl.BlockSpec((1, Q_LORA), const),
                pl.BlockSpec((1, KV_LORA), const),
                pl.BlockSpec((Q_LORA, B_HEADS * LANES), const),
                pl.BlockSpec((KV_LORA, B_HEADS * LANES), const),
                pl.BlockSpec((KV_LORA, B_HEADS * B_V), const)]
    args = [x, mod3, g, w1, qn, kvn, wuq, wk, wv]
    sds = jax.ShapeDtypeStruct
    if latent:
        per = DEC_SEQ // TM
        tab = lambda i: (i % per, 0)
        in_specs += [pl.BlockSpec((TM, LANES), tab)] * 4
        args += list(tables)
        out_shape = [sds((t, 512), BF16), sds((t, LANES), BF16), sds((t, LANES), BF16),
                     sds((t, B_HEADS * LANES), BF16), sds((t, B_HEADS * LANES), BF16),
                     sds((t, B_HEADS * B_V), BF16)]
        widths = [512, LANES, LANES, B_HEADS * LANES, B_HEADS * LANES, B_HEADS * B_V]
    else:
        out_shape = [sds((t, 512), BF16), sds((t, LANES), F32), sds((t, LANES), F32),
                     sds((t, KV_LORA), F32), sds((t, LANES), F32),
                     sds((t, B_HEADS * LANES), BF16), sds((t, B_HEADS * LANES), BF16),
                     sds((t, B_HEADS * B_V), BF16)]
        widths = [512, LANES, LANES, KV_LORA, LANES, B_HEADS * LANES, B_HEADS * LANES, B_HEADS * B_V]
    out_specs = [pl.BlockSpec((TM, w), row) for w in widths]
    return pl.pallas_call(
        functools.partial(_inproj_kernel, latent),
        grid=(n_tiles,),
        in_specs=in_specs,
        out_specs=out_specs,
        out_shape=out_shape,
        compiler_params=_params("arbitrary"),
        name="inproj_latent" if latent else "inproj_ctx",
    )(*args)


def _kvexp_kernel(ckv_ref, krp_ref, wk_ref, wv_ref, kb_o, vb_o):
    cb = ckv_ref[...].astype(BF16)
    kn = _dot(cb, wk_ref[...])
    krp = krp_ref[...]
    for hd in range(B_HEADS):
        kb_o[:, hd * LANES:(hd + 1) * LANES] = (kn[:, hd * LANES:(hd + 1) * LANES] + krp).astype(BF16)
    vb_o[...] = _dot(cb, wv_ref[...]).astype(BF16)


def _kv_expand(ckv, krp, wk, wv):
    t = ckv.shape[0]
    row = lambda i: (i, 0)
    const = lambda i: (0, 0)
    return pl.pallas_call(
        _kvexp_kernel,
        grid=(t // TM,),
        in_specs=[pl.BlockSpec((TM, KV_LORA), row), pl.BlockSpec((TM, LANES), row),
                  pl.BlockSpec((KV_LORA, B_HEADS * LANES), const),
                  pl.BlockSpec((KV_LORA, B_HEADS * B_V), const)],
        out_specs=[pl.BlockSpec((TM, B_HEADS * LANES), row), pl.BlockSpec((TM, B_HEADS * B_V), row)],
        out_shape=[jax.ShapeDtypeStruct((t, B_HEADS * LANES), BF16),
                   jax.ShapeDtypeStruct((t, B_HEADS * B_V), BF16)],
        compiler_params=_params("arbitrary"),
        name="kv_expand",
    )(ckv, krp, wk, wv)


def _lo_mask(rows):
    return lax.broadcasted_iota(jnp.int32, (rows, LANES), 1) < (LANES // 2)


def _softmax_pv(parts, sink):
    m = None
    for s, _ in parts:
        mi = jnp.max(s, axis=-1, keepdims=True)
        m = mi if m is None else jnp.maximum(m, mi)
    if sink is not None:
        m = jnp.maximum(m, sink)
    den = None
    acc = None
    for s, v in parts:
        p = jnp.exp(s - m)
        di = jnp.sum(p, axis=-1, keepdims=True)
        den = di if den is None else den + di
        oi = _dot(p.astype(BF16), v)
        acc = oi if acc is None else acc + oi
    if sink is not None:
        den = den + jnp.exp(sink - m)
    return acc / den


def _ctx_attn_kernel(sink_ref, qa_ref, ka_ref, va_ref, qb_ref, kb_ref, vb_ref, oa_ref, ob_ref):
    lo = _lo_mask(SEQ)
    k = ka_ref[...].astype(BF16)
    v = va_ref[...].astype(BF16)
    for j in range(A_GROUP):
        q = qa_ref[:, j * LANES:(j + 1) * LANES]
        zero = jnp.zeros_like(q)
        o0 = _softmax_pv([(_dot_nt(jnp.where(lo, q, zero), k), v)], sink_ref[j])
        o1 = _softmax_pv([(_dot_nt(jnp.where(lo, zero, q), k), v)], sink_ref[A_GROUP + j])
        oa_ref[:, j * LANES:(j + 1) * LANES] = jnp.where(lo, o0, o1).astype(BF16)
    for j in range(B_HEADS // 2):
        vp = vb_ref[:, j * LANES:(j + 1) * LANES]
        outs = []
        for hd in (2 * j, 2 * j + 1):
            s = _dot_nt(qb_ref[:, hd * LANES:(hd + 1) * LANES], kb_ref[:, hd * LANES:(hd + 1) * LANES])
            outs.append(_softmax_pv([(s, vp)], None))
        ob_ref[:, j * LANES:(j + 1) * LANES] = jnp.where(lo, outs[0], outs[1]).astype(BF16)


def _ctx_attention(sink, qa, ka, va, qb, kb, vb):
    blk = lambda w: pl.BlockSpec((SEQ, w), lambda b: (b, 0))
    return pl.pallas_call(
        _ctx_attn_kernel,
        grid=(BATCH,),
        in_specs=[pl.BlockSpec(memory_space=pltpu.SMEM),
                  blk(512), blk(LANES), blk(LANES), blk(B_HEADS * LANES), blk(B_HEADS * LANES),
                  blk(B_HEADS * B_V)],
        out_specs=[blk(512), blk(512)],
        out_shape=[jax.ShapeDtypeStruct((T_CTX, 512), BF16), jax.ShapeDtypeStruct((T_CTX, 512), BF16)],
        compiler_params=_params("arbitrary"),
        name="ctx_attention",
    )(sink, qa, ka, va, qb, kb, vb)


_WIN = TQ + 2 * WINDOW


def _lat_swa_kernel(sink_ref, q_ref, k_ref, v_ref, kc_ref, vc_ref, o_ref):
    qi = pl.program_id(1)
    q0 = qi * TQ
    start = pl.multiple_of(jnp.clip(q0 - WINDOW, 0, DEC_SEQ - _WIN), WINDOW)
    kw = k_ref[pl.ds(start, _WIN), :]
    vw = v_ref[pl.ds(start, _WIN), :]
    kc = kc_ref[...].astype(BF16)
    vc = vc_ref[...].astype(BF16)
    qpos = q0 + lax.broadcasted_iota(jnp.int32, (TQ, _WIN), 0)
    kpos = start + lax.broadcasted_iota(jnp.int32, (TQ, _WIN), 1)
    valid = jnp.abs(qpos - kpos) <= WINDOW
    lo = _lo_mask(TQ)
    for j in range(A_GROUP):
        q = q_ref[:, j * LANES:(j + 1) * LANES]
        zero = jnp.zeros_like(q)
        outs = []
        for g, qh in ((0, jnp.where(lo, q, zero)), (1, jnp.where(lo, zero, q))):
            sw = jnp.where(valid, _dot_nt(qh, kw), NEG)
            sc = _dot_nt(qh, kc)
            outs.append(_softmax_pv([(sw, vw), (sc, vc)], sink_ref[g * A_GROUP + j]))
        o_ref[:, j * LANES:(j + 1) * LANES] = jnp.where(lo, outs[0], outs[1]).astype(BF16)


def _lat_swa(sink, qa, ka, va, kc, vc):
    nq = DEC_SEQ // TQ
    return pl.pallas_call(
        _lat_swa_kernel,
        grid=(DEC_BATCH, nq),
        in_specs=[pl.BlockSpec(memory_space=pltpu.SMEM),
                  pl.BlockSpec((TQ, 512), lambda b, i: (b * nq + i, 0)),
                  pl.BlockSpec((None, DEC_SEQ, LANES), lambda b, i: (b, 0, 0)),
                  pl.BlockSpec((None, DEC_SEQ, LANES), lambda b, i: (b, 0, 0)),
                  pl.BlockSpec((None, PAST_LEN, LANES), lambda b, i: (b, 0, 0)),
                  pl.BlockSpec((None, PAST_LEN, LANES), lambda b, i: (b, 0, 0))],
        out_specs=pl.BlockSpec((TQ, 512), lambda b, i: (b * nq + i, 0)),
        out_shape=jax.ShapeDtypeStruct((T_LAT, 512), BF16),
        compiler_params=_params("arbitrary", "arbitrary"),
        name="latent_swa",
    )(sink, qa, ka.reshape(DEC_BATCH, DEC_SEQ, LANES), va.reshape(DEC_BATCH, DEC_SEQ, LANES), kc, vc)


def _lat_mla_kernel(q_ref, kl_ref, vl_ref, kc_ref, vc_ref, o_ref):
    lo = _lo_mask(TQ)
    vl = vl_ref[...]
    vc = vc_ref[...]
    outs = []
    for hh in range(2):
        q = q_ref[:, hh * LANES:(hh + 1) * LANES]
        sc = _dot_nt(q, kc_ref[:, hh * LANES:(hh + 1) * LANES])
        sl = _dot_nt(q, kl_ref[:, hh * LANES:(hh + 1) * LANES])
        outs.append(_softmax_pv([(sc, vc), (sl, vl)], None))
    o_ref[...] = jnp.where(lo, outs[0], outs[1]).astype(BF16)


def _lat_mla(qb, kb, vb, kc, vc):
    nq = DEC_SEQ // TQ
    npair = B_HEADS // 2
    return pl.pallas_call(
        _lat_mla_kernel,
        grid=(DEC_BATCH, npair, nq),
        in_specs=[pl.BlockSpec((TQ, 2 * LANES), lambda b, p, i: (b * nq + i, p)),
                  pl.BlockSpec((None, DEC_SEQ, 2 * LANES), lambda b, p, i: (b, 0, p)),
                  pl.BlockSpec((None, DEC_SEQ, LANES), lambda b, p, i: (b, 0, p)),
                  pl.BlockSpec((None, PAST_LEN, 2 * LANES), lambda b, p, i: (b, 0, p)),
                  pl.BlockSpec((None, PAST_LEN, LANES), lambda b, p, i: (b, 0, p))],
        out_specs=pl.BlockSpec((TQ, LANES), lambda b, p, i: (b * nq + i, p)),
        out_shape=jax.ShapeDtypeStruct((T_LAT, 512), BF16),
        compiler_params=_params("arbitrary", "arbitrary", "arbitrary"),
        name="latent_mla",
    )(qb, kb.reshape(DEC_BATCH, DEC_SEQ, B_HEADS * LANES), vb.reshape(DEC_BATCH, DEC_SEQ, B_HEADS * B_V),
      kc.reshape(DEC_BATCH, PAST_LEN, B_HEADS * LANES), vc.reshape(DEC_BATCH, PAST_LEN, B_HEADS * B_V))


_N_CTX_TILES = T_CTX // TM


def _merge_kernel(xp_ref, xs_ref, oac_ref, oal_ref, obc_ref, obl_ref, mod_ref,
                  gpre_ref, gpost_ref, gffn_ref, woa_ref, wob_ref, wout_ref, wg_ref, wr_ref,
                  x1_o, h2_o, lg_o):
    def body(x_ref, oa_ref, ob_ref):
        x = x_ref[...]
        sh1 = mod_ref[:, 0:D_MODEL]
        sc1 = mod_ref[:, D_MODEL:2 * D_MODEL]
        g1 = mod_ref[:, 2 * D_MODEL:3 * D_MODEL]
        sh2 = mod_ref[:, 3 * D_MODEL:4 * D_MODEL]
        sc2 = mod_ref[:, 4 * D_MODEL:5 * D_MODEL]
        h = (_rms(x, gpre_ref[...]) * (1.0 + sc1) + sh1).astype(BF16)
        gates = _dot(h, wg_ref[...])
        ya = _dot(oa_ref[...], woa_ref[...])
        yb = _dot(ob_ref[...], wob_ref[...])
        y = _sigmoid(gates[:, :D_MODEL]) * ya + _sigmoid(gates[:, D_MODEL:]) * yb
        mix = _dot(y.astype(BF16), wout_ref[...])
        x1 = x + g1 * _rms(mix, gpost_ref[...])
        x1_o[...] = x1
        h2 = _rms(x1, gffn_ref[...]) * (1.0 + sc2) + sh2
        h2_o[...] = h2
        lg_o[...] = _dot(h2.astype(BF16), wr_ref[...])

    is_ctx = pl.program_id(0) < _N_CTX_TILES

    @pl.when(is_ctx)
    def _():
        body(xp_ref, oac_ref, obc_ref)

    @pl.when(jnp.logical_not(is_ctx))
    def _():
        body(xs_ref, oal_ref, obl_ref)


def _mod_row_all(i):
    per = DEC_SEQ // TM
    return jnp.where(i < _N_CTX_TILES, 0, 1 + (jnp.maximum(i - _N_CTX_TILES, 0)) // per)


def _merge(xp, xs, oac, oal, obc, obl, mod3, gpre, gpost, gffn, woa, wob, wout, wg, wr):
    ctx = lambda i: (jnp.minimum(i, _N_CTX_TILES - 1), 0)
    lat = lambda i: (jnp.maximum(i - _N_CTX_TILES, 0), 0)
    row = lambda i: (i, 0)
    const = lambda i: (0, 0)
    return pl.pallas_call(
        _merge_kernel,
        grid=(T_ALL // TM,),
        in_specs=[pl.BlockSpec((TM, D_MODEL), ctx), pl.BlockSpec((TM, D_MODEL), lat),
                  pl.BlockSpec((TM, 512), ctx), pl.BlockSpec((TM, 512), lat),
                  pl.BlockSpec((TM, 512), ctx), pl.BlockSpec((TM, 512), lat),
                  pl.BlockSpec((None, 1, 6 * D_MODEL), lambda i: (_mod_row_all(i), 0, 0)),
                  pl.BlockSpec((1, D_MODEL), const), pl.BlockSpec((1, D_MODEL), const),
                  pl.BlockSpec((1, D_MODEL), const),
                  pl.BlockSpec((512, D_MODEL), const), pl.BlockSpec((512, D_MODEL), const),
                  pl.BlockSpec((D_MODEL, D_MODEL), const), pl.BlockSpec((D_MODEL, 2 * D_MODEL), const),
                  pl.BlockSpec((D_MODEL, N_EXPERTS), const)],
        out_specs=[pl.BlockSpec((TM, D_MODEL), row), pl.BlockSpec((TM, D_MODEL), row),
                   pl.BlockSpec((TM, N_EXPERTS), row)],
        out_shape=[jax.ShapeDtypeStruct((T_ALL, D_MODEL), F32), jax.ShapeDtypeStruct((T_ALL, D_MODEL), F32),
                   jax.ShapeDtypeStruct((T_ALL, N_EXPERTS), F32)],
        compiler_params=_params("arbitrary"),
        name="merge",
    )(xp, xs, oac, oal, obc, obl, mod3, gpre, gpost, gffn, woa, wob, wout, wg, wr)


_PER_GROUP = N_EXPERTS // N_GROUPS


def _router_kernel(lg_ref, bias_ref, topi_o, rank_o, topw_o, cnt_o, cnt_s):
    @pl.when(pl.program_id(0) == 0)
    def _():
        cnt_s[...] = jnp.zeros_like(cnt_s)

    scores = _sigmoid(lg_ref[...])
    sel = scores + bias_ref[...]
    lane_i = lax.broadcasted_iota(jnp.int32, (TM, N_EXPERTS), 1)
    lane = lane_i.astype(F32)
    grp = lax.shift_right_logical(lane_i, int(np.log2(_PER_GROUP)))
    ninf = -jnp.inf
    big = float(N_EXPERTS)
    gs = []
    for g in range(N_GROUPS):
        sg = jnp.where(grp == g, sel, ninf)
        m1 = jnp.max(sg, axis=-1, keepdims=True)
        i1 = jnp.min(jnp.where(sg == m1, lane, big), axis=-1, keepdims=True)
        m2 = jnp.max(jnp.where(lane == i1, ninf, sg), axis=-1, keepdims=True)
        gs.append(m1 + m2)
    keep = jnp.zeros((TM, N_EXPERTS), F32)
    for g in range(N_GROUPS):
        beat = jnp.zeros((TM, 1), F32)
        for g2 in range(N_GROUPS):
            if g2 == g:
                continue
            better = (gs[g2] >= gs[g]) if g2 < g else (gs[g2] > gs[g])
            beat = beat + jnp.where(better, 1.0, 0.0)
        keep = jnp.where(grp == g, jnp.where(beat < TOPK_GROUPS, 1.0, 0.0), keep)
    cur = jnp.where(keep > 0.5, sel, NEG)
    lane8 = lax.broadcasted_iota(jnp.int32, (TM, LANES), 1)
    topi = jnp.zeros((TM, LANES), F32)
    topw = jnp.zeros((TM, LANES), F32)
    chosen = jnp.zeros((TM, N_EXPERTS), F32)
    idxs = []
    for k in range(TOP_K):
        m = jnp.max(cur, axis=-1, keepdims=True)
        idx = jnp.min(jnp.where(cur == m, lane, big), axis=-1, keepdims=True)
        hit = lane == idx
        w = jnp.sum(jnp.where(hit, scores, 0.0), axis=-1, keepdims=True)
        cur = jnp.where(hit, ninf, cur)
        chosen = jnp.where(hit, 1.0, chosen)
        topi = jnp.where(lane8 == k, idx, topi)
        topw = jnp.where(lane8 == k, w, topw)
        idxs.append(idx)
    wsum = jnp.sum(topw, axis=-1, keepdims=True)
    topw_o[...] = topw / wsum * ROUTED_SCALE
    topi_o[...] = topi.astype(jnp.int32)
    r_i = lax.broadcasted_iota(jnp.int32, (TM, TM), 0)
    c_i = lax.broadcasted_iota(jnp.int32, (TM, TM), 1)
    tri = jnp.where(c_i < r_i, 1.0, 0.0).astype(BF16)
    prefix = _dot(tri, chosen.astype(BF16)) + cnt_s[0:1, :]
    rank = jnp.zeros((TM, LANES), jnp.int32)
    for k in range(TOP_K):
        r = jnp.sum(jnp.where(lane == idxs[k], prefix, 0.0), axis=-1, keepdims=True)
        rank = jnp.where(lane8 == k, r.astype(jnp.int32), rank)
    rank_o[...] = rank
    total = cnt_s[...] + jnp.sum(chosen, axis=0, keepdims=True)
    cnt_s[...] = total
    cnt_o[...] = total


def _router(logits, bias):
    row = lambda i: (i, 0)
    const = lambda i: (0, 0)
    return pl.pallas_call(
        _router_kernel,
        grid=(T_ALL // TM,),
        in_specs=[pl.BlockSpec((TM, N_EXPERTS), row), pl.BlockSpec((1, N_EXPERTS), const)],
        out_specs=[pl.BlockSpec((TM, LANES), row), pl.BlockSpec((TM, LANES), row),
                   pl.BlockSpec((TM, LANES), row), pl.BlockSpec((8, N_EXPERTS), const)],
        out_shape=[jax.ShapeDtypeStruct((T_ALL, LANES), jnp.int32),
                   jax.ShapeDtypeStruct((T_ALL, LANES), jnp.int32),
                   jax.ShapeDtypeStruct((T_ALL, LANES), F32),
                   jax.ShapeDtypeStruct((8, N_EXPERTS), F32)],
        scratch_shapes=[pltpu.VMEM((8, N_EXPERTS), F32)],
        compiler_params=_params("arbitrary"),
        name="router",
    )(logits, bias)


def _dest_kernel(topi_ref, rank_ref, ps_ref, o_ref):
    lane = lax.broadcasted_iota(jnp.int32, (TM, N_EXPERTS), 1)
    lane8 = lax.broadcasted_iota(jnp.int32, (TM, LANES), 1)
    ps = ps_ref[...]
    topi = topi_ref[...]
    acc = jnp.zeros((TM, LANES), F32)
    for k in range(TOP_K):
        start = jnp.sum(jnp.where(lane == topi[:, k:k + 1], ps, 0.0), axis=-1, keepdims=True)
        acc = jnp.where(lane8 == k, start, acc)
    o_ref[...] = rank_ref[...] + acc.astype(jnp.int32)


def _dest(topi_p, rank_p, pstart):
    row = lambda i: (i, 0)
    return pl.pallas_call(
        _dest_kernel,
        grid=(T_ALL // TM,),
        in_specs=[pl.BlockSpec((TM, LANES), row), pl.BlockSpec((TM, LANES), row),
                  pl.BlockSpec((1, N_EXPERTS), lambda i: (0, 0))],
        out_specs=pl.BlockSpec((TM, LANES), row),
        out_shape=jax.ShapeDtypeStruct((T_ALL, LANES), jnp.int32),
        compiler_params=_params("arbitrary"),
        name="dest",
    )(topi_p, rank_p, pstart)


TD = 512
ROW_UNROLL = 4


def _dispatch_kernel(dest_ref, h_ref, xs_hbm, sem):
    def row(t, carry):
        src = h_ref.at[pl.ds(t, 1)]
        for k in range(TOP_K):
            d = dest_ref[t * TOP_K + k]
            pltpu.make_async_copy(src, xs_hbm.at[pl.ds(d, 1)], sem).start(priority=k % 2)
        return carry

    lax.fori_loop(0, TD, row, 0, unroll=ROW_UNROLL)
    for _ in range(TOP_K):
        pltpu.make_async_copy(h_ref, xs_hbm.at[pl.ds(0, TD)], sem).wait()


def _dispatch(dest_flat, h2):
    return pl.pallas_call(
        _dispatch_kernel,
        grid=(T_ALL // TD,),
        in_specs=[pl.BlockSpec((TD * TOP_K,), lambda i: (i,), memory_space=pltpu.SMEM),
                  pl.BlockSpec((TD, D_MODEL), lambda i: (i, 0))],
        out_specs=pl.BlockSpec(memory_space=pl.ANY),
        out_shape=jax.ShapeDtypeStruct((N_SLOTS, D_MODEL), F32),
        scratch_shapes=[pltpu.SemaphoreType.DMA(())],
        compiler_params=_params("arbitrary"),
        name="dispatch",
    )(dest_flat, h2)


def _expert_kernel(be_ref, nu_ref, x_ref, wg_ref, wu_ref, wd_ref, o_ref, wgu_s, wd_s):
    i = pl.program_id(0)
    active = i < nu_ref[0]
    prev = be_ref[jnp.maximum(i - 1, 0)]
    fresh = (i == 0) | (be_ref[i] != prev)

    @pl.when(active & fresh)
    def _():
        wgu_s[:, 0:D_EXPERT] = wg_ref[...].astype(BF16)
        wgu_s[:, D_EXPERT:2 * D_EXPERT] = wu_ref[...].astype(BF16)
        wd_s[...] = wd_ref[...].astype(BF16)

    @pl.when(active)
    def _():
        gu = _dot(x_ref[...].astype(BF16), wgu_s[...])
        a = _silu(gu[:, :D_EXPERT]) * gu[:, D_EXPERT:]
        o_ref[...] = _dot(a.astype(BF16), wd_s[...])


def _experts(blk_expert, n_used, x_sorted, w_gate, w_up, w_down):
    def rows(i, be, nu):
        return (jnp.minimum(i, nu[0] - 1), 0)

    def wsel(i, be, nu):
        return (be[jnp.minimum(i, nu[0] - 1)], 0, 0)

    grid_spec = pltpu.PrefetchScalarGridSpec(
        num_scalar_prefetch=2,
        grid=(N_SLOT_BLK,),
        in_specs=[pl.BlockSpec((MOE_BLK, D_MODEL), rows),
                  pl.BlockSpec((None, D_MODEL, D_EXPERT), wsel),
                  pl.BlockSpec((None, D_MODEL, D_EXPERT), wsel),
                  pl.BlockSpec((None, D_EXPERT, D_MODEL), wsel)],
        out_specs=pl.BlockSpec((MOE_BLK, D_MODEL), rows),
        scratch_shapes=[pltpu.VMEM((D_MODEL, 2 * D_EXPERT), BF16), pltpu.VMEM((D_EXPERT, D_MODEL), BF16)],
    )
    return pl.pallas_call(
        _expert_kernel,
        grid_spec=grid_spec,
        out_shape=jax.ShapeDtypeStruct((N_SLOTS, D_MODEL), F32),
        compiler_params=_params("arbitrary"),
        name="experts",
    )(blk_expert, n_used, x_sorted, w_gate, w_up, w_down)


TC = 256


def _combine_kernel(dcur_ref, dnext_ref, eo_hbm, x1_ref, h2_ref, w_ref, mod_ref, g_ref, wgu_ref, wd_ref,
                    o_ref, buf, sem):
    i = pl.program_id(0)
    n = pl.num_programs(0)
    slot = lax.rem(i, 2)

    def gather(d_ref, s):
        def row(t, carry):
            for k in range(TOP_K):
                d = d_ref[t * TOP_K + k]
                pltpu.make_async_copy(eo_hbm.at[pl.ds(d, 1)], buf.at[s, k, pl.ds(t, 1)],
                                      sem.at[s]).start(priority=k % 2)
            return carry
        lax.fori_loop(0, TC, row, 0, unroll=ROW_UNROLL)

    @pl.when(i == 0)
    def _():
        gather(dcur_ref, 0)

    @pl.when(i + 1 < n)
    def _():
        gather(dnext_ref, 1 - slot)

    gate = mod_ref[:, 5 * D_MODEL:6 * D_MODEL]
    gu = _dot(h2_ref[...].astype(BF16), wgu_ref[...])
    a = _silu(gu[:, :D_SHARED]) * gu[:, D_SHARED:]
    moe = _dot(a.astype(BF16), wd_ref[...])
    for k in range(TOP_K):
        pltpu.make_async_copy(eo_hbm.at[pl.ds(0, TC)], buf.at[slot, k], sem.at[slot]).wait()
    w = w_ref[...]
    for k in range(TOP_K):
        moe = moe + buf[slot, k] * w[:, k:k + 1]
    o_ref[...] = x1_ref[...] + gate * _rms(moe, g_ref[...])


def _combine(dest_flat, eo, x1, h2, topw, mod3, mod_row, g, wgu, wd, tile0, n_tiles):
    row = lambda i: (tile0 + i, 0)
    const = lambda i: (0, 0)
    blk = TC * TOP_K
    return pl.pallas_call(
        _combine_kernel,
        grid=(n_tiles,),
        in_specs=[pl.BlockSpec((blk,), lambda i: (tile0 + i,), memory_space=pltpu.SMEM),
                  pl.BlockSpec((blk,), lambda i: (tile0 + jnp.minimum(i + 1, n_tiles - 1),),
                               memory_space=pltpu.SMEM),
                  pl.BlockSpec(memory_space=pl.ANY),
                  pl.BlockSpec((TC, D_MODEL), row), pl.BlockSpec((TC, D_MODEL), row),
                  pl.BlockSpec((TC, LANES), row),
                  pl.BlockSpec((None, 1, 6 * D_MODEL), lambda i: (mod_row(i), 0, 0)),
                  pl.BlockSpec((1, D_MODEL), const),
                  pl.BlockSpec((D_MODEL, 2 * D_SHARED), const), pl.BlockSpec((D_SHARED, D_MODEL), const)],
        out_specs=pl.BlockSpec((TC, D_MODEL), lambda i: (i, 0)),
        out_shape=jax.ShapeDtypeStruct((n_tiles * TC, D_MODEL), F32),
        scratch_shapes=[pltpu.VMEM((2, TOP_K, TC, D_MODEL), F32), pltpu.SemaphoreType.DMA((2,))],
        compiler_params=_params("arbitrary"),
        name="combine",
    )(dest_flat, dest_flat, eo, x1, h2, topw, mod3, g, wgu, wd)


def _qa_perm():
    cols = []
    for j in range(A_GROUP):
        for g in range(A_KV_HEADS):
            hd = g * A_GROUP + j
            cols.extend(range(hd * A_HEAD_DIM, (hd + 1) * A_HEAD_DIM))
    return np.asarray(cols, np.int32)


def _pad_heads(w, width, n_heads, offset, total=LANES):
    k = w.shape[0]
    stride = w.shape[1] // n_heads
    w3 = w.reshape(k, n_heads, stride)[:, :, offset:offset + width]
    w3 = jnp.pad(w3, ((0, 0), (0, 0), (0, total - width)))
    return w3.reshape(k, n_heads * total)


def kernel(x_prompt, x_sample, cache_k_swa, cache_v_swa, cache_ckv_mla, cache_krope_mla, c, c_ctx, w_ada, b_ada, norm_pre_mix, norm_post_mix, norm_pre_ffn, norm_post_ffn, w_in, sink_swa, q_norm_mla, kv_norm_mla, w_uq_mla, w_ukv_mla, w_o_swa, w_o_mla, w_out, router_w, router_bias, w_gate_exp, w_up_exp, w_down_exp, w_gate_sh, w_up_sh, w_down_sh):
    l = 0
    perm = _qa_perm()
    wi = w_in[l]
    kr_blk = jnp.pad(wi[:, _O_KR:_O_GATE], ((0, 0), (B_NOPE, LANES - B_NOPE - B_ROPE)))
    w1 = jnp.concatenate([wi[:, _O_QA:_O_KA][:, perm], wi[:, _O_KA:_O_KR], kr_blk], axis=1).astype(BF16)
    wg = wi[:, _O_GATE:].astype(BF16)
    wuq = _pad_heads(w_uq_mla[l], B_NOPE + B_ROPE, B_HEADS, 0).astype(BF16)
    wk = _pad_heads(w_ukv_mla[l], B_NOPE, B_HEADS, 0).astype(BF16)
    wv = w_ukv_mla[l].reshape(KV_LORA, B_HEADS, B_NOPE + B_V)[:, :, B_NOPE:].reshape(KV_LORA, B_HEADS * B_V).astype(BF16)
    woa = w_o_swa[l][perm, :].astype(BF16)
    wob = w_o_mla[l].astype(BF16)
    wout = w_out[l].astype(BF16)
    wr = router_w[l].astype(BF16)
    wgu_sh = jnp.concatenate([w_gate_sh[l], w_up_sh[l]], axis=1).astype(BF16)
    wd_sh = w_down_sh[l].astype(BF16)
    sink = sink_swa[l]
    row2 = lambda v: v.reshape(1, -1)

    cond8 = jnp.concatenate([c_ctx[None, :], c, jnp.zeros((8 - 1 - DEC_BATCH, D_MODEL), F32)], axis=0)
    mod3 = _modulation(cond8, w_ada[l], b_ada[l]).reshape(8, 1, 6 * D_MODEL)

    xp = x_prompt.reshape(T_CTX, D_MODEL)
    xs = x_sample.reshape(T_LAT, D_MODEL)
    per = DEC_SEQ // TM
    shared_in = (row2(norm_pre_mix[l]), w1, row2(q_norm_mla[l]), row2(kv_norm_mla[l]), wuq, wk, wv)
    qa_c, ka_c, va_c, ckv_c, kr_c, qb_c, kb_c, vb_c = _inproj(xp, mod3, lambda i: 0, *shared_in, None)
    qa_l, ka_l, va_l, qb_l, kb_l, vb_l = _inproj(xs, mod3, lambda i: 1 + i // per, *shared_in, _rope_tables())

    oa_c, ob_c = _ctx_attention(sink, qa_c, ka_c, va_c, qb_c, kb_c, vb_c)

    krp_cache = jnp.pad(cache_krope_mla[:, l].reshape(DEC_BATCH * PAST_LEN, B_ROPE),
                        ((0, 0), (B_NOPE, LANES - B_NOPE - B_ROPE)))
    kc_b, vc_b = _kv_expand(cache_ckv_mla[:, l].reshape(DEC_BATCH * PAST_LEN, KV_LORA), krp_cache, wk, wv)
    oa_l = _lat_swa(sink, qa_l, ka_l, va_l,
                    cache_k_swa[:, l].reshape(DEC_BATCH, PAST_LEN, LANES),
                    cache_v_swa[:, l].reshape(DEC_BATCH, PAST_LEN, LANES))
    ob_l = _lat_mla(qb_l, kb_l, vb_l, kc_b, vc_b)

    x1, h2, logits = _merge(xp, xs, oa_c, oa_l, ob_c, ob_l, mod3,
                            row2(norm_pre_mix[l]), row2(norm_post_mix[l]), row2(norm_pre_ffn[l]),
                            woa, wob, wout, wg, wr)

    topi_p, rank_p, topw_p, counts8 = _router(logits, row2(router_bias[l]))
    counts = counts8[0].astype(jnp.int32)

    padded = (counts + MOE_BLK - 1) // MOE_BLK * MOE_BLK
    pend = jnp.cumsum(padded)
    pstart = pend - padded
    dest = _dest(topi_p, rank_p, pstart.astype(F32).reshape(1, N_EXPERTS))[:, :TOP_K].reshape(-1)
    blk_start = jnp.arange(N_SLOT_BLK, dtype=jnp.int32) * MOE_BLK
    blk_expert = jnp.minimum(jnp.sum(blk_start[:, None] >= pend[None, :], axis=1), N_EXPERTS - 1).astype(jnp.int32)
    n_used = (pend[-1] // MOE_BLK).astype(jnp.int32).reshape(1)

    x_sorted = _dispatch(dest, h2)
    eo = _experts(blk_expert, n_used, x_sorted, w_gate_exp[l], w_up_exp[l], w_down_exp[l])

    g_post = row2(norm_post_ffn[l])
    per_c = DEC_SEQ // TC
    y_p = _combine(dest, eo, x1, h2, topw_p, mod3, lambda i: 0, g_post, wgu_sh, wd_sh, 0, T_CTX // TC)
    y_s = _combine(dest, eo, x1, h2, topw_p, mod3, lambda i: 1 + i // per_c, g_post, wgu_sh, wd_sh,
                   T_CTX // TC, T_LAT // TC)

    return (y_p.reshape(BATCH, SEQ, D_MODEL), y_s.reshape(DEC_BATCH, DEC_SEQ, D_MODEL),
            ka_c.reshape(BATCH, 1, SEQ, A_KV_HEADS, A_HEAD_DIM),
            va_c.reshape(BATCH, 1, SEQ, A_KV_HEADS, A_HEAD_DIM),
            ckv_c.reshape(BATCH, 1, SEQ, KV_LORA),
            kr_c[:, B_NOPE:B_NOPE + B_ROPE].reshape(BATCH, 1, SEQ, B_ROPE))
```

```python
import functools

import jax
import jax.numpy as jnp
import numpy as np
from jax import lax
from jax.experimental import pallas as pl
from jax.experimental.pallas import tpu as pltpu

D_MODEL = 1024
BATCH = 32
SEQ = 256
DEC_BATCH = 4
DEC_SEQ = 2048
PAST_LEN = 512
GRID_W = 64
WINDOW = 128
ROPE_BASE = 10000.0
EPS = 1e-6
NEG = -1e30
A_HEADS = 8
A_KV_HEADS = 2
A_GROUP = A_HEADS // A_KV_HEADS
A_HEAD_DIM = 64
B_HEADS = 8
B_NOPE = 64
B_ROPE = 32
B_V = 64
KV_LORA = 256
Q_LORA = 384
MLA_SCALE = (B_NOPE + B_ROPE) ** -0.5
N_EXPERTS = 256
TOP_K = 8
N_GROUPS = 8
TOPK_GROUPS = 4
D_EXPERT = 256
D_SHARED = 256
ROUTED_SCALE = 2.5

T_CTX = BATCH * SEQ
T_LAT = DEC_BATCH * DEC_SEQ
T_ALL = T_CTX + T_LAT

LANES = 128
TM = 512
TQ = 256
MOE_BLK = 256
N_SLOT_BLK = (T_ALL * TOP_K) // MOE_BLK + N_EXPERTS
N_SLOTS = N_SLOT_BLK * MOE_BLK
VMEM_LIMIT = 48 * 1024 * 1024

F32 = jnp.float32
BF16 = jnp.bfloat16

_O_QA = 0
_O_KA = _O_QA + A_HEADS * A_HEAD_DIM
_O_VA = _O_KA + A_KV_HEADS * A_HEAD_DIM
_O_QL = _O_VA + A_KV_HEADS * A_HEAD_DIM
_O_CKV = _O_QL + Q_LORA
_O_KR = _O_CKV + KV_LORA
_O_GATE = _O_KR + B_ROPE


def _params(*sem):
    return pltpu.CompilerParams(dimension_semantics=sem, vmem_limit_bytes=VMEM_LIMIT)


def _dot(a, b):
    return jnp.dot(a, b, preferred_element_type=F32)


def _dot_nt(a, b):
    return lax.dot_general(a, b, (((1,), (1,)), ((), ())), preferred_element_type=F32)


def _rms(x, g):
    return x * lax.rsqrt(jnp.mean(x * x, axis=-1, keepdims=True) + EPS) * g


def _sigmoid(x):
    return 1.0 / (1.0 + jnp.exp(-x))


def _silu(x):
    return x * _sigmoid(x)


def _mod_kernel(cond_ref, w_ref, b_ref, o_ref):
    s = _silu(cond_ref[...]).astype(BF16)
    o_ref[...] = _dot(s, w_ref[...].astype(BF16)) + b_ref[...]


def _modulation(cond8, w_ada, b_ada):
    tn = 768
    n = w_ada.shape[1]
    return pl.pallas_call(
        _mod_kernel,
        grid=(n // tn,),
        in_specs=[pl.BlockSpec((8, D_MODEL), lambda j: (0, 0)),
                  pl.BlockSpec((D_MODEL, tn), lambda j: (0, j)),
                  pl.BlockSpec((1, tn), lambda j: (0, j))],
        out_specs=pl.BlockSpec((8, tn), lambda j: (0, j)),
        out_shape=jax.ShapeDtypeStruct((8, n), F32),
        compiler_params=_params("arbitrary"),
        name="modulation",
    )(cond8, w_ada, b_ada.reshape(1, n))


def _rope(x, cos, sin, half):
    lane = lax.broadcasted_iota(jnp.int32, x.shape, 1)
    first = (lane & (2 * half - 1)) < half
    partner = jnp.where(first, pltpu.roll(x, LANES - half, 1), pltpu.roll(x, half, 1))
    return x * cos + partner * sin


def _rope_tables():
    t = jnp.arange(DEC_SEQ, dtype=jnp.int32)
    row = (t // GRID_W).astype(F32)
    col = (t % GRID_W).astype(F32)

    def tables(d_rot):
        nf = d_rot // 4
        inv = ROPE_BASE ** (-jnp.arange(nf, dtype=F32) / nf)
        ar = row[:, None] * inv
        ac = col[:, None] * inv
        cos = jnp.concatenate([jnp.cos(ar), jnp.cos(ar), jnp.cos(ac), jnp.cos(ac)], axis=1)
        sin = jnp.concatenate([-jnp.sin(ar), jnp.sin(ar), -jnp.sin(ac), jnp.sin(ac)], axis=1)
        return cos, sin

    cos_a, sin_a = tables(A_HEAD_DIM)
    cos_a = jnp.concatenate([cos_a, cos_a], axis=1)
    sin_a = jnp.concatenate([sin_a, sin_a], axis=1)
    cos_b, sin_b = tables(B_ROPE)
    one = jnp.ones((DEC_SEQ, B_NOPE), F32)
    zero = jnp.zeros((DEC_SEQ, B_NOPE), F32)
    pad = LANES - B_NOPE - B_ROPE
    cos_b = jnp.concatenate([one, cos_b, one[:, :pad]], axis=1)
    sin_b = jnp.concatenate([zero, sin_b, zero[:, :pad]], axis=1)
    return cos_a, sin_a, cos_b, sin_b


_N1 = 1536


def _inproj_kernel(latent, *refs):
    if latent:
        (x_ref, mod_ref, g_ref, w1_ref, qn_ref, kvn_ref, wuq_ref, wk_ref, wv_ref,
         ca_ref, sa_ref, cb_ref, sb_ref,
         qa_o, ka_o, va_o, qb_o, kb_o, vb_o) = refs
    else:
        (x_ref, mod_ref, g_ref, w1_ref, qn_ref, kvn_ref, wuq_ref, wk_ref, wv_ref,
         qa_o, ka_o, va_o, ckv_o, kr_o, qb_o, kb_o, vb_o) = refs
    x = x_ref[...]
    shift = mod_ref[:, 0:D_MODEL]
    scale = mod_ref[:, D_MODEL:2 * D_MODEL]
    h = _rms(x, g_ref[...]) * (1.0 + scale) + shift
    p = _dot(h.astype(BF16), w1_ref[...])
    qa = p[:, 0:512]
    ka = p[:, 512:640]
    va = p[:, 640:768]
    ql = p[:, 768:1152]
    ckv = p[:, 1152:1408]
    krp = p[:, 1408:1536]
    qn = _rms(ql, qn_ref[...])
    qb = _dot(qn.astype(BF16), wuq_ref[...])
    cn = _rms(ckv, kvn_ref[...])
    cnb = cn.astype(BF16)
    kn = _dot(cnb, wk_ref[...])
    vb = _dot(cnb, wv_ref[...])
    if latent:
        ca, sa, cb, sb = ca_ref[...], sa_ref[...], cb_ref[...], sb_ref[...]
        ka = _rope(ka, ca, sa, A_HEAD_DIM // 4)
        krp = _rope(krp, cb, sb, B_ROPE // 4)
        for j in range(4):
            blk = _rope(qa[:, j * LANES:(j + 1) * LANES], ca, sa, A_HEAD_DIM // 4)
            qa_o[:, j * LANES:(j + 1) * LANES] = (blk * (A_HEAD_DIM ** -0.5)).astype(BF16)
        for hd in range(B_HEADS):
            blk = _rope(qb[:, hd * LANES:(hd + 1) * LANES], cb, sb, B_ROPE // 4)
            qb_o[:, hd * LANES:(hd + 1) * LANES] = (blk * MLA_SCALE).astype(BF16)
        ka_o[...] = ka.astype(BF16)
        va_o[...] = va.astype(BF16)
    else:
        qa_o[...] = (qa * (A_HEAD_DIM ** -0.5)).astype(BF16)
        qb_o[...] = (qb * MLA_SCALE).astype(BF16)
        ka_o[...] = ka
        va_o[...] = va
        ckv_o[...] = cn
        kr_o[...] = krp
    for hd in range(B_HEADS):
        kb_o[:, hd * LANES:(hd + 1) * LANES] = (kn[:, hd * LANES:(hd + 1) * LANES] + krp).astype(BF16)
    vb_o[...] = vb.astype(BF16)


def _inproj(x, mod3, mod_row, g, w1, qn, kvn, wuq, wk, wv, tables):
    latent = tables is not None
    t = x.shape[0]
    n_tiles = t // TM
    row = lambda i: (i, 0)
    const = lambda i: (0, 0)
    in_specs = [pl.BlockSpec((TM, D_MODEL), row),
                pl.BlockSpec((None, 1, 6 * D_MODEL), lambda i: (mod_row(i), 0, 0)),
                pl.BlockSpec((1, D_MODEL), const),
                pl.BlockSpec((D_MODEL, _N1), const),
                pl.BlockSpec((1, Q_LORA), const),
                pl.BlockSpec((1, KV_LORA), const),
                pl.BlockSpec((Q_LORA, B_HEADS * LANES), const),
                pl.BlockSpec((KV_LORA, B_HEADS * LANES), const),
                pl.BlockSpec((KV_LORA, B_HEADS * B_V), const)]
    args = [x, mod3, g, w1, qn, kvn, wuq, wk, wv]
    sds = jax.ShapeDtypeStruct
    if latent:
        per = DEC_SEQ // TM
        tab = lambda i: (i % per, 0)
        in_specs += [pl.BlockSpec((TM, LANES), tab)] * 4
        args += list(tables)
        out_shape = [sds((t, 512), BF16), sds((t, LANES), BF16), sds((t, LANES), BF16),
                     sds((t, B_HEADS * LANES), BF16), sds((t, B_HEADS * LANES), BF16),
                     sds((t, B_HEADS * B_V), BF16)]
        widths = [512, LANES, LANES, B_HEADS * LANES, B_HEADS * LANES, B_HEADS * B_V]
    else:
        out_shape = [sds((t, 512), BF16), sds((t, LANES), F32), sds((t, LANES), F32),
                     sds((t, KV_LORA), F32), sds((t, LANES), F32),
                     sds((t, B_HEADS * LANES), BF16), sds((t, B_HEADS * LANES), BF16),
                     sds((t, B_HEADS * B_V), BF16)]
        widths = [512, LANES, LANES, KV_LORA, LANES, B_HEADS * LANES, B_HEADS * LANES, B_HEADS * B_V]
    out_specs = [pl.BlockSpec((TM, w), row) for w in widths]
    return pl.pallas_call(
        functools.partial(_inproj_kernel, latent),
        grid=(n_tiles,),
        in_specs=in_specs,
        out_specs=out_specs,
        out_shape=out_shape,
        compiler_params=_params("arbitrary"),
        name="inproj_latent" if latent else "inproj_ctx",
    )(*args)


def _kvexp_kernel(ckv_ref, krp_ref, wk_ref, wv_ref, kb_o, vb_o):
    cb = ckv_ref[...].astype(BF16)
    kn = _dot(cb, wk_ref[...])
    krp = krp_ref[...]
    for hd in range(B_HEADS):
        kb_o[:, hd * LANES:(hd + 1) * LANES] = (kn[:, hd * LANES:(hd + 1) * LANES] + krp).astype(BF16)
    vb_o[...] = _dot(cb, wv_ref[...]).astype(BF16)


def _kv_expand(ckv, krp, wk, wv):
    t = ckv.shape[0]
    row = lambda i: (i, 0)
    const = lambda i: (0, 0)
    return pl.pallas_call(
        _kvexp_kernel,
        grid=(t // TM,),
        in_specs=[pl.BlockSpec((TM, KV_LORA), row), pl.BlockSpec((TM, LANES), row),
                  pl.BlockSpec((KV_LORA, B_HEADS * LANES), const),
                  pl.BlockSpec((KV_LORA, B_HEADS * B_V), const)],
        out_specs=[pl.BlockSpec((TM, B_HEADS * LANES), row), pl.BlockSpec((TM, B_HEADS * B_V), row)],
        out_shape=[jax.ShapeDtypeStruct((t, B_HEADS * LANES), BF16),
                   jax.ShapeDtypeStruct((t, B_HEADS * B_V), BF16)],
        compiler_params=_params("arbitrary"),
        name="kv_expand",
    )(ckv, krp, wk, wv)


def _lo_mask(rows):
    return lax.broadcasted_iota(jnp.int32, (rows, LANES), 1) < (LANES // 2)


def _softmax_pv(parts, sink):
    m = None
    for s, _ in parts:
        mi = jnp.max(s, axis=-1, keepdims=True)
        m = mi if m is None else jnp.maximum(m, mi)
    if sink is not None:
        m = jnp.maximum(m, sink)
    den = None
    acc = None
    for s, v in parts:
        p = jnp.exp(s - m)
        di = jnp.sum(p, axis=-1, keepdims=True)
        den = di if den is None else den + di
        oi = _dot(p.astype(BF16), v)
        acc = oi if acc is None else acc + oi
    if sink is not None:
        den = den + jnp.exp(sink - m)
    return acc / den


def _ctx_attn_kernel(sink_ref, qa_ref, ka_ref, va_ref, qb_ref, kb_ref, vb_ref, oa_ref, ob_ref):
    lo = _lo_mask(SEQ)
    k = ka_ref[...].astype(BF16)
    v = va_ref[...].astype(BF16)
    for j in range(A_GROUP):
        q = qa_ref[:, j * LANES:(j + 1) * LANES]
        zero = jnp.zeros_like(q)
        o0 = _softmax_pv([(_dot_nt(jnp.where(lo, q, zero), k), v)], sink_ref[j])
        o1 = _softmax_pv([(_dot_nt(jnp.where(lo, zero, q), k), v)], sink_ref[A_GROUP + j])
        oa_ref[:, j * LANES:(j + 1) * LANES] = jnp.where(lo, o0, o1).astype(BF16)
    for j in range(B_HEADS // 2):
        vp = vb_ref[:, j * LANES:(j + 1) * LANES]
        outs = []
        for hd in (2 * j, 2 * j + 1):
            s = _dot_nt(qb_ref[:, hd * LANES:(hd + 1) * LANES], kb_ref[:, hd * LANES:(hd + 1) * LANES])
            outs.append(_softmax_pv([(s, vp)], None))
        ob_ref[:, j * LANES:(j + 1) * LANES] = jnp.where(lo, outs[0], outs[1]).astype(BF16)


def _ctx_attention(sink, qa, ka, va, qb, kb, vb):
    blk = lambda w: pl.BlockSpec((SEQ, w), lambda b: (b, 0))
    return pl.pallas_call(
        _ctx_attn_kernel,
        grid=(BATCH,),
        in_specs=[pl.BlockSpec(memory_space=pltpu.SMEM),
                  blk(512), blk(LANES), blk(LANES), blk(B_HEADS * LANES), blk(B_HEADS * LANES),
                  blk(B_HEADS * B_V)],
        out_specs=[blk(512), blk(512)],
        out_shape=[jax.ShapeDtypeStruct((T_CTX, 512), BF16), jax.ShapeDtypeStruct((T_CTX, 512), BF16)],
        compiler_params=_params("arbitrary"),
        name="ctx_attention",
    )(sink, qa, ka, va, qb, kb, vb)


_WIN = TQ + 2 * WINDOW


def _lat_swa_kernel(sink_ref, q_ref, k_ref, v_ref, kc_ref, vc_ref, o_ref):
    qi = pl.program_id(1)
    q0 = qi * TQ
    start = pl.multiple_of(jnp.clip(q0 - WINDOW, 0, DEC_SEQ - _WIN), WINDOW)
    kw = k_ref[pl.ds(start, _WIN), :]
    vw = v_ref[pl.ds(start, _WIN), :]
    kc = kc_ref[...].astype(BF16)
    vc = vc_ref[...].astype(BF16)
    qpos = q0 + lax.broadcasted_iota(jnp.int32, (TQ, _WIN), 0)
    kpos = start + lax.broadcasted_iota(jnp.int32, (TQ, _WIN), 1)
    valid = jnp.abs(qpos - kpos) <= WINDOW
    lo = _lo_mask(TQ)
    for j in range(A_GROUP):
        q = q_ref[:, j * LANES:(j + 1) * LANES]
        zero = jnp.zeros_like(q)
        outs = []
        for g, qh in ((0, jnp.where(lo, q, zero)), (1, jnp.where(lo, zero, q))):
            sw = jnp.where(valid, _dot_nt(qh, kw), NEG)
            sc = _dot_nt(qh, kc)
            outs.append(_softmax_pv([(sw, vw), (sc, vc)], sink_ref[g * A_GROUP + j]))
        o_ref[:, j * LANES:(j + 1) * LANES] = jnp.where(lo, outs[0], outs[1]).astype(BF16)


def _lat_swa(sink, qa, ka, va, kc, vc):
    nq = DEC_SEQ // TQ
    return pl.pallas_call(
        _lat_swa_kernel,
        grid=(DEC_BATCH, nq),
        in_specs=[pl.BlockSpec(memory_space=pltpu.SMEM),
                  pl.BlockSpec((TQ, 512), lambda b, i: (b * nq + i, 0)),
                  pl.BlockSpec((None, DEC_SEQ, LANES), lambda b, i: (b, 0, 0)),
                  pl.BlockSpec((None, DEC_SEQ, LANES), lambda b, i: (b, 0, 0)),
                  pl.BlockSpec((None, PAST_LEN, LANES), lambda b, i: (b, 0, 0)),
                  pl.BlockSpec((None, PAST_LEN, LANES), lambda b, i: (b, 0, 0))],
        out_specs=pl.BlockSpec((TQ, 512), lambda b, i: (b * nq + i, 0)),
        out_shape=jax.ShapeDtypeStruct((T_LAT, 512), BF16),
        compiler_params=_params("arbitrary", "arbitrary"),
        name="latent_swa",
    )(sink, qa, ka.reshape(DEC_BATCH, DEC_SEQ, LANES), va.reshape(DEC_BATCH, DEC_SEQ, LANES), kc, vc)


def _lat_mla_kernel(q_ref, kl_ref, vl_ref, kc_ref, vc_ref, o_ref):
    lo = _lo_mask(TQ)
    vl = vl_ref[...]
    vc = vc_ref[...]
    outs = []
    for hh in range(2):
        q = q_ref[:, hh * LANES:(hh + 1) * LANES]
        sc = _dot_nt(q, kc_ref[:, hh * LANES:(hh + 1) * LANES])
        sl = _dot_nt(q, kl_ref[:, hh * LANES:(hh + 1) * LANES])
        outs.append(_softmax_pv([(sc, vc), (sl, vl)], None))
    o_ref[...] = jnp.where(lo, outs[0], outs[1]).astype(BF16)


def _lat_mla(qb, kb, vb, kc, vc):
    nq = DEC_SEQ // TQ
    npair = B_HEADS // 2
    return pl.pallas_call(
        _lat_mla_kernel,
        grid=(DEC_BATCH, npair, nq),
        in_specs=[pl.BlockSpec((TQ, 2 * LANES), lambda b, p, i: (b * nq + i, p)),
                  pl.BlockSpec((None, DEC_SEQ, 2 * LANES), lambda b, p, i: (b, 0, p)),
                  pl.BlockSpec((None, DEC_SEQ, LANES), lambda b, p, i: (b, 0, p)),
                  pl.BlockSpec((None, PAST_LEN, 2 * LANES), lambda b, p, i: (b, 0, p)),
                  pl.BlockSpec((None, PAST_LEN, LANES), lambda b, p, i: (b, 0, p))],
        out_specs=pl.BlockSpec((TQ, LANES), lambda b, p, i: (b * nq + i, p)),
        out_shape=jax.ShapeDtypeStruct((T_LAT, 512), BF16),
        compiler_params=_params("arbitrary", "arbitrary", "arbitrary"),
        name="latent_mla",
    )(qb, kb.reshape(DEC_BATCH, DEC_SEQ, B_HEADS * LANES), vb.reshape(DEC_BATCH, DEC_SEQ, B_HEADS * B_V),
      kc.reshape(DEC_BATCH, PAST_LEN, B_HEADS * LANES), vc.reshape(DEC_BATCH, PAST_LEN, B_HEADS * B_V))


_N_CTX_TILES = T_CTX // TM


def _merge_kernel(xp_ref, xs_ref, oac_ref, oal_ref, obc_ref, obl_ref, mod_ref,
                  gpre_ref, gpost_ref, gffn_ref, woa_ref, wob_ref, wout_ref, wg_ref, wr_ref,
                  x1_o, h2_o, lg_o):
    def body(x_ref, oa_ref, ob_ref):
        x = x_ref[...]
        sh1 = mod_ref[:, 0:D_MODEL]
        sc1 = mod_ref[:, D_MODEL:2 * D_MODEL]
        g1 = mod_ref[:, 2 * D_MODEL:3 * D_MODEL]
        sh2 = mod_ref[:, 3 * D_MODEL:4 * D_MODEL]
        sc2 = mod_ref[:, 4 * D_MODEL:5 * D_MODEL]
        h = (_rms(x, gpre_ref[...]) * (1.0 + sc1) + sh1).astype(BF16)
        gates = _dot(h, wg_ref[...])
        ya = _dot(oa_ref[...], woa_ref[...])
        yb = _dot(ob_ref[...], wob_ref[...])
        y = _sigmoid(gates[:, :D_MODEL]) * ya + _sigmoid(gates[:, D_MODEL:]) * yb
        mix = _dot(y.astype(BF16), wout_ref[...])
        x1 = x + g1 * _rms(mix, gpost_ref[...])
        x1_o[...] = x1
        h2 = _rms(x1, gffn_ref[...]) * (1.0 + sc2) + sh2
        h2_o[...] = h2
        lg_o[...] = _dot(h2.astype(BF16), wr_ref[...])

    is_ctx = pl.program_id(0) < _N_CTX_TILES

    @pl.when(is_ctx)
    def _():
        body(xp_ref, oac_ref, obc_ref)

    @pl.when(jnp.logical_not(is_ctx))
    def _():
        body(xs_ref, oal_ref, obl_ref)


def _mod_row_all(i):
    per = DEC_SEQ // TM
    return jnp.where(i < _N_CTX_TILES, 0, 1 + (jnp.maximum(i - _N_CTX_TILES, 0)) // per)


def _merge(xp, xs, oac, oal, obc, obl, mod3, gpre, gpost, gffn, woa, wob, wout, wg, wr):
    ctx = lambda i: (jnp.minimum(i, _N_CTX_TILES - 1), 0)
    lat = lambda i: (jnp.maximum(i - _N_CTX_TILES, 0), 0)
    row = lambda i: (i, 0)
    const = lambda i: (0, 0)
    return pl.pallas_call(
        _merge_kernel,
        grid=(T_ALL // TM,),
        in_specs=[pl.BlockSpec((TM, D_MODEL), ctx), pl.BlockSpec((TM, D_MODEL), lat),
                  pl.BlockSpec((TM, 512), ctx), pl.BlockSpec((TM, 512), lat),
                  pl.BlockSpec((TM, 512), ctx), pl.BlockSpec((TM, 512), lat),
                  pl.BlockSpec((None, 1, 6 * D_MODEL), lambda i: (_mod_row_all(i), 0, 0)),
                  pl.BlockSpec((1, D_MODEL), const), pl.BlockSpec((1, D_MODEL), const),
                  pl.BlockSpec((1, D_MODEL), const),
                  pl.BlockSpec((512, D_MODEL), const), pl.BlockSpec((512, D_MODEL), const),
                  pl.BlockSpec((D_MODEL, D_MODEL), const), pl.BlockSpec((D_MODEL, 2 * D_MODEL), const),
                  pl.BlockSpec((D_MODEL, N_EXPERTS), const)],
        out_specs=[pl.BlockSpec((TM, D_MODEL), row), pl.BlockSpec((TM, D_MODEL), row),
                   pl.BlockSpec((TM, N_EXPERTS), row)],
        out_shape=[jax.ShapeDtypeStruct((T_ALL, D_MODEL), F32), jax.ShapeDtypeStruct((T_ALL, D_MODEL), F32),
                   jax.ShapeDtypeStruct((T_ALL, N_EXPERTS), F32)],
        compiler_params=_params("arbitrary"),
        name="merge",
    )(xp, xs, oac, oal, obc, obl, mod3, gpre, gpost, gffn, woa, wob, wout, wg, wr)


_PER_GROUP = N_EXPERTS // N_GROUPS


def _router_kernel(lg_ref, bias_ref, topi_o, rank_o, topw_o, cnt_o, cnt_s):
    @pl.when(pl.program_id(0) == 0)
    def _():
        cnt_s[...] = jnp.zeros_like(cnt_s)

    scores = _sigmoid(lg_ref[...])
    sel = scores + bias_ref[...]
    lane_i = lax.broadcasted_iota(jnp.int32, (TM, N_EXPERTS), 1)
    lane = lane_i.astype(F32)
    grp = lax.shift_right_logical(lane_i, int(np.log2(_PER_GROUP)))
    ninf = -jnp.inf
    big = float(N_EXPERTS)
    gs = []
    for g in range(N_GROUPS):
        sg = jnp.where(grp == g, sel, ninf)
        m1 = jnp.max(sg, axis=-1, keepdims=True)
        i1 = jnp.min(jnp.where(sg == m1, lane, big), axis=-1, keepdims=True)
        m2 = jnp.max(jnp.where(lane == i1, ninf, sg), axis=-1, keepdims=True)
        gs.append(m1 + m2)
    keep = jnp.zeros((TM, N_EXPERTS), F32)
    for g in range(N_GROUPS):
        beat = jnp.zeros((TM, 1), F32)
        for g2 in range(N_GROUPS):
            if g2 == g:
                continue
            better = (gs[g2] >= gs[g]) if g2 < g else (gs[g2] > gs[g])
            beat = beat + jnp.where(better, 1.0, 0.0)
        keep = jnp.where(grp == g, jnp.where(beat < TOPK_GROUPS, 1.0, 0.0), keep)
    cur = jnp.where(keep > 0.5, sel, NEG)
    lane8 = lax.broadcasted_iota(jnp.int32, (TM, LANES), 1)
    topi = jnp.zeros((TM, LANES), F32)
    topw = jnp.zeros((TM, LANES), F32)
    chosen = jnp.zeros((TM, N_EXPERTS), F32)
    idxs = []
    for k in range(TOP_K):
        m = jnp.max(cur, axis=-1, keepdims=True)
        idx = jnp.min(jnp.where(cur == m, lane, big), axis=-1, keepdims=True)
        hit = lane == idx
        w = jnp.sum(jnp.where(hit, scores, 0.0), axis=-1, keepdims=True)
        cur = jnp.where(hit, ninf, cur)
        chosen = jnp.where(hit, 1.0, chosen)
        topi = jnp.where(lane8 == k, idx, topi)
        topw = jnp.where(lane8 == k, w, topw)
        idxs.append(idx)
    wsum = jnp.sum(topw, axis=-1, keepdims=True)
    topw_o[...] = topw / wsum * ROUTED_SCALE
    topi_o[...] = topi.astype(jnp.int32)
    r_i = lax.broadcasted_iota(jnp.int32, (TM, TM), 0)
    c_i = lax.broadcasted_iota(jnp.int32, (TM, TM), 1)
    tri = jnp.where(c_i < r_i, 1.0, 0.0).astype(BF16)
    prefix = _dot(tri, chosen.astype(BF16)) + cnt_s[0:1, :]
    rank = jnp.zeros((TM, LANES), jnp.int32)
    for k in range(TOP_K):
        r = jnp.sum(jnp.where(lane == idxs[k], prefix, 0.0), axis=-1, keepdims=True)
        rank = jnp.where(lane8 == k, r.astype(jnp.int32), rank)
    rank_o[...] = rank
    total = cnt_s[...] + jnp.sum(chosen, axis=0, keepdims=True)
    cnt_s[...] = total
    cnt_o[...] = total


def _router(logits, bias):
    row = lambda i: (i, 0)
    const = lambda i: (0, 0)
    return pl.pallas_call(
        _router_kernel,
        grid=(T_ALL // TM,),
        in_specs=[pl.BlockSpec((TM, N_EXPERTS), row), pl.BlockSpec((1, N_EXPERTS), const)],
        out_specs=[pl.BlockSpec((TM, LANES), row), pl.BlockSpec((TM, LANES), row),
                   pl.BlockSpec((TM, LANES), row), pl.BlockSpec((8, N_EXPERTS), const)],
        out_shape=[jax.ShapeDtypeStruct((T_ALL, LANES), jnp.int32),
                   jax.ShapeDtypeStruct((T_ALL, LANES), jnp.int32),
                   jax.ShapeDtypeStruct((T_ALL, LANES), F32),
                   jax.ShapeDtypeStruct((8, N_EXPERTS), F32)],
        scratch_shapes=[pltpu.VMEM((8, N_EXPERTS), F32)],
        compiler_params=_params("arbitrary"),
        name="router",
    )(logits, bias)


def _dest_kernel(topi_ref, rank_ref, ps_ref, o_ref):
    lane = lax.broadcasted_iota(jnp.int32, (TM, N_EXPERTS), 1)
    lane8 = lax.broadcasted_iota(jnp.int32, (TM, LANES), 1)
    ps = ps_ref[...]
    topi = topi_ref[...]
    acc = jnp.zeros((TM, LANES), F32)
    for k in range(TOP_K):
        start = jnp.sum(jnp.where(lane == topi[:, k:k + 1], ps, 0.0), axis=-1, keepdims=True)
        acc = jnp.where(lane8 == k, start, acc)
    o_ref[...] = rank_ref[...] + acc.astype(jnp.int32)


def _dest(topi_p, rank_p, pstart):
    row = lambda i: (i, 0)
    return pl.pallas_call(
        _dest_kernel,
        grid=(T_ALL // TM,),
        in_specs=[pl.BlockSpec((TM, LANES), row), pl.BlockSpec((TM, LANES), row),
                  pl.BlockSpec((1, N_EXPERTS), lambda i: (0, 0))],
        out_specs=pl.BlockSpec((TM, LANES), row),
        out_shape=jax.ShapeDtypeStruct((T_ALL, LANES), jnp.int32),
        compiler_params=_params("arbitrary"),
        name="dest",
    )(topi_p, rank_p, pstart)


TD = 512
ROW_UNROLL = 4


def _dispatch_kernel(dest_ref, h_ref, xs_hbm, sem):
    def row(t, carry):
        src = h_ref.at[pl.ds(t, 1)]
        for k in range(TOP_K):
            d = dest_ref[t * TOP_K + k]
            pltpu.make_async_copy(src, xs_hbm.at[pl.ds(d, 1)], sem).start(priority=k % 2)
        return carry

    lax.fori_loop(0, TD, row, 0, unroll=ROW_UNROLL)
    for _ in range(TOP_K):
        pltpu.make_async_copy(h_ref, xs_hbm.at[pl.ds(0, TD)], sem).wait()


def _dispatch(dest_flat, h2):
    return pl.pallas_call(
        _dispatch_kernel,
        grid=(T_ALL // TD,),
        in_specs=[pl.BlockSpec((TD * TOP_K,), lambda i: (i,), memory_space=pltpu.SMEM),
                  pl.BlockSpec((TD, D_MODEL), lambda i: (i, 0))],
        out_specs=pl.BlockSpec(memory_space=pl.ANY),
        out_shape=jax.ShapeDtypeStruct((N_SLOTS, D_MODEL), F32),
        scratch_shapes=[pltpu.SemaphoreType.DMA(())],
        compiler_params=_params("arbitrary"),
        name="dispatch",
    )(dest_flat, h2)


def _expert_kernel(bs_ref, x_hbm, wg_ref, wu_ref, wd_ref, o_hbm, wgu_s, wd_s, xbuf, obuf, xsem, osem):
    e = pl.program_id(0)
    b0 = bs_ref[e]
    b1 = bs_ref[e + 1]
    n_used = bs_ref[N_EXPERTS]

    def xcopy(g, s):
        return pltpu.make_async_copy(x_hbm.at[pl.ds(pl.multiple_of(g * MOE_BLK, MOE_BLK), MOE_BLK)],
                                     xbuf.at[s], xsem.at[s])

    def ocopy(g, s):
        return pltpu.make_async_copy(obuf.at[s],
                                     o_hbm.at[pl.ds(pl.multiple_of(g * MOE_BLK, MOE_BLK), MOE_BLK)],
                                     osem.at[s])

    @pl.when(e == 0)
    def _():
        xcopy(0, 0).start()

    @pl.when(b1 > b0)
    def _():
        wgu_s[:, 0:D_EXPERT] = wg_ref[...].astype(BF16)
        wgu_s[:, D_EXPERT:2 * D_EXPERT] = wu_ref[...].astype(BF16)
        wd_s[...] = wd_ref[...].astype(BF16)

    def block(g, carry):
        s = lax.rem(g, 2)

        @pl.when(g + 1 < n_used)
        def _():
            xcopy(g + 1, 1 - s).start()

        xcopy(g, s).wait()

        @pl.when(g >= 2)
        def _():
            ocopy(g - 2, s).wait()

        gu = _dot(xbuf[s].astype(BF16), wgu_s[...])
        a = _silu(gu[:, :D_EXPERT]) * gu[:, D_EXPERT:]
        obuf[s] = _dot(a.astype(BF16), wd_s[...])
        ocopy(g, s).start()
        return carry

    lax.fori_loop(b0, b1, block, 0)

    @pl.when(e == N_EXPERTS - 1)
    def _():
        @pl.when(n_used >= 2)
        def _():
            ocopy(n_used - 2, lax.rem(n_used, 2)).wait()

        ocopy(n_used - 1, lax.rem(n_used - 1, 2)).wait()


def _experts(blk_start, x_sorted, w_gate, w_up, w_down):
    wsel = lambda e, bs: (e, 0, 0)
    grid_spec = pltpu.PrefetchScalarGridSpec(
        num_scalar_prefetch=1,
        grid=(N_EXPERTS,),
        in_specs=[pl.BlockSpec(memory_space=pl.ANY),
                  pl.BlockSpec((None, D_MODEL, D_EXPERT), wsel),
                  pl.BlockSpec((None, D_MODEL, D_EXPERT), wsel),
                  pl.BlockSpec((None, D_EXPERT, D_MODEL), wsel)],
        out_specs=pl.BlockSpec(memory_space=pl.ANY),
        scratch_shapes=[pltpu.VMEM((D_MODEL, 2 * D_EXPERT), BF16), pltpu.VMEM((D_EXPERT, D_MODEL), BF16),
                        pltpu.VMEM((2, MOE_BLK, D_MODEL), F32), pltpu.VMEM((2, MOE_BLK, D_MODEL), F32),
                        pltpu.SemaphoreType.DMA((2,)), pltpu.SemaphoreType.DMA((2,))],
    )
    return pl.pallas_call(
        _expert_kernel,
        grid_spec=grid_spec,
        out_shape=jax.ShapeDtypeStruct((N_SLOTS, D_MODEL), F32),
        compiler_params=_params("arbitrary"),
        name="experts",
    )(blk_start, x_sorted, w_gate, w_up, w_down)


TC = 256


def _combine_kernel(dcur_ref, dnext_ref, eo_hbm, x1_ref, h2_ref, w_ref, mod_ref, g_ref, wgu_ref, wd_ref,
                    o_ref, buf, sem):
    i = pl.program_id(0)
    n = pl.num_programs(0)
    slot = lax.rem(i, 2)

    def gather(d_ref, s):
        def row(t, carry):
            for k in range(TOP_K):
                d = d_ref[t * TOP_K + k]
                pltpu.make_async_copy(eo_hbm.at[pl.ds(d, 1)], buf.at[s, k, pl.ds(t, 1)],
                                      sem.at[s]).start(priority=k % 2)
            return carry
        lax.fori_loop(0, TC, row, 0, unroll=ROW_UNROLL)

    @pl.when(i == 0)
    def _():
        gather(dcur_ref, 0)

    @pl.when(i + 1 < n)
    def _():
        gather(dnext_ref, 1 - slot)

    gate = mod_ref[:, 5 * D_MODEL:6 * D_MODEL]
    gu = _dot(h2_ref[...].astype(BF16), wgu_ref[...])
    a = _silu(gu[:, :D_SHARED]) * gu[:, D_SHARED:]
    moe = _dot(a.astype(BF16), wd_ref[...])
    for k in range(TOP_K):
        pltpu.make_async_copy(eo_hbm.at[pl.ds(0, TC)], buf.at[slot, k], sem.at[slot]).wait()
    w = w_ref[...]
    for k in range(TOP_K):
        moe = moe + buf[slot, k] * w[:, k:k + 1]
    o_ref[...] = x1_ref[...] + gate * _rms(moe, g_ref[...])


def _combine(dest_flat, eo, x1, h2, topw, mod3, mod_row, g, wgu, wd, tile0, n_tiles):
    row = lambda i: (tile0 + i, 0)
    const = lambda i: (0, 0)
    blk = TC * TOP_K
    return pl.pallas_call(
        _combine_kernel,
        grid=(n_tiles,),
        in_specs=[pl.BlockSpec((blk,), lambda i: (tile0 + i,), memory_space=pltpu.SMEM),
                  pl.BlockSpec((blk,), lambda i: (tile0 + jnp.minimum(i + 1, n_tiles - 1),),
                               memory_space=pltpu.SMEM),
                  pl.BlockSpec(memory_space=pl.ANY),
                  pl.BlockSpec((TC, D_MODEL), row), pl.BlockSpec((TC, D_MODEL), row),
                  pl.BlockSpec((TC, LANES), row),
                  pl.BlockSpec((None, 1, 6 * D_MODEL), lambda i: (mod_row(i), 0, 0)),
                  pl.BlockSpec((1, D_MODEL), const),
                  pl.BlockSpec((D_MODEL, 2 * D_SHARED), const), pl.BlockSpec((D_SHARED, D_MODEL), const)],
        out_specs=pl.BlockSpec((TC, D_MODEL), lambda i: (i, 0)),
        out_shape=jax.ShapeDtypeStruct((n_tiles * TC, D_MODEL), F32),
        scratch_shapes=[pltpu.VMEM((2, TOP_K, TC, D_MODEL), F32), pltpu.SemaphoreType.DMA((2,))],
        compiler_params=_params("arbitrary"),
        name="combine",
    )(dest_flat, dest_flat, eo, x1, h2, topw, mod3, g, wgu, wd)


def _qa_perm():
    cols = []
    for j in range(A_GROUP):
        for g in range(A_KV_HEADS):
            hd = g * A_GROUP + j
            cols.extend(range(hd * A_HEAD_DIM, (hd + 1) * A_HEAD_DIM))
    return np.asarray(cols, np.int32)


def _pad_heads(w, width, n_heads, offset, total=LANES):
    k = w.shape[0]
    stride = w.shape[1] // n_heads
    w3 = w.reshape(k, n_heads, stride)[:, :, offset:offset + width]
    w3 = jnp.pad(w3, ((0, 0), (0, 0), (0, total - width)))
    return w3.reshape(k, n_heads * total)


def kernel(x_prompt, x_sample, cache_k_swa, cache_v_swa, cache_ckv_mla, cache_krope_mla, c, c_ctx, w_ada, b_ada, norm_pre_mix, norm_post_mix, norm_pre_ffn, norm_post_ffn, w_in, sink_swa, q_norm_mla, kv_norm_mla, w_uq_mla, w_ukv_mla, w_o_swa, w_o_mla, w_out, router_w, router_bias, w_gate_exp, w_up_exp, w_down_exp, w_gate_sh, w_up_sh, w_down_sh):
    l = 0
    perm = _qa_perm()
    wi = w_in[l]
    kr_blk = jnp.pad(wi[:, _O_KR:_O_GATE], ((0, 0), (B_NOPE, LANES - B_NOPE - B_ROPE)))
    w1 = jnp.concatenate([wi[:, _O_QA:_O_KA][:, perm], wi[:, _O_KA:_O_KR], kr_blk], axis=1).astype(BF16)
    wg = wi[:, _O_GATE:].astype(BF16)
    wuq = _pad_heads(w_uq_mla[l], B_NOPE + B_ROPE, B_HEADS, 0).astype(BF16)
    wk = _pad_heads(w_ukv_mla[l], B_NOPE, B_HEADS, 0).astype(BF16)
    wv = w_ukv_mla[l].reshape(KV_LORA, B_HEADS, B_NOPE + B_V)[:, :, B_NOPE:].reshape(KV_LORA, B_HEADS * B_V).astype(BF16)
    woa = w_o_swa[l][perm, :].astype(BF16)
    wob = w_o_mla[l].astype(BF16)
    wout = w_out[l].astype(BF16)
    wr = router_w[l].astype(BF16)
    wgu_sh = jnp.concatenate([w_gate_sh[l], w_up_sh[l]], axis=1).astype(BF16)
    wd_sh = w_down_sh[l].astype(BF16)
    sink = sink_swa[l]
    row2 = lambda v: v.reshape(1, -1)

    cond8 = jnp.concatenate([c_ctx[None, :], c, jnp.zeros((8 - 1 - DEC_BATCH, D_MODEL), F32)], axis=0)
    mod3 = _modulation(cond8, w_ada[l], b_ada[l]).reshape(8, 1, 6 * D_MODEL)

    xp = x_prompt.reshape(T_CTX, D_MODEL)
    xs = x_sample.reshape(T_LAT, D_MODEL)
    per = DEC_SEQ // TM
    shared_in = (row2(norm_pre_mix[l]), w1, row2(q_norm_mla[l]), row2(kv_norm_mla[l]), wuq, wk, wv)
    qa_c, ka_c, va_c, ckv_c, kr_c, qb_c, kb_c, vb_c = _inproj(xp, mod3, lambda i: 0, *shared_in, None)
    qa_l, ka_l, va_l, qb_l, kb_l, vb_l = _inproj(xs, mod3, lambda i: 1 + i // per, *shared_in, _rope_tables())

    oa_c, ob_c = _ctx_attention(sink, qa_c, ka_c, va_c, qb_c, kb_c, vb_c)

    krp_cache = jnp.pad(cache_krope_mla[:, l].reshape(DEC_BATCH * PAST_LEN, B_ROPE),
                        ((0, 0), (B_NOPE, LANES - B_NOPE - B_ROPE)))
    kc_b, vc_b = _kv_expand(cache_ckv_mla[:, l].reshape(DEC_BATCH * PAST_LEN, KV_LORA), krp_cache, wk, wv)
    oa_l = _lat_swa(sink, qa_l, ka_l, va_l,
                    cache_k_swa[:, l].reshape(DEC_BATCH, PAST_LEN, LANES),
                    cache_v_swa[:, l].reshape(DEC_BATCH, PAST_LEN, LANES))
    ob_l = _lat_mla(qb_l, kb_l, vb_l, kc_b, vc_b)

    x1, h2, logits = _merge(xp, xs, oa_c, oa_l, ob_c, ob_l, mod3,
                            row2(norm_pre_mix[l]), row2(norm_post_mix[l]), row2(norm_pre_ffn[l]),
                            woa, wob, wout, wg, wr)

    topi_p, rank_p, topw_p, counts8 = _router(logits, row2(router_bias[l]))
    counts = counts8[0].astype(jnp.int32)

    padded = (counts + MOE_BLK - 1) // MOE_BLK * MOE_BLK
    pend = jnp.cumsum(padded)
    pstart = pend - padded
    dest = _dest(topi_p, rank_p, pstart.astype(F32).reshape(1, N_EXPERTS))[:, :TOP_K].reshape(-1)
    blk_start = (jnp.concatenate([pstart, pend[-1:]]) // MOE_BLK).astype(jnp.int32)

    x_sorted = _dispatch(dest, h2)
    eo = _experts(blk_start, x_sorted, w_gate_exp[l], w_up_exp[l], w_down_exp[l])

    g_post = row2(norm_post_ffn[l])
    per_c = DEC_SEQ // TC
    y_p = _combine(dest, eo, x1, h2, topw_p, mod3, lambda i: 0, g_post, wgu_sh, wd_sh, 0, T_CTX // TC)
    y_s = _combine(dest, eo, x1, h2, topw_p, mod3, lambda i: 1 + i // per_c, g_post, wgu_sh, wd_sh,
                   T_CTX // TC, T_LAT // TC)

    return (y_p.reshape(BATCH, SEQ, D_MODEL), y_s.reshape(DEC_BATCH, DEC_SEQ, D_MODEL),
            ka_c.reshape(BATCH, 1, SEQ, A_KV_HEADS, A_HEAD_DIM),
            va_c.reshape(BATCH, 1, SEQ, A_KV_HEADS, A_HEAD_DIM),
            ckv_c.reshape(BATCH, 1, SEQ, KV_LORA),
            kr_c[:, B_NOPE:B_NOPE + B_ROPE].reshape(BATCH, 1, SEQ, B_ROPE))
```

```python
import functools

import jax
import jax.numpy as jnp
import numpy as np
from jax import lax
from jax.experimental import pallas as pl
from jax.experimental.pallas import tpu as pltpu

D_MODEL = 1024
BATCH = 32
SEQ = 256
DEC_BATCH = 4
DEC_SEQ = 2048
PAST_LEN = 512
GRID_W = 64
WINDOW = 128
ROPE_BASE = 10000.0
EPS = 1e-6
NEG = -1e30
A_HEADS = 8
A_KV_HEADS = 2
A_GROUP = A_HEADS // A_KV_HEADS
A_HEAD_DIM = 64
B_HEADS = 8
B_NOPE = 64
B_ROPE = 32
B_V = 64
KV_LORA = 256
Q_LORA = 384
MLA_SCALE = (B_NOPE + B_ROPE) ** -0.5
N_EXPERTS = 256
TOP_K = 8
N_GROUPS = 8
TOPK_GROUPS = 4
D_EXPERT = 256
D_SHARED = 256
ROUTED_SCALE = 2.5

T_CTX = BATCH * SEQ
T_LAT = DEC_BATCH * DEC_SEQ
T_ALL = T_CTX + T_LAT

LANES = 128
TM = 512
TQ = 256
MOE_BLK = 256
N_SLOT_BLK = (T_ALL * TOP_K) // MOE_BLK + N_EXPERTS
N_SLOTS = N_SLOT_BLK * MOE_BLK
VMEM_LIMIT = 48 * 1024 * 1024

F32 = jnp.float32
BF16 = jnp.bfloat16

_O_QA = 0
_O_KA = _O_QA + A_HEADS * A_HEAD_DIM
_O_VA = _O_KA + A_KV_HEADS * A_HEAD_DIM
_O_QL = _O_VA + A_KV_HEADS * A_HEAD_DIM
_O_CKV = _O_QL + Q_LORA
_O_KR = _O_CKV + KV_LORA
_O_GATE = _O_KR + B_ROPE


def _params(*sem):
    return pltpu.CompilerParams(dimension_semantics=sem, vmem_limit_bytes=VMEM_LIMIT)


def _dot(a, b):
    return jnp.dot(a, b, preferred_element_type=F32)


def _dot_nt(a, b):
    return lax.dot_general(a, b, (((1,), (1,)), ((), ())), preferred_element_type=F32)


def _rms(x, g):
    return x * lax.rsqrt(jnp.mean(x * x, axis=-1, keepdims=True) + EPS) * g


def _sigmoid(x):
    return 1.0 / (1.0 + jnp.exp(-x))


SUB = 8
assert SUB * LANES == D_MODEL


def _load_rows(ref, n):
    return jnp.concatenate([ref[pl.ds(j, n, stride=SUB), :] for j in range(SUB)], axis=1)


def _store_rows(ref, val, n):
    for j in range(SUB):
        ref[pl.ds(j, n, stride=SUB), :] = val[:, j * LANES:(j + 1) * LANES]


def _silu(x):
    return x * _sigmoid(x)


def _mod_kernel(cond_ref, w_ref, b_ref, o_ref):
    s = _silu(cond_ref[...]).astype(BF16)
    o_ref[...] = _dot(s, w_ref[...].astype(BF16)) + b_ref[...]


def _modulation(cond8, w_ada, b_ada):
    tn = 768
    n = w_ada.shape[1]
    return pl.pallas_call(
        _mod_kernel,
        grid=(n // tn,),
        in_specs=[pl.BlockSpec((8, D_MODEL), lambda j: (0, 0)),
                  pl.BlockSpec((D_MODEL, tn), lambda j: (0, j)),
                  pl.BlockSpec((1, tn), lambda j: (0, j))],
        out_specs=pl.BlockSpec((8, tn), lambda j: (0, j)),
        out_shape=jax.ShapeDtypeStruct((8, n), F32),
        compiler_params=_params("arbitrary"),
        name="modulation",
    )(cond8, w_ada, b_ada.reshape(1, n))


def _rope(x, cos, sin, half):
    lane = lax.broadcasted_iota(jnp.int32, x.shape, 1)
    first = (lane & (2 * half - 1)) < half
    partner = jnp.where(first, pltpu.roll(x, LANES - half, 1), pltpu.roll(x, half, 1))
    return x * cos + partner * sin


def _rope_tables():
    t = jnp.arange(DEC_SEQ, dtype=jnp.int32)
    row = (t // GRID_W).astype(F32)
    col = (t % GRID_W).astype(F32)

    def tables(d_rot):
        nf = d_rot // 4
        inv = ROPE_BASE ** (-jnp.arange(nf, dtype=F32) / nf)
        ar = row[:, None] * inv
        ac = col[:, None] * inv
        cos = jnp.concatenate([jnp.cos(ar), jnp.cos(ar), jnp.cos(ac), jnp.cos(ac)], axis=1)
        sin = jnp.concatenate([-jnp.sin(ar), jnp.sin(ar), -jnp.sin(ac), jnp.sin(ac)], axis=1)
        return cos, sin

    cos_a, sin_a = tables(A_HEAD_DIM)
    cos_a = jnp.concatenate([cos_a, cos_a], axis=1)
    sin_a = jnp.concatenate([sin_a, sin_a], axis=1)
    cos_b, sin_b = tables(B_ROPE)
    one = jnp.ones((DEC_SEQ, B_NOPE), F32)
    zero = jnp.zeros((DEC_SEQ, B_NOPE), F32)
    pad = LANES - B_NOPE - B_ROPE
    cos_b = jnp.concatenate([one, cos_b, one[:, :pad]], axis=1)
    sin_b = jnp.concatenate([zero, sin_b, zero[:, :pad]], axis=1)
    return cos_a, sin_a, cos_b, sin_b


_N1 = 1536


def _inproj_kernel(latent, *refs):
    if latent:
        (x_ref, mod_ref, g_ref, w1_ref, qn_ref, kvn_ref, wuq_ref, wk_ref, wv_ref,
         ca_ref, sa_ref, cb_ref, sb_ref,
         qa_o, ka_o, va_o, qb_o, kb_o, vb_o) = refs
    else:
        (x_ref, mod_ref, g_ref, w1_ref, qn_ref, kvn_ref, wuq_ref, wk_ref, wv_ref,
         qa_o, ka_o, va_o, ckv_o, kr_o, qb_o, kb_o, vb_o) = refs
    x = x_ref[...]
    shift = mod_ref[:, 0:D_MODEL]
    scale = mod_ref[:, D_MODEL:2 * D_MODEL]
    h = _rms(x, g_ref[...]) * (1.0 + scale) + shift
    p = _dot(h.astype(BF16), w1_ref[...])
    qa = p[:, 0:512]
    ka = p[:, 512:640]
    va = p[:, 640:768]
    ql = p[:, 768:1152]
    ckv = p[:, 1152:1408]
    krp = p[:, 1408:1536]
    qn = _rms(ql, qn_ref[...])
    qb = _dot(qn.astype(BF16), wuq_ref[...])
    cn = _rms(ckv, kvn_ref[...])
    cnb = cn.astype(BF16)
    kn = _dot(cnb, wk_ref[...])
    vb = _dot(cnb, wv_ref[...])
    if latent:
        ca, sa, cb, sb = ca_ref[...], sa_ref[...], cb_ref[...], sb_ref[...]
        ka = _rope(ka, ca, sa, A_HEAD_DIM // 4)
        krp = _rope(krp, cb, sb, B_ROPE // 4)
        for j in range(4):
            blk = _rope(qa[:, j * LANES:(j + 1) * LANES], ca, sa, A_HEAD_DIM // 4)
            qa_o[:, j * LANES:(j + 1) * LANES] = (blk * (A_HEAD_DIM ** -0.5)).astype(BF16)
        for hd in range(B_HEADS):
            blk = _rope(qb[:, hd * LANES:(hd + 1) * LANES], cb, sb, B_ROPE // 4)
            qb_o[:, hd * LANES:(hd + 1) * LANES] = (blk * MLA_SCALE).astype(BF16)
        ka_o[...] = ka.astype(BF16)
        va_o[...] = va.astype(BF16)
    else:
        qa_o[...] = (qa * (A_HEAD_DIM ** -0.5)).astype(BF16)
        qb_o[...] = (qb * MLA_SCALE).astype(BF16)
        ka_o[...] = ka
        va_o[...] = va
        ckv_o[...] = cn
        kr_o[...] = krp
    for hd in range(B_HEADS):
        kb_o[:, hd * LANES:(hd + 1) * LANES] = (kn[:, hd * LANES:(hd + 1) * LANES] + krp).astype(BF16)
    vb_o[...] = vb.astype(BF16)


def _inproj(x, mod3, mod_row, g, w1, qn, kvn, wuq, wk, wv, tables):
    latent = tables is not None
    t = x.shape[0]
    n_tiles = t // TM
    row = lambda i: (i, 0)
    const = lambda i: (0, 0)
    in_specs = [pl.BlockSpec((TM, D_MODEL), row),
                pl.BlockSpec((None, 1, 6 * D_MODEL), lambda i: (mod_row(i), 0, 0)),
                pl.BlockSpec((1, D_MODEL), const),
                pl.BlockSpec((D_MODEL, _N1), const),
                pl.BlockSpec((1, Q_LORA), const),
                pl.BlockSpec((1, KV_LORA), const),
                pl.BlockSpec((Q_LORA, B_HEADS * LANES), const),
                pl.BlockSpec((KV_LORA, B_HEADS * LANES), const),
                pl.BlockSpec((KV_LORA, B_HEADS * B_V), const)]
    args = [x, mod3, g, w1, qn, kvn, wuq, wk, wv]
    sds = jax.ShapeDtypeStruct
    if latent:
        per = DEC_SEQ // TM
        tab = lambda i: (i % per, 0)
        in_specs += [pl.BlockSpec((TM, LANES), tab)] * 4
        args += list(tables)
        out_shape = [sds((t, 512), BF16), sds((t, LANES), BF16), sds((t, LANES), BF16),
                     sds((t, B_HEADS * LANES), BF16), sds((t, B_HEADS * LANES), BF16),
                     sds((t, B_HEADS * B_V), BF16)]
        widths = [512, LANES, LANES, B_HEADS * LANES, B_HEADS * LANES, B_HEADS * B_V]
    else:
        out_shape = [sds((t, 512), BF16), sds((t, LANES), F32), sds((t, LANES), F32),
                     sds((t, KV_LORA), F32), sds((t, LANES), F32),
                     sds((t, B_HEADS * LANES), BF16), sds((t, B_HEADS * LANES), BF16),
                     sds((t, B_HEADS * B_V), BF16)]
        widths = [512, LANES, LANES, KV_LORA, LANES, B_HEADS * LANES, B_HEADS * LANES, B_HEADS * B_V]
    out_specs = [pl.BlockSpec((TM, w), row) for w in widths]
    return pl.pallas_call(
        functools.partial(_inproj_kernel, latent),
        grid=(n_tiles,),
        in_specs=in_specs,
        out_specs=out_specs,
        out_shape=out_shape,
        compiler_params=_params("arbitrary"),
        name="inproj_latent" if latent else "inproj_ctx",
    )(*args)


def _kvexp_kernel(ckv_ref, krp_ref, wk_ref, wv_ref, kb_o, vb_o):
    cb = ckv_ref[...].astype(BF16)
    kn = _dot(cb, wk_ref[...])
    krp = krp_ref[...]
    for hd in range(B_HEADS):
        kb_o[:, hd * LANES:(hd + 1) * LANES] = (kn[:, hd * LANES:(hd + 1) * LANES] + krp).astype(BF16)
    vb_o[...] = _dot(cb, wv_ref[...]).astype(BF16)


def _kv_expand(ckv, krp, wk, wv):
    t = ckv.shape[0]
    row = lambda i: (i, 0)
    const = lambda i: (0, 0)
    return pl.pallas_call(
        _kvexp_kernel,
        grid=(t // TM,),
        in_specs=[pl.BlockSpec((TM, KV_LORA), row), pl.BlockSpec((TM, LANES), row),
                  pl.BlockSpec((KV_LORA, B_HEADS * LANES), const),
                  pl.BlockSpec((KV_LORA, B_HEADS * B_V), const)],
        out_specs=[pl.BlockSpec((TM, B_HEADS * LANES), row), pl.BlockSpec((TM, B_HEADS * B_V), row)],
        out_shape=[jax.ShapeDtypeStruct((t, B_HEADS * LANES), BF16),
                   jax.ShapeDtypeStruct((t, B_HEADS * B_V), BF16)],
        compiler_params=_params("arbitrary"),
        name="kv_expand",
    )(ckv, krp, wk, wv)


def _lo_mask(rows):
    return lax.broadcasted_iota(jnp.int32, (rows, LANES), 1) < (LANES // 2)


def _softmax_pv(parts, sink):
    m = None
    for s, _ in parts:
        mi = jnp.max(s, axis=-1, keepdims=True)
        m = mi if m is None else jnp.maximum(m, mi)
    if sink is not None:
        m = jnp.maximum(m, sink)
    den = None
    acc = None
    for s, v in parts:
        p = jnp.exp(s - m)
        di = jnp.sum(p, axis=-1, keepdims=True)
        den = di if den is None else den + di
        oi = _dot(p.astype(BF16), v)
        acc = oi if acc is None else acc + oi
    if sink is not None:
        den = den + jnp.exp(sink - m)
    return acc / den


def _ctx_attn_kernel(sink_ref, qa_ref, ka_ref, va_ref, qb_ref, kb_ref, vb_ref, oa_ref, ob_ref):
    lo = _lo_mask(SEQ)
    k = ka_ref[...].astype(BF16)
    v = va_ref[...].astype(BF16)
    for j in range(A_GROUP):
        q = qa_ref[:, j * LANES:(j + 1) * LANES]
        zero = jnp.zeros_like(q)
        o0 = _softmax_pv([(_dot_nt(jnp.where(lo, q, zero), k), v)], sink_ref[j])
        o1 = _softmax_pv([(_dot_nt(jnp.where(lo, zero, q), k), v)], sink_ref[A_GROUP + j])
        oa_ref[:, j * LANES:(j + 1) * LANES] = jnp.where(lo, o0, o1).astype(BF16)
    for j in range(B_HEADS // 2):
        vp = vb_ref[:, j * LANES:(j + 1) * LANES]
        outs = []
        for hd in (2 * j, 2 * j + 1):
            s = _dot_nt(qb_ref[:, hd * LANES:(hd + 1) * LANES], kb_ref[:, hd * LANES:(hd + 1) * LANES])
            outs.append(_softmax_pv([(s, vp)], None))
        ob_ref[:, j * LANES:(j + 1) * LANES] = jnp.where(lo, outs[0], outs[1]).astype(BF16)


def _ctx_attention(sink, qa, ka, va, qb, kb, vb):
    blk = lambda w: pl.BlockSpec((SEQ, w), lambda b: (b, 0))
    return pl.pallas_call(
        _ctx_attn_kernel,
        grid=(BATCH,),
        in_specs=[pl.BlockSpec(memory_space=pltpu.SMEM),
                  blk(512), blk(LANES), blk(LANES), blk(B_HEADS * LANES), blk(B_HEADS * LANES),
                  blk(B_HEADS * B_V)],
        out_specs=[blk(512), blk(512)],
        out_shape=[jax.ShapeDtypeStruct((T_CTX, 512), BF16), jax.ShapeDtypeStruct((T_CTX, 512), BF16)],
        compiler_params=_params("arbitrary"),
        name="ctx_attention",
    )(sink, qa, ka, va, qb, kb, vb)


_WIN = TQ + 2 * WINDOW


def _lat_swa_kernel(sink_ref, q_ref, k_ref, v_ref, kc_ref, vc_ref, o_ref):
    qi = pl.program_id(1)
    q0 = qi * TQ
    start = pl.multiple_of(jnp.clip(q0 - WINDOW, 0, DEC_SEQ - _WIN), WINDOW)
    kw = k_ref[pl.ds(start, _WIN), :]
    vw = v_ref[pl.ds(start, _WIN), :]
    kc = kc_ref[...].astype(BF16)
    vc = vc_ref[...].astype(BF16)
    qpos = q0 + lax.broadcasted_iota(jnp.int32, (TQ, _WIN), 0)
    kpos = start + lax.broadcasted_iota(jnp.int32, (TQ, _WIN), 1)
    valid = jnp.abs(qpos - kpos) <= WINDOW
    lo = _lo_mask(TQ)
    for j in range(A_GROUP):
        q = q_ref[:, j * LANES:(j + 1) * LANES]
        zero = jnp.zeros_like(q)
        outs = []
        for g, qh in ((0, jnp.where(lo, q, zero)), (1, jnp.where(lo, zero, q))):
            sw = jnp.where(valid, _dot_nt(qh, kw), NEG)
            sc = _dot_nt(qh, kc)
            outs.append(_softmax_pv([(sw, vw), (sc, vc)], sink_ref[g * A_GROUP + j]))
        o_ref[:, j * LANES:(j + 1) * LANES] = jnp.where(lo, outs[0], outs[1]).astype(BF16)


def _lat_swa(sink, qa, ka, va, kc, vc):
    nq = DEC_SEQ // TQ
    return pl.pallas_call(
        _lat_swa_kernel,
        grid=(DEC_BATCH, nq),
        in_specs=[pl.BlockSpec(memory_space=pltpu.SMEM),
                  pl.BlockSpec((TQ, 512), lambda b, i: (b * nq + i, 0)),
                  pl.BlockSpec((None, DEC_SEQ, LANES), lambda b, i: (b, 0, 0)),
                  pl.BlockSpec((None, DEC_SEQ, LANES), lambda b, i: (b, 0, 0)),
                  pl.BlockSpec((None, PAST_LEN, LANES), lambda b, i: (b, 0, 0)),
                  pl.BlockSpec((None, PAST_LEN, LANES), lambda b, i: (b, 0, 0))],
        out_specs=pl.BlockSpec((TQ, 512), lambda b, i: (b * nq + i, 0)),
        out_shape=jax.ShapeDtypeStruct((T_LAT, 512), BF16),
        compiler_params=_params("arbitrary", "arbitrary"),
        name="latent_swa",
    )(sink, qa, ka.reshape(DEC_BATCH, DEC_SEQ, LANES), va.reshape(DEC_BATCH, DEC_SEQ, LANES), kc, vc)


def _lat_mla_kernel(q_ref, kl_ref, vl_ref, kc_ref, vc_ref, o_ref):
    lo = _lo_mask(TQ)
    vl = vl_ref[...]
    vc = vc_ref[...]
    outs = []
    for hh in range(2):
        q = q_ref[:, hh * LANES:(hh + 1) * LANES]
        sc = _dot_nt(q, kc_ref[:, hh * LANES:(hh + 1) * LANES])
        sl = _dot_nt(q, kl_ref[:, hh * LANES:(hh + 1) * LANES])
        outs.append(_softmax_pv([(sc, vc), (sl, vl)], None))
    o_ref[...] = jnp.where(lo, outs[0], outs[1]).astype(BF16)


def _lat_mla(qb, kb, vb, kc, vc):
    nq = DEC_SEQ // TQ
    npair = B_HEADS // 2
    return pl.pallas_call(
        _lat_mla_kernel,
        grid=(DEC_BATCH, npair, nq),
        in_specs=[pl.BlockSpec((TQ, 2 * LANES), lambda b, p, i: (b * nq + i, p)),
                  pl.BlockSpec((None, DEC_SEQ, 2 * LANES), lambda b, p, i: (b, 0, p)),
                  pl.BlockSpec((None, DEC_SEQ, LANES), lambda b, p, i: (b, 0, p)),
                  pl.BlockSpec((None, PAST_LEN, 2 * LANES), lambda b, p, i: (b, 0, p)),
                  pl.BlockSpec((None, PAST_LEN, LANES), lambda b, p, i: (b, 0, p))],
        out_specs=pl.BlockSpec((TQ, LANES), lambda b, p, i: (b * nq + i, p)),
        out_shape=jax.ShapeDtypeStruct((T_LAT, 512), BF16),
        compiler_params=_params("arbitrary", "arbitrary", "arbitrary"),
        name="latent_mla",
    )(qb, kb.reshape(DEC_BATCH, DEC_SEQ, B_HEADS * LANES), vb.reshape(DEC_BATCH, DEC_SEQ, B_HEADS * B_V),
      kc.reshape(DEC_BATCH, PAST_LEN, B_HEADS * LANES), vc.reshape(DEC_BATCH, PAST_LEN, B_HEADS * B_V))


_N_CTX_TILES = T_CTX // TM


def _merge_kernel(xp_ref, xs_ref, oac_ref, oal_ref, obc_ref, obl_ref, mod_ref,
                  gpre_ref, gpost_ref, gffn_ref, woa_ref, wob_ref, wout_ref, wg_ref, wr_ref,
                  x1_o, h2_o, lg_o):
    def body(x_ref, oa_ref, ob_ref):
        x = x_ref[...]
        sh1 = mod_ref[:, 0:D_MODEL]
        sc1 = mod_ref[:, D_MODEL:2 * D_MODEL]
        g1 = mod_ref[:, 2 * D_MODEL:3 * D_MODEL]
        sh2 = mod_ref[:, 3 * D_MODEL:4 * D_MODEL]
        sc2 = mod_ref[:, 4 * D_MODEL:5 * D_MODEL]
        h = (_rms(x, gpre_ref[...]) * (1.0 + sc1) + sh1).astype(BF16)
        gates = _dot(h, wg_ref[...])
        ya = _dot(oa_ref[...], woa_ref[...])
        yb = _dot(ob_ref[...], wob_ref[...])
        y = _sigmoid(gates[:, :D_MODEL]) * ya + _sigmoid(gates[:, D_MODEL:]) * yb
        mix = _dot(y.astype(BF16), wout_ref[...])
        x1 = x + g1 * _rms(mix, gpost_ref[...])
        x1_o[...] = x1
        h2 = _rms(x1, gffn_ref[...]) * (1.0 + sc2) + sh2
        _store_rows(h2_o, h2, TM)
        lg_o[...] = _dot(h2.astype(BF16), wr_ref[...])

    is_ctx = pl.program_id(0) < _N_CTX_TILES

    @pl.when(is_ctx)
    def _():
        body(xp_ref, oac_ref, obc_ref)

    @pl.when(jnp.logical_not(is_ctx))
    def _():
        body(xs_ref, oal_ref, obl_ref)


def _mod_row_all(i):
    per = DEC_SEQ // TM
    return jnp.where(i < _N_CTX_TILES, 0, 1 + (jnp.maximum(i - _N_CTX_TILES, 0)) // per)


def _merge(xp, xs, oac, oal, obc, obl, mod3, gpre, gpost, gffn, woa, wob, wout, wg, wr):
    ctx = lambda i: (jnp.minimum(i, _N_CTX_TILES - 1), 0)
    lat = lambda i: (jnp.maximum(i - _N_CTX_TILES, 0), 0)
    row = lambda i: (i, 0)
    const = lambda i: (0, 0)
    return pl.pallas_call(
        _merge_kernel,
        grid=(T_ALL // TM,),
        in_specs=[pl.BlockSpec((TM, D_MODEL), ctx), pl.BlockSpec((TM, D_MODEL), lat),
                  pl.BlockSpec((TM, 512), ctx), pl.BlockSpec((TM, 512), lat),
                  pl.BlockSpec((TM, 512), ctx), pl.BlockSpec((TM, 512), lat),
                  pl.BlockSpec((None, 1, 6 * D_MODEL), lambda i: (_mod_row_all(i), 0, 0)),
                  pl.BlockSpec((1, D_MODEL), const), pl.BlockSpec((1, D_MODEL), const),
                  pl.BlockSpec((1, D_MODEL), const),
                  pl.BlockSpec((512, D_MODEL), const), pl.BlockSpec((512, D_MODEL), const),
                  pl.BlockSpec((D_MODEL, D_MODEL), const), pl.BlockSpec((D_MODEL, 2 * D_MODEL), const),
                  pl.BlockSpec((D_MODEL, N_EXPERTS), const)],
        out_specs=[pl.BlockSpec((TM, D_MODEL), row), pl.BlockSpec((TM * SUB, LANES), row),
                   pl.BlockSpec((TM, N_EXPERTS), row)],
        out_shape=[jax.ShapeDtypeStruct((T_ALL, D_MODEL), F32), jax.ShapeDtypeStruct((T_ALL * SUB, LANES), F32),
                   jax.ShapeDtypeStruct((T_ALL, N_EXPERTS), F32)],
        compiler_params=_params("arbitrary"),
        name="merge",
    )(xp, xs, oac, oal, obc, obl, mod3, gpre, gpost, gffn, woa, wob, wout, wg, wr)


_PER_GROUP = N_EXPERTS // N_GROUPS


def _router_kernel(lg_ref, bias_ref, topi_o, rank_o, topw_o, cnt_o, cnt_s):
    @pl.when(pl.program_id(0) == 0)
    def _():
        cnt_s[...] = jnp.zeros_like(cnt_s)

    scores = _sigmoid(lg_ref[...])
    sel = scores + bias_ref[...]
    lane_i = lax.broadcasted_iota(jnp.int32, (TM, N_EXPERTS), 1)
    lane = lane_i.astype(F32)
    grp = lax.shift_right_logical(lane_i, int(np.log2(_PER_GROUP)))
    ninf = -jnp.inf
    big = float(N_EXPERTS)
    gs = []
    for g in range(N_GROUPS):
        sg = jnp.where(grp == g, sel, ninf)
        m1 = jnp.max(sg, axis=-1, keepdims=True)
        i1 = jnp.min(jnp.where(sg == m1, lane, big), axis=-1, keepdims=True)
        m2 = jnp.max(jnp.where(lane == i1, ninf, sg), axis=-1, keepdims=True)
        gs.append(m1 + m2)
    keep = jnp.zeros((TM, N_EXPERTS), F32)
    for g in range(N_GROUPS):
        beat = jnp.zeros((TM, 1), F32)
        for g2 in range(N_GROUPS):
            if g2 == g:
                continue
            better = (gs[g2] >= gs[g]) if g2 < g else (gs[g2] > gs[g])
            beat = beat + jnp.where(better, 1.0, 0.0)
        keep = jnp.where(grp == g, jnp.where(beat < TOPK_GROUPS, 1.0, 0.0), keep)
    cur = jnp.where(keep > 0.5, sel, NEG)
    lane8 = lax.broadcasted_iota(jnp.int32, (TM, LANES), 1)
    topi = jnp.zeros((TM, LANES), F32)
    topw = jnp.zeros((TM, LANES), F32)
    chosen = jnp.zeros((TM, N_EXPERTS), F32)
    idxs = []
    for k in range(TOP_K):
        m = jnp.max(cur, axis=-1, keepdims=True)
        idx = jnp.min(jnp.where(cur == m, lane, big), axis=-1, keepdims=True)
        hit = lane == idx
        w = jnp.sum(jnp.where(hit, scores, 0.0), axis=-1, keepdims=True)
        cur = jnp.where(hit, ninf, cur)
        chosen = jnp.where(hit, 1.0, chosen)
        topi = jnp.where(lane8 == k, idx, topi)
        topw = jnp.where(lane8 == k, w, topw)
        idxs.append(idx)
    wsum = jnp.sum(topw, axis=-1, keepdims=True)
    topw_o[...] = topw / wsum * ROUTED_SCALE
    topi_o[...] = topi.astype(jnp.int32)
    r_i = lax.broadcasted_iota(jnp.int32, (TM, TM), 0)
    c_i = lax.broadcasted_iota(jnp.int32, (TM, TM), 1)
    tri = jnp.where(c_i < r_i, 1.0, 0.0).astype(BF16)
    prefix = _dot(tri, chosen.astype(BF16)) + cnt_s[0:1, :]
    rank = jnp.zeros((TM, LANES), jnp.int32)
    for k in range(TOP_K):
        r = jnp.sum(jnp.where(lane == idxs[k], prefix, 0.0), axis=-1, keepdims=True)
        rank = jnp.where(lane8 == k, r.astype(jnp.int32), rank)
    rank_o[...] = rank
    total = cnt_s[...] + jnp.sum(chosen, axis=0, keepdims=True)
    cnt_s[...] = total
    cnt_o[...] = total


def _router(logits, bias):
    row = lambda i: (i, 0)
    const = lambda i: (0, 0)
    return pl.pallas_call(
        _router_kernel,
        grid=(T_ALL // TM,),
        in_specs=[pl.BlockSpec((TM, N_EXPERTS), row), pl.BlockSpec((1, N_EXPERTS), const)],
        out_specs=[pl.BlockSpec((TM, LANES), row), pl.BlockSpec((TM, LANES), row),
                   pl.BlockSpec((TM, LANES), row), pl.BlockSpec((8, N_EXPERTS), const)],
        out_shape=[jax.ShapeDtypeStruct((T_ALL, LANES), jnp.int32),
                   jax.ShapeDtypeStruct((T_ALL, LANES), jnp.int32),
                   jax.ShapeDtypeStruct((T_ALL, LANES), F32),
                   jax.ShapeDtypeStruct((8, N_EXPERTS), F32)],
        scratch_shapes=[pltpu.VMEM((8, N_EXPERTS), F32)],
        compiler_params=_params("arbitrary"),
        name="router",
    )(logits, bias)


def _dest_kernel(topi_ref, rank_ref, ps_ref, o_ref):
    lane = lax.broadcasted_iota(jnp.int32, (TM, N_EXPERTS), 1)
    lane8 = lax.broadcasted_iota(jnp.int32, (TM, LANES), 1)
    ps = ps_ref[...]
    topi = topi_ref[...]
    acc = jnp.zeros((TM, LANES), F32)
    for k in range(TOP_K):
        start = jnp.sum(jnp.where(lane == topi[:, k:k + 1], ps, 0.0), axis=-1, keepdims=True)
        acc = jnp.where(lane8 == k, start, acc)
    o_ref[...] = rank_ref[...] + acc.astype(jnp.int32)


def _dest(topi_p, rank_p, pstart):
    row = lambda i: (i, 0)
    return pl.pallas_call(
        _dest_kernel,
        grid=(T_ALL // TM,),
        in_specs=[pl.BlockSpec((TM, LANES), row), pl.BlockSpec((TM, LANES), row),
                  pl.BlockSpec((1, N_EXPERTS), lambda i: (0, 0))],
        out_specs=pl.BlockSpec((TM, LANES), row),
        out_shape=jax.ShapeDtypeStruct((T_ALL, LANES), jnp.int32),
        compiler_params=_params("arbitrary"),
        name="dest",
    )(topi_p, rank_p, pstart)


TD = 512
ROW_UNROLL = 4


def _dispatch_kernel(dest_ref, h_ref, xs_hbm, sem):
    def row(t, carry):
        src = h_ref.at[pl.ds(pl.multiple_of(t * SUB, SUB), SUB)]
        for k in range(TOP_K):
            d = dest_ref[t * TOP_K + k]
            pltpu.make_async_copy(src, xs_hbm.at[pl.ds(pl.multiple_of(d * SUB, SUB), SUB)],
                                  sem).start(priority=k % 2)
        return carry

    lax.fori_loop(0, TD, row, 0, unroll=ROW_UNROLL)
    for _ in range(TOP_K):
        pltpu.make_async_copy(h_ref, xs_hbm.at[pl.ds(0, TD * SUB)], sem).wait()


def _dispatch(dest_flat, h2):
    return pl.pallas_call(
        _dispatch_kernel,
        grid=(T_ALL // TD,),
        in_specs=[pl.BlockSpec((TD * TOP_K,), lambda i: (i,), memory_space=pltpu.SMEM),
                  pl.BlockSpec((TD * SUB, LANES), lambda i: (i, 0))],
        out_specs=pl.BlockSpec(memory_space=pl.ANY),
        out_shape=jax.ShapeDtypeStruct((N_SLOTS * SUB, LANES), F32),
        scratch_shapes=[pltpu.SemaphoreType.DMA(())],
        compiler_params=_params("arbitrary"),
        name="dispatch",
    )(dest_flat, h2)


_BLK_TILES = MOE_BLK * SUB


def _expert_kernel(bs_ref, x_hbm, wg_ref, wu_ref, wd_ref, o_hbm, wgu_s, wd_s, xbuf, obuf, xsem, osem):
    e = pl.program_id(0)
    b0 = bs_ref[e]
    b1 = bs_ref[e + 1]
    n_used = bs_ref[N_EXPERTS]

    def xcopy(g, s):
        return pltpu.make_async_copy(x_hbm.at[pl.ds(pl.multiple_of(g * _BLK_TILES, _BLK_TILES), _BLK_TILES)],
                                     xbuf.at[s], xsem.at[s])

    def ocopy(g, s):
        return pltpu.make_async_copy(obuf.at[s],
                                     o_hbm.at[pl.ds(pl.multiple_of(g * _BLK_TILES, _BLK_TILES), _BLK_TILES)],
                                     osem.at[s])

    @pl.when(e == 0)
    def _():
        xcopy(0, 0).start()

    @pl.when(b1 > b0)
    def _():
        wgu_s[:, 0:D_EXPERT] = wg_ref[...].astype(BF16)
        wgu_s[:, D_EXPERT:2 * D_EXPERT] = wu_ref[...].astype(BF16)
        wd_s[...] = wd_ref[...].astype(BF16)

    def block(g, carry):
        s = lax.rem(g, 2)

        @pl.when(g + 1 < n_used)
        def _():
            xcopy(g + 1, 1 - s).start()

        xcopy(g, s).wait()

        @pl.when(g >= 2)
        def _():
            ocopy(g - 2, s).wait()

        gu = _dot(_load_rows(xbuf.at[s], MOE_BLK).astype(BF16), wgu_s[...])
        a = _silu(gu[:, :D_EXPERT]) * gu[:, D_EXPERT:]
        _store_rows(obuf.at[s], _dot(a.astype(BF16), wd_s[...]), MOE_BLK)
        ocopy(g, s).start(priority=1)
        return carry

    lax.fori_loop(b0, b1, block, 0)

    @pl.when(e == N_EXPERTS - 1)
    def _():
        @pl.when(n_used >= 2)
        def _():
            ocopy(n_used - 2, lax.rem(n_used, 2)).wait()

        ocopy(n_used - 1, lax.rem(n_used - 1, 2)).wait()


def _experts(blk_start, x_sorted, w_gate, w_up, w_down):
    wsel = lambda e, bs: (e, 0, 0)
    grid_spec = pltpu.PrefetchScalarGridSpec(
        num_scalar_prefetch=1,
        grid=(N_EXPERTS,),
        in_specs=[pl.BlockSpec(memory_space=pl.ANY),
                  pl.BlockSpec((None, D_MODEL, D_EXPERT), wsel),
                  pl.BlockSpec((None, D_MODEL, D_EXPERT), wsel),
                  pl.BlockSpec((None, D_EXPERT, D_MODEL), wsel)],
        out_specs=pl.BlockSpec(memory_space=pl.ANY),
        scratch_shapes=[pltpu.VMEM((D_MODEL, 2 * D_EXPERT), BF16), pltpu.VMEM((D_EXPERT, D_MODEL), BF16),
                        pltpu.VMEM((2, _BLK_TILES, LANES), F32), pltpu.VMEM((2, _BLK_TILES, LANES), F32),
                        pltpu.SemaphoreType.DMA((2,)), pltpu.SemaphoreType.DMA((2,))],
    )
    return pl.pallas_call(
        _expert_kernel,
        grid_spec=grid_spec,
        out_shape=jax.ShapeDtypeStruct((N_SLOTS * SUB, LANES), F32),
        compiler_params=_params("arbitrary"),
        name="experts",
    )(blk_start, x_sorted, w_gate, w_up, w_down)


TC = 256


def _combine_kernel(dcur_ref, dnext_ref, eo_hbm, x1_ref, h2_ref, w_ref, mod_ref, g_ref, wgu_ref, wd_ref,
                    o_ref, buf, sem):
    i = pl.program_id(0)
    n = pl.num_programs(0)
    slot = lax.rem(i, 2)

    def gather(d_ref, s):
        def row(t, carry):
            for k in range(TOP_K):
                d = d_ref[t * TOP_K + k]
                pltpu.make_async_copy(eo_hbm.at[pl.ds(pl.multiple_of(d * SUB, SUB), SUB)],
                                      buf.at[s, k, pl.ds(pl.multiple_of(t * SUB, SUB), SUB)],
                                      sem.at[s]).start(priority=k % 2)
            return carry
        lax.fori_loop(0, TC, row, 0, unroll=ROW_UNROLL)

    @pl.when(i == 0)
    def _():
        gather(dcur_ref, 0)

    @pl.when(i + 1 < n)
    def _():
        gather(dnext_ref, 1 - slot)

    gate = mod_ref[:, 5 * D_MODEL:6 * D_MODEL]
    gu = _dot(_load_rows(h2_ref, TC).astype(BF16), wgu_ref[...])
    a = _silu(gu[:, :D_SHARED]) * gu[:, D_SHARED:]
    moe = _dot(a.astype(BF16), wd_ref[...])
    for k in range(TOP_K):
        pltpu.make_async_copy(eo_hbm.at[pl.ds(0, TC * SUB)], buf.at[slot, k], sem.at[slot]).wait()
    w = w_ref[...]
    for k in range(TOP_K):
        moe = moe + _load_rows(buf.at[slot, k], TC) * w[:, k:k + 1]
    o_ref[...] = x1_ref[...] + gate * _rms(moe, g_ref[...])


def _combine(dest_flat, eo, x1, h2, topw, mod3, mod_row, g, wgu, wd, tile0, n_tiles):
    row = lambda i: (tile0 + i, 0)
    const = lambda i: (0, 0)
    blk = TC * TOP_K
    return pl.pallas_call(
        _combine_kernel,
        grid=(n_tiles,),
        in_specs=[pl.BlockSpec((blk,), lambda i: (tile0 + i,), memory_space=pltpu.SMEM),
                  pl.BlockSpec((blk,), lambda i: (tile0 + jnp.minimum(i + 1, n_tiles - 1),),
                               memory_space=pltpu.SMEM),
                  pl.BlockSpec(memory_space=pl.ANY),
                  pl.BlockSpec((TC, D_MODEL), row), pl.BlockSpec((TC * SUB, LANES), row),
                  pl.BlockSpec((TC, LANES), row),
                  pl.BlockSpec((None, 1, 6 * D_MODEL), lambda i: (mod_row(i), 0, 0)),
                  pl.BlockSpec((1, D_MODEL), const),
                  pl.BlockSpec((D_MODEL, 2 * D_SHARED), const), pl.BlockSpec((D_SHARED, D_MODEL), const)],
        out_specs=pl.BlockSpec((TC, D_MODEL), lambda i: (i, 0)),
        out_shape=jax.ShapeDtypeStruct((n_tiles * TC, D_MODEL), F32),
        scratch_shapes=[pltpu.VMEM((2, TOP_K, TC * SUB, LANES), F32), pltpu.SemaphoreType.DMA((2,))],
        compiler_params=_params("arbitrary"),
        name="combine",
    )(dest_flat, dest_flat, eo, x1, h2, topw, mod3, g, wgu, wd)


def _qa_perm():
    cols = []
    for j in range(A_GROUP):
        for g in range(A_KV_HEADS):
            hd = g * A_GROUP + j
            cols.extend(range(hd * A_HEAD_DIM, (hd + 1) * A_HEAD_DIM))
    return np.asarray(cols, np.int32)


def _pad_heads(w, width, n_heads, offset, total=LANES):
    k = w.shape[0]
    stride = w.shape[1] // n_heads
    w3 = w.reshape(k, n_heads, stride)[:, :, offset:offset + width]
    w3 = jnp.pad(w3, ((0, 0), (0, 0), (0, total - width)))
    return w3.reshape(k, n_heads * total)


def kernel(x_prompt, x_sample, cache_k_swa, cache_v_swa, cache_ckv_mla, cache_krope_mla, c, c_ctx, w_ada, b_ada, norm_pre_mix, norm_post_mix, norm_pre_ffn, norm_post_ffn, w_in, sink_swa, q_norm_mla, kv_norm_mla, w_uq_mla, w_ukv_mla, w_o_swa, w_o_mla, w_out, router_w, router_bias, w_gate_exp, w_up_exp, w_down_exp, w_gate_sh, w_up_sh, w_down_sh):
    l = 0
    perm = _qa_perm()
    wi = w_in[l]
    kr_blk = jnp.pad(wi[:, _O_KR:_O_GATE], ((0, 0), (B_NOPE, LANES - B_NOPE - B_ROPE)))
    w1 = jnp.concatenate([wi[:, _O_QA:_O_KA][:, perm], wi[:, _O_KA:_O_KR], kr_blk], axis=1).astype(BF16)
    wg = wi[:, _O_GATE:].astype(BF16)
    wuq = _pad_heads(w_uq_mla[l], B_NOPE + B_ROPE, B_HEADS, 0).astype(BF16)
    wk = _pad_heads(w_ukv_mla[l], B_NOPE, B_HEADS, 0).astype(BF16)
    wv = w_ukv_mla[l].reshape(KV_LORA, B_HEADS, B_NOPE + B_V)[:, :, B_NOPE:].reshape(KV_LORA, B_HEADS * B_V).astype(BF16)
    woa = w_o_swa[l][perm, :].astype(BF16)
    wob = w_o_mla[l].astype(BF16)
    wout = w_out[l].astype(BF16)
    wr = router_w[l].astype(BF16)
    wgu_sh = jnp.concatenate([w_gate_sh[l], w_up_sh[l]], axis=1).astype(BF16)
    wd_sh = w_down_sh[l].astype(BF16)
    sink = sink_swa[l]
    row2 = lambda v: v.reshape(1, -1)

    cond8 = jnp.concatenate([c_ctx[None, :], c, jnp.zeros((8 - 1 - DEC_BATCH, D_MODEL), F32)], axis=0)
    mod3 = _modulation(cond8, w_ada[l], b_ada[l]).reshape(8, 1, 6 * D_MODEL)

    xp = x_prompt.reshape(T_CTX, D_MODEL)
    xs = x_sample.reshape(T_LAT, D_MODEL)
    per = DEC_SEQ // TM
    shared_in = (row2(norm_pre_mix[l]), w1, row2(q_norm_mla[l]), row2(kv_norm_mla[l]), wuq, wk, wv)
    qa_c, ka_c, va_c, ckv_c, kr_c, qb_c, kb_c, vb_c = _inproj(xp, mod3, lambda i: 0, *shared_in, None)
    qa_l, ka_l, va_l, qb_l, kb_l, vb_l = _inproj(xs, mod3, lambda i: 1 + i // per, *shared_in, _rope_tables())

    oa_c, ob_c = _ctx_attention(sink, qa_c, ka_c, va_c, qb_c, kb_c, vb_c)

    krp_cache = jnp.pad(cache_krope_mla[:, l].reshape(DEC_BATCH * PAST_LEN, B_ROPE),
                        ((0, 0), (B_NOPE, LANES - B_NOPE - B_ROPE)))
    kc_b, vc_b = _kv_expand(cache_ckv_mla[:, l].reshape(DEC_BATCH * PAST_LEN, KV_LORA), krp_cache, wk, wv)
    oa_l = _lat_swa(sink, qa_l, ka_l, va_l,
                    cache_k_swa[:, l].reshape(DEC_BATCH, PAST_LEN, LANES),
                    cache_v_swa[:, l].reshape(DEC_BATCH, PAST_LEN, LANES))
    ob_l = _lat_mla(qb_l, kb_l, vb_l, kc_b, vc_b)

    x1, h2, logits = _merge(xp, xs, oa_c, oa_l, ob_c, ob_l, mod3,
                            row2(norm_pre_mix[l]), row2(norm_post_mix[l]), row2(norm_pre_ffn[l]),
                            woa, wob, wout, wg, wr)

    topi_p, rank_p, topw_p, counts8 = _router(logits, row2(router_bias[l]))
    counts = counts8[0].astype(jnp.int32)

    padded = (counts + MOE_BLK - 1) // MOE_BLK * MOE_BLK
    pend = jnp.cumsum(padded)
    pstart = pend - padded
    dest = _dest(topi_p, rank_p, pstart.astype(F32).reshape(1, N_EXPERTS))[:, :TOP_K].reshape(-1)
    blk_start = (jnp.concatenate([pstart, pend[-1:]]) // MOE_BLK).astype(jnp.int32)

    x_sorted = _dispatch(dest, h2)
    eo = _experts(blk_start, x_sorted, w_gate_exp[l], w_up_exp[l], w_down_exp[l])

    g_post = row2(norm_post_ffn[l])
    per_c = DEC_SEQ // TC
    y_p = _combine(dest, eo, x1, h2, topw_p, mod3, lambda i: 0, g_post, wgu_sh, wd_sh, 0, T_CTX // TC)
    y_s = _combine(dest, eo, x1, h2, topw_p, mod3, lambda i: 1 + i // per_c, g_post, wgu_sh, wd_sh,
                   T_CTX // TC, T_LAT // TC)

    return (y_p.reshape(BATCH, SEQ, D_MODEL), y_s.reshape(DEC_BATCH, DEC_SEQ, D_MODEL),
            ka_c.reshape(BATCH, 1, SEQ, A_KV_HEADS, A_HEAD_DIM),
            va_c.reshape(BATCH, 1, SEQ, A_KV_HEADS, A_HEAD_DIM),
            ckv_c.reshape(BATCH, 1, SEQ, KV_LORA),
            kr_c[:, B_NOPE:B_NOPE + B_ROPE].reshape(BATCH, 1, SEQ, B_ROPE))
```

```python
import functools

import jax
import jax.numpy as jnp
import numpy as np
from jax import lax
from jax.experimental import pallas as pl
from jax.experimental.pallas import tpu as pltpu

D_MODEL = 1024
BATCH = 32
SEQ = 256
DEC_BATCH = 4
DEC_SEQ = 2048
PAST_LEN = 512
GRID_W = 64
WINDOW = 128
ROPE_BASE = 10000.0
EPS = 1e-6
NEG = -1e30
A_HEADS = 8
A_KV_HEADS = 2
A_GROUP = A_HEADS // A_KV_HEADS
A_HEAD_DIM = 64
B_HEADS = 8
B_NOPE = 64
B_ROPE = 32
B_V = 64
KV_LORA = 256
Q_LORA = 384
MLA_SCALE = (B_NOPE + B_ROPE) ** -0.5
N_EXPERTS = 256
TOP_K = 8
N_GROUPS = 8
TOPK_GROUPS = 4
D_EXPERT = 256
D_SHARED = 256
ROUTED_SCALE = 2.5

T_CTX = BATCH * SEQ
T_LAT = DEC_BATCH * DEC_SEQ
T_ALL = T_CTX + T_LAT

LANES = 128
TM = 512
TQ = 256
MOE_BLK = 256
N_SLOT_BLK = (T_ALL * TOP_K) // MOE_BLK + N_EXPERTS
N_SLOTS = N_SLOT_BLK * MOE_BLK
VMEM_LIMIT = 48 * 1024 * 1024

F32 = jnp.float32
BF16 = jnp.bfloat16

_O_QA = 0
_O_KA = _O_QA + A_HEADS * A_HEAD_DIM
_O_VA = _O_KA + A_KV_HEADS * A_HEAD_DIM
_O_QL = _O_VA + A_KV_HEADS * A_HEAD_DIM
_O_CKV = _O_QL + Q_LORA
_O_KR = _O_CKV + KV_LORA
_O_GATE = _O_KR + B_ROPE


def _params(*sem):
    return pltpu.CompilerParams(dimension_semantics=sem, vmem_limit_bytes=VMEM_LIMIT)


def _dot(a, b):
    return jnp.dot(a, b, preferred_element_type=F32)


def _dot_nt(a, b):
    return lax.dot_general(a, b, (((1,), (1,)), ((), ())), preferred_element_type=F32)


def _rms(x, g):
    return x * lax.rsqrt(jnp.mean(x * x, axis=-1, keepdims=True) + EPS) * g


def _sigmoid(x):
    return 1.0 / (1.0 + jnp.exp(-x))


SUB = 8
assert SUB * LANES == D_MODEL


def _load_rows(ref, n):
    return jnp.concatenate([ref[pl.ds(j, n, stride=SUB), :] for j in range(SUB)], axis=1)


def _store_rows(ref, val, n):
    for j in range(SUB):
        ref[pl.ds(j, n, stride=SUB), :] = val[:, j * LANES:(j + 1) * LANES]


def _silu(x):
    return x * _sigmoid(x)


def _mod_kernel(cond_ref, w_ref, b_ref, o_ref):
    s = _silu(cond_ref[...]).astype(BF16)
    o_ref[...] = _dot(s, w_ref[...].astype(BF16)) + b_ref[...]


def _modulation(cond8, w_ada, b_ada):
    tn = 768
    n = w_ada.shape[1]
    return pl.pallas_call(
        _mod_kernel,
        grid=(n // tn,),
        in_specs=[pl.BlockSpec((8, D_MODEL), lambda j: (0, 0)),
                  pl.BlockSpec((D_MODEL, tn), lambda j: (0, j)),
                  pl.BlockSpec((1, tn), lambda j: (0, j))],
        out_specs=pl.BlockSpec((8, tn), lambda j: (0, j)),
        out_shape=jax.ShapeDtypeStruct((8, n), F32),
        compiler_params=_params("arbitrary"),
        name="modulation",
    )(cond8, w_ada, b_ada.reshape(1, n))


def _rope(x, cos, sin, half):
    lane = lax.broadcasted_iota(jnp.int32, x.shape, 1)
    first = (lane & (2 * half - 1)) < half
    partner = jnp.where(first, pltpu.roll(x, LANES - half, 1), pltpu.roll(x, half, 1))
    return x * cos + partner * sin


def _rope_tables():
    t = np.arange(DEC_SEQ)
    row = (t // GRID_W).astype(np.float32)
    col = (t % GRID_W).astype(np.float32)

    def tables(d_rot):
        nf = d_rot // 4
        inv = np.float32(ROPE_BASE) ** (-np.arange(nf, dtype=np.float32) / np.float32(nf))
        ar = row[:, None] * inv
        ac = col[:, None] * inv
        cos = np.concatenate([np.cos(ar), np.cos(ar), np.cos(ac), np.cos(ac)], axis=1)
        sin = np.concatenate([-np.sin(ar), np.sin(ar), -np.sin(ac), np.sin(ac)], axis=1)
        return cos.astype(np.float32), sin.astype(np.float32)

    cos_a, sin_a = tables(A_HEAD_DIM)
    cos_a = np.concatenate([cos_a, cos_a], axis=1)
    sin_a = np.concatenate([sin_a, sin_a], axis=1)
    cos_b, sin_b = tables(B_ROPE)
    one = np.ones((DEC_SEQ, B_NOPE), np.float32)
    zero = np.zeros((DEC_SEQ, B_NOPE), np.float32)
    pad = LANES - B_NOPE - B_ROPE
    cos_b = np.concatenate([one, cos_b, one[:, :pad]], axis=1)
    sin_b = np.concatenate([zero, sin_b, zero[:, :pad]], axis=1)
    return tuple(jnp.asarray(a) for a in (cos_a, sin_a, cos_b, sin_b))


_N1 = 1536


def _inproj_kernel(latent, *refs):
    if latent:
        (x_ref, mod_ref, g_ref, w1_ref, qn_ref, kvn_ref, wuq_ref, wk_ref, wv_ref,
         ca_ref, sa_ref, cb_ref, sb_ref,
         qa_o, ka_o, va_o, qb_o, kb_o, vb_o) = refs
    else:
        (x_ref, mod_ref, g_ref, w1_ref, qn_ref, kvn_ref, wuq_ref, wk_ref, wv_ref,
         qa_o, ka_o, va_o, ckv_o, kr_o, qb_o, kb_o, vb_o) = refs
    x = x_ref[...]
    shift = mod_ref[:, 0:D_MODEL]
    scale = mod_ref[:, D_MODEL:2 * D_MODEL]
    h = _rms(x, g_ref[...]) * (1.0 + scale) + shift
    p = _dot(h.astype(BF16), w1_ref[...])
    qa = p[:, 0:512]
    ka = p[:, 512:640]
    va = p[:, 640:768]
    ql = p[:, 768:1152]
    ckv = p[:, 1152:1408]
    krp = p[:, 1408:1536]
    qn = _rms(ql, qn_ref[...])
    qb = _dot(qn.astype(BF16), wuq_ref[...])
    cn = _rms(ckv, kvn_ref[...])
    cnb = cn.astype(BF16)
    kn = _dot(cnb, wk_ref[...])
    vb = _dot(cnb, wv_ref[...])
    if latent:
        ca, sa, cb, sb = ca_ref[...], sa_ref[...], cb_ref[...], sb_ref[...]
        ka = _rope(ka, ca, sa, A_HEAD_DIM // 4)
        krp = _rope(krp, cb, sb, B_ROPE // 4)
        for j in range(4):
            blk = _rope(qa[:, j * LANES:(j + 1) * LANES], ca, sa, A_HEAD_DIM // 4)
            qa_o[:, j * LANES:(j + 1) * LANES] = (blk * (A_HEAD_DIM ** -0.5)).astype(BF16)
        for hd in range(B_HEADS):
            blk = _rope(qb[:, hd * LANES:(hd + 1) * LANES], cb, sb, B_ROPE // 4)
            qb_o[:, hd * LANES:(hd + 1) * LANES] = (blk * MLA_SCALE).astype(BF16)
        ka_o[...] = ka.astype(BF16)
        va_o[...] = va.astype(BF16)
    else:
        qa_o[...] = (qa * (A_HEAD_DIM ** -0.5)).astype(BF16)
        qb_o[...] = (qb * MLA_SCALE).astype(BF16)
        ka_o[...] = ka
        va_o[...] = va
        ckv_o[...] = cn
        kr_o[...] = krp
    for hd in range(B_HEADS):
        kb_o[:, hd * LANES:(hd + 1) * LANES] = (kn[:, hd * LANES:(hd + 1) * LANES] + krp).astype(BF16)
    vb_o[...] = vb.astype(BF16)


def _inproj(x, mod3, mod_row, g, w1, qn, kvn, wuq, wk, wv, tables):
    latent = tables is not None
    t = x.shape[0]
    n_tiles = t // TM
    row = lambda i: (i, 0)
    const = lambda i: (0, 0)
    in_specs = [pl.BlockSpec((TM, D_MODEL), row),
                pl.BlockSpec((None, 1, 6 * D_MODEL), lambda i: (mod_row(i), 0, 0)),
                pl.BlockSpec((1, D_MODEL), const),
                pl.BlockSpec((D_MODEL, _N1), const),
                pl.BlockSpec((1, Q_LORA), const),
                pl.BlockSpec((1, KV_LORA), const),
                pl.BlockSpec((Q_LORA, B_HEADS * LANES), const),
                pl.BlockSpec((KV_LORA, B_HEADS * LANES), const),
                pl.BlockSpec((KV_LORA, B_HEADS * B_V), const)]
    args = [x, mod3, g, w1, qn, kvn, wuq, wk, wv]
    sds = jax.ShapeDtypeStruct
    if latent:
        per = DEC_SEQ // TM
        tab = lambda i: (i % per, 0)
        in_specs += [pl.BlockSpec((TM, LANES), tab)] * 4
        args += list(tables)
        out_shape = [sds((t, 512), BF16), sds((t, LANES), BF16), sds((t, LANES), BF16),
                     sds((t, B_HEADS * LANES), BF16), sds((t, B_HEADS * LANES), BF16),
                     sds((t, B_HEADS * B_V), BF16)]
        widths = [512, LANES, LANES, B_HEADS * LANES, B_HEADS * LANES, B_HEADS * B_V]
    else:
        out_shape = [sds((t, 512), BF16), sds((t, LANES), F32), sds((t, LANES), F32),
                     sds((t, KV_LORA), F32), sds((t, LANES), F32),
                     sds((t, B_HEADS * LANES), BF16), sds((t, B_HEADS * LANES), BF16),
                     sds((t, B_HEADS * B_V), BF16)]
        widths = [512, LANES, LANES, KV_LORA, LANES, B_HEADS * LANES, B_HEADS * LANES, B_HEADS * B_V]
    out_specs = [pl.BlockSpec((TM, w), row) for w in widths]
    return pl.pallas_call(
        functools.partial(_inproj_kernel, latent),
        grid=(n_tiles,),
        in_specs=in_specs,
        out_specs=out_specs,
        out_shape=out_shape,
        compiler_params=_params("arbitrary"),
        name="inproj_latent" if latent else "inproj_ctx",
    )(*args)


def _kvexp_kernel(ckv_ref, krp_ref, wk_ref, wv_ref, kb_o, vb_o):
    cb = ckv_ref[...].astype(BF16)
    kn = _dot(cb, wk_ref[...])
    krp = krp_ref[...]
    for hd in range(B_HEADS):
        kb_o[:, hd * LANES:(hd + 1) * LANES] = (kn[:, hd * LANES:(hd + 1) * LANES] + krp).astype(BF16)
    vb_o[...] = _dot(cb, wv_ref[...]).astype(BF16)


def _kv_expand(ckv, krp, wk, wv):
    t = ckv.shape[0]
    row = lambda i: (i, 0)
    const = lambda i: (0, 0)
    return pl.pallas_call(
        _kvexp_kernel,
        grid=(t // TM,),
        in_specs=[pl.BlockSpec((TM, KV_LORA), row), pl.BlockSpec((TM, LANES), row),
                  pl.BlockSpec((KV_LORA, B_HEADS * LANES), const),
                  pl.BlockSpec((KV_LORA, B_HEADS * B_V), const)],
        out_specs=[pl.BlockSpec((TM, B_HEADS * LANES), row), pl.BlockSpec((TM, B_HEADS * B_V), row)],
        out_shape=[jax.ShapeDtypeStruct((t, B_HEADS * LANES), BF16),
                   jax.ShapeDtypeStruct((t, B_HEADS * B_V), BF16)],
        compiler_params=_params("arbitrary"),
        name="kv_expand",
    )(ckv, krp, wk, wv)


def _lo_mask(rows):
    return lax.broadcasted_iota(jnp.int32, (rows, LANES), 1) < (LANES // 2)


def _softmax_pv(parts, sink):
    m = None
    for s, _ in parts:
        mi = jnp.max(s, axis=-1, keepdims=True)
        m = mi if m is None else jnp.maximum(m, mi)
    if sink is not None:
        m = jnp.maximum(m, sink)
    den = None
    acc = None
    for s, v in parts:
        p = jnp.exp(s - m)
        di = jnp.sum(p, axis=-1, keepdims=True)
        den = di if den is None else den + di
        oi = _dot(p.astype(BF16), v)
        acc = oi if acc is None else acc + oi
    if sink is not None:
        den = den + jnp.exp(sink - m)
    return acc / den


def _ctx_attn_kernel(sink_ref, qa_ref, ka_ref, va_ref, qb_ref, kb_ref, vb_ref, oa_ref, ob_ref):
    lo = _lo_mask(SEQ)
    k = ka_ref[...].astype(BF16)
    v = va_ref[...].astype(BF16)
    for j in range(A_GROUP):
        q = qa_ref[:, j * LANES:(j + 1) * LANES]
        zero = jnp.zeros_like(q)
        o0 = _softmax_pv([(_dot_nt(jnp.where(lo, q, zero), k), v)], sink_ref[j])
        o1 = _softmax_pv([(_dot_nt(jnp.where(lo, zero, q), k), v)], sink_ref[A_GROUP + j])
        oa_ref[:, j * LANES:(j + 1) * LANES] = jnp.where(lo, o0, o1).astype(BF16)
    for j in range(B_HEADS // 2):
        vp = vb_ref[:, j * LANES:(j + 1) * LANES]
        outs = []
        for hd in (2 * j, 2 * j + 1):
            s = _dot_nt(qb_ref[:, hd * LANES:(hd + 1) * LANES], kb_ref[:, hd * LANES:(hd + 1) * LANES])
            outs.append(_softmax_pv([(s, vp)], None))
        ob_ref[:, j * LANES:(j + 1) * LANES] = jnp.where(lo, outs[0], outs[1]).astype(BF16)


def _ctx_attention(sink, qa, ka, va, qb, kb, vb):
    blk = lambda w: pl.BlockSpec((SEQ, w), lambda b: (b, 0))
    return pl.pallas_call(
        _ctx_attn_kernel,
        grid=(BATCH,),
        in_specs=[pl.BlockSpec(memory_space=pltpu.SMEM),
                  blk(512), blk(LANES), blk(LANES), blk(B_HEADS * LANES), blk(B_HEADS * LANES),
                  blk(B_HEADS * B_V)],
        out_specs=[blk(512), blk(512)],
        out_shape=[jax.ShapeDtypeStruct((T_CTX, 512), BF16), jax.ShapeDtypeStruct((T_CTX, 512), BF16)],
        compiler_params=_params("arbitrary"),
        name="ctx_attention",
    )(sink, qa, ka, va, qb, kb, vb)


_WIN = TQ + 2 * WINDOW


def _lat_swa_kernel(sink_ref, q_ref, k_ref, v_ref, kc_ref, vc_ref, o_ref):
    qi = pl.program_id(1)
    q0 = qi * TQ
    start = pl.multiple_of(jnp.clip(q0 - WINDOW, 0, DEC_SEQ - _WIN), WINDOW)
    kw = k_ref[pl.ds(start, _WIN), :]
    vw = v_ref[pl.ds(start, _WIN), :]
    kc = kc_ref[...].astype(BF16)
    vc = vc_ref[...].astype(BF16)
    qpos = q0 + lax.broadcasted_iota(jnp.int32, (TQ, _WIN), 0)
    kpos = start + lax.broadcasted_iota(jnp.int32, (TQ, _WIN), 1)
    valid = jnp.abs(qpos - kpos) <= WINDOW
    lo = _lo_mask(TQ)
    for j in range(A_GROUP):
        q = q_ref[:, j * LANES:(j + 1) * LANES]
        zero = jnp.zeros_like(q)
        outs = []
        for g, qh in ((0, jnp.where(lo, q, zero)), (1, jnp.where(lo, zero, q))):
            sw = jnp.where(valid, _dot_nt(qh, kw), NEG)
            sc = _dot_nt(qh, kc)
            outs.append(_softmax_pv([(sw, vw), (sc, vc)], sink_ref[g * A_GROUP + j]))
        o_ref[:, j * LANES:(j + 1) * LANES] = jnp.where(lo, outs[0], outs[1]).astype(BF16)


def _lat_swa(sink, qa, ka, va, kc, vc):
    nq = DEC_SEQ // TQ
    return pl.pallas_call(
        _lat_swa_kernel,
        grid=(DEC_BATCH, nq),
        in_specs=[pl.BlockSpec(memory_space=pltpu.SMEM),
                  pl.BlockSpec((TQ, 512), lambda b, i: (b * nq + i, 0)),
                  pl.BlockSpec((None, DEC_SEQ, LANES), lambda b, i: (b, 0, 0)),
                  pl.BlockSpec((None, DEC_SEQ, LANES), lambda b, i: (b, 0, 0)),
                  pl.BlockSpec((None, PAST_LEN, LANES), lambda b, i: (b, 0, 0)),
                  pl.BlockSpec((None, PAST_LEN, LANES), lambda b, i: (b, 0, 0))],
        out_specs=pl.BlockSpec((TQ, 512), lambda b, i: (b * nq + i, 0)),
        out_shape=jax.ShapeDtypeStruct((T_LAT, 512), BF16),
        compiler_params=_params("arbitrary", "arbitrary"),
        name="latent_swa",
    )(sink, qa, ka.reshape(DEC_BATCH, DEC_SEQ, LANES), va.reshape(DEC_BATCH, DEC_SEQ, LANES), kc, vc)


def _lat_mla_kernel(q_ref, kl_ref, vl_ref, kc_ref, vc_ref, o_ref):
    lo = _lo_mask(TQ)
    vl = vl_ref[...]
    vc = vc_ref[...]
    outs = []
    for hh in range(2):
        q = q_ref[:, hh * LANES:(hh + 1) * LANES]
        sc = _dot_nt(q, kc_ref[:, hh * LANES:(hh + 1) * LANES])
        sl = _dot_nt(q, kl_ref[:, hh * LANES:(hh + 1) * LANES])
        outs.append(_softmax_pv([(sc, vc), (sl, vl)], None))
    o_ref[...] = jnp.where(lo, outs[0], outs[1]).astype(BF16)


def _lat_mla(qb, kb, vb, kc, vc):
    nq = DEC_SEQ // TQ
    npair = B_HEADS // 2
    return pl.pallas_call(
        _lat_mla_kernel,
        grid=(DEC_BATCH, npair, nq),
        in_specs=[pl.BlockSpec((TQ, 2 * LANES), lambda b, p, i: (b * nq + i, p)),
                  pl.BlockSpec((None, DEC_SEQ, 2 * LANES), lambda b, p, i: (b, 0, p)),
                  pl.BlockSpec((None, DEC_SEQ, LANES), lambda b, p, i: (b, 0, p)),
                  pl.BlockSpec((None, PAST_LEN, 2 * LANES), lambda b, p, i: (b, 0, p)),
                  pl.BlockSpec((None, PAST_LEN, LANES), lambda b, p, i: (b, 0, p))],
        out_specs=pl.BlockSpec((TQ, LANES), lambda b, p, i: (b * nq + i, p)),
        out_shape=jax.ShapeDtypeStruct((T_LAT, 512), BF16),
        compiler_params=_params("arbitrary", "arbitrary", "arbitrary"),
        name="latent_mla",
    )(qb, kb.reshape(DEC_BATCH, DEC_SEQ, B_HEADS * LANES), vb.reshape(DEC_BATCH, DEC_SEQ, B_HEADS * B_V),
      kc.reshape(DEC_BATCH, PAST_LEN, B_HEADS * LANES), vc.reshape(DEC_BATCH, PAST_LEN, B_HEADS * B_V))


_N_CTX_TILES = T_CTX // TM


def _merge_kernel(xp_ref, xs_ref, oac_ref, oal_ref, obc_ref, obl_ref, mod_ref,
                  gpre_ref, gpost_ref, gffn_ref, woa_ref, wob_ref, wout_ref, wg_ref, wr_ref,
                  x1_o, h2_o, lg_o):
    def body(x_ref, oa_ref, ob_ref):
        x = x_ref[...]
        sh1 = mod_ref[:, 0:D_MODEL]
        sc1 = mod_ref[:, D_MODEL:2 * D_MODEL]
        g1 = mod_ref[:, 2 * D_MODEL:3 * D_MODEL]
        sh2 = mod_ref[:, 3 * D_MODEL:4 * D_MODEL]
        sc2 = mod_ref[:, 4 * D_MODEL:5 * D_MODEL]
        h = (_rms(x, gpre_ref[...]) * (1.0 + sc1) + sh1).astype(BF16)
        gates = _dot(h, wg_ref[...])
        ya = _dot(oa_ref[...], woa_ref[...])
        yb = _dot(ob_ref[...], wob_ref[...])
        y = _sigmoid(gates[:, :D_MODEL]) * ya + _sigmoid(gates[:, D_MODEL:]) * yb
        mix = _dot(y.astype(BF16), wout_ref[...])
        x1 = x + g1 * _rms(mix, gpost_ref[...])
        x1_o[...] = x1
        h2 = _rms(x1, gffn_ref[...]) * (1.0 + sc2) + sh2
        _store_rows(h2_o, h2, TM)
        lg_o[...] = _dot(h2.astype(BF16), wr_ref[...])

    is_ctx = pl.program_id(0) < _N_CTX_TILES

    @pl.when(is_ctx)
    def _():
        body(xp_ref, oac_ref, obc_ref)

    @pl.when(jnp.logical_not(is_ctx))
    def _():
        body(xs_ref, oal_ref, obl_ref)


def _mod_row_all(i):
    per = DEC_SEQ // TM
    return jnp.where(i < _N_CTX_TILES, 0, 1 + (jnp.maximum(i - _N_CTX_TILES, 0)) // per)


def _merge(xp, xs, oac, oal, obc, obl, mod3, gpre, gpost, gffn, woa, wob, wout, wg, wr):
    ctx = lambda i: (jnp.minimum(i, _N_CTX_TILES - 1), 0)
    lat = lambda i: (jnp.maximum(i - _N_CTX_TILES, 0), 0)
    row = lambda i: (i, 0)
    const = lambda i: (0, 0)
    return pl.pallas_call(
        _merge_kernel,
        grid=(T_ALL // TM,),
        in_specs=[pl.BlockSpec((TM, D_MODEL), ctx), pl.BlockSpec((TM, D_MODEL), lat),
                  pl.BlockSpec((TM, 512), ctx), pl.BlockSpec((TM, 512), lat),
                  pl.BlockSpec((TM, 512), ctx), pl.BlockSpec((TM, 512), lat),
                  pl.BlockSpec((None, 1, 6 * D_MODEL), lambda i: (_mod_row_all(i), 0, 0)),
                  pl.BlockSpec((1, D_MODEL), const), pl.BlockSpec((1, D_MODEL), const),
                  pl.BlockSpec((1, D_MODEL), const),
                  pl.BlockSpec((512, D_MODEL), const), pl.BlockSpec((512, D_MODEL), const),
                  pl.BlockSpec((D_MODEL, D_MODEL), const), pl.BlockSpec((D_MODEL, 2 * D_MODEL), const),
                  pl.BlockSpec((D_MODEL, N_EXPERTS), const)],
        out_specs=[pl.BlockSpec((TM, D_MODEL), row), pl.BlockSpec((TM * SUB, LANES), row),
                   pl.BlockSpec((TM, N_EXPERTS), row)],
        out_shape=[jax.ShapeDtypeStruct((T_ALL, D_MODEL), F32), jax.ShapeDtypeStruct((T_ALL * SUB, LANES), F32),
                   jax.ShapeDtypeStruct((T_ALL, N_EXPERTS), F32)],
        compiler_params=_params("arbitrary"),
        name="merge",
    )(xp, xs, oac, oal, obc, obl, mod3, gpre, gpost, gffn, woa, wob, wout, wg, wr)


_PER_GROUP = N_EXPERTS // N_GROUPS


def _router_kernel(lg_ref, bias_ref, topi_o, rank_o, topw_o, cnt_o, cnt_s):
    @pl.when(pl.program_id(0) == 0)
    def _():
        cnt_s[...] = jnp.zeros_like(cnt_s)

    scores = _sigmoid(lg_ref[...])
    sel = scores + bias_ref[...]
    lane_i = lax.broadcasted_iota(jnp.int32, (TM, N_EXPERTS), 1)
    lane = lane_i.astype(F32)
    grp = lax.shift_right_logical(lane_i, int(np.log2(_PER_GROUP)))
    ninf = -jnp.inf
    big = float(N_EXPERTS)
    gs = []
    for g in range(N_GROUPS):
        sg = jnp.where(grp == g, sel, ninf)
        m1 = jnp.max(sg, axis=-1, keepdims=True)
        i1 = jnp.min(jnp.where(sg == m1, lane, big), axis=-1, keepdims=True)
        m2 = jnp.max(jnp.where(lane == i1, ninf, sg), axis=-1, keepdims=True)
        gs.append(m1 + m2)
    keep = jnp.zeros((TM, N_EXPERTS), F32)
    for g in range(N_GROUPS):
        beat = jnp.zeros((TM, 1), F32)
        for g2 in range(N_GROUPS):
            if g2 == g:
                continue
            better = (gs[g2] >= gs[g]) if g2 < g else (gs[g2] > gs[g])
            beat = beat + jnp.where(better, 1.0, 0.0)
        keep = jnp.where(grp == g, jnp.where(beat < TOPK_GROUPS, 1.0, 0.0), keep)
    cur = jnp.where(keep > 0.5, sel, NEG)
    lane8 = lax.broadcasted_iota(jnp.int32, (TM, LANES), 1)
    topi = jnp.zeros((TM, LANES), F32)
    topw = jnp.zeros((TM, LANES), F32)
    chosen = jnp.zeros((TM, N_EXPERTS), F32)
    idxs = []
    for k in range(TOP_K):
        m = jnp.max(cur, axis=-1, keepdims=True)
        idx = jnp.min(jnp.where(cur == m, lane, big), axis=-1, keepdims=True)
        hit = lane == idx
        w = jnp.sum(jnp.where(hit, scores, 0.0), axis=-1, keepdims=True)
        cur = jnp.where(hit, ninf, cur)
        chosen = jnp.where(hit, 1.0, chosen)
        topi = jnp.where(lane8 == k, idx, topi)
        topw = jnp.where(lane8 == k, w, topw)
        idxs.append(idx)
    wsum = jnp.sum(topw, axis=-1, keepdims=True)
    topw_o[...] = topw / wsum * ROUTED_SCALE
    topi_o[...] = topi.astype(jnp.int32)
    r_i = lax.broadcasted_iota(jnp.int32, (TM, TM), 0)
    c_i = lax.broadcasted_iota(jnp.int32, (TM, TM), 1)
    tri = jnp.where(c_i < r_i, 1.0, 0.0).astype(BF16)
    prefix = _dot(tri, chosen.astype(BF16)) + cnt_s[0:1, :]
    rank = jnp.zeros((TM, LANES), jnp.int32)
    for k in range(TOP_K):
        r = jnp.sum(jnp.where(lane == idxs[k], prefix, 0.0), axis=-1, keepdims=True)
        rank = jnp.where(lane8 == k, r.astype(jnp.int32), rank)
    rank_o[...] = rank
    total = cnt_s[...] + jnp.sum(chosen, axis=0, keepdims=True)
    cnt_s[...] = total
    cnt_o[...] = total


def _router(logits, bias):
    row = lambda i: (i, 0)
    const = lambda i: (0, 0)
    return pl.pallas_call(
        _router_kernel,
        grid=(T_ALL // TM,),
        in_specs=[pl.BlockSpec((TM, N_EXPERTS), row), pl.BlockSpec((1, N_EXPERTS), const)],
        out_specs=[pl.BlockSpec((TM, LANES), row), pl.BlockSpec((TM, LANES), row),
                   pl.BlockSpec((TM, LANES), row), pl.BlockSpec((8, N_EXPERTS), const)],
        out_shape=[jax.ShapeDtypeStruct((T_ALL, LANES), jnp.int32),
                   jax.ShapeDtypeStruct((T_ALL, LANES), jnp.int32),
                   jax.ShapeDtypeStruct((T_ALL, LANES), F32),
                   jax.ShapeDtypeStruct((8, N_EXPERTS), F32)],
        scratch_shapes=[pltpu.VMEM((8, N_EXPERTS), F32)],
        compiler_params=_params("arbitrary"),
        name="router",
    )(logits, bias)


def _dest_kernel(topi_ref, rank_ref, ps_ref, o_ref):
    lane = lax.broadcasted_iota(jnp.int32, (TM, N_EXPERTS), 1)
    lane8 = lax.broadcasted_iota(jnp.int32, (TM, LANES), 1)
    ps = ps_ref[...]
    topi = topi_ref[...]
    acc = jnp.zeros((TM, LANES), F32)
    for k in range(TOP_K):
        start = jnp.sum(jnp.where(lane == topi[:, k:k + 1], ps, 0.0), axis=-1, keepdims=True)
        acc = jnp.where(lane8 == k, start, acc)
    o_ref[...] = rank_ref[...] + acc.astype(jnp.int32)


def _dest(topi_p, rank_p, pstart):
    row = lambda i: (i, 0)
    return pl.pallas_call(
        _dest_kernel,
        grid=(T_ALL // TM,),
        in_specs=[pl.BlockSpec((TM, LANES), row), pl.BlockSpec((TM, LANES), row),
                  pl.BlockSpec((1, N_EXPERTS), lambda i: (0, 0))],
        out_specs=pl.BlockSpec((TM, LANES), row),
        out_shape=jax.ShapeDtypeStruct((T_ALL, LANES), jnp.int32),
        compiler_params=_params("arbitrary"),
        name="dest",
    )(topi_p, rank_p, pstart)


TD = 512
ROW_UNROLL = 4


def _dispatch_kernel(dest_ref, h_ref, xs_hbm, sem):
    def row(t, carry):
        src = h_ref.at[pl.ds(pl.multiple_of(t * SUB, SUB), SUB)]
        for k in range(TOP_K):
            d = dest_ref[t * TOP_K + k]
            pltpu.make_async_copy(src, xs_hbm.at[pl.ds(pl.multiple_of(d * SUB, SUB), SUB)],
                                  sem).start(priority=k % 2)
        return carry

    lax.fori_loop(0, TD, row, 0, unroll=ROW_UNROLL)
    for _ in range(TOP_K):
        pltpu.make_async_copy(h_ref, xs_hbm.at[pl.ds(0, TD * SUB)], sem).wait()


def _dispatch(dest_flat, h2):
    return pl.pallas_call(
        _dispatch_kernel,
        grid=(T_ALL // TD,),
        in_specs=[pl.BlockSpec((TD * TOP_K,), lambda i: (i,), memory_space=pltpu.SMEM),
                  pl.BlockSpec((TD * SUB, LANES), lambda i: (i, 0))],
        out_specs=pl.BlockSpec(memory_space=pl.ANY),
        out_shape=jax.ShapeDtypeStruct((N_SLOTS * SUB, LANES), F32),
        scratch_shapes=[pltpu.SemaphoreType.DMA(())],
        compiler_params=_params("arbitrary"),
        name="dispatch",
    )(dest_flat, h2)


_BLK_TILES = MOE_BLK * SUB
NBX = 4
NBO = 3


def _expert_kernel(bs_ref, x_hbm, wg_ref, wu_ref, wd_ref, o_hbm, wgu_s, wd_s, xbuf, obuf, xsem, osem):
    e = pl.program_id(0)
    b0 = bs_ref[e]
    b1 = bs_ref[e + 1]
    n_used = bs_ref[N_EXPERTS]

    def xcopy(g):
        s = lax.rem(g, NBX)
        return pltpu.make_async_copy(x_hbm.at[pl.ds(pl.multiple_of(g * _BLK_TILES, _BLK_TILES), _BLK_TILES)],
                                     xbuf.at[s], xsem.at[s])

    def ocopy(g):
        s = lax.rem(g, NBO)
        return pltpu.make_async_copy(obuf.at[s],
                                     o_hbm.at[pl.ds(pl.multiple_of(g * _BLK_TILES, _BLK_TILES), _BLK_TILES)],
                                     osem.at[s])

    @pl.when(e == 0)
    def _():
        for j in range(NBX - 1):
            @pl.when(j < n_used)
            def _():
                xcopy(j).start()

    @pl.when(b1 > b0)
    def _():
        wgu_s[:, 0:D_EXPERT] = wg_ref[...].astype(BF16)
        wgu_s[:, D_EXPERT:2 * D_EXPERT] = wu_ref[...].astype(BF16)
        wd_s[...] = wd_ref[...].astype(BF16)

    def block(g, carry):
        @pl.when(g + NBX - 1 < n_used)
        def _():
            xcopy(g + NBX - 1).start()

        xcopy(g).wait()

        @pl.when(g >= NBO)
        def _():
            ocopy(g - NBO).wait()

        gu = _dot(_load_rows(xbuf.at[lax.rem(g, NBX)], MOE_BLK).astype(BF16), wgu_s[...])
        a = _silu(gu[:, :D_EXPERT]) * gu[:, D_EXPERT:]
        _store_rows(obuf.at[lax.rem(g, NBO)], _dot(a.astype(BF16), wd_s[...]), MOE_BLK)
        ocopy(g).start(priority=1)
        return carry

    lax.fori_loop(b0, b1, block, 0)

    @pl.when(e == N_EXPERTS - 1)
    def _():
        for j in range(NBO, 0, -1):
            @pl.when(n_used >= j)
            def _():
                ocopy(n_used - j).wait()


def _experts(blk_start, x_sorted, w_gate, w_up, w_down):
    wsel = lambda e, bs: (e, 0, 0)
    grid_spec = pltpu.PrefetchScalarGridSpec(
        num_scalar_prefetch=1,
        grid=(N_EXPERTS,),
        in_specs=[pl.BlockSpec(memory_space=pl.ANY),
                  pl.BlockSpec((None, D_MODEL, D_EXPERT), wsel),
                  pl.BlockSpec((None, D_MODEL, D_EXPERT), wsel),
                  pl.BlockSpec((None, D_EXPERT, D_MODEL), wsel)],
        out_specs=pl.BlockSpec(memory_space=pl.ANY),
        scratch_shapes=[pltpu.VMEM((D_MODEL, 2 * D_EXPERT), BF16), pltpu.VMEM((D_EXPERT, D_MODEL), BF16),
                        pltpu.VMEM((NBX, _BLK_TILES, LANES), F32), pltpu.VMEM((NBO, _BLK_TILES, LANES), F32),
                        pltpu.SemaphoreType.DMA((NBX,)), pltpu.SemaphoreType.DMA((NBO,))],
    )
    return pl.pallas_call(
        _expert_kernel,
        grid_spec=grid_spec,
        out_shape=jax.ShapeDtypeStruct((N_SLOTS * SUB, LANES), F32),
        compiler_params=_params("arbitrary"),
        name="experts",
    )(blk_start, x_sorted, w_gate, w_up, w_down)


TC = 256


def _combine_kernel(dcur_ref, dnext_ref, eo_hbm, x1_ref, h2_ref, w_ref, mod_ref, g_ref, wgu_ref, wd_ref,
                    o_ref, buf, sem):
    i = pl.program_id(0)
    n = pl.num_programs(0)
    slot = lax.rem(i, 2)

    def gather(d_ref, s):
        def row(t, carry):
            for k in range(TOP_K):
                d = d_ref[t * TOP_K + k]
                pltpu.make_async_copy(eo_hbm.at[pl.ds(pl.multiple_of(d * SUB, SUB), SUB)],
                                      buf.at[s, k, pl.ds(pl.multiple_of(t * SUB, SUB), SUB)],
                                      sem.at[s]).start(priority=k % 2)
            return carry
        lax.fori_loop(0, TC, row, 0, unroll=ROW_UNROLL)

    @pl.when(i == 0)
    def _():
        gather(dcur_ref, 0)

    @pl.when(i + 1 < n)
    def _():
        gather(dnext_ref, 1 - slot)

    gate = mod_ref[:, 5 * D_MODEL:6 * D_MODEL]
    gu = _dot(_load_rows(h2_ref, TC).astype(BF16), wgu_ref[...])
    a = _silu(gu[:, :D_SHARED]) * gu[:, D_SHARED:]
    moe = _dot(a.astype(BF16), wd_ref[...])
    for k in range(TOP_K):
        pltpu.make_async_copy(eo_hbm.at[pl.ds(0, TC * SUB)], buf.at[slot, k], sem.at[slot]).wait()
    w = w_ref[...]
    for k in range(TOP_K):
        moe = moe + _load_rows(buf.at[slot, k], TC) * w[:, k:k + 1]
    o_ref[...] = x1_ref[...] + gate * _rms(moe, g_ref[...])


def _combine(dest_flat, eo, x1, h2, topw, mod3, mod_row, g, wgu, wd, tile0, n_tiles):
    row = lambda i: (tile0 + i, 0)
    const = lambda i: (0, 0)
    blk = TC * TOP_K
    return pl.pallas_call(
        _combine_kernel,
        grid=(n_tiles,),
        in_specs=[pl.BlockSpec((blk,), lambda i: (tile0 + i,), memory_space=pltpu.SMEM),
                  pl.BlockSpec((blk,), lambda i: (tile0 + jnp.minimum(i + 1, n_tiles - 1),),
                               memory_space=pltpu.SMEM),
                  pl.BlockSpec(memory_space=pl.ANY),
                  pl.BlockSpec((TC, D_MODEL), row), pl.BlockSpec((TC * SUB, LANES), row),
                  pl.BlockSpec((TC, LANES), row),
                  pl.BlockSpec((None, 1, 6 * D_MODEL), lambda i: (mod_row(i), 0, 0)),
                  pl.BlockSpec((1, D_MODEL), const),
                  pl.BlockSpec((D_MODEL, 2 * D_SHARED), const), pl.BlockSpec((D_SHARED, D_MODEL), const)],
        out_specs=pl.BlockSpec((TC, D_MODEL), lambda i: (i, 0)),
        out_shape=jax.ShapeDtypeStruct((n_tiles * TC, D_MODEL), F32),
        scratch_shapes=[pltpu.VMEM((2, TOP_K, TC * SUB, LANES), F32), pltpu.SemaphoreType.DMA((2,))],
        compiler_params=_params("arbitrary"),
        name="combine",
    )(dest_flat, dest_flat, eo, x1, h2, topw, mod3, g, wgu, wd)


def _qa_perm():
    cols = []
    for j in range(A_GROUP):
        for g in range(A_KV_HEADS):
            hd = g * A_GROUP + j
            cols.extend(range(hd * A_HEAD_DIM, (hd + 1) * A_HEAD_DIM))
    return np.asarray(cols, np.int32)


def _pad_heads(w, width, n_heads, offset, total=LANES):
    k = w.shape[0]
    stride = w.shape[1] // n_heads
    w3 = w.reshape(k, n_heads, stride)[:, :, offset:offset + width]
    w3 = jnp.pad(w3, ((0, 0), (0, 0), (0, total - width)))
    return w3.reshape(k, n_heads * total)


def kernel(x_prompt, x_sample, cache_k_swa, cache_v_swa, cache_ckv_mla, cache_krope_mla, c, c_ctx, w_ada, b_ada, norm_pre_mix, norm_post_mix, norm_pre_ffn, norm_post_ffn, w_in, sink_swa, q_norm_mla, kv_norm_mla, w_uq_mla, w_ukv_mla, w_o_swa, w_o_mla, w_out, router_w, router_bias, w_gate_exp, w_up_exp, w_down_exp, w_gate_sh, w_up_sh, w_down_sh):
    l = 0
    perm = _qa_perm()
    wi = w_in[l]
    kr_blk = jnp.pad(wi[:, _O_KR:_O_GATE], ((0, 0), (B_NOPE, LANES - B_NOPE - B_ROPE)))
    w1 = jnp.concatenate([wi[:, _O_QA:_O_KA][:, perm], wi[:, _O_KA:_O_KR], kr_blk], axis=1).astype(BF16)
    wg = wi[:, _O_GATE:].astype(BF16)
    wuq = _pad_heads(w_uq_mla[l], B_NOPE + B_ROPE, B_HEADS, 0).astype(BF16)
    wk = _pad_heads(w_ukv_mla[l], B_NOPE, B_HEADS, 0).astype(BF16)
    wv = w_ukv_mla[l].reshape(KV_LORA, B_HEADS, B_NOPE + B_V)[:, :, B_NOPE:].reshape(KV_LORA, B_HEADS * B_V).astype(BF16)
    woa = w_o_swa[l][perm, :].astype(BF16)
    wob = w_o_mla[l].astype(BF16)
    wout = w_out[l].astype(BF16)
    wr = router_w[l].astype(BF16)
    wgu_sh = jnp.concatenate([w_gate_sh[l], w_up_sh[l]], axis=1).astype(BF16)
    wd_sh = w_down_sh[l].astype(BF16)
    sink = sink_swa[l]
    row2 = lambda v: v.reshape(1, -1)

    cond8 = jnp.concatenate([c_ctx[None, :], c, jnp.zeros((8 - 1 - DEC_BATCH, D_MODEL), F32)], axis=0)
    mod3 = _modulation(cond8, w_ada[l], b_ada[l]).reshape(8, 1, 6 * D_MODEL)

    xp = x_prompt.reshape(T_CTX, D_MODEL)
    xs = x_sample.reshape(T_LAT, D_MODEL)
    per = DEC_SEQ // TM
    shared_in = (row2(norm_pre_mix[l]), w1, row2(q_norm_mla[l]), row2(kv_norm_mla[l]), wuq, wk, wv)
    qa_c, ka_c, va_c, ckv_c, kr_c, qb_c, kb_c, vb_c = _inproj(xp, mod3, lambda i: 0, *shared_in, None)
    qa_l, ka_l, va_l, qb_l, kb_l, vb_l = _inproj(xs, mod3, lambda i: 1 + i // per, *shared_in, _rope_tables())

    oa_c, ob_c = _ctx_attention(sink, qa_c, ka_c, va_c, qb_c, kb_c, vb_c)

    krp_cache = jnp.pad(cache_krope_mla[:, l].reshape(DEC_BATCH * PAST_LEN, B_ROPE),
                        ((0, 0), (B_NOPE, LANES - B_NOPE - B_ROPE)))
    kc_b, vc_b = _kv_expand(cache_ckv_mla[:, l].reshape(DEC_BATCH * PAST_LEN, KV_LORA), krp_cache, wk, wv)
    oa_l = _lat_swa(sink, qa_l, ka_l, va_l,
                    cache_k_swa[:, l].reshape(DEC_BATCH, PAST_LEN, LANES),
                    cache_v_swa[:, l].reshape(DEC_BATCH, PAST_LEN, LANES))
    ob_l = _lat_mla(qb_l, kb_l, vb_l, kc_b, vc_b)

    x1, h2, logits = _merge(xp, xs, oa_c, oa_l, ob_c, ob_l, mod3,
                            row2(norm_pre_mix[l]), row2(norm_post_mix[l]), row2(norm_pre_ffn[l]),
                            woa, wob, wout, wg, wr)

    topi_p, rank_p, topw_p, counts8 = _router(logits, row2(router_bias[l]))
    counts = counts8[0].astype(jnp.int32)

    padded = (counts + MOE_BLK - 1) // MOE_BLK * MOE_BLK
    pend = jnp.cumsum(padded)
    pstart = pend - padded
    dest = _dest(topi_p, rank_p, pstart.astype(F32).reshape(1, N_EXPERTS))[:, :TOP_K].reshape(-1)
    blk_start = (jnp.concatenate([pstart, pend[-1:]]) // MOE_BLK).astype(jnp.int32)

    x_sorted = _dispatch(dest, h2)
    eo = _experts(blk_start, x_sorted, w_gate_exp[l], w_up_exp[l], w_down_exp[l])

    g_post = row2(norm_post_ffn[l])
    per_c = DEC_SEQ // TC
    y_p = _combine(dest, eo, x1, h2, topw_p, mod3, lambda i: 0, g_post, wgu_sh, wd_sh, 0, T_CTX // TC)
    y_s = _combine(dest, eo, x1, h2, topw_p, mod3, lambda i: 1 + i // per_c, g_post, wgu_sh, wd_sh,
                   T_CTX // TC, T_LAT // TC)

    return (y_p.reshape(BATCH, SEQ, D_MODEL), y_s.reshape(DEC_BATCH, DEC_SEQ, D_MODEL),
            ka_c.reshape(BATCH, 1, SEQ, A_KV_HEADS, A_HEAD_DIM),
            va_c.reshape(BATCH, 1, SEQ, A_KV_HEADS, A_HEAD_DIM),
            ckv_c.reshape(BATCH, 1, SEQ, KV_LORA),
            kr_c[:, B_NOPE:B_NOPE + B_ROPE].reshape(BATCH, 1, SEQ, B_ROPE))
```

```python
import functools

import jax
import jax.numpy as jnp
import numpy as np
from jax import lax
from jax.experimental import pallas as pl
from jax.experimental.pallas import tpu as pltpu

D_MODEL = 1024
BATCH = 32
SEQ = 256
DEC_BATCH = 4
DEC_SEQ = 2048
PAST_LEN = 512
GRID_W = 64
WINDOW = 128
ROPE_BASE = 10000.0
EPS = 1e-6
NEG = -1e30
A_HEADS = 8
A_KV_HEADS = 2
A_GROUP = A_HEADS // A_KV_HEADS
A_HEAD_DIM = 64
B_HEADS = 8
B_NOPE = 64
B_ROPE = 32
B_V = 64
KV_LORA = 256
Q_LORA = 384
MLA_SCALE = (B_NOPE + B_ROPE) ** -0.5
LOG2E = 1.4426950408889634
N_EXPERTS = 256
TOP_K = 8
N_GROUPS = 8
TOPK_GROUPS = 4
D_EXPERT = 256
D_SHARED = 256
ROUTED_SCALE = 2.5

T_CTX = BATCH * SEQ
T_LAT = DEC_BATCH * DEC_SEQ
T_ALL = T_CTX + T_LAT

LANES = 128
TM = 512
TQ = 256
MOE_BLK = 128
N_SLOT_BLK = (T_ALL * TOP_K) // MOE_BLK + N_EXPERTS
N_SLOTS = N_SLOT_BLK * MOE_BLK
VMEM_LIMIT = 48 * 1024 * 1024

F32 = jnp.float32
BF16 = jnp.bfloat16

_O_QA = 0
_O_KA = _O_QA + A_HEADS * A_HEAD_DIM
_O_VA = _O_KA + A_KV_HEADS * A_HEAD_DIM
_O_QL = _O_VA + A_KV_HEADS * A_HEAD_DIM
_O_CKV = _O_QL + Q_LORA
_O_KR = _O_CKV + KV_LORA
_O_GATE = _O_KR + B_ROPE


def _params(*sem):
    return pltpu.CompilerParams(dimension_semantics=sem, vmem_limit_bytes=VMEM_LIMIT)


def _dot(a, b):
    return jnp.dot(a, b, preferred_element_type=F32)


def _dot_nt(a, b):
    return lax.dot_general(a, b, (((1,), (1,)), ((), ())), preferred_element_type=F32)


def _rms(x, g):
    return x * lax.rsqrt(jnp.mean(x * x, axis=-1, keepdims=True) + EPS) * g


def _sigmoid(x):
    return 1.0 / (1.0 + jnp.exp(-x))


SUB = 8
assert SUB * LANES == D_MODEL


def _load_rows(ref, n):
    return jnp.concatenate([ref[pl.ds(j, n, stride=SUB), :] for j in range(SUB)], axis=1)


def _store_rows(ref, val, n):
    for j in range(SUB):
        ref[pl.ds(j, n, stride=SUB), :] = val[:, j * LANES:(j + 1) * LANES]


def _silu(x):
    return x * _sigmoid(x)


def _mod_kernel(cond_ref, w_ref, b_ref, o_ref):
    s = _silu(cond_ref[...]).astype(BF16)
    o_ref[...] = _dot(s, w_ref[...].astype(BF16)) + b_ref[...]


def _modulation(cond8, w_ada, b_ada):
    tn = 768
    n = w_ada.shape[1]
    return pl.pallas_call(
        _mod_kernel,
        grid=(n // tn,),
        in_specs=[pl.BlockSpec((8, D_MODEL), lambda j: (0, 0)),
                  pl.BlockSpec((D_MODEL, tn), lambda j: (0, j)),
                  pl.BlockSpec((1, tn), lambda j: (0, j))],
        out_specs=pl.BlockSpec((8, tn), lambda j: (0, j)),
        out_shape=jax.ShapeDtypeStruct((8, n), F32),
        compiler_params=_params("arbitrary"),
        name="modulation",
    )(cond8, w_ada, b_ada.reshape(1, n))


def _rope(x, cos, sin, half):
    lane = lax.broadcasted_iota(jnp.int32, x.shape, 1)
    first = (lane & (2 * half - 1)) < half
    partner = jnp.where(first, pltpu.roll(x, LANES - half, 1), pltpu.roll(x, half, 1))
    return x * cos + partner * sin


def _rope_tables():
    t = np.arange(DEC_SEQ)
    row = (t // GRID_W).astype(np.float32)
    col = (t % GRID_W).astype(np.float32)

    def tables(d_rot):
        nf = d_rot // 4
        inv = np.float32(ROPE_BASE) ** (-np.arange(nf, dtype=np.float32) / np.float32(nf))
        ar = row[:, None] * inv
        ac = col[:, None] * inv
        cos = np.concatenate([np.cos(ar), np.cos(ar), np.cos(ac), np.cos(ac)], axis=1)
        sin = np.concatenate([-np.sin(ar), np.sin(ar), -np.sin(ac), np.sin(ac)], axis=1)
        return cos.astype(np.float32), sin.astype(np.float32)

    cos_a, sin_a = tables(A_HEAD_DIM)
    cos_a = np.concatenate([cos_a, cos_a], axis=1)
    sin_a = np.concatenate([sin_a, sin_a], axis=1)
    cos_b, sin_b = tables(B_ROPE)
    one = np.ones((DEC_SEQ, B_NOPE), np.float32)
    zero = np.zeros((DEC_SEQ, B_NOPE), np.float32)
    pad = LANES - B_NOPE - B_ROPE
    cos_b = np.concatenate([one, cos_b, one[:, :pad]], axis=1)
    sin_b = np.concatenate([zero, sin_b, zero[:, :pad]], axis=1)
    return tuple(jnp.asarray(a) for a in (cos_a, sin_a, cos_b, sin_b))


_N1 = 1536


def _inproj_kernel(latent, *refs):
    if latent:
        (x_ref, mod_ref, g_ref, w1_ref, qn_ref, kvn_ref, wuq_ref, wk_ref, wv_ref,
         ca_ref, sa_ref, cb_ref, sb_ref,
         qa_o, ka_o, va_o, qb_o, kb_o, vb_o) = refs
    else:
        (x_ref, mod_ref, g_ref, w1_ref, qn_ref, kvn_ref, wuq_ref, wk_ref, wv_ref,
         qa_o, ka_o, va_o, ckv_o, kr_o, qb_o, kb_o, vb_o) = refs
    x = x_ref[...]
    shift = mod_ref[:, 0:D_MODEL]
    scale = mod_ref[:, D_MODEL:2 * D_MODEL]
    h = _rms(x, g_ref[...]) * (1.0 + scale) + shift
    p = _dot(h.astype(BF16), w1_ref[...])
    qa = p[:, 0:512]
    ka = p[:, 512:640]
    va = p[:, 640:768]
    ql = p[:, 768:1152]
    ckv = p[:, 1152:1408]
    krp = p[:, 1408:1536]
    qn = _rms(ql, qn_ref[...])
    qb = _dot(qn.astype(BF16), wuq_ref[...])
    cn = _rms(ckv, kvn_ref[...])
    cnb = cn.astype(BF16)
    kn = _dot(cnb, wk_ref[...])
    vb = _dot(cnb, wv_ref[...])
    if latent:
        ca, sa, cb, sb = ca_ref[...], sa_ref[...], cb_ref[...], sb_ref[...]
        ka = _rope(ka, ca, sa, A_HEAD_DIM // 4)
        krp = _rope(krp, cb, sb, B_ROPE // 4)
        for j in range(4):
            blk = _rope(qa[:, j * LANES:(j + 1) * LANES], ca, sa, A_HEAD_DIM // 4)
            qa_o[:, j * LANES:(j + 1) * LANES] = (blk * (A_HEAD_DIM ** -0.5 * LOG2E)).astype(BF16)
        for hd in range(B_HEADS):
            blk = _rope(qb[:, hd * LANES:(hd + 1) * LANES], cb, sb, B_ROPE // 4)
            qb_o[:, hd * LANES:(hd + 1) * LANES] = (blk * (MLA_SCALE * LOG2E)).astype(BF16)
        ka_o[...] = ka.astype(BF16)
        va_o[...] = va.astype(BF16)
    else:
        qa_o[...] = (qa * (A_HEAD_DIM ** -0.5 * LOG2E)).astype(BF16)
        qb_o[...] = (qb * (MLA_SCALE * LOG2E)).astype(BF16)
        ka_o[...] = ka
        va_o[...] = va
        ckv_o[...] = cn
        kr_o[...] = krp
    for hd in range(B_HEADS):
        kb_o[:, hd * LANES:(hd + 1) * LANES] = (kn[:, hd * LANES:(hd + 1) * LANES] + krp).astype(BF16)
    vb_o[...] = vb.astype(BF16)


def _inproj(x, mod3, mod_row, g, w1, qn, kvn, wuq, wk, wv, tables):
    latent = tables is not None
    t = x.shape[0]
    n_tiles = t // TM
    row = lambda i: (i, 0)
    const = lambda i: (0, 0)
    in_specs = [pl.BlockSpec((TM, D_MODEL), row),
                pl.BlockSpec((None, 1, 6 * D_MODEL), lambda i: (mod_row(i), 0, 0)),
                pl.BlockSpec((1, D_MODEL), const),
                pl.BlockSpec((D_MODEL, _N1), const),
                pl.BlockSpec((1, Q_LORA), const),
                pl.BlockSpec((1, KV_LORA), const),
                pl.BlockSpec((Q_LORA, B_HEADS * LANES), const),
                pl.BlockSpec((KV_LORA, B_HEADS * LANES), const),
                pl.BlockSpec((KV_LORA, B_HEADS * B_V), const)]
    args = [x, mod3, g, w1, qn, kvn, wuq, wk, wv]
    sds = jax.ShapeDtypeStruct
    if latent:
        per = DEC_SEQ // TM
        tab = lambda i: (i % per, 0)
        in_specs += [pl.BlockSpec((TM, LANES), tab)] * 4
        args += list(tables)
        out_shape = [sds((t, 512), BF16), sds((t, LANES), BF16), sds((t, LANES), BF16),
                     sds((t, B_HEADS * LANES), BF16), sds((t, B_HEADS * LANES), BF16),
                     sds((t, B_HEADS * B_V), BF16)]
        widths = [512, LANES, LANES, B_HEADS * LANES, B_HEADS * LANES, B_HEADS * B_V]
    else:
        out_shape = [sds((t, 512), BF16), sds((t, LANES), F32), sds((t, LANES), F32),
                     sds((t, KV_LORA), F32), sds((t, LANES), F32),
                     sds((t, B_HEADS * LANES), BF16), sds((t, B_HEADS * LANES), BF16),
                     sds((t, B_HEADS * B_V), BF16)]
        widths = [512, LANES, LANES, KV_LORA, LANES, B_HEADS * LANES, B_HEADS * LANES, B_HEADS * B_V]
    out_specs = [pl.BlockSpec((TM, w), row) for w in widths]
    return pl.pallas_call(
        functools.partial(_inproj_kernel, latent),
        grid=(n_tiles,),
        in_specs=in_specs,
        out_specs=out_specs,
        out_shape=out_shape,
        compiler_params=_params("arbitrary"),
        name="inproj_latent" if latent else "inproj_ctx",
    )(*args)


def _kvexp_kernel(ckv_ref, krp_ref, wk_ref, wv_ref, kb_o, vb_o):
    cb = ckv_ref[...].astype(BF16)
    kn = _dot(cb, wk_ref[...])
    krp = krp_ref[...]
    for hd in range(B_HEADS):
        kb_o[:, hd * LANES:(hd + 1) * LANES] = (kn[:, hd * LANES:(hd + 1) * LANES] + krp).astype(BF16)
    vb_o[...] = _dot(cb, wv_ref[...]).astype(BF16)


def _kv_expand(ckv, krp, wk, wv):
    t = ckv.shape[0]
    row = lambda i: (i, 0)
    const = lambda i: (0, 0)
    return pl.pallas_call(
        _kvexp_kernel,
        grid=(t // TM,),
        in_specs=[pl.BlockSpec((TM, KV_LORA), row), pl.BlockSpec((TM, LANES), row),
                  pl.BlockSpec((KV_LORA, B_HEADS * LANES), const),
                  pl.BlockSpec((KV_LORA, B_HEADS * B_V), const)],
        out_specs=[pl.BlockSpec((TM, B_HEADS * LANES), row), pl.BlockSpec((TM, B_HEADS * B_V), row)],
        out_shape=[jax.ShapeDtypeStruct((t, B_HEADS * LANES), BF16),
                   jax.ShapeDtypeStruct((t, B_HEADS * B_V), BF16)],
        compiler_params=_params("arbitrary"),
        name="kv_expand",
    )(ckv, krp, wk, wv)


def _lo_mask(rows):
    return lax.broadcasted_iota(jnp.int32, (rows, LANES), 1) < (LANES // 2)


def _softmax_pv(parts, sink):
    if sink is not None:
        sink = sink * LOG2E
    m = None
    for s, _ in parts:
        mi = jnp.max(s, axis=-1, keepdims=True)
        m = mi if m is None else jnp.maximum(m, mi)
    if sink is not None:
        m = jnp.maximum(m, sink)
    den = None
    acc = None
    for s, v in parts:
        p = jnp.exp2(s - m)
        di = jnp.sum(p, axis=-1, keepdims=True)
        den = di if den is None else den + di
        oi = _dot(p.astype(BF16), v)
        acc = oi if acc is None else acc + oi
    if sink is not None:
        den = den + jnp.exp2(sink - m)
    return acc / den


def _ctx_attn_kernel(sink_ref, qa_ref, ka_ref, va_ref, qb_ref, kb_ref, vb_ref, oa_ref, ob_ref):
    lo = _lo_mask(SEQ)
    k = ka_ref[...].astype(BF16)
    v = va_ref[...].astype(BF16)
    for j in range(A_GROUP):
        q = qa_ref[:, j * LANES:(j + 1) * LANES]
        zero = jnp.zeros_like(q)
        o0 = _softmax_pv([(_dot_nt(jnp.where(lo, q, zero), k), v)], sink_ref[j])
        o1 = _softmax_pv([(_dot_nt(jnp.where(lo, zero, q), k), v)], sink_ref[A_GROUP + j])
        oa_ref[:, j * LANES:(j + 1) * LANES] = jnp.where(lo, o0, o1).astype(BF16)
    for j in range(B_HEADS // 2):
        vp = vb_ref[:, j * LANES:(j + 1) * LANES]
        outs = []
        for hd in (2 * j, 2 * j + 1):
            s = _dot_nt(qb_ref[:, hd * LANES:(hd + 1) * LANES], kb_ref[:, hd * LANES:(hd + 1) * LANES])
            outs.append(_softmax_pv([(s, vp)], None))
        ob_ref[:, j * LANES:(j + 1) * LANES] = jnp.where(lo, outs[0], outs[1]).astype(BF16)


def _ctx_attention(sink, qa, ka, va, qb, kb, vb):
    blk = lambda w: pl.BlockSpec((SEQ, w), lambda b: (b, 0))
    return pl.pallas_call(
        _ctx_attn_kernel,
        grid=(BATCH,),
        in_specs=[pl.BlockSpec(memory_space=pltpu.SMEM),
                  blk(512), blk(LANES), blk(LANES), blk(B_HEADS * LANES), blk(B_HEADS * LANES),
                  blk(B_HEADS * B_V)],
        out_specs=[blk(512), blk(512)],
        out_shape=[jax.ShapeDtypeStruct((T_CTX, 512), BF16), jax.ShapeDtypeStruct((T_CTX, 512), BF16)],
        compiler_params=_params("arbitrary"),
        name="ctx_attention",
    )(sink, qa, ka, va, qb, kb, vb)


_WIN = TQ + 2 * WINDOW


def _lat_swa_kernel(sink_ref, q_ref, k_ref, v_ref, kc_ref, vc_ref, o_ref):
    qi = pl.program_id(1)
    q0 = qi * TQ
    start = pl.multiple_of(jnp.clip(q0 - WINDOW, 0, DEC_SEQ - _WIN), WINDOW)
    kw = k_ref[pl.ds(start, _WIN), :]
    vw = v_ref[pl.ds(start, _WIN), :]
    kc = kc_ref[...].astype(BF16)
    vc = vc_ref[...].astype(BF16)
    qpos = q0 + lax.broadcasted_iota(jnp.int32, (TQ, _WIN), 0)
    kpos = start + lax.broadcasted_iota(jnp.int32, (TQ, _WIN), 1)
    valid = jnp.abs(qpos - kpos) <= WINDOW
    lo = _lo_mask(TQ)
    for j in range(A_GROUP):
        q = q_ref[:, j * LANES:(j + 1) * LANES]
        zero = jnp.zeros_like(q)
        outs = []
        for g, qh in ((0, jnp.where(lo, q, zero)), (1, jnp.where(lo, zero, q))):
            sw = jnp.where(valid, _dot_nt(qh, kw), NEG)
            sc = _dot_nt(qh, kc)
            outs.append(_softmax_pv([(sw, vw), (sc, vc)], sink_ref[g * A_GROUP + j]))
        o_ref[:, j * LANES:(j + 1) * LANES] = jnp.where(lo, outs[0], outs[1]).astype(BF16)


def _lat_swa(sink, qa, ka, va, kc, vc):
    nq = DEC_SEQ // TQ
    return pl.pallas_call(
        _lat_swa_kernel,
        grid=(DEC_BATCH, nq),
        in_specs=[pl.BlockSpec(memory_space=pltpu.SMEM),
                  pl.BlockSpec((TQ, 512), lambda b, i: (b * nq + i, 0)),
                  pl.BlockSpec((None, DEC_SEQ, LANES), lambda b, i: (b, 0, 0)),
                  pl.BlockSpec((None, DEC_SEQ, LANES), lambda b, i: (b, 0, 0)),
                  pl.BlockSpec((None, PAST_LEN, LANES), lambda b, i: (b, 0, 0)),
                  pl.BlockSpec((None, PAST_LEN, LANES), lambda b, i: (b, 0, 0))],
        out_specs=pl.BlockSpec((TQ, 512), lambda b, i: (b * nq + i, 0)),
        out_shape=jax.ShapeDtypeStruct((T_LAT, 512), BF16),
        compiler_params=_params("arbitrary", "arbitrary"),
        name="latent_swa",
    )(sink, qa, ka.reshape(DEC_BATCH, DEC_SEQ, LANES), va.reshape(DEC_BATCH, DEC_SEQ, LANES), kc, vc)


def _lat_mla_kernel(q_ref, kl_ref, vl_ref, kc_ref, vc_ref, o_ref):
    lo = _lo_mask(TQ)
    vl = vl_ref[...]
    vc = vc_ref[...]
    outs = []
    for hh in range(2):
        q = q_ref[:, hh * LANES:(hh + 1) * LANES]
        sc = _dot_nt(q, kc_ref[:, hh * LANES:(hh + 1) * LANES])
        sl = _dot_nt(q, kl_ref[:, hh * LANES:(hh + 1) * LANES])
        outs.append(_softmax_pv([(sc, vc), (sl, vl)], None))
    o_ref[...] = jnp.where(lo, outs[0], outs[1]).astype(BF16)


def _lat_mla(qb, kb, vb, kc, vc):
    nq = DEC_SEQ // TQ
    npair = B_HEADS // 2
    return pl.pallas_call(
        _lat_mla_kernel,
        grid=(DEC_BATCH, npair, nq),
        in_specs=[pl.BlockSpec((TQ, 2 * LANES), lambda b, p, i: (b * nq + i, p)),
                  pl.BlockSpec((None, DEC_SEQ, 2 * LANES), lambda b, p, i: (b, 0, p)),
                  pl.BlockSpec((None, DEC_SEQ, LANES), lambda b, p, i: (b, 0, p)),
                  pl.BlockSpec((None, PAST_LEN, 2 * LANES), lambda b, p, i: (b, 0, p)),
                  pl.BlockSpec((None, PAST_LEN, LANES), lambda b, p, i: (b, 0, p))],
        out_specs=pl.BlockSpec((TQ, LANES), lambda b, p, i: (b * nq + i, p)),
        out_shape=jax.ShapeDtypeStruct((T_LAT, 512), BF16),
        compiler_params=_params("arbitrary", "arbitrary", "arbitrary"),
        name="latent_mla",
    )(qb, kb.reshape(DEC_BATCH, DEC_SEQ, B_HEADS * LANES), vb.reshape(DEC_BATCH, DEC_SEQ, B_HEADS * B_V),
      kc.reshape(DEC_BATCH, PAST_LEN, B_HEADS * LANES), vc.reshape(DEC_BATCH, PAST_LEN, B_HEADS * B_V))


_N_CTX_TILES = T_CTX // TM


def _merge_kernel(xp_ref, xs_ref, oac_ref, oal_ref, obc_ref, obl_ref, mod_ref,
                  gpre_ref, gpost_ref, gffn_ref, woa_ref, wob_ref, wout_ref, wg_ref, wr_ref,
                  x1_o, h2_o, lg_o):
    def body(x_ref, oa_ref, ob_ref):
        x = x_ref[...]
        sh1 = mod_ref[:, 0:D_MODEL]
        sc1 = mod_ref[:, D_MODEL:2 * D_MODEL]
        g1 = mod_ref[:, 2 * D_MODEL:3 * D_MODEL]
        sh2 = mod_ref[:, 3 * D_MODEL:4 * D_MODEL]
        sc2 = mod_ref[:, 4 * D_MODEL:5 * D_MODEL]
        h = (_rms(x, gpre_ref[...]) * (1.0 + sc1) + sh1).astype(BF16)
        gates = _dot(h, wg_ref[...])
        ya = _dot(oa_ref[...], woa_ref[...])
        yb = _dot(ob_ref[...], wob_ref[...])
        y = _sigmoid(gates[:, :D_MODEL]) * ya + _sigmoid(gates[:, D_MODEL:]) * yb
        mix = _dot(y.astype(BF16), wout_ref[...])
        x1 = x + g1 * _rms(mix, gpost_ref[...])
        x1_o[...] = x1
        h2 = _rms(x1, gffn_ref[...]) * (1.0 + sc2) + sh2
        _store_rows(h2_o, h2, TM)
        lg_o[...] = _dot(h2.astype(BF16), wr_ref[...])

    is_ctx = pl.program_id(0) < _N_CTX_TILES

    @pl.when(is_ctx)
    def _():
        body(xp_ref, oac_ref, obc_ref)

    @pl.when(jnp.logical_not(is_ctx))
    def _():
        body(xs_ref, oal_ref, obl_ref)


def _mod_row_all(i):
    per = DEC_SEQ // TM
    return jnp.where(i < _N_CTX_TILES, 0, 1 + (jnp.maximum(i - _N_CTX_TILES, 0)) // per)


def _merge(xp, xs, oac, oal, obc, obl, mod3, gpre, gpost, gffn, woa, wob, wout, wg, wr):
    ctx = lambda i: (jnp.minimum(i, _N_CTX_TILES - 1), 0)
    lat = lambda i: (jnp.maximum(i - _N_CTX_TILES, 0), 0)
    row = lambda i: (i, 0)
    const = lambda i: (0, 0)
    return pl.pallas_call(
        _merge_kernel,
        grid=(T_ALL // TM,),
        in_specs=[pl.BlockSpec((TM, D_MODEL), ctx), pl.BlockSpec((TM, D_MODEL), lat),
                  pl.BlockSpec((TM, 512), ctx), pl.BlockSpec((TM, 512), lat),
                  pl.BlockSpec((TM, 512), ctx), pl.BlockSpec((TM, 512), lat),
                  pl.BlockSpec((None, 1, 6 * D_MODEL), lambda i: (_mod_row_all(i), 0, 0)),
                  pl.BlockSpec((1, D_MODEL), const), pl.BlockSpec((1, D_MODEL), const),
                  pl.BlockSpec((1, D_MODEL), const),
                  pl.BlockSpec((512, D_MODEL), const), pl.BlockSpec((512, D_MODEL), const),
                  pl.BlockSpec((D_MODEL, D_MODEL), const), pl.BlockSpec((D_MODEL, 2 * D_MODEL), const),
                  pl.BlockSpec((D_MODEL, N_EXPERTS), const)],
        out_specs=[pl.BlockSpec((TM, D_MODEL), row), pl.BlockSpec((TM * SUB, LANES), row),
                   pl.BlockSpec((TM, N_EXPERTS), row)],
        out_shape=[jax.ShapeDtypeStruct((T_ALL, D_MODEL), F32), jax.ShapeDtypeStruct((T_ALL * SUB, LANES), F32),
                   jax.ShapeDtypeStruct((T_ALL, N_EXPERTS), F32)],
        compiler_params=_params("arbitrary"),
        name="merge",
    )(xp, xs, oac, oal, obc, obl, mod3, gpre, gpost, gffn, woa, wob, wout, wg, wr)


_PER_GROUP = N_EXPERTS // N_GROUPS


def _router_kernel(lg_ref, bias_ref, topi_o, rank_o, topw_o, cnt_o, cnt_s):
    @pl.when(pl.program_id(0) == 0)
    def _():
        cnt_s[...] = jnp.zeros_like(cnt_s)

    scores = _sigmoid(lg_ref[...])
    sel = scores + bias_ref[...]
    lane_i = lax.broadcasted_iota(jnp.int32, (TM, N_EXPERTS), 1)
    lane = lane_i.astype(F32)
    grp = lax.shift_right_logical(lane_i, int(np.log2(_PER_GROUP)))
    ninf = -jnp.inf
    big = float(N_EXPERTS)
    gs = []
    for g in range(N_GROUPS):
        sg = jnp.where(grp == g, sel, ninf)
        m1 = jnp.max(sg, axis=-1, keepdims=True)
        i1 = jnp.min(jnp.where(sg == m1, lane, big), axis=-1, keepdims=True)
        m2 = jnp.max(jnp.where(lane == i1, ninf, sg), axis=-1, keepdims=True)
        gs.append(m1 + m2)
    keep = jnp.zeros((TM, N_EXPERTS), F32)
    for g in range(N_GROUPS):
        beat = jnp.zeros((TM, 1), F32)
        for g2 in range(N_GROUPS):
            if g2 == g:
                continue
            better = (gs[g2] >= gs[g]) if g2 < g else (gs[g2] > gs[g])
            beat = beat + jnp.where(better, 1.0, 0.0)
        keep = jnp.where(grp == g, jnp.where(beat < TOPK_GROUPS, 1.0, 0.0), keep)
    cur = jnp.where(keep > 0.5, sel, NEG)
    lane8 = lax.broadcasted_iota(jnp.int32, (TM, LANES), 1)
    topi = jnp.zeros((TM, LANES), F32)
    topw = jnp.zeros((TM, LANES), F32)
    chosen = jnp.zeros((TM, N_EXPERTS), F32)
    idxs = []
    for k in range(TOP_K):
        m = jnp.max(cur, axis=-1, keepdims=True)
        idx = jnp.min(jnp.where(cur == m, lane, big), axis=-1, keepdims=True)
        hit = lane == idx
        w = jnp.sum(jnp.where(hit, scores, 0.0), axis=-1, keepdims=True)
        cur = jnp.where(hit, ninf, cur)
        chosen = jnp.where(hit, 1.0, chosen)
        topi = jnp.where(lane8 == k, idx, topi)
        topw = jnp.where(lane8 == k, w, topw)
        idxs.append(idx)
    wsum = jnp.sum(topw, axis=-1, keepdims=True)
    topw_o[...] = topw / wsum * ROUTED_SCALE
    topi_o[...] = topi.astype(jnp.int32)
    r_i = lax.broadcasted_iota(jnp.int32, (TM, TM), 0)
    c_i = lax.broadcasted_iota(jnp.int32, (TM, TM), 1)
    tri = jnp.where(c_i < r_i, 1.0, 0.0).astype(BF16)
    prefix = _dot(tri, chosen.astype(BF16)) + cnt_s[0:1, :]
    rank = jnp.zeros((TM, LANES), jnp.int32)
    for k in range(TOP_K):
        r = jnp.sum(jnp.where(lane == idxs[k], prefix, 0.0), axis=-1, keepdims=True)
        rank = jnp.where(lane8 == k, r.astype(jnp.int32), rank)
    rank_o[...] = rank
    total = cnt_s[...] + jnp.sum(chosen, axis=0, keepdims=True)
    cnt_s[...] = total
    cnt_o[...] = total


def _router(logits, bias):
    row = lambda i: (i, 0)
    const = lambda i: (0, 0)
    return pl.pallas_call(
        _router_kernel,
        grid=(T_ALL // TM,),
        in_specs=[pl.BlockSpec((TM, N_EXPERTS), row), pl.BlockSpec((1, N_EXPERTS), const)],
        out_specs=[pl.BlockSpec((TM, LANES), row), pl.BlockSpec((TM, LANES), row),
                   pl.BlockSpec((TM, LANES), row), pl.BlockSpec((8, N_EXPERTS), const)],
        out_shape=[jax.ShapeDtypeStruct((T_ALL, LANES), jnp.int32),
                   jax.ShapeDtypeStruct((T_ALL, LANES), jnp.int32),
                   jax.ShapeDtypeStruct((T_ALL, LANES), F32),
                   jax.ShapeDtypeStruct((8, N_EXPERTS), F32)],
        scratch_shapes=[pltpu.VMEM((8, N_EXPERTS), F32)],
        compiler_params=_params("arbitrary"),
        name="router",
    )(logits, bias)


def _dest_kernel(topi_ref, rank_ref, ps_ref, o_ref):
    lane = lax.broadcasted_iota(jnp.int32, (TM, N_EXPERTS), 1)
    lane8 = lax.broadcasted_iota(jnp.int32, (TM, LANES), 1)
    ps = ps_ref[...]
    topi = topi_ref[...]
    acc = jnp.zeros((TM, LANES), F32)
    for k in range(TOP_K):
        start = jnp.sum(jnp.where(lane == topi[:, k:k + 1], ps, 0.0), axis=-1, keepdims=True)
        acc = jnp.where(lane8 == k, start, acc)
    o_ref[...] = rank_ref[...] + acc.astype(jnp.int32)


def _dest(topi_p, rank_p, pstart):
    row = lambda i: (i, 0)
    return pl.pallas_call(
        _dest_kernel,
        grid=(T_ALL // TM,),
        in_specs=[pl.BlockSpec((TM, LANES), row), pl.BlockSpec((TM, LANES), row),
                  pl.BlockSpec((1, N_EXPERTS), lambda i: (0, 0))],
        out_specs=pl.BlockSpec((TM, LANES), row),
        out_shape=jax.ShapeDtypeStruct((T_ALL, LANES), jnp.int32),
        compiler_params=_params("arbitrary"),
        name="dest",
    )(topi_p, rank_p, pstart)


TD = 512
ROW_UNROLL = 4


def _dispatch_kernel(dest_ref, h_ref, xs_hbm, sem):
    def row(t, carry):
        src = h_ref.at[pl.ds(pl.multiple_of(t * SUB, SUB), SUB)]
        for k in range(TOP_K):
            d = dest_ref[t * TOP_K + k]
            pltpu.make_async_copy(src, xs_hbm.at[pl.ds(pl.multiple_of(d * SUB, SUB), SUB)],
                                  sem).start(priority=k % 2)
        return carry

    lax.fori_loop(0, TD, row, 0, unroll=ROW_UNROLL)
    for _ in range(TOP_K):
        pltpu.make_async_copy(h_ref, xs_hbm.at[pl.ds(0, TD * SUB)], sem).wait()


def _dispatch(dest_flat, h2):
    return pl.pallas_call(
        _dispatch_kernel,
        grid=(T_ALL // TD,),
        in_specs=[pl.BlockSpec((TD * TOP_K,), lambda i: (i,), memory_space=pltpu.SMEM),
                  pl.BlockSpec((TD * SUB, LANES), lambda i: (i, 0))],
        out_specs=pl.BlockSpec(memory_space=pl.ANY),
        out_shape=jax.ShapeDtypeStruct((N_SLOTS * SUB, LANES), F32),
        scratch_shapes=[pltpu.SemaphoreType.DMA(())],
        compiler_params=_params("arbitrary"),
        name="dispatch",
    )(dest_flat, h2)


_BLK_TILES = MOE_BLK * SUB
NBX = 6
NBO = 4


def _expert_kernel(bs_ref, x_hbm, wg_ref, wu_ref, wd_ref, o_hbm, wgu_s, wd_s, xbuf, obuf, xsem, osem):
    e = pl.program_id(0)
    b0 = bs_ref[e]
    b1 = bs_ref[e + 1]
    n_used = bs_ref[N_EXPERTS]

    def xcopy(g):
        s = lax.rem(g, NBX)
        return pltpu.make_async_copy(x_hbm.at[pl.ds(pl.multiple_of(g * _BLK_TILES, _BLK_TILES), _BLK_TILES)],
                                     xbuf.at[s], xsem.at[s])

    def ocopy(g):
        s = lax.rem(g, NBO)
        return pltpu.make_async_copy(obuf.at[s],
                                     o_hbm.at[pl.ds(pl.multiple_of(g * _BLK_TILES, _BLK_TILES), _BLK_TILES)],
                                     osem.at[s])

    @pl.when(e == 0)
    def _():
        for j in range(NBX - 1):
            @pl.when(j < n_used)
            def _():
                xcopy(j).start()

    @pl.when(b1 > b0)
    def _():
        wgu_s[:, 0:D_EXPERT] = wg_ref[...].astype(BF16)
        wgu_s[:, D_EXPERT:2 * D_EXPERT] = wu_ref[...].astype(BF16)
        wd_s[...] = wd_ref[...].astype(BF16)

    def block(g, carry):
        @pl.when(g + NBX - 1 < n_used)
        def _():
            xcopy(g + NBX - 1).start()

        xcopy(g).wait()

        @pl.when(g >= NBO)
        def _():
            ocopy(g - NBO).wait()

        gu = _dot(_load_rows(xbuf.at[lax.rem(g, NBX)], MOE_BLK).astype(BF16), wgu_s[...])
        a = _silu(gu[:, :D_EXPERT]) * gu[:, D_EXPERT:]
        _store_rows(obuf.at[lax.rem(g, NBO)], _dot(a.astype(BF16), wd_s[...]), MOE_BLK)
        ocopy(g).start(priority=1)
        return carry

    lax.fori_loop(b0, b1, block, 0)

    @pl.when(e == N_EXPERTS - 1)
    def _():
        for j in range(NBO, 0, -1):
            @pl.when(n_used >= j)
            def _():
                ocopy(n_used - j).wait()


def _experts(blk_start, x_sorted, w_gate, w_up, w_down):
    wsel = lambda e, bs: (e, 0, 0)
    grid_spec = pltpu.PrefetchScalarGridSpec(
        num_scalar_prefetch=1,
        grid=(N_EXPERTS,),
        in_specs=[pl.BlockSpec(memory_space=pl.ANY),
                  pl.BlockSpec((None, D_MODEL, D_EXPERT), wsel),
                  pl.BlockSpec((None, D_MODEL, D_EXPERT), wsel),
                  pl.BlockSpec((None, D_EXPERT, D_MODEL), wsel)],
        out_specs=pl.BlockSpec(memory_space=pl.ANY),
        scratch_shapes=[pltpu.VMEM((D_MODEL, 2 * D_EXPERT), BF16), pltpu.VMEM((D_EXPERT, D_MODEL), BF16),
                        pltpu.VMEM((NBX, _BLK_TILES, LANES), F32), pltpu.VMEM((NBO, _BLK_TILES, LANES), F32),
                        pltpu.SemaphoreType.DMA((NBX,)), pltpu.SemaphoreType.DMA((NBO,))],
    )
    return pl.pallas_call(
        _expert_kernel,
        grid_spec=grid_spec,
        out_shape=jax.ShapeDtypeStruct((N_SLOTS * SUB, LANES), F32),
        compiler_params=_params("arbitrary"),
        name="experts",
    )(blk_start, x_sorted, w_gate, w_up, w_down)


TC = 256


def _combine_kernel(dcur_ref, dnext_ref, eo_hbm, x1_ref, h2_ref, w_ref, mod_ref, g_ref, wgu_ref, wd_ref,
                    o_ref, buf, sem):
    i = pl.program_id(0)
    n = pl.num_programs(0)
    slot = lax.rem(i, 2)

    def gather(d_ref, s):
        def row(t, carry):
            for k in range(TOP_K):
                d = d_ref[t * TOP_K + k]
                pltpu.make_async_copy(eo_hbm.at[pl.ds(pl.multiple_of(d * SUB, SUB), SUB)],
                                      buf.at[s, k, pl.ds(pl.multiple_of(t * SUB, SUB), SUB)],
                                      sem.at[s]).start(priority=k % 2)
            return carry
        lax.fori_loop(0, TC, row, 0, unroll=ROW_UNROLL)

    @pl.when(i == 0)
    def _():
        gather(dcur_ref, 0)

    @pl.when(i + 1 < n)
    def _():
        gather(dnext_ref, 1 - slot)

    gate = mod_ref[:, 5 * D_MODEL:6 * D_MODEL]
    gu = _dot(_load_rows(h2_ref, TC).astype(BF16), wgu_ref[...])
    a = _silu(gu[:, :D_SHARED]) * gu[:, D_SHARED:]
    moe = _dot(a.astype(BF16), wd_ref[...])
    for k in range(TOP_K):
        pltpu.make_async_copy(eo_hbm.at[pl.ds(0, TC * SUB)], buf.at[slot, k], sem.at[slot]).wait()
    w = w_ref[...]
    for k in range(TOP_K):
        moe = moe + _load_rows(buf.at[slot, k], TC) * w[:, k:k + 1]
    o_ref[...] = x1_ref[...] + gate * _rms(moe, g_ref[...])


def _combine(dest_flat, eo, x1, h2, topw, mod3, mod_row, g, wgu, wd, tile0, n_tiles):
    row = lambda i: (tile0 + i, 0)
    const = lambda i: (0, 0)
    blk = TC * TOP_K
    return pl.pallas_call(
        _combine_kernel,
        grid=(n_tiles,),
        in_specs=[pl.BlockSpec((blk,), lambda i: (tile0 + i,), memory_space=pltpu.SMEM),
                  pl.BlockSpec((blk,), lambda i: (tile0 + jnp.minimum(i + 1, n_tiles - 1),),
                               memory_space=pltpu.SMEM),
                  pl.BlockSpec(memory_space=pl.ANY),
                  pl.BlockSpec((TC, D_MODEL), row), pl.BlockSpec((TC * SUB, LANES), row),
                  pl.BlockSpec((TC, LANES), row),
                  pl.BlockSpec((None, 1, 6 * D_MODEL), lambda i: (mod_row(i), 0, 0)),
                  pl.BlockSpec((1, D_MODEL), const),
                  pl.BlockSpec((D_MODEL, 2 * D_SHARED), const), pl.BlockSpec((D_SHARED, D_MODEL), const)],
        out_specs=pl.BlockSpec((TC, D_MODEL), lambda i: (i, 0)),
        out_shape=jax.ShapeDtypeStruct((n_tiles * TC, D_MODEL), F32),
        scratch_shapes=[pltpu.VMEM((2, TOP_K, TC * SUB, LANES), F32), pltpu.SemaphoreType.DMA((2,))],
        compiler_params=_params("arbitrary"),
        name="combine",
    )(dest_flat, dest_flat, eo, x1, h2, topw, mod3, g, wgu, wd)


def _qa_perm():
    cols = []
    for j in range(A_GROUP):
        for g in range(A_KV_HEADS):
            hd = g * A_GROUP + j
            cols.extend(range(hd * A_HEAD_DIM, (hd + 1) * A_HEAD_DIM))
    return np.asarray(cols, np.int32)


def _pad_heads(w, width, n_heads, offset, total=LANES):
    k = w.shape[0]
    stride = w.shape[1] // n_heads
    w3 = w.reshape(k, n_heads, stride)[:, :, offset:offset + width]
    w3 = jnp.pad(w3, ((0, 0), (0, 0), (0, total - width)))
    return w3.reshape(k, n_heads * total)


def kernel(x_prompt, x_sample, cache_k_swa, cache_v_swa, cache_ckv_mla, cache_krope_mla, c, c_ctx, w_ada, b_ada, norm_pre_mix, norm_post_mix, norm_pre_ffn, norm_post_ffn, w_in, sink_swa, q_norm_mla, kv_norm_mla, w_uq_mla, w_ukv_mla, w_o_swa, w_o_mla, w_out, router_w, router_bias, w_gate_exp, w_up_exp, w_down_exp, w_gate_sh, w_up_sh, w_down_sh):
    l = 0
    perm = _qa_perm()
    wi = w_in[l]
    kr_blk = jnp.pad(wi[:, _O_KR:_O_GATE], ((0, 0), (B_NOPE, LANES - B_NOPE - B_ROPE)))
    w1 = jnp.concatenate([wi[:, _O_QA:_O_KA][:, perm], wi[:, _O_KA:_O_KR], kr_blk], axis=1).astype(BF16)
    wg = wi[:, _O_GATE:].astype(BF16)
    wuq = _pad_heads(w_uq_mla[l], B_NOPE + B_ROPE, B_HEADS, 0).astype(BF16)
    wk = _pad_heads(w_ukv_mla[l], B_NOPE, B_HEADS, 0).astype(BF16)
    wv = w_ukv_mla[l].reshape(KV_LORA, B_HEADS, B_NOPE + B_V)[:, :, B_NOPE:].reshape(KV_LORA, B_HEADS * B_V).astype(BF16)
    woa = w_o_swa[l][perm, :].astype(BF16)
    wob = w_o_mla[l].astype(BF16)
    wout = w_out[l].astype(BF16)
    wr = router_w[l].astype(BF16)
    wgu_sh = jnp.concatenate([w_gate_sh[l], w_up_sh[l]], axis=1).astype(BF16)
    wd_sh = w_down_sh[l].astype(BF16)
    sink = sink_swa[l]
    row2 = lambda v: v.reshape(1, -1)

    cond8 = jnp.concatenate([c_ctx[None, :], c, jnp.zeros((8 - 1 - DEC_BATCH, D_MODEL), F32)], axis=0)
    mod3 = _modulation(cond8, w_ada[l], b_ada[l]).reshape(8, 1, 6 * D_MODEL)

    xp = x_prompt.reshape(T_CTX, D_MODEL)
    xs = x_sample.reshape(T_LAT, D_MODEL)
    per = DEC_SEQ // TM
    shared_in = (row2(norm_pre_mix[l]), w1, row2(q_norm_mla[l]), row2(kv_norm_mla[l]), wuq, wk, wv)
    qa_c, ka_c, va_c, ckv_c, kr_c, qb_c, kb_c, vb_c = _inproj(xp, mod3, lambda i: 0, *shared_in, None)
    qa_l, ka_l, va_l, qb_l, kb_l, vb_l = _inproj(xs, mod3, lambda i: 1 + i // per, *shared_in, _rope_tables())

    oa_c, ob_c = _ctx_attention(sink, qa_c, ka_c, va_c, qb_c, kb_c, vb_c)

    krp_cache = jnp.pad(cache_krope_mla[:, l].reshape(DEC_BATCH * PAST_LEN, B_ROPE),
                        ((0, 0), (B_NOPE, LANES - B_NOPE - B_ROPE)))
    kc_b, vc_b = _kv_expand(cache_ckv_mla[:, l].reshape(DEC_BATCH * PAST_LEN, KV_LORA), krp_cache, wk, wv)
    oa_l = _lat_swa(sink, qa_l, ka_l, va_l,
                    cache_k_swa[:, l].reshape(DEC_BATCH, PAST_LEN, LANES),
                    cache_v_swa[:, l].reshape(DEC_BATCH, PAST_LEN, LANES))
    ob_l = _lat_mla(qb_l, kb_l, vb_l, kc_b, vc_b)

    x1, h2, logits = _merge(xp, xs, oa_c, oa_l, ob_c, ob_l, mod3,
                            row2(norm_pre_mix[l]), row2(norm_post_mix[l]), row2(norm_pre_ffn[l]),
                            woa, wob, wout, wg, wr)

    topi_p, rank_p, topw_p, counts8 = _router(logits, row2(router_bias[l]))
    counts = counts8[0].astype(jnp.int32)

    padded = (counts + MOE_BLK - 1) // MOE_BLK * MOE_BLK
    pend = jnp.cumsum(padded)
    pstart = pend - padded
    dest = _dest(topi_p, rank_p, pstart.astype(F32).reshape(1, N_EXPERTS))[:, :TOP_K].reshape(-1)
    blk_start = (jnp.concatenate([pstart, pend[-1:]]) // MOE_BLK).astype(jnp.int32)

    x_sorted = _dispatch(dest, h2)
    eo = _experts(blk_start, x_sorted, w_gate_exp[l], w_up_exp[l], w_down_exp[l])

    g_post = row2(norm_post_ffn[l])
    per_c = DEC_SEQ // TC
    y_p = _combine(dest, eo, x1, h2, topw_p, mod3, lambda i: 0, g_post, wgu_sh, wd_sh, 0, T_CTX // TC)
    y_s = _combine(dest, eo, x1, h2, topw_p, mod3, lambda i: 1 + i // per_c, g_post, wgu_sh, wd_sh,
                   T_CTX // TC, T_LAT // TC)

    return (y_p.reshape(BATCH, SEQ, D_MODEL), y_s.reshape(DEC_BATCH, DEC_SEQ, D_MODEL),
            ka_c.reshape(BATCH, 1, SEQ, A_KV_HEADS, A_HEAD_DIM),
            va_c.reshape(BATCH, 1, SEQ, A_KV_HEADS, A_HEAD_DIM),
            ckv_c.reshape(BATCH, 1, SEQ, KV_LORA),
            kr_c[:, B_NOPE:B_NOPE + B_ROPE].reshape(BATCH, 1, SEQ, B_ROPE))
```

```python
import functools

import jax
import jax.numpy as jnp
import numpy as np
from jax import lax
from jax.experimental import pallas as pl
from jax.experimental.pallas import tpu as pltpu

D_MODEL = 1024
BATCH = 32
SEQ = 256
DEC_BATCH = 4
DEC_SEQ = 2048
PAST_LEN = 512
GRID_W = 64
WINDOW = 128
ROPE_BASE = 10000.0
EPS = 1e-6
NEG = -1e30
A_HEADS = 8
A_KV_HEADS = 2
A_GROUP = A_HEADS // A_KV_HEADS
A_HEAD_DIM = 64
B_HEADS = 8
B_NOPE = 64
B_ROPE = 32
B_V = 64
KV_LORA = 256
Q_LORA = 384
MLA_SCALE = (B_NOPE + B_ROPE) ** -0.5
LOG2E = 1.4426950408889634
N_EXPERTS = 256
TOP_K = 8
N_GROUPS = 8
TOPK_GROUPS = 4
D_EXPERT = 256
D_SHARED = 256
ROUTED_SCALE = 2.5

T_CTX = BATCH * SEQ
T_LAT = DEC_BATCH * DEC_SEQ
T_ALL = T_CTX + T_LAT

LANES = 128
TM = 512
TQ = 256
MOE_BLK = 256
N_SLOT_BLK = (T_ALL * TOP_K) // MOE_BLK + N_EXPERTS
N_SLOTS = N_SLOT_BLK * MOE_BLK
VMEM_LIMIT = 48 * 1024 * 1024

F32 = jnp.float32
BF16 = jnp.bfloat16

_O_QA = 0
_O_KA = _O_QA + A_HEADS * A_HEAD_DIM
_O_VA = _O_KA + A_KV_HEADS * A_HEAD_DIM
_O_QL = _O_VA + A_KV_HEADS * A_HEAD_DIM
_O_CKV = _O_QL + Q_LORA
_O_KR = _O_CKV + KV_LORA
_O_GATE = _O_KR + B_ROPE


def _params(*sem):
    return pltpu.CompilerParams(dimension_semantics=sem, vmem_limit_bytes=VMEM_LIMIT)


def _dot(a, b):
    return jnp.dot(a, b, preferred_element_type=F32)


def _dot_nt(a, b):
    return lax.dot_general(a, b, (((1,), (1,)), ((), ())), preferred_element_type=F32)


def _rms(x, g):
    return x * lax.rsqrt(jnp.mean(x * x, axis=-1, keepdims=True) + EPS) * g


def _sigmoid(x):
    return 1.0 / (1.0 + jnp.exp(-x))


SUB = 8
assert SUB * LANES == D_MODEL


def _load_rows(ref, n):
    return jnp.concatenate([ref[pl.ds(j, n, stride=SUB), :] for j in range(SUB)], axis=1)


def _store_rows(ref, val, n):
    for j in range(SUB):
        ref[pl.ds(j, n, stride=SUB), :] = val[:, j * LANES:(j + 1) * LANES]


def _silu(x):
    return x * _sigmoid(x)


def _mod_kernel(cond_ref, w_ref, b_ref, o_ref):
    s = _silu(cond_ref[...]).astype(BF16)
    o_ref[...] = _dot(s, w_ref[...].astype(BF16)) + b_ref[...]


def _modulation(cond8, w_ada, b_ada):
    tn = 768
    n = w_ada.shape[1]
    return pl.pallas_call(
        _mod_kernel,
        grid=(n // tn,),
        in_specs=[pl.BlockSpec((8, D_MODEL), lambda j: (0, 0)),
                  pl.BlockSpec((D_MODEL, tn), lambda j: (0, j)),
                  pl.BlockSpec((1, tn), lambda j: (0, j))],
        out_specs=pl.BlockSpec((8, tn), lambda j: (0, j)),
        out_shape=jax.ShapeDtypeStruct((8, n), F32),
        compiler_params=_params("arbitrary"),
        name="modulation",
    )(cond8, w_ada, b_ada.reshape(1, n))


def _rope(x, cos, sin, half):
    lane = lax.broadcasted_iota(jnp.int32, x.shape, 1)
    first = (lane & (2 * half - 1)) < half
    partner = jnp.where(first, pltpu.roll(x, LANES - half, 1), pltpu.roll(x, half, 1))
    return x * cos + partner * sin


def _rope_tables():
    t = np.arange(DEC_SEQ)
    row = (t // GRID_W).astype(np.float32)
    col = (t % GRID_W).astype(np.float32)

    def tables(d_rot):
        nf = d_rot // 4
        inv = np.float32(ROPE_BASE) ** (-np.arange(nf, dtype=np.float32) / np.float32(nf))
        ar = row[:, None] * inv
        ac = col[:, None] * inv
        cos = np.concatenate([np.cos(ar), np.cos(ar), np.cos(ac), np.cos(ac)], axis=1)
        sin = np.concatenate([-np.sin(ar), np.sin(ar), -np.sin(ac), np.sin(ac)], axis=1)
        return cos.astype(np.float32), sin.astype(np.float32)

    cos_a, sin_a = tables(A_HEAD_DIM)
    cos_a = np.concatenate([cos_a, cos_a], axis=1)
    sin_a = np.concatenate([sin_a, sin_a], axis=1)
    cos_b, sin_b = tables(B_ROPE)
    one = np.ones((DEC_SEQ, B_NOPE), np.float32)
    zero = np.zeros((DEC_SEQ, B_NOPE), np.float32)
    pad = LANES - B_NOPE - B_ROPE
    cos_b = np.concatenate([one, cos_b, one[:, :pad]], axis=1)
    sin_b = np.concatenate([zero, sin_b, zero[:, :pad]], axis=1)
    return tuple(jnp.asarray(a) for a in (cos_a, sin_a, cos_b, sin_b))


_N1 = 1536


def _inproj_kernel(latent, *refs):
    if latent:
        (x_ref, mod_ref, g_ref, w1_ref, qn_ref, kvn_ref, wuq_ref, wk_ref, wv_ref,
         ca_ref, sa_ref, cb_ref, sb_ref,
         qa_o, ka_o, va_o, qb_o, kb_o, vb_o) = refs
    else:
        (x_ref, mod_ref, g_ref, w1_ref, qn_ref, kvn_ref, wuq_ref, wk_ref, wv_ref,
         qa_o, ka_o, va_o, ckv_o, kr_o, qb_o, kb_o, vb_o) = refs
    x = x_ref[...]
    shift = mod_ref[:, 0:D_MODEL]
    scale = mod_ref[:, D_MODEL:2 * D_MODEL]
    h = _rms(x, g_ref[...]) * (1.0 + scale) + shift
    p = _dot(h.astype(BF16), w1_ref[...])
    qa = p[:, 0:512]
    ka = p[:, 512:640]
    va = p[:, 640:768]
    ql = p[:, 768:1152]
    ckv = p[:, 1152:1408]
    krp = p[:, 1408:1536]
    qn = _rms(ql, qn_ref[...])
    qb = _dot(qn.astype(BF16), wuq_ref[...])
    cn = _rms(ckv, kvn_ref[...])
    cnb = cn.astype(BF16)
    kn = _dot(cnb, wk_ref[...])
    vb = _dot(cnb, wv_ref[...])
    if latent:
        ca, sa, cb, sb = ca_ref[...], sa_ref[...], cb_ref[...], sb_ref[...]
        ka = _rope(ka, ca, sa, A_HEAD_DIM // 4)
        krp = _rope(krp, cb, sb, B_ROPE // 4)
        for j in range(4):
            blk = _rope(qa[:, j * LANES:(j + 1) * LANES], ca, sa, A_HEAD_DIM // 4)
            qa_o[:, j * LANES:(j + 1) * LANES] = (blk * (A_HEAD_DIM ** -0.5 * LOG2E)).astype(BF16)
        for hd in range(B_HEADS):
            blk = _rope(qb[:, hd * LANES:(hd + 1) * LANES], cb, sb, B_ROPE // 4)
            qb_o[:, hd * LANES:(hd + 1) * LANES] = (blk * (MLA_SCALE * LOG2E)).astype(BF16)
        ka_o[...] = ka.astype(BF16)
        va_o[...] = va.astype(BF16)
    else:
        qa_o[...] = (qa * (A_HEAD_DIM ** -0.5 * LOG2E)).astype(BF16)
        qb_o[...] = (qb * (MLA_SCALE * LOG2E)).astype(BF16)
        ka_o[...] = ka
        va_o[...] = va
        ckv_o[...] = cn
        kr_o[...] = krp
    for hd in range(B_HEADS):
        kb_o[:, hd * LANES:(hd + 1) * LANES] = (kn[:, hd * LANES:(hd + 1) * LANES] + krp).astype(BF16)
    vb_o[...] = vb.astype(BF16)


def _inproj(x, mod3, mod_row, g, w1, qn, kvn, wuq, wk, wv, tables):
    latent = tables is not None
    t = x.shape[0]
    n_tiles = t // TM
    row = lambda i: (i, 0)
    const = lambda i: (0, 0)
    in_specs = [pl.BlockSpec((TM, D_MODEL), row),
                pl.BlockSpec((None, 1, 6 * D_MODEL), lambda i: (mod_row(i), 0, 0)),
                pl.BlockSpec((1, D_MODEL), const),
                pl.BlockSpec((D_MODEL, _N1), const),
                pl.BlockSpec((1, Q_LORA), const),
                pl.BlockSpec((1, KV_LORA), const),
                pl.BlockSpec((Q_LORA, B_HEADS * LANES), const),
                pl.BlockSpec((KV_LORA, B_HEADS * LANES), const),
                pl.BlockSpec((KV_LORA, B_HEADS * B_V), const)]
    args = [x, mod3, g, w1, qn, kvn, wuq, wk, wv]
    sds = jax.ShapeDtypeStruct
    if latent:
        per = DEC_SEQ // TM
        tab = lambda i: (i % per, 0)
        in_specs += [pl.BlockSpec((TM, LANES), tab)] * 4
        args += list(tables)
        out_shape = [sds((t, 512), BF16), sds((t, LANES), BF16), sds((t, LANES), BF16),
                     sds((t, B_HEADS * LANES), BF16), sds((t, B_HEADS * LANES), BF16),
                     sds((t, B_HEADS * B_V), BF16)]
        widths = [512, LANES, LANES, B_HEADS * LANES, B_HEADS * LANES, B_HEADS * B_V]
    else:
        out_shape = [sds((t, 512), BF16), sds((t, LANES), F32), sds((t, LANES), F32),
                     sds((t, KV_LORA), F32), sds((t, LANES), F32),
                     sds((t, B_HEADS * LANES), BF16), sds((t, B_HEADS * LANES), BF16),
                     sds((t, B_HEADS * B_V), BF16)]
        widths = [512, LANES, LANES, KV_LORA, LANES, B_HEADS * LANES, B_HEADS * LANES, B_HEADS * B_V]
    out_specs = [pl.BlockSpec((TM, w), row) for w in widths]
    return pl.pallas_call(
        functools.partial(_inproj_kernel, latent),
        grid=(n_tiles,),
        in_specs=in_specs,
        out_specs=out_specs,
        out_shape=out_shape,
        compiler_params=_params("arbitrary"),
        name="inproj_latent" if latent else "inproj_ctx",
    )(*args)


def _kvexp_kernel(ckv_ref, krp_ref, wk_ref, wv_ref, kb_o, vb_o):
    cb = ckv_ref[...].astype(BF16)
    kn = _dot(cb, wk_ref[...])
    krp = krp_ref[...]
    for hd in range(B_HEADS):
        kb_o[:, hd * LANES:(hd + 1) * LANES] = (kn[:, hd * LANES:(hd + 1) * LANES] + krp).astype(BF16)
    vb_o[...] = _dot(cb, wv_ref[...]).astype(BF16)


def _kv_expand(ckv, krp, wk, wv):
    t = ckv.shape[0]
    row = lambda i: (i, 0)
    const = lambda i: (0, 0)
    return pl.pallas_call(
        _kvexp_kernel,
        grid=(t // TM,),
        in_specs=[pl.BlockSpec((TM, KV_LORA), row), pl.BlockSpec((TM, LANES), row),
                  pl.BlockSpec((KV_LORA, B_HEADS * LANES), const),
                  pl.BlockSpec((KV_LORA, B_HEADS * B_V), const)],
        out_specs=[pl.BlockSpec((TM, B_HEADS * LANES), row), pl.BlockSpec((TM, B_HEADS * B_V), row)],
        out_shape=[jax.ShapeDtypeStruct((t, B_HEADS * LANES), BF16),
                   jax.ShapeDtypeStruct((t, B_HEADS * B_V), BF16)],
        compiler_params=_params("arbitrary"),
        name="kv_expand",
    )(ckv, krp, wk, wv)


def _lo_mask(rows):
    return lax.broadcasted_iota(jnp.int32, (rows, LANES), 1) < (LANES // 2)


def _softmax_pv(parts, sink):
    if sink is not None:
        sink = sink * LOG2E
    m = None
    for s, _ in parts:
        mi = jnp.max(s, axis=-1, keepdims=True)
        m = mi if m is None else jnp.maximum(m, mi)
    if sink is not None:
        m = jnp.maximum(m, sink)
    den = None
    acc = None
    for s, v in parts:
        p = jnp.exp2(s - m)
        di = jnp.sum(p, axis=-1, keepdims=True)
        den = di if den is None else den + di
        oi = _dot(p.astype(BF16), v)
        acc = oi if acc is None else acc + oi
    if sink is not None:
        den = den + jnp.exp2(sink - m)
    return acc / den


def _ctx_attn_kernel(sink_ref, qa_ref, ka_ref, va_ref, qb_ref, kb_ref, vb_ref, oa_ref, ob_ref):
    lo = _lo_mask(SEQ)
    k = ka_ref[...].astype(BF16)
    v = va_ref[...].astype(BF16)
    for j in range(A_GROUP):
        q = qa_ref[:, j * LANES:(j + 1) * LANES]
        zero = jnp.zeros_like(q)
        o0 = _softmax_pv([(_dot_nt(jnp.where(lo, q, zero), k), v)], sink_ref[j])
        o1 = _softmax_pv([(_dot_nt(jnp.where(lo, zero, q), k), v)], sink_ref[A_GROUP + j])
        oa_ref[:, j * LANES:(j + 1) * LANES] = jnp.where(lo, o0, o1).astype(BF16)
    for j in range(B_HEADS // 2):
        vp = vb_ref[:, j * LANES:(j + 1) * LANES]
        outs = []
        for hd in (2 * j, 2 * j + 1):
            s = _dot_nt(qb_ref[:, hd * LANES:(hd + 1) * LANES], kb_ref[:, hd * LANES:(hd + 1) * LANES])
            outs.append(_softmax_pv([(s, vp)], None))
        ob_ref[:, j * LANES:(j + 1) * LANES] = jnp.where(lo, outs[0], outs[1]).astype(BF16)


def _ctx_attention(sink, qa, ka, va, qb, kb, vb):
    blk = lambda w: pl.BlockSpec((SEQ, w), lambda b: (b, 0))
    return pl.pallas_call(
        _ctx_attn_kernel,
        grid=(BATCH,),
        in_specs=[pl.BlockSpec(memory_space=pltpu.SMEM),
                  blk(512), blk(LANES), blk(LANES), blk(B_HEADS * LANES), blk(B_HEADS * LANES),
                  blk(B_HEADS * B_V)],
        out_specs=[blk(512), blk(512)],
        out_shape=[jax.ShapeDtypeStruct((T_CTX, 512), BF16), jax.ShapeDtypeStruct((T_CTX, 512), BF16)],
        compiler_params=_params("arbitrary"),
        name="ctx_attention",
    )(sink, qa, ka, va, qb, kb, vb)


_WIN = TQ + 2 * WINDOW


def _lat_swa_kernel(sink_ref, q_ref, k_ref, v_ref, kc_ref, vc_ref, o_ref):
    qi = pl.program_id(1)
    q0 = qi * TQ
    start = pl.multiple_of(jnp.clip(q0 - WINDOW, 0, DEC_SEQ - _WIN), WINDOW)
    kw = k_ref[pl.ds(start, _WIN), :]
    vw = v_ref[pl.ds(start, _WIN), :]
    kc = kc_ref[...].astype(BF16)
    vc = vc_ref[...].astype(BF16)
    qpos = q0 + lax.broadcasted_iota(jnp.int32, (TQ, _WIN), 0)
    kpos = start + lax.broadcasted_iota(jnp.int32, (TQ, _WIN), 1)
    valid = jnp.abs(qpos - kpos) <= WINDOW
    lo = _lo_mask(TQ)
    for j in range(A_GROUP):
        q = q_ref[:, j * LANES:(j + 1) * LANES]
        zero = jnp.zeros_like(q)
        outs = []
        for g, qh in ((0, jnp.where(lo, q, zero)), (1, jnp.where(lo, zero, q))):
            sw = jnp.where(valid, _dot_nt(qh, kw), NEG)
            sc = _dot_nt(qh, kc)
            outs.append(_softmax_pv([(sw, vw), (sc, vc)], sink_ref[g * A_GROUP + j]))
        o_ref[:, j * LANES:(j + 1) * LANES] = jnp.where(lo, outs[0], outs[1]).astype(BF16)


def _lat_swa(sink, qa, ka, va, kc, vc):
    nq = DEC_SEQ // TQ
    return pl.pallas_call(
        _lat_swa_kernel,
        grid=(DEC_BATCH, nq),
        in_specs=[pl.BlockSpec(memory_space=pltpu.SMEM),
                  pl.BlockSpec((TQ, 512), lambda b, i: (b * nq + i, 0)),
                  pl.BlockSpec((None, DEC_SEQ, LANES), lambda b, i: (b, 0, 0)),
                  pl.BlockSpec((None, DEC_SEQ, LANES), lambda b, i: (b, 0, 0)),
                  pl.BlockSpec((None, PAST_LEN, LANES), lambda b, i: (b, 0, 0)),
                  pl.BlockSpec((None, PAST_LEN, LANES), lambda b, i: (b, 0, 0))],
        out_specs=pl.BlockSpec((TQ, 512), lambda b, i: (b * nq + i, 0)),
        out_shape=jax.ShapeDtypeStruct((T_LAT, 512), BF16),
        compiler_params=_params("arbitrary", "arbitrary"),
        name="latent_swa",
    )(sink, qa, ka.reshape(DEC_BATCH, DEC_SEQ, LANES), va.reshape(DEC_BATCH, DEC_SEQ, LANES), kc, vc)


def _lat_mla_kernel(q_ref, kl_ref, vl_ref, kc_ref, vc_ref, o_ref):
    lo = _lo_mask(TQ)
    vl = vl_ref[...]
    vc = vc_ref[...]
    outs = []
    for hh in range(2):
        q = q_ref[:, hh * LANES:(hh + 1) * LANES]
        sc = _dot_nt(q, kc_ref[:, hh * LANES:(hh + 1) * LANES])
        sl = _dot_nt(q, kl_ref[:, hh * LANES:(hh + 1) * LANES])
        outs.append(_softmax_pv([(sc, vc), (sl, vl)], None))
    o_ref[...] = jnp.where(lo, outs[0], outs[1]).astype(BF16)


def _lat_mla(qb, kb, vb, kc, vc):
    nq = DEC_SEQ // TQ
    npair = B_HEADS // 2
    return pl.pallas_call(
        _lat_mla_kernel,
        grid=(DEC_BATCH, npair, nq),
        in_specs=[pl.BlockSpec((TQ, 2 * LANES), lambda b, p, i: (b * nq + i, p)),
                  pl.BlockSpec((None, DEC_SEQ, 2 * LANES), lambda b, p, i: (b, 0, p)),
                  pl.BlockSpec((None, DEC_SEQ, LANES), lambda b, p, i: (b, 0, p)),
                  pl.BlockSpec((None, PAST_LEN, 2 * LANES), lambda b, p, i: (b, 0, p)),
                  pl.BlockSpec((None, PAST_LEN, LANES), lambda b, p, i: (b, 0, p))],
        out_specs=pl.BlockSpec((TQ, LANES), lambda b, p, i: (b * nq + i, p)),
        out_shape=jax.ShapeDtypeStruct((T_LAT, 512), BF16),
        compiler_params=_params("arbitrary", "arbitrary", "arbitrary"),
        name="latent_mla",
    )(qb, kb.reshape(DEC_BATCH, DEC_SEQ, B_HEADS * LANES), vb.reshape(DEC_BATCH, DEC_SEQ, B_HEADS * B_V),
      kc.reshape(DEC_BATCH, PAST_LEN, B_HEADS * LANES), vc.reshape(DEC_BATCH, PAST_LEN, B_HEADS * B_V))


_N_CTX_TILES = T_CTX // TM


def _merge_kernel(xp_ref, xs_ref, oac_ref, oal_ref, obc_ref, obl_ref, mod_ref,
                  gpre_ref, gpost_ref, gffn_ref, woa_ref, wob_ref, wout_ref, wg_ref, wr_ref,
                  x1_o, h2_o, lg_o):
    def body(x_ref, oa_ref, ob_ref):
        x = x_ref[...]
        sh1 = mod_ref[:, 0:D_MODEL]
        sc1 = mod_ref[:, D_MODEL:2 * D_MODEL]
        g1 = mod_ref[:, 2 * D_MODEL:3 * D_MODEL]
        sh2 = mod_ref[:, 3 * D_MODEL:4 * D_MODEL]
        sc2 = mod_ref[:, 4 * D_MODEL:5 * D_MODEL]
        h = (_rms(x, gpre_ref[...]) * (1.0 + sc1) + sh1).astype(BF16)
        gates = _dot(h, wg_ref[...])
        ya = _dot(oa_ref[...], woa_ref[...])
        yb = _dot(ob_ref[...], wob_ref[...])
        y = _sigmoid(gates[:, :D_MODEL]) * ya + _sigmoid(gates[:, D_MODEL:]) * yb
        mix = _dot(y.astype(BF16), wout_ref[...])
        x1 = x + g1 * _rms(mix, gpost_ref[...])
        x1_o[...] = x1
        h2 = _rms(x1, gffn_ref[...]) * (1.0 + sc2) + sh2
        _store_rows(h2_o, h2, TM)
        lg_o[...] = _dot_nt(wr_ref[...], h2.astype(BF16))

    is_ctx = pl.program_id(0) < _N_CTX_TILES

    @pl.when(is_ctx)
    def _():
        body(xp_ref, oac_ref, obc_ref)

    @pl.when(jnp.logical_not(is_ctx))
    def _():
        body(xs_ref, oal_ref, obl_ref)


def _mod_row_all(i):
    per = DEC_SEQ // TM
    return jnp.where(i < _N_CTX_TILES, 0, 1 + (jnp.maximum(i - _N_CTX_TILES, 0)) // per)


def _merge(xp, xs, oac, oal, obc, obl, mod3, gpre, gpost, gffn, woa, wob, wout, wg, wr):
    ctx = lambda i: (jnp.minimum(i, _N_CTX_TILES - 1), 0)
    lat = lambda i: (jnp.maximum(i - _N_CTX_TILES, 0), 0)
    row = lambda i: (i, 0)
    const = lambda i: (0, 0)
    return pl.pallas_call(
        _merge_kernel,
        grid=(T_ALL // TM,),
        in_specs=[pl.BlockSpec((TM, D_MODEL), ctx), pl.BlockSpec((TM, D_MODEL), lat),
                  pl.BlockSpec((TM, 512), ctx), pl.BlockSpec((TM, 512), lat),
                  pl.BlockSpec((TM, 512), ctx), pl.BlockSpec((TM, 512), lat),
                  pl.BlockSpec((None, 1, 6 * D_MODEL), lambda i: (_mod_row_all(i), 0, 0)),
                  pl.BlockSpec((1, D_MODEL), const), pl.BlockSpec((1, D_MODEL), const),
                  pl.BlockSpec((1, D_MODEL), const),
                  pl.BlockSpec((512, D_MODEL), const), pl.BlockSpec((512, D_MODEL), const),
                  pl.BlockSpec((D_MODEL, D_MODEL), const), pl.BlockSpec((D_MODEL, 2 * D_MODEL), const),
                  pl.BlockSpec((N_EXPERTS, D_MODEL), const)],
        out_specs=[pl.BlockSpec((TM, D_MODEL), row), pl.BlockSpec((TM * SUB, LANES), row),
                   pl.BlockSpec((N_EXPERTS, TM), lambda i: (0, i))],
        out_shape=[jax.ShapeDtypeStruct((T_ALL, D_MODEL), F32), jax.ShapeDtypeStruct((T_ALL * SUB, LANES), F32),
                   jax.ShapeDtypeStruct((N_EXPERTS, T_ALL), F32)],
        compiler_params=_params("arbitrary"),
        name="merge",
    )(xp, xs, oac, oal, obc, obl, mod3, gpre, gpost, gffn, woa, wob, wout, wg, wr)


_PER_GROUP = N_EXPERTS // N_GROUPS


def _router_kernel(lg_ref, bias_ref, topi_o, rank_o, topw_o, cnt_o, cnt_s):
    @pl.when(pl.program_id(0) == 0)
    def _():
        cnt_s[...] = jnp.zeros_like(cnt_s)

    scores = _sigmoid(lg_ref[...])
    sel = scores + bias_ref[...]
    eidx = lax.broadcasted_iota(jnp.int32, (N_EXPERTS, TM), 0).astype(F32)
    ninf = -jnp.inf
    big = float(N_EXPERTS)
    sel_g = [sel[g * _PER_GROUP:(g + 1) * _PER_GROUP, :] for g in range(N_GROUPS)]
    gs = []
    for sg in sel_g:
        m1 = jnp.max(sg, axis=0, keepdims=True)
        top = sg == m1
        n_top = jnp.sum(jnp.where(top, 1.0, 0.0), axis=0, keepdims=True)
        rest = jnp.max(jnp.where(top, ninf, sg), axis=0, keepdims=True)
        gs.append(m1 + jnp.where(n_top > 1.5, m1, rest))
    cur_g = []
    for g in range(N_GROUPS):
        beat = jnp.zeros((1, TM), F32)
        for g2 in range(N_GROUPS):
            if g2 == g:
                continue
            better = (gs[g2] >= gs[g]) if g2 < g else (gs[g2] > gs[g])
            beat = beat + jnp.where(better, 1.0, 0.0)
        cur_g.append(jnp.where(beat < TOPK_GROUPS, sel_g[g], NEG))
    cur = jnp.concatenate(cur_g, axis=0)
    chosen = jnp.zeros((N_EXPERTS, TM), F32)
    idxs, ws = [], []
    for k in range(TOP_K):
        m = jnp.max(cur, axis=0, keepdims=True)
        idx = jnp.min(jnp.where(cur == m, eidx, big), axis=0, keepdims=True)
        hit = eidx == idx
        ws.append(jnp.sum(jnp.where(hit, scores, 0.0), axis=0, keepdims=True))
        cur = jnp.where(hit, ninf, cur)
        chosen = jnp.where(hit, 1.0, chosen)
        idxs.append(idx)
    wsum = ws[0]
    for k in range(1, TOP_K):
        wsum = wsum + ws[k]
    for k in range(TOP_K):
        topw_o[k:k + 1, :] = ws[k] / wsum * ROUTED_SCALE
        topi_o[k:k + 1, :] = idxs[k].astype(jnp.int32)
    r_i = lax.broadcasted_iota(jnp.int32, (TM, TM), 0)
    c_i = lax.broadcasted_iota(jnp.int32, (TM, TM), 1)
    earlier = jnp.where(r_i < c_i, 1.0, 0.0).astype(BF16)
    prefix = _dot(chosen.astype(BF16), earlier) + cnt_s[:, 0:1]
    for k in range(TOP_K):
        r = jnp.sum(jnp.where(eidx == idxs[k], prefix, 0.0), axis=0, keepdims=True)
        rank_o[k:k + 1, :] = r.astype(jnp.int32)
    total = cnt_s[...] + jnp.sum(chosen, axis=1, keepdims=True)
    cnt_s[...] = total
    cnt_o[...] = total


def _router(logits_t, bias_col):
    col = lambda i: (0, i)
    const = lambda i: (0, 0)
    return pl.pallas_call(
        _router_kernel,
        grid=(T_ALL // TM,),
        in_specs=[pl.BlockSpec((N_EXPERTS, TM), col), pl.BlockSpec((N_EXPERTS, 1), const)],
        out_specs=[pl.BlockSpec((TOP_K, TM), col), pl.BlockSpec((TOP_K, TM), col),
                   pl.BlockSpec((TOP_K, TM), col), pl.BlockSpec((N_EXPERTS, LANES), const)],
        out_shape=[jax.ShapeDtypeStruct((TOP_K, T_ALL), jnp.int32),
                   jax.ShapeDtypeStruct((TOP_K, T_ALL), jnp.int32),
                   jax.ShapeDtypeStruct((TOP_K, T_ALL), F32),
                   jax.ShapeDtypeStruct((N_EXPERTS, LANES), F32)],
        scratch_shapes=[pltpu.VMEM((N_EXPERTS, LANES), F32)],
        compiler_params=_params("arbitrary"),
        name="router",
    )(logits_t, bias_col)


def _dest_kernel(topi_ref, rank_ref, ps_ref, o_ref):
    eidx = lax.broadcasted_iota(jnp.int32, (N_EXPERTS, TM), 0)
    ps = jnp.broadcast_to(ps_ref[...], (N_EXPERTS, TM))
    for k in range(TOP_K):
        start = jnp.sum(jnp.where(eidx == topi_ref[k:k + 1, :], ps, 0.0), axis=0, keepdims=True)
        o_ref[k:k + 1, :] = rank_ref[k:k + 1, :] + start.astype(jnp.int32)


def _dest(topi, rank, pstart_col):
    col = lambda i: (0, i)
    return pl.pallas_call(
        _dest_kernel,
        grid=(T_ALL // TM,),
        in_specs=[pl.BlockSpec((TOP_K, TM), col), pl.BlockSpec((TOP_K, TM), col),
                  pl.BlockSpec((N_EXPERTS, 1), lambda i: (0, 0))],
        out_specs=pl.BlockSpec((TOP_K, TM), col),
        out_shape=jax.ShapeDtypeStruct((TOP_K, T_ALL), jnp.int32),
        compiler_params=_params("arbitrary"),
        name="dest",
    )(topi, rank, pstart_col)


TD = 512
ROW_UNROLL = 4


def _dispatch_kernel(dest_ref, h_ref, xs_hbm, sem):
    def row(t, carry):
        src = h_ref.at[pl.ds(pl.multiple_of(t * SUB, SUB), SUB)]
        for k in range(TOP_K):
            d = dest_ref[t * TOP_K + k]
            pltpu.make_async_copy(src, xs_hbm.at[pl.ds(pl.multiple_of(d * SUB, SUB), SUB)],
                                  sem).start(priority=k % 2)
        return carry

    lax.fori_loop(0, TD, row, 0, unroll=ROW_UNROLL)
    for _ in range(TOP_K):
        pltpu.make_async_copy(h_ref, xs_hbm.at[pl.ds(0, TD * SUB)], sem).wait()


def _dispatch(dest_flat, h2):
    return pl.pallas_call(
        _dispatch_kernel,
        grid=(T_ALL // TD,),
        in_specs=[pl.BlockSpec((TD * TOP_K,), lambda i: (i,), memory_space=pltpu.SMEM),
                  pl.BlockSpec((TD * SUB, LANES), lambda i: (i, 0))],
        out_specs=pl.BlockSpec(memory_space=pl.ANY),
        out_shape=jax.ShapeDtypeStruct((N_SLOTS * SUB, LANES), F32),
        scratch_shapes=[pltpu.SemaphoreType.DMA(())],
        compiler_params=_params("arbitrary"),
        name="dispatch",
    )(dest_flat, h2)


_BLK_TILES = MOE_BLK * SUB
NBX = 5
NBO = 3


def _expert_kernel(bs_ref, x_hbm, wg_ref, wu_ref, wd_ref, o_hbm, wgu_s, wd_s, xbuf, obuf, xsem, osem):
    e = pl.program_id(0)
    b0 = bs_ref[e]
    b1 = bs_ref[e + 1]
    n_used = bs_ref[N_EXPERTS]

    def xcopy(g):
        s = lax.rem(g, NBX)
        return pltpu.make_async_copy(x_hbm.at[pl.ds(pl.multiple_of(g * _BLK_TILES, _BLK_TILES), _BLK_TILES)],
                                     xbuf.at[s], xsem.at[s])

    def ocopy(g):
        s = lax.rem(g, NBO)
        return pltpu.make_async_copy(obuf.at[s],
                                     o_hbm.at[pl.ds(pl.multiple_of(g * _BLK_TILES, _BLK_TILES), _BLK_TILES)],
                                     osem.at[s])

    @pl.when(e == 0)
    def _():
        for j in range(NBX - 1):
            @pl.when(j < n_used)
            def _():
                xcopy(j).start()

    @pl.when(b1 > b0)
    def _():
        wgu_s[:, 0:D_EXPERT] = wg_ref[...].astype(BF16)
        wgu_s[:, D_EXPERT:2 * D_EXPERT] = wu_ref[...].astype(BF16)
        wd_s[...] = wd_ref[...].astype(BF16)

    def block(g, carry):
        @pl.when(g + NBX - 1 < n_used)
        def _():
            xcopy(g + NBX - 1).start()

        xcopy(g).wait()

        @pl.when(g >= NBO)
        def _():
            ocopy(g - NBO).wait()

        gu = _dot(_load_rows(xbuf.at[lax.rem(g, NBX)], MOE_BLK).astype(BF16), wgu_s[...])
        a = _silu(gu[:, :D_EXPERT]) * gu[:, D_EXPERT:]
        _store_rows(obuf.at[lax.rem(g, NBO)], _dot(a.astype(BF16), wd_s[...]), MOE_BLK)
        ocopy(g).start(priority=1)
        return carry

    lax.fori_loop(b0, b1, block, 0)

    @pl.when(e == N_EXPERTS - 1)
    def _():
        for j in range(NBO, 0, -1):
            @pl.when(n_used >= j)
            def _():
                ocopy(n_used - j).wait()


def _experts(blk_start, x_sorted, w_gate, w_up, w_down):
    wsel = lambda e, bs: (e, 0, 0)
    grid_spec = pltpu.PrefetchScalarGridSpec(
        num_scalar_prefetch=1,
        grid=(N_EXPERTS,),
        in_specs=[pl.BlockSpec(memory_space=pl.ANY),
                  pl.BlockSpec((None, D_MODEL, D_EXPERT), wsel),
                  pl.BlockSpec((None, D_MODEL, D_EXPERT), wsel),
                  pl.BlockSpec((None, D_EXPERT, D_MODEL), wsel)],
        out_specs=pl.BlockSpec(memory_space=pl.ANY),
        scratch_shapes=[pltpu.VMEM((D_MODEL, 2 * D_EXPERT), BF16), pltpu.VMEM((D_EXPERT, D_MODEL), BF16),
                        pltpu.VMEM((NBX, _BLK_TILES, LANES), F32), pltpu.VMEM((NBO, _BLK_TILES, LANES), F32),
                        pltpu.SemaphoreType.DMA((NBX,)), pltpu.SemaphoreType.DMA((NBO,))],
    )
    return pl.pallas_call(
        _expert_kernel,
        grid_spec=grid_spec,
        out_shape=jax.ShapeDtypeStruct((N_SLOTS * SUB, LANES), F32),
        compiler_params=_params("arbitrary"),
        name="experts",
    )(blk_start, x_sorted, w_gate, w_up, w_down)


TC = 256


def _combine_kernel(dcur_ref, dnext_ref, eo_hbm, x1_ref, h2_ref, w_ref, mod_ref, g_ref, wgu_ref, wd_ref,
                    o_ref, buf, sem):
    i = pl.program_id(0)
    n = pl.num_programs(0)
    slot = lax.rem(i, 2)

    def gather(d_ref, s):
        def row(t, carry):
            for k in range(TOP_K):
                d = d_ref[t * TOP_K + k]
                pltpu.make_async_copy(eo_hbm.at[pl.ds(pl.multiple_of(d * SUB, SUB), SUB)],
                                      buf.at[s, k, pl.ds(pl.multiple_of(t * SUB, SUB), SUB)],
                                      sem.at[s]).start(priority=k % 2)
            return carry
        lax.fori_loop(0, TC, row, 0, unroll=ROW_UNROLL)

    @pl.when(i == 0)
    def _():
        gather(dcur_ref, 0)

    @pl.when(i + 1 < n)
    def _():
        gather(dnext_ref, 1 - slot)

    gate = mod_ref[:, 5 * D_MODEL:6 * D_MODEL]
    gu = _dot(_load_rows(h2_ref, TC).astype(BF16), wgu_ref[...])
    a = _silu(gu[:, :D_SHARED]) * gu[:, D_SHARED:]
    moe = _dot(a.astype(BF16), wd_ref[...])
    for k in range(TOP_K):
        pltpu.make_async_copy(eo_hbm.at[pl.ds(0, TC * SUB)], buf.at[slot, k], sem.at[slot]).wait()
    w = w_ref[...]
    for k in range(TOP_K):
        moe = moe + _load_rows(buf.at[slot, k], TC) * w[:, k:k + 1]
    o_ref[...] = x1_ref[...] + gate * _rms(moe, g_ref[...])


def _combine(dest_flat, eo, x1, h2, topw, mod3, mod_row, g, wgu, wd, tile0, n_tiles):
    row = lambda i: (tile0 + i, 0)
    const = lambda i: (0, 0)
    blk = TC * TOP_K
    return pl.pallas_call(
        _combine_kernel,
        grid=(n_tiles,),
        in_specs=[pl.BlockSpec((blk,), lambda i: (tile0 + i,), memory_space=pltpu.SMEM),
                  pl.BlockSpec((blk,), lambda i: (tile0 + jnp.minimum(i + 1, n_tiles - 1),),
                               memory_space=pltpu.SMEM),
                  pl.BlockSpec(memory_space=pl.ANY),
                  pl.BlockSpec((TC, D_MODEL), row), pl.BlockSpec((TC * SUB, LANES), row),
                  pl.BlockSpec((TC, LANES), row),
                  pl.BlockSpec((None, 1, 6 * D_MODEL), lambda i: (mod_row(i), 0, 0)),
                  pl.BlockSpec((1, D_MODEL), const),
                  pl.BlockSpec((D_MODEL, 2 * D_SHARED), const), pl.BlockSpec((D_SHARED, D_MODEL), const)],
        out_specs=pl.BlockSpec((TC, D_MODEL), lambda i: (i, 0)),
        out_shape=jax.ShapeDtypeStruct((n_tiles * TC, D_MODEL), F32),
        scratch_shapes=[pltpu.VMEM((2, TOP_K, TC * SUB, LANES), F32), pltpu.SemaphoreType.DMA((2,))],
        compiler_params=_params("arbitrary"),
        name="combine",
    )(dest_flat, dest_flat, eo, x1, h2, topw, mod3, g, wgu, wd)


def _qa_perm():
    cols = []
    for j in range(A_GROUP):
        for g in range(A_KV_HEADS):
            hd = g * A_GROUP + j
            cols.extend(range(hd * A_HEAD_DIM, (hd + 1) * A_HEAD_DIM))
    return np.asarray(cols, np.int32)


def _pad_heads(w, width, n_heads, offset, total=LANES):
    k = w.shape[0]
    stride = w.shape[1] // n_heads
    w3 = w.reshape(k, n_heads, stride)[:, :, offset:offset + width]
    w3 = jnp.pad(w3, ((0, 0), (0, 0), (0, total - width)))
    return w3.reshape(k, n_heads * total)


def kernel(x_prompt, x_sample, cache_k_swa, cache_v_swa, cache_ckv_mla, cache_krope_mla, c, c_ctx, w_ada, b_ada, norm_pre_mix, norm_post_mix, norm_pre_ffn, norm_post_ffn, w_in, sink_swa, q_norm_mla, kv_norm_mla, w_uq_mla, w_ukv_mla, w_o_swa, w_o_mla, w_out, router_w, router_bias, w_gate_exp, w_up_exp, w_down_exp, w_gate_sh, w_up_sh, w_down_sh):
    l = 0
    perm = _qa_perm()
    wi = w_in[l]
    kr_blk = jnp.pad(wi[:, _O_KR:_O_GATE], ((0, 0), (B_NOPE, LANES - B_NOPE - B_ROPE)))
    w1 = jnp.concatenate([wi[:, _O_QA:_O_KA][:, perm], wi[:, _O_KA:_O_KR], kr_blk], axis=1).astype(BF16)
    wg = wi[:, _O_GATE:].astype(BF16)
    wuq = _pad_heads(w_uq_mla[l], B_NOPE + B_ROPE, B_HEADS, 0).astype(BF16)
    wk = _pad_heads(w_ukv_mla[l], B_NOPE, B_HEADS, 0).astype(BF16)
    wv = w_ukv_mla[l].reshape(KV_LORA, B_HEADS, B_NOPE + B_V)[:, :, B_NOPE:].reshape(KV_LORA, B_HEADS * B_V).astype(BF16)
    woa = w_o_swa[l][perm, :].astype(BF16)
    wob = w_o_mla[l].astype(BF16)
    wout = w_out[l].astype(BF16)
    wr = router_w[l].T.astype(BF16)
    wgu_sh = jnp.concatenate([w_gate_sh[l], w_up_sh[l]], axis=1).astype(BF16)
    wd_sh = w_down_sh[l].astype(BF16)
    sink = sink_swa[l]
    row2 = lambda v: v.reshape(1, -1)

    cond8 = jnp.concatenate([c_ctx[None, :], c, jnp.zeros((8 - 1 - DEC_BATCH, D_MODEL), F32)], axis=0)
    mod3 = _modulation(cond8, w_ada[l], b_ada[l]).reshape(8, 1, 6 * D_MODEL)

    xp = x_prompt.reshape(T_CTX, D_MODEL)
    xs = x_sample.reshape(T_LAT, D_MODEL)
    per = DEC_SEQ // TM
    shared_in = (row2(norm_pre_mix[l]), w1, row2(q_norm_mla[l]), row2(kv_norm_mla[l]), wuq, wk, wv)
    qa_c, ka_c, va_c, ckv_c, kr_c, qb_c, kb_c, vb_c = _inproj(xp, mod3, lambda i: 0, *shared_in, None)
    qa_l, ka_l, va_l, qb_l, kb_l, vb_l = _inproj(xs, mod3, lambda i: 1 + i // per, *shared_in, _rope_tables())

    oa_c, ob_c = _ctx_attention(sink, qa_c, ka_c, va_c, qb_c, kb_c, vb_c)

    krp_cache = jnp.pad(cache_krope_mla[:, l].reshape(DEC_BATCH * PAST_LEN, B_ROPE),
                        ((0, 0), (B_NOPE, LANES - B_NOPE - B_ROPE)))
    kc_b, vc_b = _kv_expand(cache_ckv_mla[:, l].reshape(DEC_BATCH * PAST_LEN, KV_LORA), krp_cache, wk, wv)
    oa_l = _lat_swa(sink, qa_l, ka_l, va_l,
                    cache_k_swa[:, l].reshape(DEC_BATCH, PAST_LEN, LANES),
                    cache_v_swa[:, l].reshape(DEC_BATCH, PAST_LEN, LANES))
    ob_l = _lat_mla(qb_l, kb_l, vb_l, kc_b, vc_b)

    x1, h2, logits = _merge(xp, xs, oa_c, oa_l, ob_c, ob_l, mod3,
                            row2(norm_pre_mix[l]), row2(norm_post_mix[l]), row2(norm_pre_ffn[l]),
                            woa, wob, wout, wg, wr)

    topi8, rank8, topw8, counts_b = _router(logits, router_bias[l].reshape(N_EXPERTS, 1))
    counts = counts_b[:, 0].astype(jnp.int32)

    padded = (counts + MOE_BLK - 1) // MOE_BLK * MOE_BLK
    pend = jnp.cumsum(padded)
    pstart = pend - padded
    dest = _dest(topi8, rank8, pstart.astype(F32).reshape(N_EXPERTS, 1)).T.reshape(-1)
    topw_p = jnp.pad(topw8.T, ((0, 0), (0, LANES - TOP_K)))
    blk_start = (jnp.concatenate([pstart, pend[-1:]]) // MOE_BLK).astype(jnp.int32)

    x_sorted = _dispatch(dest, h2)
    eo = _experts(blk_start, x_sorted, w_gate_exp[l], w_up_exp[l], w_down_exp[l])

    g_post = row2(norm_post_ffn[l])
    per_c = DEC_SEQ // TC
    y_p = _combine(dest, eo, x1, h2, topw_p, mod3, lambda i: 0, g_post, wgu_sh, wd_sh, 0, T_CTX // TC)
    y_s = _combine(dest, eo, x1, h2, topw_p, mod3, lambda i: 1 + i // per_c, g_post, wgu_sh, wd_sh,
                   T_CTX // TC, T_LAT // TC)

    return (y_p.reshape(BATCH, SEQ, D_MODEL), y_s.reshape(DEC_BATCH, DEC_SEQ, D_MODEL),
            ka_c.reshape(BATCH, 1, SEQ, A_KV_HEADS, A_HEAD_DIM),
            va_c.reshape(BATCH, 1, SEQ, A_KV_HEADS, A_HEAD_DIM),
            ckv_c.reshape(BATCH, 1, SEQ, KV_LORA),
            kr_c[:, B_NOPE:B_NOPE + B_ROPE].reshape(BATCH, 1, SEQ, B_ROPE))
```

```python
import functools

import jax
import jax.numpy as jnp
import numpy as np
from jax import lax
from jax.experimental import pallas as pl
from jax.experimental.pallas import tpu as pltpu

D_MODEL = 1024
BATCH = 32
SEQ = 256
DEC_BATCH = 4
DEC_SEQ = 2048
PAST_LEN = 512
GRID_W = 64
WINDOW = 128
ROPE_BASE = 10000.0
EPS = 1e-6
NEG = -1e30
A_HEADS = 8
A_KV_HEADS = 2
A_GROUP = A_HEADS // A_KV_HEADS
A_HEAD_DIM = 64
B_HEADS = 8
B_NOPE = 64
B_ROPE = 32
B_V = 64
KV_LORA = 256
Q_LORA = 384
MLA_SCALE = (B_NOPE + B_ROPE) ** -0.5
LOG2E = 1.4426950408889634
N_EXPERTS = 256
TOP_K = 8
N_GROUPS = 8
TOPK_GROUPS = 4
D_EXPERT = 256
D_SHARED = 256
ROUTED_SCALE = 2.5

T_CTX = BATCH * SEQ
T_LAT = DEC_BATCH * DEC_SEQ
T_ALL = T_CTX + T_LAT

LANES = 128
TM = 512
TQ = 256
MOE_BLK = 256
N_SLOT_BLK = (T_ALL * TOP_K) // MOE_BLK + N_EXPERTS
N_SLOTS = N_SLOT_BLK * MOE_BLK
VMEM_LIMIT = 48 * 1024 * 1024

F32 = jnp.float32
BF16 = jnp.bfloat16

_O_QA = 0
_O_KA = _O_QA + A_HEADS * A_HEAD_DIM
_O_VA = _O_KA + A_KV_HEADS * A_HEAD_DIM
_O_QL = _O_VA + A_KV_HEADS * A_HEAD_DIM
_O_CKV = _O_QL + Q_LORA
_O_KR = _O_CKV + KV_LORA
_O_GATE = _O_KR + B_ROPE


def _params(*sem):
    return pltpu.CompilerParams(dimension_semantics=sem, vmem_limit_bytes=VMEM_LIMIT)


def _dot(a, b):
    return jnp.dot(a, b, preferred_element_type=F32)


def _dot_nt(a, b):
    return lax.dot_general(a, b, (((1,), (1,)), ((), ())), preferred_element_type=F32)


def _rms(x, g):
    return x * lax.rsqrt(jnp.mean(x * x, axis=-1, keepdims=True) + EPS) * g


def _sigmoid(x):
    return 1.0 / (1.0 + jnp.exp(-x))


SUB = 8
assert SUB * LANES == D_MODEL


def _load_rows(ref, n):
    return jnp.concatenate([ref[pl.ds(j, n, stride=SUB), :] for j in range(SUB)], axis=1)


def _store_rows(ref, val, n):
    for j in range(SUB):
        ref[pl.ds(j, n, stride=SUB), :] = val[:, j * LANES:(j + 1) * LANES]


def _silu(x):
    return x * _sigmoid(x)


def _mod_kernel(cond_ref, w_ref, b_ref, o_ref):
    s = _silu(cond_ref[...]).astype(BF16)
    o_ref[...] = _dot(s, w_ref[...].astype(BF16)) + b_ref[...]


def _modulation(cond8, w_ada, b_ada):
    tn = 768
    n = w_ada.shape[1]
    return pl.pallas_call(
        _mod_kernel,
        grid=(n // tn,),
        in_specs=[pl.BlockSpec((8, D_MODEL), lambda j: (0, 0)),
                  pl.BlockSpec((D_MODEL, tn), lambda j: (0, j)),
                  pl.BlockSpec((1, tn), lambda j: (0, j))],
        out_specs=pl.BlockSpec((8, tn), lambda j: (0, j)),
        out_shape=jax.ShapeDtypeStruct((8, n), F32),
        compiler_params=_params("arbitrary"),
        name="modulation",
    )(cond8, w_ada, b_ada.reshape(1, n))


def _rope(x, cos, sin, half):
    lane = lax.broadcasted_iota(jnp.int32, x.shape, 1)
    first = (lane & (2 * half - 1)) < half
    partner = jnp.where(first, pltpu.roll(x, LANES - half, 1), pltpu.roll(x, half, 1))
    return x * cos + partner * sin


def _rope_tables():
    t = np.arange(DEC_SEQ)
    row = (t // GRID_W).astype(np.float32)
    col = (t % GRID_W).astype(np.float32)

    def tables(d_rot):
        nf = d_rot // 4
        inv = np.float32(ROPE_BASE) ** (-np.arange(nf, dtype=np.float32) / np.float32(nf))
        ar = row[:, None] * inv
        ac = col[:, None] * inv
        cos = np.concatenate([np.cos(ar), np.cos(ar), np.cos(ac), np.cos(ac)], axis=1)
        sin = np.concatenate([-np.sin(ar), np.sin(ar), -np.sin(ac), np.sin(ac)], axis=1)
        return cos.astype(np.float32), sin.astype(np.float32)

    cos_a, sin_a = tables(A_HEAD_DIM)
    cos_a = np.concatenate([cos_a, cos_a], axis=1)
    sin_a = np.concatenate([sin_a, sin_a], axis=1)
    cos_b, sin_b = tables(B_ROPE)
    one = np.ones((DEC_SEQ, B_NOPE), np.float32)
    zero = np.zeros((DEC_SEQ, B_NOPE), np.float32)
    pad = LANES - B_NOPE - B_ROPE
    cos_b = np.concatenate([one, cos_b, one[:, :pad]], axis=1)
    sin_b = np.concatenate([zero, sin_b, zero[:, :pad]], axis=1)
    return tuple(jnp.asarray(a) for a in (cos_a, sin_a, cos_b, sin_b))


_N1 = 1536


def _inproj_kernel(latent, *refs):
    if latent:
        (x_ref, mod_ref, g_ref, w1_ref, qn_ref, kvn_ref, wuq_ref, wk_ref, wv_ref,
         ca_ref, sa_ref, cb_ref, sb_ref,
         qa_o, ka_o, va_o, qb_o, kb_o, vb_o) = refs
    else:
        (x_ref, mod_ref, g_ref, w1_ref, qn_ref, kvn_ref, wuq_ref, wk_ref, wv_ref,
         qa_o, ka_o, va_o, ckv_o, kr_o, qb_o, kb_o, vb_o) = refs
    x = x_ref[...]
    shift = mod_ref[:, 0:D_MODEL]
    scale = mod_ref[:, D_MODEL:2 * D_MODEL]
    h = _rms(x, g_ref[...]) * (1.0 + scale) + shift
    p = _dot(h.astype(BF16), w1_ref[...])
    qa = p[:, 0:512]
    ka = p[:, 512:640]
    va = p[:, 640:768]
    ql = p[:, 768:1152]
    ckv = p[:, 1152:1408]
    krp = p[:, 1408:1536]
    qn = _rms(ql, qn_ref[...])
    qb = _dot(qn.astype(BF16), wuq_ref[...])
    cn = _rms(ckv, kvn_ref[...])
    cnb = cn.astype(BF16)
    kn = _dot(cnb, wk_ref[...])
    vb = _dot(cnb, wv_ref[...])
    if latent:
        ca, sa, cb, sb = ca_ref[...], sa_ref[...], cb_ref[...], sb_ref[...]
        ka = _rope(ka, ca, sa, A_HEAD_DIM // 4)
        krp = _rope(krp, cb, sb, B_ROPE // 4)
        for j in range(4):
            blk = _rope(qa[:, j * LANES:(j + 1) * LANES], ca, sa, A_HEAD_DIM // 4)
            qa_o[:, j * LANES:(j + 1) * LANES] = (blk * (A_HEAD_DIM ** -0.5 * LOG2E)).astype(BF16)
        for hd in range(B_HEADS):
            blk = _rope(qb[:, hd * LANES:(hd + 1) * LANES], cb, sb, B_ROPE // 4)
            qb_o[:, hd * LANES:(hd + 1) * LANES] = (blk * (MLA_SCALE * LOG2E)).astype(BF16)
        ka_o[...] = ka.astype(BF16)
        va_o[...] = va.astype(BF16)
    else:
        qa_o[...] = (qa * (A_HEAD_DIM ** -0.5 * LOG2E)).astype(BF16)
        qb_o[...] = (qb * (MLA_SCALE * LOG2E)).astype(BF16)
        ka_o[...] = ka
        va_o[...] = va
        ckv_o[...] = cn
        kr_o[...] = krp
    for hd in range(B_HEADS):
        kb_o[:, hd * LANES:(hd + 1) * LANES] = (kn[:, hd * LANES:(hd + 1) * LANES] + krp).astype(BF16)
    vb_o[...] = vb.astype(BF16)


def _inproj(x, mod3, mod_row, g, w1, qn, kvn, wuq, wk, wv, tables):
    latent = tables is not None
    t = x.shape[0]
    n_tiles = t // TM
    row = lambda i: (i, 0)
    const = lambda i: (0, 0)
    in_specs = [pl.BlockSpec((TM, D_MODEL), row),
                pl.BlockSpec((None, 1, 6 * D_MODEL), lambda i: (mod_row(i), 0, 0)),
                pl.BlockSpec((1, D_MODEL), const),
                pl.BlockSpec((D_MODEL, _N1), const),
                pl.BlockSpec((1, Q_LORA), const),
                pl.BlockSpec((1, KV_LORA), const),
                pl.BlockSpec((Q_LORA, B_HEADS * LANES), const),
                pl.BlockSpec((KV_LORA, B_HEADS * LANES), const),
                pl.BlockSpec((KV_LORA, B_HEADS * B_V), const)]
    args = [x, mod3, g, w1, qn, kvn, wuq, wk, wv]
    sds = jax.ShapeDtypeStruct
    if latent:
        per = DEC_SEQ // TM
        tab = lambda i: (i % per, 0)
        in_specs += [pl.BlockSpec((TM, LANES), tab)] * 4
        args += list(tables)
        out_shape = [sds((t, 512), BF16), sds((t, LANES), BF16), sds((t, LANES), BF16),
                     sds((t, B_HEADS * LANES), BF16), sds((t, B_HEADS * LANES), BF16),
                     sds((t, B_HEADS * B_V), BF16)]
        widths = [512, LANES, LANES, B_HEADS * LANES, B_HEADS * LANES, B_HEADS * B_V]
    else:
        out_shape = [sds((t, 512), BF16), sds((t, LANES), F32), sds((t, LANES), F32),
                     sds((t, KV_LORA), F32), sds((t, LANES), F32),
                     sds((t, B_HEADS * LANES), BF16), sds((t, B_HEADS * LANES), BF16),
                     sds((t, B_HEADS * B_V), BF16)]
        widths = [512, LANES, LANES, KV_LORA, LANES, B_HEADS * LANES, B_HEADS * LANES, B_HEADS * B_V]
    out_specs = [pl.BlockSpec((TM, w), row) for w in widths]
    return pl.pallas_call(
        functools.partial(_inproj_kernel, latent),
        grid=(n_tiles,),
        in_specs=in_specs,
        out_specs=out_specs,
        out_shape=out_shape,
        compiler_params=_params("arbitrary"),
        name="inproj_latent" if latent else "inproj_ctx",
    )(*args)


def _kvexp_kernel(ckv_ref, krp_ref, wk_ref, wv_ref, kb_o, vb_o):
    cb = ckv_ref[...].astype(BF16)
    kn = _dot(cb, wk_ref[...])
    krp = krp_ref[...]
    for hd in range(B_HEADS):
        kb_o[:, hd * LANES:(hd + 1) * LANES] = (kn[:, hd * LANES:(hd + 1) * LANES] + krp).astype(BF16)
    vb_o[...] = _dot(cb, wv_ref[...]).astype(BF16)


def _kv_expand(ckv, krp, wk, wv):
    t = ckv.shape[0]
    row = lambda i: (i, 0)
    const = lambda i: (0, 0)
    return pl.pallas_call(
        _kvexp_kernel,
        grid=(t // TM,),
        in_specs=[pl.BlockSpec((TM, KV_LORA), row), pl.BlockSpec((TM, LANES), row),
                  pl.BlockSpec((KV_LORA, B_HEADS * LANES), const),
                  pl.BlockSpec((KV_LORA, B_HEADS * B_V), const)],
        out_specs=[pl.BlockSpec((TM, B_HEADS * LANES), row), pl.BlockSpec((TM, B_HEADS * B_V), row)],
        out_shape=[jax.ShapeDtypeStruct((t, B_HEADS * LANES), BF16),
                   jax.ShapeDtypeStruct((t, B_HEADS * B_V), BF16)],
        compiler_params=_params("arbitrary"),
        name="kv_expand",
    )(ckv, krp, wk, wv)


def _lo_mask(rows):
    return lax.broadcasted_iota(jnp.int32, (rows, LANES), 1) < (LANES // 2)


def _softmax_pv(parts, sink):
    if sink is not None:
        sink = sink * LOG2E
    m = None
    for s, _ in parts:
        mi = jnp.max(s, axis=-1, keepdims=True)
        m = mi if m is None else jnp.maximum(m, mi)
    if sink is not None:
        m = jnp.maximum(m, sink)
    den = None
    acc = None
    for s, v in parts:
        p = jnp.exp2(s - m)
        di = jnp.sum(p, axis=-1, keepdims=True)
        den = di if den is None else den + di
        oi = _dot(p.astype(BF16), v)
        acc = oi if acc is None else acc + oi
    if sink is not None:
        den = den + jnp.exp2(sink - m)
    return acc / den


def _ctx_attn_kernel(sink_ref, qa_ref, ka_ref, va_ref, qb_ref, kb_ref, vb_ref, oa_ref, ob_ref):
    lo = _lo_mask(SEQ)
    k = ka_ref[...].astype(BF16)
    v = va_ref[...].astype(BF16)
    for j in range(A_GROUP):
        q = qa_ref[:, j * LANES:(j + 1) * LANES]
        zero = jnp.zeros_like(q)
        o0 = _softmax_pv([(_dot_nt(jnp.where(lo, q, zero), k), v)], sink_ref[j])
        o1 = _softmax_pv([(_dot_nt(jnp.where(lo, zero, q), k), v)], sink_ref[A_GROUP + j])
        oa_ref[:, j * LANES:(j + 1) * LANES] = jnp.where(lo, o0, o1).astype(BF16)
    for j in range(B_HEADS // 2):
        vp = vb_ref[:, j * LANES:(j + 1) * LANES]
        outs = []
        for hd in (2 * j, 2 * j + 1):
            s = _dot_nt(qb_ref[:, hd * LANES:(hd + 1) * LANES], kb_ref[:, hd * LANES:(hd + 1) * LANES])
            outs.append(_softmax_pv([(s, vp)], None))
        ob_ref[:, j * LANES:(j + 1) * LANES] = jnp.where(lo, outs[0], outs[1]).astype(BF16)


def _ctx_attention(sink, qa, ka, va, qb, kb, vb):
    blk = lambda w: pl.BlockSpec((SEQ, w), lambda b: (b, 0))
    return pl.pallas_call(
        _ctx_attn_kernel,
        grid=(BATCH,),
        in_specs=[pl.BlockSpec(memory_space=pltpu.SMEM),
                  blk(512), blk(LANES), blk(LANES), blk(B_HEADS * LANES), blk(B_HEADS * LANES),
                  blk(B_HEADS * B_V)],
        out_specs=[blk(512), blk(512)],
        out_shape=[jax.ShapeDtypeStruct((T_CTX, 512), BF16), jax.ShapeDtypeStruct((T_CTX, 512), BF16)],
        compiler_params=_params("arbitrary"),
        name="ctx_attention",
    )(sink, qa, ka, va, qb, kb, vb)


_WIN = TQ + 2 * WINDOW


def _lat_swa_kernel(sink_ref, q_ref, k_ref, v_ref, kc_ref, vc_ref, o_ref):
    qi = pl.program_id(1)
    q0 = qi * TQ
    start = pl.multiple_of(jnp.clip(q0 - WINDOW, 0, DEC_SEQ - _WIN), WINDOW)
    kw = k_ref[pl.ds(start, _WIN), :]
    vw = v_ref[pl.ds(start, _WIN), :]
    kc = kc_ref[...].astype(BF16)
    vc = vc_ref[...].astype(BF16)
    qpos = q0 + lax.broadcasted_iota(jnp.int32, (TQ, _WIN), 0)
    kpos = start + lax.broadcasted_iota(jnp.int32, (TQ, _WIN), 1)
    valid = jnp.abs(qpos - kpos) <= WINDOW
    lo = _lo_mask(TQ)
    for j in range(A_GROUP):
        q = q_ref[:, j * LANES:(j + 1) * LANES]
        zero = jnp.zeros_like(q)
        outs = []
        for g, qh in ((0, jnp.where(lo, q, zero)), (1, jnp.where(lo, zero, q))):
            sw = jnp.where(valid, _dot_nt(qh, kw), NEG)
            sc = _dot_nt(qh, kc)
            outs.append(_softmax_pv([(sw, vw), (sc, vc)], sink_ref[g * A_GROUP + j]))
        o_ref[:, j * LANES:(j + 1) * LANES] = jnp.where(lo, outs[0], outs[1]).astype(BF16)


def _lat_swa(sink, qa, ka, va, kc, vc):
    nq = DEC_SEQ // TQ
    return pl.pallas_call(
        _lat_swa_kernel,
        grid=(DEC_BATCH, nq),
        in_specs=[pl.BlockSpec(memory_space=pltpu.SMEM),
                  pl.BlockSpec((TQ, 512), lambda b, i: (b * nq + i, 0)),
                  pl.BlockSpec((None, DEC_SEQ, LANES), lambda b, i: (b, 0, 0)),
                  pl.BlockSpec((None, DEC_SEQ, LANES), lambda b, i: (b, 0, 0)),
                  pl.BlockSpec((None, PAST_LEN, LANES), lambda b, i: (b, 0, 0)),
                  pl.BlockSpec((None, PAST_LEN, LANES), lambda b, i: (b, 0, 0))],
        out_specs=pl.BlockSpec((TQ, 512), lambda b, i: (b * nq + i, 0)),
        out_shape=jax.ShapeDtypeStruct((T_LAT, 512), BF16),
        compiler_params=_params("arbitrary", "arbitrary"),
        name="latent_swa",
    )(sink, qa, ka.reshape(DEC_BATCH, DEC_SEQ, LANES), va.reshape(DEC_BATCH, DEC_SEQ, LANES), kc, vc)


TQ_MLA = 256


def _lat_mla_kernel(q_ref, kl_ref, vl_ref, kc_ref, vc_ref, o_ref):
    lo = _lo_mask(TQ_MLA)
    vl = vl_ref[...]
    vc = vc_ref[...]
    outs = []
    for hh in range(2):
        q = q_ref[:, hh * LANES:(hh + 1) * LANES]
        sc = _dot_nt(q, kc_ref[:, hh * LANES:(hh + 1) * LANES])
        sl = _dot_nt(q, kl_ref[:, hh * LANES:(hh + 1) * LANES])
        outs.append(_softmax_pv([(sc, vc), (sl, vl)], None))
    o_ref[...] = jnp.where(lo, outs[0], outs[1]).astype(BF16)


def _lat_mla(qb, kb, vb, kc, vc):
    nq = DEC_SEQ // TQ_MLA
    npair = B_HEADS // 2
    return pl.pallas_call(
        _lat_mla_kernel,
        grid=(DEC_BATCH, npair, nq),
        in_specs=[pl.BlockSpec((TQ_MLA, 2 * LANES), lambda b, p, i: (b * nq + i, p)),
                  pl.BlockSpec((None, DEC_SEQ, 2 * LANES), lambda b, p, i: (b, 0, p)),
                  pl.BlockSpec((None, DEC_SEQ, LANES), lambda b, p, i: (b, 0, p)),
                  pl.BlockSpec((None, PAST_LEN, 2 * LANES), lambda b, p, i: (b, 0, p)),
                  pl.BlockSpec((None, PAST_LEN, LANES), lambda b, p, i: (b, 0, p))],
        out_specs=pl.BlockSpec((TQ_MLA, LANES), lambda b, p, i: (b * nq + i, p)),
        out_shape=jax.ShapeDtypeStruct((T_LAT, 512), BF16),
        compiler_params=_params("arbitrary", "arbitrary", "arbitrary"),
        name="latent_mla",
    )(qb, kb.reshape(DEC_BATCH, DEC_SEQ, B_HEADS * LANES), vb.reshape(DEC_BATCH, DEC_SEQ, B_HEADS * B_V),
      kc.reshape(DEC_BATCH, PAST_LEN, B_HEADS * LANES), vc.reshape(DEC_BATCH, PAST_LEN, B_HEADS * B_V))


_N_CTX_TILES = T_CTX // TM


def _merge_kernel(xp_ref, xs_ref, oac_ref, oal_ref, obc_ref, obl_ref, mod_ref,
                  gpre_ref, gpost_ref, gffn_ref, woa_ref, wob_ref, wout_ref, wg_ref, wr_ref,
                  x1_o, h2_o, lg_o):
    def body(x_ref, oa_ref, ob_ref):
        x = x_ref[...]
        sh1 = mod_ref[:, 0:D_MODEL]
        sc1 = mod_ref[:, D_MODEL:2 * D_MODEL]
        g1 = mod_ref[:, 2 * D_MODEL:3 * D_MODEL]
        sh2 = mod_ref[:, 3 * D_MODEL:4 * D_MODEL]
        sc2 = mod_ref[:, 4 * D_MODEL:5 * D_MODEL]
        h = (_rms(x, gpre_ref[...]) * (1.0 + sc1) + sh1).astype(BF16)
        gates = _dot(h, wg_ref[...])
        ya = _dot(oa_ref[...], woa_ref[...])
        yb = _dot(ob_ref[...], wob_ref[...])
        y = _sigmoid(gates[:, :D_MODEL]) * ya + _sigmoid(gates[:, D_MODEL:]) * yb
        mix = _dot(y.astype(BF16), wout_ref[...])
        x1 = x + g1 * _rms(mix, gpost_ref[...])
        x1_o[...] = x1
        h2 = _rms(x1, gffn_ref[...]) * (1.0 + sc2) + sh2
        _store_rows(h2_o, h2, TM)
        lg_o[...] = _dot_nt(wr_ref[...], h2.astype(BF16))

    is_ctx = pl.program_id(0) < _N_CTX_TILES

    @pl.when(is_ctx)
    def _():
        body(xp_ref, oac_ref, obc_ref)

    @pl.when(jnp.logical_not(is_ctx))
    def _():
        body(xs_ref, oal_ref, obl_ref)


def _mod_row_all(i):
    per = DEC_SEQ // TM
    return jnp.where(i < _N_CTX_TILES, 0, 1 + (jnp.maximum(i - _N_CTX_TILES, 0)) // per)


def _merge(xp, xs, oac, oal, obc, obl, mod3, gpre, gpost, gffn, woa, wob, wout, wg, wr):
    ctx = lambda i: (jnp.minimum(i, _N_CTX_TILES - 1), 0)
    lat = lambda i: (jnp.maximum(i - _N_CTX_TILES, 0), 0)
    row = lambda i: (i, 0)
    const = lambda i: (0, 0)
    return pl.pallas_call(
        _merge_kernel,
        grid=(T_ALL // TM,),
        in_specs=[pl.BlockSpec((TM, D_MODEL), ctx), pl.BlockSpec((TM, D_MODEL), lat),
                  pl.BlockSpec((TM, 512), ctx), pl.BlockSpec((TM, 512), lat),
                  pl.BlockSpec((TM, 512), ctx), pl.BlockSpec((TM, 512), lat),
                  pl.BlockSpec((None, 1, 6 * D_MODEL), lambda i: (_mod_row_all(i), 0, 0)),
                  pl.BlockSpec((1, D_MODEL), const), pl.BlockSpec((1, D_MODEL), const),
                  pl.BlockSpec((1, D_MODEL), const),
                  pl.BlockSpec((512, D_MODEL), const), pl.BlockSpec((512, D_MODEL), const),
                  pl.BlockSpec((D_MODEL, D_MODEL), const), pl.BlockSpec((D_MODEL, 2 * D_MODEL), const),
                  pl.BlockSpec((N_EXPERTS, D_MODEL), const)],
        out_specs=[pl.BlockSpec((TM, D_MODEL), row), pl.BlockSpec((TM * SUB, LANES), row),
                   pl.BlockSpec((N_EXPERTS, TM), lambda i: (0, i))],
        out_shape=[jax.ShapeDtypeStruct((T_ALL, D_MODEL), F32), jax.ShapeDtypeStruct((T_ALL * SUB, LANES), F32),
                   jax.ShapeDtypeStruct((N_EXPERTS, T_ALL), F32)],
        compiler_params=_params("arbitrary"),
        name="merge",
    )(xp, xs, oac, oal, obc, obl, mod3, gpre, gpost, gffn, woa, wob, wout, wg, wr)


_PER_GROUP = N_EXPERTS // N_GROUPS


def _router_kernel(lg_ref, bias_ref, topi_o, rank_o, topw_o, cnt_o, cnt_s):
    @pl.when(pl.program_id(0) == 0)
    def _():
        cnt_s[...] = jnp.zeros_like(cnt_s)

    scores = _sigmoid(lg_ref[...])
    sel = scores + bias_ref[...]
    eidx = lax.broadcasted_iota(jnp.int32, (N_EXPERTS, TM), 0).astype(F32)
    ninf = -jnp.inf
    big = float(N_EXPERTS)
    sel_g = [sel[g * _PER_GROUP:(g + 1) * _PER_GROUP, :] for g in range(N_GROUPS)]
    gs = []
    for sg in sel_g:
        m1 = jnp.max(sg, axis=0, keepdims=True)
        top = sg == m1
        n_top = jnp.sum(jnp.where(top, 1.0, 0.0), axis=0, keepdims=True)
        rest = jnp.max(jnp.where(top, ninf, sg), axis=0, keepdims=True)
        gs.append(m1 + jnp.where(n_top > 1.5, m1, rest))
    cur_g = []
    for g in range(N_GROUPS):
        beat = jnp.zeros((1, TM), F32)
        for g2 in range(N_GROUPS):
            if g2 == g:
                continue
            better = (gs[g2] >= gs[g]) if g2 < g else (gs[g2] > gs[g])
            beat = beat + jnp.where(better, 1.0, 0.0)
        cur_g.append(jnp.where(beat < TOPK_GROUPS, sel_g[g], NEG))
    cur = jnp.concatenate(cur_g, axis=0)
    chosen = jnp.zeros((N_EXPERTS, TM), F32)
    idxs, ws = [], []
    for k in range(TOP_K):
        m = jnp.max(cur, axis=0, keepdims=True)
        idx = jnp.min(jnp.where(cur == m, eidx, big), axis=0, keepdims=True)
        hit = eidx == idx
        ws.append(jnp.sum(jnp.where(hit, scores, 0.0), axis=0, keepdims=True))
        cur = jnp.where(hit, ninf, cur)
        chosen = jnp.where(hit, 1.0, chosen)
        idxs.append(idx)
    wsum = ws[0]
    for k in range(1, TOP_K):
        wsum = wsum + ws[k]
    for k in range(TOP_K):
        topw_o[k:k + 1, :] = ws[k] / wsum * ROUTED_SCALE
        topi_o[k:k + 1, :] = idxs[k].astype(jnp.int32)
    r_i = lax.broadcasted_iota(jnp.int32, (TM, TM), 0)
    c_i = lax.broadcasted_iota(jnp.int32, (TM, TM), 1)
    earlier = jnp.where(r_i < c_i, 1.0, 0.0).astype(BF16)
    prefix = _dot(chosen.astype(BF16), earlier) + cnt_s[:, 0:1]
    for k in range(TOP_K):
        r = jnp.sum(jnp.where(eidx == idxs[k], prefix, 0.0), axis=0, keepdims=True)
        rank_o[k:k + 1, :] = r.astype(jnp.int32)
    total = cnt_s[...] + jnp.sum(chosen, axis=1, keepdims=True)
    cnt_s[...] = total
    cnt_o[...] = total


def _router(logits_t, bias_col):
    col = lambda i: (0, i)
    const = lambda i: (0, 0)
    return pl.pallas_call(
        _router_kernel,
        grid=(T_ALL // TM,),
        in_specs=[pl.BlockSpec((N_EXPERTS, TM), col), pl.BlockSpec((N_EXPERTS, 1), const)],
        out_specs=[pl.BlockSpec((TOP_K, TM), col), pl.BlockSpec((TOP_K, TM), col),
                   pl.BlockSpec((TOP_K, TM), col), pl.BlockSpec((N_EXPERTS, LANES), const)],
        out_shape=[jax.ShapeDtypeStruct((TOP_K, T_ALL), jnp.int32),
                   jax.ShapeDtypeStruct((TOP_K, T_ALL), jnp.int32),
                   jax.ShapeDtypeStruct((TOP_K, T_ALL), F32),
                   jax.ShapeDtypeStruct((N_EXPERTS, LANES), F32)],
        scratch_shapes=[pltpu.VMEM((N_EXPERTS, LANES), F32)],
        compiler_params=_params("arbitrary"),
        name="router",
    )(logits_t, bias_col)


def _dest_kernel(topi_ref, rank_ref, ps_ref, o_ref):
    eidx = lax.broadcasted_iota(jnp.int32, (N_EXPERTS, TM), 0)
    ps = jnp.broadcast_to(ps_ref[...], (N_EXPERTS, TM))
    for k in range(TOP_K):
        start = jnp.sum(jnp.where(eidx == topi_ref[k:k + 1, :], ps, 0.0), axis=0, keepdims=True)
        o_ref[k:k + 1, :] = rank_ref[k:k + 1, :] + start.astype(jnp.int32)


def _dest(topi, rank, pstart_col):
    col = lambda i: (0, i)
    return pl.pallas_call(
        _dest_kernel,
        grid=(T_ALL // TM,),
        in_specs=[pl.BlockSpec((TOP_K, TM), col), pl.BlockSpec((TOP_K, TM), col),
                  pl.BlockSpec((N_EXPERTS, 1), lambda i: (0, 0))],
        out_specs=pl.BlockSpec((TOP_K, TM), col),
        out_shape=jax.ShapeDtypeStruct((TOP_K, T_ALL), jnp.int32),
        compiler_params=_params("arbitrary"),
        name="dest",
    )(topi, rank, pstart_col)


TD = 512
DISPATCH_PARTS = 4
ROW_UNROLL = 4


def _dispatch_kernel(dest_ref, h_ref, wgu_ref, wd_ref, xs_hbm, sh_o, sem):
    def row(t, carry):
        src = h_ref.at[pl.ds(pl.multiple_of(t * SUB, SUB), SUB)]
        for k in range(TOP_K):
            d = dest_ref[t * TOP_K + k]
            pltpu.make_async_copy(src, xs_hbm.at[pl.ds(pl.multiple_of(d * SUB, SUB), SUB)],
                                  sem).start(priority=k % 2)
        return carry

    part = TD // DISPATCH_PARTS
    for p in range(DISPATCH_PARTS):
        lax.fori_loop(p * part, (p + 1) * part, row, 0, unroll=ROW_UNROLL)
        x = _load_rows(h_ref.at[pl.ds(p * part * SUB, part * SUB)], part).astype(BF16)
        gu = _dot(x, wgu_ref[...])
        a = _silu(gu[:, :D_SHARED]) * gu[:, D_SHARED:]
        sh_o[p * part:(p + 1) * part, :] = _dot(a.astype(BF16), wd_ref[...])
    for _ in range(TOP_K):
        pltpu.make_async_copy(h_ref, xs_hbm.at[pl.ds(0, TD * SUB)], sem).wait()


def _dispatch(dest_flat, h2, wgu, wd):
    const = lambda i: (0, 0)
    return pl.pallas_call(
        _dispatch_kernel,
        grid=(T_ALL // TD,),
        in_specs=[pl.BlockSpec((TD * TOP_K,), lambda i: (i,), memory_space=pltpu.SMEM),
                  pl.BlockSpec((TD * SUB, LANES), lambda i: (i, 0)),
                  pl.BlockSpec((D_MODEL, 2 * D_SHARED), const), pl.BlockSpec((D_SHARED, D_MODEL), const)],
        out_specs=[pl.BlockSpec(memory_space=pl.ANY), pl.BlockSpec((TD, D_MODEL), lambda i: (i, 0))],
        out_shape=[jax.ShapeDtypeStruct((N_SLOTS * SUB, LANES), F32),
                   jax.ShapeDtypeStruct((T_ALL, D_MODEL), F32)],
        scratch_shapes=[pltpu.SemaphoreType.DMA(())],
        compiler_params=_params("arbitrary"),
        name="dispatch",
    )(dest_flat, h2, wgu, wd)


_BLK_TILES = MOE_BLK * SUB
NBX = 5
NBO = 3


def _expert_kernel(bs_ref, x_hbm, wg_ref, wu_ref, wd_ref, o_hbm, wgu_s, wd_s, xbuf, obuf, xsem, osem):
    e = pl.program_id(0)
    b0 = bs_ref[e]
    b1 = bs_ref[e + 1]
    n_used = bs_ref[N_EXPERTS]

    def xcopy(g):
        s = lax.rem(g, NBX)
        return pltpu.make_async_copy(x_hbm.at[pl.ds(pl.multiple_of(g * _BLK_TILES, _BLK_TILES), _BLK_TILES)],
                                     xbuf.at[s], xsem.at[s])

    def ocopy(g):
        s = lax.rem(g, NBO)
        return pltpu.make_async_copy(obuf.at[s],
                                     o_hbm.at[pl.ds(pl.multiple_of(g * _BLK_TILES, _BLK_TILES), _BLK_TILES)],
                                     osem.at[s])

    @pl.when(e == 0)
    def _():
        for j in range(NBX - 1):
            @pl.when(j < n_used)
            def _():
                xcopy(j).start()

    @pl.when(b1 > b0)
    def _():
        wgu_s[:, 0:D_EXPERT] = wg_ref[...].astype(BF16)
        wgu_s[:, D_EXPERT:2 * D_EXPERT] = wu_ref[...].astype(BF16)
        wd_s[...] = wd_ref[...].astype(BF16)

    def block(g, carry):
        @pl.when(g + NBX - 1 < n_used)
        def _():
            xcopy(g + NBX - 1).start()

        xcopy(g).wait()

        @pl.when(g >= NBO)
        def _():
            ocopy(g - NBO).wait()

        gu = _dot(_load_rows(xbuf.at[lax.rem(g, NBX)], MOE_BLK).astype(BF16), wgu_s[...])
        a = _silu(gu[:, :D_EXPERT]) * gu[:, D_EXPERT:]
        _store_rows(obuf.at[lax.rem(g, NBO)], _dot(a.astype(BF16), wd_s[...]), MOE_BLK)
        ocopy(g).start(priority=1)
        return carry

    lax.fori_loop(b0, b1, block, 0)

    @pl.when(e == N_EXPERTS - 1)
    def _():
        for j in range(NBO, 0, -1):
            @pl.when(n_used >= j)
            def _():
                ocopy(n_used - j).wait()


def _experts(blk_start, x_sorted, w_gate, w_up, w_down):
    wsel = lambda e, bs: (e, 0, 0)
    grid_spec = pltpu.PrefetchScalarGridSpec(
        num_scalar_prefetch=1,
        grid=(N_EXPERTS,),
        in_specs=[pl.BlockSpec(memory_space=pl.ANY),
                  pl.BlockSpec((None, D_MODEL, D_EXPERT), wsel),
                  pl.BlockSpec((None, D_MODEL, D_EXPERT), wsel),
                  pl.BlockSpec((None, D_EXPERT, D_MODEL), wsel)],
        out_specs=pl.BlockSpec(memory_space=pl.ANY),
        scratch_shapes=[pltpu.VMEM((D_MODEL, 2 * D_EXPERT), BF16), pltpu.VMEM((D_EXPERT, D_MODEL), BF16),
                        pltpu.VMEM((NBX, _BLK_TILES, LANES), F32), pltpu.VMEM((NBO, _BLK_TILES, LANES), F32),
                        pltpu.SemaphoreType.DMA((NBX,)), pltpu.SemaphoreType.DMA((NBO,))],
    )
    return pl.pallas_call(
        _expert_kernel,
        grid_spec=grid_spec,
        out_shape=jax.ShapeDtypeStruct((N_SLOTS * SUB, LANES), F32),
        compiler_params=_params("arbitrary"),
        name="experts",
    )(blk_start, x_sorted, w_gate, w_up, w_down)


TC = 256


def _combine_kernel(dcur_ref, dnext_ref, eo_hbm, x1_ref, sh_ref, w_ref, mod_ref, g_ref, o_ref, buf, sem):
    i = pl.program_id(0)
    n = pl.num_programs(0)
    slot = lax.rem(i, 2)

    def gather(d_ref, s, t0, t1):
        def row(t, carry):
            for k in range(TOP_K):
                d = d_ref[t * TOP_K + k]
                pltpu.make_async_copy(eo_hbm.at[pl.ds(pl.multiple_of(d * SUB, SUB), SUB)],
                                      buf.at[s, k, pl.ds(pl.multiple_of(t * SUB, SUB), SUB)],
                                      sem.at[s]).start(priority=k % 2)
            return carry
        lax.fori_loop(t0, t1, row, 0, unroll=ROW_UNROLL)

    def gather_next(part):
        @pl.when(i + 1 < n)
        def _():
            gather(dnext_ref, 1 - slot, part * (TC // TOP_K), (part + 1) * (TC // TOP_K))

    @pl.when(i == 0)
    def _():
        gather(dcur_ref, 0, 0, TC)

    gather_next(0)
    gate = mod_ref[:, 5 * D_MODEL:6 * D_MODEL]
    moe = sh_ref[...]
    for k in range(TOP_K):
        pltpu.make_async_copy(eo_hbm.at[pl.ds(0, TC * SUB)], buf.at[slot, k], sem.at[slot]).wait()
    w = w_ref[...]
    for k in range(TOP_K):
        if k + 1 < TOP_K:
            gather_next(k + 1)
        moe = moe + _load_rows(buf.at[slot, k], TC) * w[:, k:k + 1]
    o_ref[...] = x1_ref[...] + gate * _rms(moe, g_ref[...])


def _combine(dest_flat, eo, x1, shared, topw, mod3, mod_row, g, tile0, n_tiles):
    row = lambda i: (tile0 + i, 0)
    const = lambda i: (0, 0)
    blk = TC * TOP_K
    return pl.pallas_call(
        _combine_kernel,
        grid=(n_tiles,),
        in_specs=[pl.BlockSpec((blk,), lambda i: (tile0 + i,), memory_space=pltpu.SMEM),
                  pl.BlockSpec((blk,), lambda i: (tile0 + jnp.minimum(i + 1, n_tiles - 1),),
                               memory_space=pltpu.SMEM),
                  pl.BlockSpec(memory_space=pl.ANY),
                  pl.BlockSpec((TC, D_MODEL), row), pl.BlockSpec((TC, D_MODEL), row),
                  pl.BlockSpec((TC, LANES), row),
                  pl.BlockSpec((None, 1, 6 * D_MODEL), lambda i: (mod_row(i), 0, 0)),
                  pl.BlockSpec((1, D_MODEL), const)],
        out_specs=pl.BlockSpec((TC, D_MODEL), lambda i: (i, 0)),
        out_shape=jax.ShapeDtypeStruct((n_tiles * TC, D_MODEL), F32),
        scratch_shapes=[pltpu.VMEM((2, TOP_K, TC * SUB, LANES), F32), pltpu.SemaphoreType.DMA((2,))],
        compiler_params=_params("arbitrary"),
        name="combine",
    )(dest_flat, dest_flat, eo, x1, shared, topw, mod3, g)


def _qa_perm():
    cols = []
    for j in range(A_GROUP):
        for g in range(A_KV_HEADS):
            hd = g * A_GROUP + j
            cols.extend(range(hd * A_HEAD_DIM, (hd + 1) * A_HEAD_DIM))
    return np.asarray(cols, np.int32)


def _pad_heads(w, width, n_heads, offset, total=LANES):
    k = w.shape[0]
    stride = w.shape[1] // n_heads
    w3 = w.reshape(k, n_heads, stride)[:, :, offset:offset + width]
    w3 = jnp.pad(w3, ((0, 0), (0, 0), (0, total - width)))
    return w3.reshape(k, n_heads * total)


def kernel(x_prompt, x_sample, cache_k_swa, cache_v_swa, cache_ckv_mla, cache_krope_mla, c, c_ctx, w_ada, b_ada, norm_pre_mix, norm_post_mix, norm_pre_ffn, norm_post_ffn, w_in, sink_swa, q_norm_mla, kv_norm_mla, w_uq_mla, w_ukv_mla, w_o_swa, w_o_mla, w_out, router_w, router_bias, w_gate_exp, w_up_exp, w_down_exp, w_gate_sh, w_up_sh, w_down_sh):
    l = 0
    perm = _qa_perm()
    wi = w_in[l]
    kr_blk = jnp.pad(wi[:, _O_KR:_O_GATE], ((0, 0), (B_NOPE, LANES - B_NOPE - B_ROPE)))
    w1 = jnp.concatenate([wi[:, _O_QA:_O_KA][:, perm], wi[:, _O_KA:_O_KR], kr_blk], axis=1).astype(BF16)
    wg = wi[:, _O_GATE:].astype(BF16)
    wuq = _pad_heads(w_uq_mla[l], B_NOPE + B_ROPE, B_HEADS, 0).astype(BF16)
    wk = _pad_heads(w_ukv_mla[l], B_NOPE, B_HEADS, 0).astype(BF16)
    wv = w_ukv_mla[l].reshape(KV_LORA, B_HEADS, B_NOPE + B_V)[:, :, B_NOPE:].reshape(KV_LORA, B_HEADS * B_V).astype(BF16)
    woa = w_o_swa[l][perm, :].astype(BF16)
    wob = w_o_mla[l].astype(BF16)
    wout = w_out[l].astype(BF16)
    wr = router_w[l].T.astype(BF16)
    wgu_sh = jnp.concatenate([w_gate_sh[l], w_up_sh[l]], axis=1).astype(BF16)
    wd_sh = w_down_sh[l].astype(BF16)
    sink = sink_swa[l]
    row2 = lambda v: v.reshape(1, -1)

    cond8 = jnp.concatenate([c_ctx[None, :], c, jnp.zeros((8 - 1 - DEC_BATCH, D_MODEL), F32)], axis=0)
    mod3 = _modulation(cond8, w_ada[l], b_ada[l]).reshape(8, 1, 6 * D_MODEL)

    xp = x_prompt.reshape(T_CTX, D_MODEL)
    xs = x_sample.reshape(T_LAT, D_MODEL)
    per = DEC_SEQ // TM
    shared_in = (row2(norm_pre_mix[l]), w1, row2(q_norm_mla[l]), row2(kv_norm_mla[l]), wuq, wk, wv)
    qa_c, ka_c, va_c, ckv_c, kr_c, qb_c, kb_c, vb_c = _inproj(xp, mod3, lambda i: 0, *shared_in, None)
    qa_l, ka_l, va_l, qb_l, kb_l, vb_l = _inproj(xs, mod3, lambda i: 1 + i // per, *shared_in, _rope_tables())

    oa_c, ob_c = _ctx_attention(sink, qa_c, ka_c, va_c, qb_c, kb_c, vb_c)

    krp_cache = jnp.pad(cache_krope_mla[:, l].reshape(DEC_BATCH * PAST_LEN, B_ROPE),
                        ((0, 0), (B_NOPE, LANES - B_NOPE - B_ROPE)))
    kc_b, vc_b = _kv_expand(cache_ckv_mla[:, l].reshape(DEC_BATCH * PAST_LEN, KV_LORA), krp_cache, wk, wv)
    oa_l = _lat_swa(sink, qa_l, ka_l, va_l,
                    cache_k_swa[:, l].reshape(DEC_BATCH, PAST_LEN, LANES),
                    cache_v_swa[:, l].reshape(DEC_BATCH, PAST_LEN, LANES))
    ob_l = _lat_mla(qb_l, kb_l, vb_l, kc_b, vc_b)

    x1, h2, logits = _merge(xp, xs, oa_c, oa_l, ob_c, ob_l, mod3,
                            row2(norm_pre_mix[l]), row2(norm_post_mix[l]), row2(norm_pre_ffn[l]),
                            woa, wob, wout, wg, wr)

    topi8, rank8, topw8, counts_b = _router(logits, router_bias[l].reshape(N_EXPERTS, 1))
    counts = counts_b[:, 0].astype(jnp.int32)

    padded = (counts + MOE_BLK - 1) // MOE_BLK * MOE_BLK
    pend = jnp.cumsum(padded)
    pstart = pend - padded
    dest = _dest(topi8, rank8, pstart.astype(F32).reshape(N_EXPERTS, 1)).T.reshape(-1)
    topw_p = jnp.pad(topw8.T, ((0, 0), (0, LANES - TOP_K)))
    blk_start = (jnp.concatenate([pstart, pend[-1:]]) // MOE_BLK).astype(jnp.int32)

    x_sorted, shared = _dispatch(dest, h2, wgu_sh, wd_sh)
    eo = _experts(blk_start, x_sorted, w_gate_exp[l], w_up_exp[l], w_down_exp[l])

    g_post = row2(norm_post_ffn[l])
    per_c = DEC_SEQ // TC
    y_p = _combine(dest, eo, x1, shared, topw_p, mod3, lambda i: 0, g_post, 0, T_CTX // TC)
    y_s = _combine(dest, eo, x1, shared, topw_p, mod3, lambda i: 1 + i // per_c, g_post,
                   T_CTX // TC, T_LAT // TC)

    return (y_p.reshape(BATCH, SEQ, D_MODEL), y_s.reshape(DEC_BATCH, DEC_SEQ, D_MODEL),
            ka_c.reshape(BATCH, 1, SEQ, A_KV_HEADS, A_HEAD_DIM),
            va_c.reshape(BATCH, 1, SEQ, A_KV_HEADS, A_HEAD_DIM),
            ckv_c.reshape(BATCH, 1, SEQ, KV_LORA),
            kr_c[:, B_NOPE:B_NOPE + B_ROPE].reshape(BATCH, 1, SEQ, B_ROPE))
```

```python
import functools

import jax
import jax.numpy as jnp
import numpy as np
from jax import lax
from jax.experimental import pallas as pl
from jax.experimental.pallas import tpu as pltpu

D_MODEL = 1024
BATCH = 32
SEQ = 256
DEC_BATCH = 4
DEC_SEQ = 2048
PAST_LEN = 512
GRID_W = 64
WINDOW = 128
ROPE_BASE = 10000.0
EPS = 1e-6
NEG = -1e30
A_HEADS = 8
A_KV_HEADS = 2
A_GROUP = A_HEADS // A_KV_HEADS
A_HEAD_DIM = 64
B_HEADS = 8
B_NOPE = 64
B_ROPE = 32
B_V = 64
KV_LORA = 256
Q_LORA = 384
MLA_SCALE = (B_NOPE + B_ROPE) ** -0.5
LOG2E = 1.4426950408889634
N_EXPERTS = 256
TOP_K = 8
N_GROUPS = 8
TOPK_GROUPS = 4
D_EXPERT = 256
D_SHARED = 256
ROUTED_SCALE = 2.5

T_CTX = BATCH * SEQ
T_LAT = DEC_BATCH * DEC_SEQ
T_ALL = T_CTX + T_LAT

LANES = 128
TM = 512
TQ = 256
MOE_BLK = 256
N_SLOT_BLK = (T_ALL * TOP_K) // MOE_BLK + N_EXPERTS
N_SLOTS = N_SLOT_BLK * MOE_BLK
VMEM_LIMIT = 48 * 1024 * 1024

F32 = jnp.float32
BF16 = jnp.bfloat16

_O_QA = 0
_O_KA = _O_QA + A_HEADS * A_HEAD_DIM
_O_VA = _O_KA + A_KV_HEADS * A_HEAD_DIM
_O_QL = _O_VA + A_KV_HEADS * A_HEAD_DIM
_O_CKV = _O_QL + Q_LORA
_O_KR = _O_CKV + KV_LORA
_O_GATE = _O_KR + B_ROPE


def _params(*sem):
    return pltpu.CompilerParams(dimension_semantics=sem, vmem_limit_bytes=VMEM_LIMIT)


def _dot(a, b):
    return jnp.dot(a, b, preferred_element_type=F32)


def _dot_nt(a, b):
    return lax.dot_general(a, b, (((1,), (1,)), ((), ())), preferred_element_type=F32)


def _rms(x, g):
    return x * lax.rsqrt(jnp.mean(x * x, axis=-1, keepdims=True) + EPS) * g


def _sigmoid(x):
    return 1.0 / (1.0 + jnp.exp(-x))


SUB = 8
assert SUB * LANES == D_MODEL


def _load_rows(ref, n):
    return jnp.concatenate([ref[pl.ds(j, n, stride=SUB), :] for j in range(SUB)], axis=1)


def _store_rows(ref, val, n):
    for j in range(SUB):
        ref[pl.ds(j, n, stride=SUB), :] = val[:, j * LANES:(j + 1) * LANES]


def _silu(x):
    return x * _sigmoid(x)


def _mod_kernel(cond_ref, w_ref, b_ref, o_ref):
    s = _silu(cond_ref[...]).astype(BF16)
    o_ref[...] = _dot(s, w_ref[...].astype(BF16)) + b_ref[...]


def _modulation(cond8, w_ada, b_ada):
    tn = 768
    n = w_ada.shape[1]
    return pl.pallas_call(
        _mod_kernel,
        grid=(n // tn,),
        in_specs=[pl.BlockSpec((8, D_MODEL), lambda j: (0, 0)),
                  pl.BlockSpec((D_MODEL, tn), lambda j: (0, j)),
                  pl.BlockSpec((1, tn), lambda j: (0, j))],
        out_specs=pl.BlockSpec((8, tn), lambda j: (0, j)),
        out_shape=jax.ShapeDtypeStruct((8, n), F32),
        compiler_params=_params("arbitrary"),
        name="modulation",
    )(cond8, w_ada, b_ada.reshape(1, n))


def _rope(x, cos, sin, half):
    lane = lax.broadcasted_iota(jnp.int32, x.shape, 1)
    first = (lane & (2 * half - 1)) < half
    partner = jnp.where(first, pltpu.roll(x, LANES - half, 1), pltpu.roll(x, half, 1))
    return x * cos + partner * sin


def _rope_tables():
    t = np.arange(DEC_SEQ)
    row = (t // GRID_W).astype(np.float32)
    col = (t % GRID_W).astype(np.float32)

    def tables(d_rot):
        nf = d_rot // 4
        inv = np.float32(ROPE_BASE) ** (-np.arange(nf, dtype=np.float32) / np.float32(nf))
        ar = row[:, None] * inv
        ac = col[:, None] * inv
        cos = np.concatenate([np.cos(ar), np.cos(ar), np.cos(ac), np.cos(ac)], axis=1)
        sin = np.concatenate([-np.sin(ar), np.sin(ar), -np.sin(ac), np.sin(ac)], axis=1)
        return cos.astype(np.float32), sin.astype(np.float32)

    cos_a, sin_a = tables(A_HEAD_DIM)
    cos_a = np.concatenate([cos_a, cos_a], axis=1)
    sin_a = np.concatenate([sin_a, sin_a], axis=1)
    cos_b, sin_b = tables(B_ROPE)
    one = np.ones((DEC_SEQ, B_NOPE), np.float32)
    zero = np.zeros((DEC_SEQ, B_NOPE), np.float32)
    pad = LANES - B_NOPE - B_ROPE
    cos_b = np.concatenate([one, cos_b, one[:, :pad]], axis=1)
    sin_b = np.concatenate([zero, sin_b, zero[:, :pad]], axis=1)
    return tuple(jnp.asarray(a) for a in (cos_a, sin_a, cos_b, sin_b))


_N1 = 1536


def _inproj_kernel(latent, *refs):
    if latent:
        (x_ref, mod_ref, g_ref, w1_ref, qn_ref, kvn_ref, wuq_ref, wk_ref, wv_ref,
         ca_ref, sa_ref, cb_ref, sb_ref,
         qa_o, ka_o, va_o, qb_o, kb_o, vb_o) = refs
    else:
        (x_ref, mod_ref, g_ref, w1_ref, qn_ref, kvn_ref, wuq_ref, wk_ref, wv_ref,
         qa_o, ka_o, va_o, ckv_o, kr_o, qb_o, kb_o, vb_o) = refs
    x = x_ref[...]
    shift = mod_ref[:, 0:D_MODEL]
    scale = mod_ref[:, D_MODEL:2 * D_MODEL]
    h = _rms(x, g_ref[...]) * (1.0 + scale) + shift
    p = _dot(h.astype(BF16), w1_ref[...])
    qa = p[:, 0:512]
    ka = p[:, 512:640]
    va = p[:, 640:768]
    ql = p[:, 768:1152]
    ckv = p[:, 1152:1408]
    krp = p[:, 1408:1536]
    qn = _rms(ql, qn_ref[...])
    qb = _dot(qn.astype(BF16), wuq_ref[...])
    cn = _rms(ckv, kvn_ref[...])
    cnb = cn.astype(BF16)
    kn = _dot(cnb, wk_ref[...])
    vb = _dot(cnb, wv_ref[...])
    if latent:
        ca, sa, cb, sb = ca_ref[...], sa_ref[...], cb_ref[...], sb_ref[...]
        ka = _rope(ka, ca, sa, A_HEAD_DIM // 4)
        krp = _rope(krp, cb, sb, B_ROPE // 4)
        for j in range(4):
            blk = _rope(qa[:, j * LANES:(j + 1) * LANES], ca, sa, A_HEAD_DIM // 4)
            qa_o[:, j * LANES:(j + 1) * LANES] = (blk * (A_HEAD_DIM ** -0.5 * LOG2E)).astype(BF16)
        for hd in range(B_HEADS):
            blk = _rope(qb[:, hd * LANES:(hd + 1) * LANES], cb, sb, B_ROPE // 4)
            qb_o[:, hd * LANES:(hd + 1) * LANES] = (blk * (MLA_SCALE * LOG2E)).astype(BF16)
        ka_o[...] = ka.astype(BF16)
        va_o[...] = va.astype(BF16)
    else:
        qa_o[...] = (qa * (A_HEAD_DIM ** -0.5 * LOG2E)).astype(BF16)
        qb_o[...] = (qb * (MLA_SCALE * LOG2E)).astype(BF16)
        ka_o[...] = ka
        va_o[...] = va
        ckv_o[...] = cn
        kr_o[...] = krp
    for hd in range(B_HEADS):
        kb_o[:, hd * LANES:(hd + 1) * LANES] = (kn[:, hd * LANES:(hd + 1) * LANES] + krp).astype(BF16)
    vb_o[...] = vb.astype(BF16)


def _inproj(x, mod3, mod_row, g, w1, qn, kvn, wuq, wk, wv, tables):
    latent = tables is not None
    t = x.shape[0]
    n_tiles = t // TM
    row = lambda i: (i, 0)
    const = lambda i: (0, 0)
    in_specs = [pl.BlockSpec((TM, D_MODEL), row),
                pl.BlockSpec((None, 1, 6 * D_MODEL), lambda i: (mod_row(i), 0, 0)),
                pl.BlockSpec((1, D_MODEL), const),
                pl.BlockSpec((D_MODEL, _N1), const),
                pl.BlockSpec((1, Q_LORA), const),
                pl.BlockSpec((1, KV_LORA), const),
                pl.BlockSpec((Q_LORA, B_HEADS * LANES), const),
                pl.BlockSpec((KV_LORA, B_HEADS * LANES), const),
                pl.BlockSpec((KV_LORA, B_HEADS * B_V), const)]
    args = [x, mod3, g, w1, qn, kvn, wuq, wk, wv]
    sds = jax.ShapeDtypeStruct
    if latent:
        per = DEC_SEQ // TM
        tab = lambda i: (i % per, 0)
        in_specs += [pl.BlockSpec((TM, LANES), tab)] * 4
        args += list(tables)
        out_shape = [sds((t, 512), BF16), sds((t, LANES), BF16), sds((t, LANES), BF16),
                     sds((t, B_HEADS * LANES), BF16), sds((t, B_HEADS * LANES), BF16),
                     sds((t, B_HEADS * B_V), BF16)]
        widths = [512, LANES, LANES, B_HEADS * LANES, B_HEADS * LANES, B_HEADS * B_V]
    else:
        out_shape = [sds((t, 512), BF16), sds((t, LANES), F32), sds((t, LANES), F32),
                     sds((t, KV_LORA), F32), sds((t, LANES), F32),
                     sds((t, B_HEADS * LANES), BF16), sds((t, B_HEADS * LANES), BF16),
                     sds((t, B_HEADS * B_V), BF16)]
        widths = [512, LANES, LANES, KV_LORA, LANES, B_HEADS * LANES, B_HEADS * LANES, B_HEADS * B_V]
    out_specs = [pl.BlockSpec((TM, w), row) for w in widths]
    return pl.pallas_call(
        functools.partial(_inproj_kernel, latent),
        grid=(n_tiles,),
        in_specs=in_specs,
        out_specs=out_specs,
        out_shape=out_shape,
        compiler_params=_params("arbitrary"),
        name="inproj_latent" if latent else "inproj_ctx",
    )(*args)


def _kvexp_kernel(ckv_ref, krp_ref, wk_ref, wv_ref, kb_o, vb_o):
    cb = ckv_ref[...].astype(BF16)
    kn = _dot(cb, wk_ref[...])
    krp = krp_ref[...]
    for hd in range(B_HEADS):
        kb_o[:, hd * LANES:(hd + 1) * LANES] = (kn[:, hd * LANES:(hd + 1) * LANES] + krp).astype(BF16)
    vb_o[...] = _dot(cb, wv_ref[...]).astype(BF16)


def _kv_expand(ckv, krp, wk, wv):
    t = ckv.shape[0]
    row = lambda i: (i, 0)
    const = lambda i: (0, 0)
    return pl.pallas_call(
        _kvexp_kernel,
        grid=(t // TM,),
        in_specs=[pl.BlockSpec((TM, KV_LORA), row), pl.BlockSpec((TM, LANES), row),
                  pl.BlockSpec((KV_LORA, B_HEADS * LANES), const),
                  pl.BlockSpec((KV_LORA, B_HEADS * B_V), const)],
        out_specs=[pl.BlockSpec((TM, B_HEADS * LANES), row), pl.BlockSpec((TM, B_HEADS * B_V), row)],
        out_shape=[jax.ShapeDtypeStruct((t, B_HEADS * LANES), BF16),
                   jax.ShapeDtypeStruct((t, B_HEADS * B_V), BF16)],
        compiler_params=_params("arbitrary"),
        name="kv_expand",
    )(ckv, krp, wk, wv)


def _lo_mask(rows):
    return lax.broadcasted_iota(jnp.int32, (rows, LANES), 1) < (LANES // 2)


def _softmax_pv(parts, sink):
    if sink is not None:
        sink = sink * LOG2E
    m = None
    for s, _ in parts:
        mi = jnp.max(s, axis=-1, keepdims=True)
        m = mi if m is None else jnp.maximum(m, mi)
    if sink is not None:
        m = jnp.maximum(m, sink)
    den = None
    acc = None
    for s, v in parts:
        p = jnp.exp2(s - m)
        di = jnp.sum(p, axis=-1, keepdims=True)
        den = di if den is None else den + di
        oi = _dot(p.astype(BF16), v)
        acc = oi if acc is None else acc + oi
    if sink is not None:
        den = den + jnp.exp2(sink - m)
    return acc / den


def _ctx_attn_kernel(sink_ref, qa_ref, ka_ref, va_ref, qb_ref, kb_ref, vb_ref, oa_ref, ob_ref):
    lo = _lo_mask(SEQ)
    k = ka_ref[...].astype(BF16)
    v = va_ref[...].astype(BF16)
    for j in range(A_GROUP):
        q = qa_ref[:, j * LANES:(j + 1) * LANES]
        zero = jnp.zeros_like(q)
        o0 = _softmax_pv([(_dot_nt(jnp.where(lo, q, zero), k), v)], sink_ref[j])
        o1 = _softmax_pv([(_dot_nt(jnp.where(lo, zero, q), k), v)], sink_ref[A_GROUP + j])
        oa_ref[:, j * LANES:(j + 1) * LANES] = jnp.where(lo, o0, o1).astype(BF16)
    for j in range(B_HEADS // 2):
        vp = vb_ref[:, j * LANES:(j + 1) * LANES]
        outs = []
        for hd in (2 * j, 2 * j + 1):
            s = _dot_nt(qb_ref[:, hd * LANES:(hd + 1) * LANES], kb_ref[:, hd * LANES:(hd + 1) * LANES])
            outs.append(_softmax_pv([(s, vp)], None))
        ob_ref[:, j * LANES:(j + 1) * LANES] = jnp.where(lo, outs[0], outs[1]).astype(BF16)


def _ctx_attention(sink, qa, ka, va, qb, kb, vb):
    blk = lambda w: pl.BlockSpec((SEQ, w), lambda b: (b, 0))
    return pl.pallas_call(
        _ctx_attn_kernel,
        grid=(BATCH,),
        in_specs=[pl.BlockSpec(memory_space=pltpu.SMEM),
                  blk(512), blk(LANES), blk(LANES), blk(B_HEADS * LANES), blk(B_HEADS * LANES),
                  blk(B_HEADS * B_V)],
        out_specs=[blk(512), blk(512)],
        out_shape=[jax.ShapeDtypeStruct((T_CTX, 512), BF16), jax.ShapeDtypeStruct((T_CTX, 512), BF16)],
        compiler_params=_params("arbitrary"),
        name="ctx_attention",
    )(sink, qa, ka, va, qb, kb, vb)


_WIN = TQ + 2 * WINDOW


def _lat_swa_kernel(sink_ref, q_ref, k_ref, v_ref, kc_ref, vc_ref, o_ref):
    qi = pl.program_id(1)
    q0 = qi * TQ
    start = pl.multiple_of(jnp.clip(q0 - WINDOW, 0, DEC_SEQ - _WIN), WINDOW)
    kw = k_ref[pl.ds(start, _WIN), :]
    vw = v_ref[pl.ds(start, _WIN), :]
    kc = kc_ref[...].astype(BF16)
    vc = vc_ref[...].astype(BF16)
    qpos = q0 + lax.broadcasted_iota(jnp.int32, (TQ, _WIN), 0)
    kpos = start + lax.broadcasted_iota(jnp.int32, (TQ, _WIN), 1)
    valid = jnp.abs(qpos - kpos) <= WINDOW
    lo = _lo_mask(TQ)
    for j in range(A_GROUP):
        q = q_ref[:, j * LANES:(j + 1) * LANES]
        zero = jnp.zeros_like(q)
        outs = []
        for g, qh in ((0, jnp.where(lo, q, zero)), (1, jnp.where(lo, zero, q))):
            sw = jnp.where(valid, _dot_nt(qh, kw), NEG)
            sc = _dot_nt(qh, kc)
            outs.append(_softmax_pv([(sw, vw), (sc, vc)], sink_ref[g * A_GROUP + j]))
        o_ref[:, j * LANES:(j + 1) * LANES] = jnp.where(lo, outs[0], outs[1]).astype(BF16)


def _lat_swa(sink, qa, ka, va, kc, vc):
    nq = DEC_SEQ // TQ
    return pl.pallas_call(
        _lat_swa_kernel,
        grid=(DEC_BATCH, nq),
        in_specs=[pl.BlockSpec(memory_space=pltpu.SMEM),
                  pl.BlockSpec((TQ, 512), lambda b, i: (b * nq + i, 0)),
                  pl.BlockSpec((None, DEC_SEQ, LANES), lambda b, i: (b, 0, 0)),
                  pl.BlockSpec((None, DEC_SEQ, LANES), lambda b, i: (b, 0, 0)),
                  pl.BlockSpec((None, PAST_LEN, LANES), lambda b, i: (b, 0, 0)),
                  pl.BlockSpec((None, PAST_LEN, LANES), lambda b, i: (b, 0, 0))],
        out_specs=pl.BlockSpec((TQ, 512), lambda b, i: (b * nq + i, 0)),
        out_shape=jax.ShapeDtypeStruct((T_LAT, 512), BF16),
        compiler_params=_params("arbitrary", "arbitrary"),
        name="latent_swa",
    )(sink, qa, ka.reshape(DEC_BATCH, DEC_SEQ, LANES), va.reshape(DEC_BATCH, DEC_SEQ, LANES), kc, vc)


TQ_MLA = 256


def _lat_mla_kernel(q_ref, kl_ref, vl_ref, kc_ref, vc_ref, o_ref):
    lo = _lo_mask(TQ_MLA)
    vl = vl_ref[...]
    vc = vc_ref[...]
    outs = []
    for hh in range(2):
        q = q_ref[:, hh * LANES:(hh + 1) * LANES]
        sc = _dot_nt(q, kc_ref[:, hh * LANES:(hh + 1) * LANES])
        sl = _dot_nt(q, kl_ref[:, hh * LANES:(hh + 1) * LANES])
        outs.append(_softmax_pv([(sc, vc), (sl, vl)], None))
    o_ref[...] = jnp.where(lo, outs[0], outs[1]).astype(BF16)


def _lat_mla(qb, kb, vb, kc, vc):
    nq = DEC_SEQ // TQ_MLA
    npair = B_HEADS // 2
    return pl.pallas_call(
        _lat_mla_kernel,
        grid=(DEC_BATCH, npair, nq),
        in_specs=[pl.BlockSpec((TQ_MLA, 2 * LANES), lambda b, p, i: (b * nq + i, p)),
                  pl.BlockSpec((None, DEC_SEQ, 2 * LANES), lambda b, p, i: (b, 0, p)),
                  pl.BlockSpec((None, DEC_SEQ, LANES), lambda b, p, i: (b, 0, p)),
                  pl.BlockSpec((None, PAST_LEN, 2 * LANES), lambda b, p, i: (b, 0, p)),
                  pl.BlockSpec((None, PAST_LEN, LANES), lambda b, p, i: (b, 0, p))],
        out_specs=pl.BlockSpec((TQ_MLA, LANES), lambda b, p, i: (b * nq + i, p)),
        out_shape=jax.ShapeDtypeStruct((T_LAT, 512), BF16),
        compiler_params=_params("arbitrary", "arbitrary", "arbitrary"),
        name="latent_mla",
    )(qb, kb.reshape(DEC_BATCH, DEC_SEQ, B_HEADS * LANES), vb.reshape(DEC_BATCH, DEC_SEQ, B_HEADS * B_V),
      kc.reshape(DEC_BATCH, PAST_LEN, B_HEADS * LANES), vc.reshape(DEC_BATCH, PAST_LEN, B_HEADS * B_V))


_N_CTX_TILES = T_CTX // TM


def _merge_kernel(xp_ref, xs_ref, oac_ref, oal_ref, obc_ref, obl_ref, mod_ref,
                  gpre_ref, gpost_ref, gffn_ref, woa_ref, wob_ref, wout_ref, wg_ref, wr_ref,
                  x1_o, h2_o, lg_o):
    def body(x_ref, oa_ref, ob_ref):
        x = x_ref[...]
        sh1 = mod_ref[:, 0:D_MODEL]
        sc1 = mod_ref[:, D_MODEL:2 * D_MODEL]
        g1 = mod_ref[:, 2 * D_MODEL:3 * D_MODEL]
        sh2 = mod_ref[:, 3 * D_MODEL:4 * D_MODEL]
        sc2 = mod_ref[:, 4 * D_MODEL:5 * D_MODEL]
        h = (_rms(x, gpre_ref[...]) * (1.0 + sc1) + sh1).astype(BF16)
        gates = _dot(h, wg_ref[...])
        ya = _dot(oa_ref[...], woa_ref[...])
        yb = _dot(ob_ref[...], wob_ref[...])
        y = _sigmoid(gates[:, :D_MODEL]) * ya + _sigmoid(gates[:, D_MODEL:]) * yb
        mix = _dot(y.astype(BF16), wout_ref[...])
        x1 = x + g1 * _rms(mix, gpost_ref[...])
        x1_o[...] = x1
        h2 = _rms(x1, gffn_ref[...]) * (1.0 + sc2) + sh2
        _store_rows(h2_o, h2, TM)
        lg_o[...] = _dot_nt(wr_ref[...], h2.astype(BF16))

    is_ctx = pl.program_id(0) < _N_CTX_TILES

    @pl.when(is_ctx)
    def _():
        body(xp_ref, oac_ref, obc_ref)

    @pl.when(jnp.logical_not(is_ctx))
    def _():
        body(xs_ref, oal_ref, obl_ref)


def _mod_row_all(i):
    per = DEC_SEQ // TM
    return jnp.where(i < _N_CTX_TILES, 0, 1 + (jnp.maximum(i - _N_CTX_TILES, 0)) // per)


def _merge(xp, xs, oac, oal, obc, obl, mod3, gpre, gpost, gffn, woa, wob, wout, wg, wr):
    ctx = lambda i: (jnp.minimum(i, _N_CTX_TILES - 1), 0)
    lat = lambda i: (jnp.maximum(i - _N_CTX_TILES, 0), 0)
    row = lambda i: (i, 0)
    const = lambda i: (0, 0)
    return pl.pallas_call(
        _merge_kernel,
        grid=(T_ALL // TM,),
        in_specs=[pl.BlockSpec((TM, D_MODEL), ctx), pl.BlockSpec((TM, D_MODEL), lat),
                  pl.BlockSpec((TM, 512), ctx), pl.BlockSpec((TM, 512), lat),
                  pl.BlockSpec((TM, 512), ctx), pl.BlockSpec((TM, 512), lat),
                  pl.BlockSpec((None, 1, 6 * D_MODEL), lambda i: (_mod_row_all(i), 0, 0)),
                  pl.BlockSpec((1, D_MODEL), const), pl.BlockSpec((1, D_MODEL), const),
                  pl.BlockSpec((1, D_MODEL), const),
                  pl.BlockSpec((512, D_MODEL), const), pl.BlockSpec((512, D_MODEL), const),
                  pl.BlockSpec((D_MODEL, D_MODEL), const), pl.BlockSpec((D_MODEL, 2 * D_MODEL), const),
                  pl.BlockSpec((N_EXPERTS, D_MODEL), const)],
        out_specs=[pl.BlockSpec((TM, D_MODEL), row), pl.BlockSpec((TM * SUB, LANES), row),
                   pl.BlockSpec((N_EXPERTS, TM), lambda i: (0, i))],
        out_shape=[jax.ShapeDtypeStruct((T_ALL, D_MODEL), F32), jax.ShapeDtypeStruct((T_ALL * SUB, LANES), F32),
                   jax.ShapeDtypeStruct((N_EXPERTS, T_ALL), F32)],
        compiler_params=_params("arbitrary"),
        name="merge",
    )(xp, xs, oac, oal, obc, obl, mod3, gpre, gpost, gffn, woa, wob, wout, wg, wr)


_PER_GROUP = N_EXPERTS // N_GROUPS


def _router_kernel(lg_ref, bias_ref, topi_o, rank_o, topw_o, cnt_o, cnt_s):
    @pl.when(pl.program_id(0) == 0)
    def _():
        cnt_s[...] = jnp.zeros_like(cnt_s)

    scores = _sigmoid(lg_ref[...])
    sel = scores + bias_ref[...]
    eidx = lax.broadcasted_iota(jnp.int32, (N_EXPERTS, TM), 0).astype(F32)
    ninf = -jnp.inf
    big = float(N_EXPERTS)
    sel_g = [sel[g * _PER_GROUP:(g + 1) * _PER_GROUP, :] for g in range(N_GROUPS)]
    gs = []
    for sg in sel_g:
        m1 = jnp.max(sg, axis=0, keepdims=True)
        top = sg == m1
        n_top = jnp.sum(jnp.where(top, 1.0, 0.0), axis=0, keepdims=True)
        rest = jnp.max(jnp.where(top, ninf, sg), axis=0, keepdims=True)
        gs.append(m1 + jnp.where(n_top > 1.5, m1, rest))
    cur_g = []
    for g in range(N_GROUPS):
        beat = jnp.zeros((1, TM), F32)
        for g2 in range(N_GROUPS):
            if g2 == g:
                continue
            better = (gs[g2] >= gs[g]) if g2 < g else (gs[g2] > gs[g])
            beat = beat + jnp.where(better, 1.0, 0.0)
        cur_g.append(jnp.where(beat < TOPK_GROUPS, sel_g[g], NEG))
    cur = jnp.concatenate(cur_g, axis=0)
    chosen = jnp.zeros((N_EXPERTS, TM), F32)
    idxs, ws = [], []
    for k in range(TOP_K):
        m = jnp.max(cur, axis=0, keepdims=True)
        idx = jnp.min(jnp.where(cur == m, eidx, big), axis=0, keepdims=True)
        hit = eidx == idx
        ws.append(jnp.sum(jnp.where(hit, scores, 0.0), axis=0, keepdims=True))
        cur = jnp.where(hit, ninf, cur)
        chosen = jnp.where(hit, 1.0, chosen)
        idxs.append(idx)
    wsum = ws[0]
    for k in range(1, TOP_K):
        wsum = wsum + ws[k]
    for k in range(TOP_K):
        topw_o[k:k + 1, :] = ws[k] / wsum * ROUTED_SCALE
        topi_o[k:k + 1, :] = idxs[k].astype(jnp.int32)
    r_i = lax.broadcasted_iota(jnp.int32, (TM, TM), 0)
    c_i = lax.broadcasted_iota(jnp.int32, (TM, TM), 1)
    earlier = jnp.where(r_i < c_i, 1.0, 0.0).astype(BF16)
    prefix = _dot(chosen.astype(BF16), earlier) + cnt_s[:, 0:1]
    for k in range(TOP_K):
        r = jnp.sum(jnp.where(eidx == idxs[k], prefix, 0.0), axis=0, keepdims=True)
        rank_o[k:k + 1, :] = r.astype(jnp.int32)
    total = cnt_s[...] + jnp.sum(chosen, axis=1, keepdims=True)
    cnt_s[...] = total
    cnt_o[...] = total


def _router(logits_t, bias_col):
    col = lambda i: (0, i)
    const = lambda i: (0, 0)
    return pl.pallas_call(
        _router_kernel,
        grid=(T_ALL // TM,),
        in_specs=[pl.BlockSpec((N_EXPERTS, TM), col), pl.BlockSpec((N_EXPERTS, 1), const)],
        out_specs=[pl.BlockSpec((TOP_K, TM), col), pl.BlockSpec((TOP_K, TM), col),
                   pl.BlockSpec((TOP_K, TM), col), pl.BlockSpec((N_EXPERTS, LANES), const)],
        out_shape=[jax.ShapeDtypeStruct((TOP_K, T_ALL), jnp.int32),
                   jax.ShapeDtypeStruct((TOP_K, T_ALL), jnp.int32),
                   jax.ShapeDtypeStruct((TOP_K, T_ALL), F32),
                   jax.ShapeDtypeStruct((N_EXPERTS, LANES), F32)],
        scratch_shapes=[pltpu.VMEM((N_EXPERTS, LANES), F32)],
        compiler_params=_params("arbitrary"),
        name="router",
    )(logits_t, bias_col)


def _dest_kernel(topi_ref, rank_ref, ps_ref, o_ref):
    eidx = lax.broadcasted_iota(jnp.int32, (N_EXPERTS, TM), 0)
    ps = jnp.broadcast_to(ps_ref[...], (N_EXPERTS, TM))
    for k in range(TOP_K):
        start = jnp.sum(jnp.where(eidx == topi_ref[k:k + 1, :], ps, 0.0), axis=0, keepdims=True)
        o_ref[k:k + 1, :] = rank_ref[k:k + 1, :] + start.astype(jnp.int32)


def _dest(topi, rank, pstart_col):
    col = lambda i: (0, i)
    return pl.pallas_call(
        _dest_kernel,
        grid=(T_ALL // TM,),
        in_specs=[pl.BlockSpec((TOP_K, TM), col), pl.BlockSpec((TOP_K, TM), col),
                  pl.BlockSpec((N_EXPERTS, 1), lambda i: (0, 0))],
        out_specs=pl.BlockSpec((TOP_K, TM), col),
        out_shape=jax.ShapeDtypeStruct((TOP_K, T_ALL), jnp.int32),
        compiler_params=_params("arbitrary"),
        name="dest",
    )(topi, rank, pstart_col)


TD = 512
DISPATCH_PARTS = 4
ROW_UNROLL = 4


def _dispatch_kernel(dest_ref, h_ref, wgu_ref, wd_ref, xs_hbm, sh_o, sem):
    def row(t, carry):
        src = h_ref.at[pl.ds(pl.multiple_of(t * SUB, SUB), SUB)]
        for k in range(TOP_K):
            d = dest_ref[t * TOP_K + k]
            pltpu.make_async_copy(src, xs_hbm.at[pl.ds(pl.multiple_of(d * SUB, SUB), SUB)],
                                  sem).start(priority=k % 2)
        return carry

    part = TD // DISPATCH_PARTS
    for p in range(DISPATCH_PARTS):
        lax.fori_loop(p * part, (p + 1) * part, row, 0, unroll=ROW_UNROLL)
        x = _load_rows(h_ref.at[pl.ds(p * part * SUB, part * SUB)], part).astype(BF16)
        gu = _dot(x, wgu_ref[...])
        a = _silu(gu[:, :D_SHARED]) * gu[:, D_SHARED:]
        sh_o[p * part:(p + 1) * part, :] = _dot(a.astype(BF16), wd_ref[...])
    for _ in range(TOP_K):
        pltpu.make_async_copy(h_ref, xs_hbm.at[pl.ds(0, TD * SUB)], sem).wait()


def _dispatch(dest_flat, h2, wgu, wd):
    const = lambda i: (0, 0)
    return pl.pallas_call(
        _dispatch_kernel,
        grid=(T_ALL // TD,),
        in_specs=[pl.BlockSpec((TD * TOP_K,), lambda i: (i,), memory_space=pltpu.SMEM),
                  pl.BlockSpec((TD * SUB, LANES), lambda i: (i, 0)),
                  pl.BlockSpec((D_MODEL, 2 * D_SHARED), const), pl.BlockSpec((D_SHARED, D_MODEL), const)],
        out_specs=[pl.BlockSpec(memory_space=pl.ANY), pl.BlockSpec((TD, D_MODEL), lambda i: (i, 0))],
        out_shape=[jax.ShapeDtypeStruct((N_SLOTS * SUB, LANES), F32),
                   jax.ShapeDtypeStruct((T_ALL, D_MODEL), F32)],
        scratch_shapes=[pltpu.SemaphoreType.DMA(())],
        compiler_params=_params("arbitrary"),
        name="dispatch",
    )(dest_flat, h2, wgu, wd)


_BLK_TILES = MOE_BLK * SUB
NBX = 5
NBO = 3


def _expert_kernel(bs_ref, x_hbm, wg_ref, wu_ref, wd_ref, o_hbm, wgu_s, wd_s, xbuf, obuf, xsem, osem):
    e = pl.program_id(0)
    b0 = bs_ref[e]
    b1 = bs_ref[e + 1]
    n_used = bs_ref[N_EXPERTS]

    def xcopy(g):
        s = lax.rem(g, NBX)
        return pltpu.make_async_copy(x_hbm.at[pl.ds(pl.multiple_of(g * _BLK_TILES, _BLK_TILES), _BLK_TILES)],
                                     xbuf.at[s], xsem.at[s])

    def ocopy(g):
        s = lax.rem(g, NBO)
        return pltpu.make_async_copy(obuf.at[s],
                                     o_hbm.at[pl.ds(pl.multiple_of(g * _BLK_TILES, _BLK_TILES), _BLK_TILES)],
                                     osem.at[s])

    @pl.when(e == 0)
    def _():
        for j in range(NBX - 1):
            @pl.when(j < n_used)
            def _():
                xcopy(j).start()

    @pl.when(b1 > b0)
    def _():
        wgu_s[:, 0:D_EXPERT] = wg_ref[...].astype(BF16)
        wgu_s[:, D_EXPERT:2 * D_EXPERT] = wu_ref[...].astype(BF16)
        wd_s[...] = wd_ref[...].astype(BF16)

    def block(g, carry):
        @pl.when(g + NBX - 1 < n_used)
        def _():
            xcopy(g + NBX - 1).start()

        xcopy(g).wait()

        @pl.when(g >= NBO)
        def _():
            ocopy(g - NBO).wait()

        gu = _dot(_load_rows(xbuf.at[lax.rem(g, NBX)], MOE_BLK).astype(BF16), wgu_s[...])
        a = _silu(gu[:, :D_EXPERT]) * gu[:, D_EXPERT:]
        _store_rows(obuf.at[lax.rem(g, NBO)], _dot(a.astype(BF16), wd_s[...]), MOE_BLK)
        ocopy(g).start(priority=1)
        return carry

    lax.fori_loop(b0, b1, block, 0)

    @pl.when(e == N_EXPERTS - 1)
    def _():
        for j in range(NBO, 0, -1):
            @pl.when(n_used >= j)
            def _():
                ocopy(n_used - j).wait()


def _experts(blk_start, x_sorted, w_gate, w_up, w_down):
    wsel = lambda e, bs: (e, 0, 0)
    grid_spec = pltpu.PrefetchScalarGridSpec(
        num_scalar_prefetch=1,
        grid=(N_EXPERTS,),
        in_specs=[pl.BlockSpec(memory_space=pl.ANY),
                  pl.BlockSpec((None, D_MODEL, D_EXPERT), wsel),
                  pl.BlockSpec((None, D_MODEL, D_EXPERT), wsel),
                  pl.BlockSpec((None, D_EXPERT, D_MODEL), wsel)],
        out_specs=pl.BlockSpec(memory_space=pl.ANY),
        scratch_shapes=[pltpu.VMEM((D_MODEL, 2 * D_EXPERT), BF16), pltpu.VMEM((D_EXPERT, D_MODEL), BF16),
                        pltpu.VMEM((NBX, _BLK_TILES, LANES), F32), pltpu.VMEM((NBO, _BLK_TILES, LANES), F32),
                        pltpu.SemaphoreType.DMA((NBX,)), pltpu.SemaphoreType.DMA((NBO,))],
    )
    return pl.pallas_call(
        _expert_kernel,
        grid_spec=grid_spec,
        out_shape=jax.ShapeDtypeStruct((N_SLOTS * SUB, LANES), F32),
        compiler_params=_params("arbitrary"),
        name="experts",
    )(blk_start, x_sorted, w_gate, w_up, w_down)


TC = 256
CR = 16


def _combine_kernel(dcur_ref, dnext_ref, eo_hbm, x1_ref, sh_ref, w_ref, mod_ref, g_ref, o_ref, buf, sem, wb_s):
    i = pl.program_id(0)
    n = pl.num_programs(0)
    slot = lax.rem(i, 2)

    def gather(d_ref, s, t0, t1):
        def row(t, carry):
            for k in range(TOP_K):
                d = d_ref[t * TOP_K + k]
                pltpu.make_async_copy(eo_hbm.at[pl.ds(pl.multiple_of(d * SUB, SUB), SUB)],
                                      buf.at[s, k, pl.ds(pl.multiple_of(t * SUB, SUB), SUB)],
                                      sem.at[s]).start(priority=k % 2)
            return carry
        lax.fori_loop(t0, t1, row, 0, unroll=ROW_UNROLL)

    @pl.when(i == 0)
    def _():
        gather(dcur_ref, 0, 0, TC)

    @pl.when(i + 1 < n)
    def _():
        gather(dnext_ref, 1 - slot, 0, TC)

    gate = mod_ref[:, 5 * D_MODEL:6 * D_MODEL]
    for k in range(TOP_K):
        wb_s[k] = jnp.broadcast_to(w_ref[:, k:k + 1], (TC, LANES))
    for k in range(TOP_K):
        pltpu.make_async_copy(eo_hbm.at[pl.ds(0, TC * SUB)], buf.at[slot, k], sem.at[slot]).wait()

    lanes = [slice(j * LANES, (j + 1) * LANES) for j in range(SUB)]

    def chunk(r, carry):
        r0 = pl.multiple_of(r * CR, CR)
        rows = pl.ds(r0, CR)
        wk = [wb_s[k, rows, :] for k in range(TOP_K)]
        pieces = []
        sq = jnp.zeros((CR, LANES), F32)
        for j in range(SUB):
            acc = sh_ref[rows, lanes[j]]
            for k in range(TOP_K):
                acc = acc + buf[slot, k, pl.ds(r0 * SUB + j, CR, stride=SUB), :] * wk[k]
            pieces.append(acc)
            sq = sq + acc * acc
        inv = lax.rsqrt(jnp.sum(sq, axis=-1, keepdims=True) / D_MODEL + EPS)
        for j in range(SUB):
            o_ref[rows, lanes[j]] = (x1_ref[rows, lanes[j]]
                                     + gate[:, lanes[j]] * (pieces[j] * inv * g_ref[:, lanes[j]]))
        return carry

    lax.fori_loop(0, TC // CR, chunk, 0, unroll=8)


def _combine(dest_flat, eo, x1, shared, topw, mod3, mod_row, g, tile0, n_tiles):
    row = lambda i: (tile0 + i, 0)
    const = lambda i: (0, 0)
    blk = TC * TOP_K
    return pl.pallas_call(
        _combine_kernel,
        grid=(n_tiles,),
        in_specs=[pl.BlockSpec((blk,), lambda i: (tile0 + i,), memory_space=pltpu.SMEM),
                  pl.BlockSpec((blk,), lambda i: (tile0 + jnp.minimum(i + 1, n_tiles - 1),),
                               memory_space=pltpu.SMEM),
                  pl.BlockSpec(memory_space=pl.ANY),
                  pl.BlockSpec((TC, D_MODEL), row), pl.BlockSpec((TC, D_MODEL), row),
                  pl.BlockSpec((TC, LANES), row),
                  pl.BlockSpec((None, 1, 6 * D_MODEL), lambda i: (mod_row(i), 0, 0)),
                  pl.BlockSpec((1, D_MODEL), const)],
        out_specs=pl.BlockSpec((TC, D_MODEL), lambda i: (i, 0)),
        out_shape=jax.ShapeDtypeStruct((n_tiles * TC, D_MODEL), F32),
        scratch_shapes=[pltpu.VMEM((2, TOP_K, TC * SUB, LANES), F32), pltpu.SemaphoreType.DMA((2,)),
                        pltpu.VMEM((TOP_K, TC, LANES), F32)],
        compiler_params=_params("arbitrary"),
        name="combine",
    )(dest_flat, dest_flat, eo, x1, shared, topw, mod3, g)


def _qa_perm():
    cols = []
    for j in range(A_GROUP):
        for g in range(A_KV_HEADS):
            hd = g * A_GROUP + j
            cols.extend(range(hd * A_HEAD_DIM, (hd + 1) * A_HEAD_DIM))
    return np.asarray(cols, np.int32)


def _pad_heads(w, width, n_heads, offset, total=LANES):
    k = w.shape[0]
    stride = w.shape[1] // n_heads
    w3 = w.reshape(k, n_heads, stride)[:, :, offset:offset + width]
    w3 = jnp.pad(w3, ((0, 0), (0, 0), (0, total - width)))
    return w3.reshape(k, n_heads * total)


def kernel(x_prompt, x_sample, cache_k_swa, cache_v_swa, cache_ckv_mla, cache_krope_mla, c, c_ctx, w_ada, b_ada, norm_pre_mix, norm_post_mix, norm_pre_ffn, norm_post_ffn, w_in, sink_swa, q_norm_mla, kv_norm_mla, w_uq_mla, w_ukv_mla, w_o_swa, w_o_mla, w_out, router_w, router_bias, w_gate_exp, w_up_exp, w_down_exp, w_gate_sh, w_up_sh, w_down_sh):
    l = 0
    perm = _qa_perm()
    wi = w_in[l]
    kr_blk = jnp.pad(wi[:, _O_KR:_O_GATE], ((0, 0), (B_NOPE, LANES - B_NOPE - B_ROPE)))
    w1 = jnp.concatenate([wi[:, _O_QA:_O_KA][:, perm], wi[:, _O_KA:_O_KR], kr_blk], axis=1).astype(BF16)
    wg = wi[:, _O_GATE:].astype(BF16)
    wuq = _pad_heads(w_uq_mla[l], B_NOPE + B_ROPE, B_HEADS, 0).astype(BF16)
    wk = _pad_heads(w_ukv_mla[l], B_NOPE, B_HEADS, 0).astype(BF16)
    wv = w_ukv_mla[l].reshape(KV_LORA, B_HEADS, B_NOPE + B_V)[:, :, B_NOPE:].reshape(KV_LORA, B_HEADS * B_V).astype(BF16)
    woa = w_o_swa[l][perm, :].astype(BF16)
    wob = w_o_mla[l].astype(BF16)
    wout = w_out[l].astype(BF16)
    wr = router_w[l].T.astype(BF16)
    wgu_sh = jnp.concatenate([w_gate_sh[l], w_up_sh[l]], axis=1).astype(BF16)
    wd_sh = w_down_sh[l].astype(BF16)
    sink = sink_swa[l]
    row2 = lambda v: v.reshape(1, -1)

    cond8 = jnp.concatenate([c_ctx[None, :], c, jnp.zeros((8 - 1 - DEC_BATCH, D_MODEL), F32)], axis=0)
    mod3 = _modulation(cond8, w_ada[l], b_ada[l]).reshape(8, 1, 6 * D_MODEL)

    xp = x_prompt.reshape(T_CTX, D_MODEL)
    xs = x_sample.reshape(T_LAT, D_MODEL)
    per = DEC_SEQ // TM
    shared_in = (row2(norm_pre_mix[l]), w1, row2(q_norm_mla[l]), row2(kv_norm_mla[l]), wuq, wk, wv)
    qa_c, ka_c, va_c, ckv_c, kr_c, qb_c, kb_c, vb_c = _inproj(xp, mod3, lambda i: 0, *shared_in, None)
    qa_l, ka_l, va_l, qb_l, kb_l, vb_l = _inproj(xs, mod3, lambda i: 1 + i // per, *shared_in, _rope_tables())

    oa_c, ob_c = _ctx_attention(sink, qa_c, ka_c, va_c, qb_c, kb_c, vb_c)

    krp_cache = jnp.pad(cache_krope_mla[:, l].reshape(DEC_BATCH * PAST_LEN, B_ROPE),
                        ((0, 0), (B_NOPE, LANES - B_NOPE - B_ROPE)))
    kc_b, vc_b = _kv_expand(cache_ckv_mla[:, l].reshape(DEC_BATCH * PAST_LEN, KV_LORA), krp_cache, wk, wv)
    oa_l = _lat_swa(sink, qa_l, ka_l, va_l,
                    cache_k_swa[:, l].reshape(DEC_BATCH, PAST_LEN, LANES),
                    cache_v_swa[:, l].reshape(DEC_BATCH, PAST_LEN, LANES))
    ob_l = _lat_mla(qb_l, kb_l, vb_l, kc_b, vc_b)

    x1, h2, logits = _merge(xp, xs, oa_c, oa_l, ob_c, ob_l, mod3,
                            row2(norm_pre_mix[l]), row2(norm_post_mix[l]), row2(norm_pre_ffn[l]),
                            woa, wob, wout, wg, wr)

    topi8, rank8, topw8, counts_b = _router(logits, router_bias[l].reshape(N_EXPERTS, 1))
    counts = counts_b[:, 0].astype(jnp.int32)

    padded = (counts + MOE_BLK - 1) // MOE_BLK * MOE_BLK
    pend = jnp.cumsum(padded)
    pstart = pend - padded
    dest = _dest(topi8, rank8, pstart.astype(F32).reshape(N_EXPERTS, 1)).T.reshape(-1)
    topw_p = jnp.pad(topw8.T, ((0, 0), (0, LANES - TOP_K)))
    blk_start = (jnp.concatenate([pstart, pend[-1:]]) // MOE_BLK).astype(jnp.int32)

    x_sorted, shared = _dispatch(dest, h2, wgu_sh, wd_sh)
    eo = _experts(blk_start, x_sorted, w_gate_exp[l], w_up_exp[l], w_down_exp[l])

    g_post = row2(norm_post_ffn[l])
    per_c = DEC_SEQ // TC
    y_p = _combine(dest, eo, x1, shared, topw_p, mod3, lambda i: 0, g_post, 0, T_CTX // TC)
    y_s = _combine(dest, eo, x1, shared, topw_p, mod3, lambda i: 1 + i // per_c, g_post,
                   T_CTX // TC, T_LAT // TC)

    return (y_p.reshape(BATCH, SEQ, D_MODEL), y_s.reshape(DEC_BATCH, DEC_SEQ, D_MODEL),
            ka_c.reshape(BATCH, 1, SEQ, A_KV_HEADS, A_HEAD_DIM),
            va_c.reshape(BATCH, 1, SEQ, A_KV_HEADS, A_HEAD_DIM),
            ckv_c.reshape(BATCH, 1, SEQ, KV_LORA),
            kr_c[:, B_NOPE:B_NOPE + B_ROPE].reshape(BATCH, 1, SEQ, B_ROPE))
```

```python
import functools

import jax
import jax.numpy as jnp
import numpy as np
from jax import lax
from jax.experimental import pallas as pl
from jax.experimental.pallas import tpu as pltpu

D_MODEL = 1024
BATCH = 32
SEQ = 256
DEC_BATCH = 4
DEC_SEQ = 2048
PAST_LEN = 512
GRID_W = 64
WINDOW = 128
ROPE_BASE = 10000.0
EPS = 1e-6
NEG = -1e30
A_HEADS = 8
A_KV_HEADS = 2
A_GROUP = A_HEADS // A_KV_HEADS
A_HEAD_DIM = 64
B_HEADS = 8
B_NOPE = 64
B_ROPE = 32
B_V = 64
KV_LORA = 256
Q_LORA = 384
MLA_SCALE = (B_NOPE + B_ROPE) ** -0.5
LOG2E = 1.4426950408889634
N_EXPERTS = 256
TOP_K = 8
N_GROUPS = 8
TOPK_GROUPS = 4
D_EXPERT = 256
D_SHARED = 256
ROUTED_SCALE = 2.5

T_CTX = BATCH * SEQ
T_LAT = DEC_BATCH * DEC_SEQ
T_ALL = T_CTX + T_LAT

LANES = 128
TM = 512
TQ = 256
MOE_BLK = 256
N_SLOT_BLK = (T_ALL * TOP_K) // MOE_BLK + N_EXPERTS
N_SLOTS = N_SLOT_BLK * MOE_BLK
VMEM_LIMIT = 48 * 1024 * 1024

F32 = jnp.float32
BF16 = jnp.bfloat16

_O_QA = 0
_O_KA = _O_QA + A_HEADS * A_HEAD_DIM
_O_VA = _O_KA + A_KV_HEADS * A_HEAD_DIM
_O_QL = _O_VA + A_KV_HEADS * A_HEAD_DIM
_O_CKV = _O_QL + Q_LORA
_O_KR = _O_CKV + KV_LORA
_O_GATE = _O_KR + B_ROPE


def _params(*sem):
    return pltpu.CompilerParams(dimension_semantics=sem, vmem_limit_bytes=VMEM_LIMIT)


def _dot(a, b):
    return jnp.dot(a, b, preferred_element_type=F32)


def _dot_nt(a, b):
    return lax.dot_general(a, b, (((1,), (1,)), ((), ())), preferred_element_type=F32)


def _rms(x, g):
    return x * lax.rsqrt(jnp.mean(x * x, axis=-1, keepdims=True) + EPS) * g


def _sigmoid(x):
    return 1.0 / (1.0 + jnp.exp(-x))


SUB = 8
assert SUB * LANES == D_MODEL


def _load_rows(ref, n):
    return jnp.concatenate([ref[pl.ds(j, n, stride=SUB), :] for j in range(SUB)], axis=1)


def _store_rows(ref, val, n):
    for j in range(SUB):
        ref[pl.ds(j, n, stride=SUB), :] = val[:, j * LANES:(j + 1) * LANES]


def _silu(x):
    return x * _sigmoid(x)


def _mod_kernel(cond_ref, w_ref, b_ref, o_ref):
    s = _silu(cond_ref[...]).astype(BF16)
    o_ref[...] = _dot(s, w_ref[...].astype(BF16)) + b_ref[...]


def _modulation(cond8, w_ada, b_ada):
    tn = 768
    n = w_ada.shape[1]
    return pl.pallas_call(
        _mod_kernel,
        grid=(n // tn,),
        in_specs=[pl.BlockSpec((8, D_MODEL), lambda j: (0, 0)),
                  pl.BlockSpec((D_MODEL, tn), lambda j: (0, j)),
                  pl.BlockSpec((1, tn), lambda j: (0, j))],
        out_specs=pl.BlockSpec((8, tn), lambda j: (0, j)),
        out_shape=jax.ShapeDtypeStruct((8, n), F32),
        compiler_params=_params("arbitrary"),
        name="modulation",
    )(cond8, w_ada, b_ada.reshape(1, n))


def _rope(x, cos, sin, half):
    lane = lax.broadcasted_iota(jnp.int32, x.shape, 1)
    first = (lane & (2 * half - 1)) < half
    partner = jnp.where(first, pltpu.roll(x, LANES - half, 1), pltpu.roll(x, half, 1))
    return x * cos + partner * sin


def _rope_tables():
    t = np.arange(DEC_SEQ)
    row = (t // GRID_W).astype(np.float32)
    col = (t % GRID_W).astype(np.float32)

    def tables(d_rot):
        nf = d_rot // 4
        inv = np.float32(ROPE_BASE) ** (-np.arange(nf, dtype=np.float32) / np.float32(nf))
        ar = row[:, None] * inv
        ac = col[:, None] * inv
        cos = np.concatenate([np.cos(ar), np.cos(ar), np.cos(ac), np.cos(ac)], axis=1)
        sin = np.concatenate([-np.sin(ar), np.sin(ar), -np.sin(ac), np.sin(ac)], axis=1)
        return cos.astype(np.float32), sin.astype(np.float32)

    cos_a, sin_a = tables(A_HEAD_DIM)
    cos_a = np.concatenate([cos_a, cos_a], axis=1)
    sin_a = np.concatenate([sin_a, sin_a], axis=1)
    cos_b, sin_b = tables(B_ROPE)
    one = np.ones((DEC_SEQ, B_NOPE), np.float32)
    zero = np.zeros((DEC_SEQ, B_NOPE), np.float32)
    pad = LANES - B_NOPE - B_ROPE
    cos_b = np.concatenate([one, cos_b, one[:, :pad]], axis=1)
    sin_b = np.concatenate([zero, sin_b, zero[:, :pad]], axis=1)
    return tuple(jnp.asarray(a) for a in (cos_a, sin_a, cos_b, sin_b))


_N1 = 1536


def _inproj_kernel(latent, *refs):
    if latent:
        (x_ref, mod_ref, g_ref, w1_ref, qn_ref, kvn_ref, wuq_ref, wk_ref, wv_ref,
         ca_ref, sa_ref, cb_ref, sb_ref,
         qa_o, ka_o, va_o, qb_o, kb_o, vb_o) = refs
    else:
        (x_ref, mod_ref, g_ref, w1_ref, qn_ref, kvn_ref, wuq_ref, wk_ref, wv_ref,
         qa_o, ka_o, va_o, ckv_o, kr_o, qb_o, kb_o, vb_o) = refs
    x = x_ref[...]
    shift = mod_ref[:, 0:D_MODEL]
    scale = mod_ref[:, D_MODEL:2 * D_MODEL]
    h = _rms(x, g_ref[...]) * (1.0 + scale) + shift
    p = _dot(h.astype(BF16), w1_ref[...])
    qa = p[:, 0:512]
    ka = p[:, 512:640]
    va = p[:, 640:768]
    ql = p[:, 768:1152]
    ckv = p[:, 1152:1408]
    krp = p[:, 1408:1536]
    qn = _rms(ql, qn_ref[...])
    qb = _dot(qn.astype(BF16), wuq_ref[...])
    cn = _rms(ckv, kvn_ref[...])
    cnb = cn.astype(BF16)
    kn = _dot(cnb, wk_ref[...])
    vb = _dot(cnb, wv_ref[...])
    if latent:
        ca, sa, cb, sb = ca_ref[...], sa_ref[...], cb_ref[...], sb_ref[...]
        ka = _rope(ka, ca, sa, A_HEAD_DIM // 4)
        krp = _rope(krp, cb, sb, B_ROPE // 4)
        for j in range(4):
            blk = _rope(qa[:, j * LANES:(j + 1) * LANES], ca, sa, A_HEAD_DIM // 4)
            qa_o[:, j * LANES:(j + 1) * LANES] = (blk * (A_HEAD_DIM ** -0.5 * LOG2E)).astype(BF16)
        for hd in range(B_HEADS):
            blk = _rope(qb[:, hd * LANES:(hd + 1) * LANES], cb, sb, B_ROPE // 4)
            qb_o[:, hd * LANES:(hd + 1) * LANES] = (blk * (MLA_SCALE * LOG2E)).astype(BF16)
        ka_o[...] = ka.astype(BF16)
        va_o[...] = va.astype(BF16)
    else:
        qa_o[...] = (qa * (A_HEAD_DIM ** -0.5 * LOG2E)).astype(BF16)
        qb_o[...] = (qb * (MLA_SCALE * LOG2E)).astype(BF16)
        ka_o[...] = ka
        va_o[...] = va
        ckv_o[...] = cn
        kr_o[...] = krp
    for hd in range(B_HEADS):
        kb_o[:, hd * LANES:(hd + 1) * LANES] = (kn[:, hd * LANES:(hd + 1) * LANES] + krp).astype(BF16)
    vb_o[...] = vb.astype(BF16)


def _inproj(x, mod3, mod_row, g, w1, qn, kvn, wuq, wk, wv, tables):
    latent = tables is not None
    t = x.shape[0]
    n_tiles = t // TM
    row = lambda i: (i, 0)
    const = lambda i: (0, 0)
    in_specs = [pl.BlockSpec((TM, D_MODEL), row),
                pl.BlockSpec((None, 1, 6 * D_MODEL), lambda i: (mod_row(i), 0, 0)),
                pl.BlockSpec((1, D_MODEL), const),
                pl.BlockSpec((D_MODEL, _N1), const),
                pl.BlockSpec((1, Q_LORA), const),
                pl.BlockSpec((1, KV_LORA), const),
                pl.BlockSpec((Q_LORA, B_HEADS * LANES), const),
                pl.BlockSpec((KV_LORA, B_HEADS * LANES), const),
                pl.BlockSpec((KV_LORA, B_HEADS * B_V), const)]
    args = [x, mod3, g, w1, qn, kvn, wuq, wk, wv]
    sds = jax.ShapeDtypeStruct
    if latent:
        per = DEC_SEQ // TM
        tab = lambda i: (i % per, 0)
        in_specs += [pl.BlockSpec((TM, LANES), tab)] * 4
        args += list(tables)
        out_shape = [sds((t, 512), BF16), sds((t, LANES), BF16), sds((t, LANES), BF16),
                     sds((t, B_HEADS * LANES), BF16), sds((t, B_HEADS * LANES), BF16),
                     sds((t, B_HEADS * B_V), BF16)]
        widths = [512, LANES, LANES, B_HEADS * LANES, B_HEADS * LANES, B_HEADS * B_V]
    else:
        out_shape = [sds((t, 512), BF16), sds((t, LANES), F32), sds((t, LANES), F32),
                     sds((t, KV_LORA), F32), sds((t, LANES), F32),
                     sds((t, B_HEADS * LANES), BF16), sds((t, B_HEADS * LANES), BF16),
                     sds((t, B_HEADS * B_V), BF16)]
        widths = [512, LANES, LANES, KV_LORA, LANES, B_HEADS * LANES, B_HEADS * LANES, B_HEADS * B_V]
    out_specs = [pl.BlockSpec((TM, w), row) for w in widths]
    return pl.pallas_call(
        functools.partial(_inproj_kernel, latent),
        grid=(n_tiles,),
        in_specs=in_specs,
        out_specs=out_specs,
        out_shape=out_shape,
        compiler_params=_params("arbitrary"),
        name="inproj_latent" if latent else "inproj_ctx",
    )(*args)


def _kvexp_kernel(ckv_ref, krp_ref, wk_ref, wv_ref, kb_o, vb_o):
    cb = ckv_ref[...].astype(BF16)
    kn = _dot(cb, wk_ref[...])
    krp = krp_ref[...]
    for hd in range(B_HEADS):
        kb_o[:, hd * LANES:(hd + 1) * LANES] = (kn[:, hd * LANES:(hd + 1) * LANES] + krp).astype(BF16)
    vb_o[...] = _dot(cb, wv_ref[...]).astype(BF16)


def _kv_expand(ckv, krp, wk, wv):
    t = ckv.shape[0]
    row = lambda i: (i, 0)
    const = lambda i: (0, 0)
    return pl.pallas_call(
        _kvexp_kernel,
        grid=(t // TM,),
        in_specs=[pl.BlockSpec((TM, KV_LORA), row), pl.BlockSpec((TM, LANES), row),
                  pl.BlockSpec((KV_LORA, B_HEADS * LANES), const),
                  pl.BlockSpec((KV_LORA, B_HEADS * B_V), const)],
        out_specs=[pl.BlockSpec((TM, B_HEADS * LANES), row), pl.BlockSpec((TM, B_HEADS * B_V), row)],
        out_shape=[jax.ShapeDtypeStruct((t, B_HEADS * LANES), BF16),
                   jax.ShapeDtypeStruct((t, B_HEADS * B_V), BF16)],
        compiler_params=_params("arbitrary"),
        name="kv_expand",
    )(ckv, krp, wk, wv)


def _lo_mask(rows):
    return lax.broadcasted_iota(jnp.int32, (rows, LANES), 1) < (LANES // 2)


def _softmax_pv(parts, sink):
    if sink is not None:
        sink = sink * LOG2E
    m = None
    for s, _ in parts:
        mi = jnp.max(s, axis=-1, keepdims=True)
        m = mi if m is None else jnp.maximum(m, mi)
    if sink is not None:
        m = jnp.maximum(m, sink)
    den = None
    acc = None
    for s, v in parts:
        p = jnp.exp2(s - m)
        di = jnp.sum(p, axis=-1, keepdims=True)
        den = di if den is None else den + di
        oi = _dot(p.astype(BF16), v)
        acc = oi if acc is None else acc + oi
    if sink is not None:
        den = den + jnp.exp2(sink - m)
    return acc / den


def _ctx_attn_kernel(sink_ref, qa_ref, ka_ref, va_ref, qb_ref, kb_ref, vb_ref, oa_ref, ob_ref):
    lo = _lo_mask(SEQ)
    k = ka_ref[...].astype(BF16)
    v = va_ref[...].astype(BF16)
    for j in range(A_GROUP):
        q = qa_ref[:, j * LANES:(j + 1) * LANES]
        zero = jnp.zeros_like(q)
        o0 = _softmax_pv([(_dot_nt(jnp.where(lo, q, zero), k), v)], sink_ref[j])
        o1 = _softmax_pv([(_dot_nt(jnp.where(lo, zero, q), k), v)], sink_ref[A_GROUP + j])
        oa_ref[:, j * LANES:(j + 1) * LANES] = jnp.where(lo, o0, o1).astype(BF16)
    for j in range(B_HEADS // 2):
        vp = vb_ref[:, j * LANES:(j + 1) * LANES]
        outs = []
        for hd in (2 * j, 2 * j + 1):
            s = _dot_nt(qb_ref[:, hd * LANES:(hd + 1) * LANES], kb_ref[:, hd * LANES:(hd + 1) * LANES])
            outs.append(_softmax_pv([(s, vp)], None))
        ob_ref[:, j * LANES:(j + 1) * LANES] = jnp.where(lo, outs[0], outs[1]).astype(BF16)


def _ctx_attention(sink, qa, ka, va, qb, kb, vb):
    blk = lambda w: pl.BlockSpec((SEQ, w), lambda b: (b, 0))
    return pl.pallas_call(
        _ctx_attn_kernel,
        grid=(BATCH,),
        in_specs=[pl.BlockSpec(memory_space=pltpu.SMEM),
                  blk(512), blk(LANES), blk(LANES), blk(B_HEADS * LANES), blk(B_HEADS * LANES),
                  blk(B_HEADS * B_V)],
        out_specs=[blk(512), blk(512)],
        out_shape=[jax.ShapeDtypeStruct((T_CTX, 512), BF16), jax.ShapeDtypeStruct((T_CTX, 512), BF16)],
        compiler_params=_params("arbitrary"),
        name="ctx_attention",
    )(sink, qa, ka, va, qb, kb, vb)


_WIN = TQ + 2 * WINDOW


def _lat_swa_kernel(sink_ref, q_ref, k_ref, v_ref, kc_ref, vc_ref, o_ref):
    qi = pl.program_id(1)
    q0 = qi * TQ
    start = pl.multiple_of(jnp.clip(q0 - WINDOW, 0, DEC_SEQ - _WIN), WINDOW)
    kw = k_ref[pl.ds(start, _WIN), :]
    vw = v_ref[pl.ds(start, _WIN), :]
    kc = kc_ref[...].astype(BF16)
    vc = vc_ref[...].astype(BF16)
    qpos = q0 + lax.broadcasted_iota(jnp.int32, (TQ, _WIN), 0)
    kpos = start + lax.broadcasted_iota(jnp.int32, (TQ, _WIN), 1)
    valid = jnp.abs(qpos - kpos) <= WINDOW
    lo = _lo_mask(TQ)
    for j in range(A_GROUP):
        q = q_ref[:, j * LANES:(j + 1) * LANES]
        zero = jnp.zeros_like(q)
        outs = []
        for g, qh in ((0, jnp.where(lo, q, zero)), (1, jnp.where(lo, zero, q))):
            sw = jnp.where(valid, _dot_nt(qh, kw), NEG)
            sc = _dot_nt(qh, kc)
            outs.append(_softmax_pv([(sw, vw), (sc, vc)], sink_ref[g * A_GROUP + j]))
        o_ref[:, j * LANES:(j + 1) * LANES] = jnp.where(lo, outs[0], outs[1]).astype(BF16)


def _lat_swa(sink, qa, ka, va, kc, vc):
    nq = DEC_SEQ // TQ
    return pl.pallas_call(
        _lat_swa_kernel,
        grid=(DEC_BATCH, nq),
        in_specs=[pl.BlockSpec(memory_space=pltpu.SMEM),
                  pl.BlockSpec((TQ, 512), lambda b, i: (b * nq + i, 0)),
                  pl.BlockSpec((None, DEC_SEQ, LANES), lambda b, i: (b, 0, 0)),
                  pl.BlockSpec((None, DEC_SEQ, LANES), lambda b, i: (b, 0, 0)),
                  pl.BlockSpec((None, PAST_LEN, LANES), lambda b, i: (b, 0, 0)),
                  pl.BlockSpec((None, PAST_LEN, LANES), lambda b, i: (b, 0, 0))],
        out_specs=pl.BlockSpec((TQ, 512), lambda b, i: (b * nq + i, 0)),
        out_shape=jax.ShapeDtypeStruct((T_LAT, 512), BF16),
        compiler_params=_params("arbitrary", "arbitrary"),
        name="latent_swa",
    )(sink, qa, ka.reshape(DEC_BATCH, DEC_SEQ, LANES), va.reshape(DEC_BATCH, DEC_SEQ, LANES), kc, vc)


TQ_MLA = 256


def _lat_mla_kernel(q_ref, kl_ref, vl_ref, kc_ref, vc_ref, o_ref):
    lo = _lo_mask(TQ_MLA)
    vl = vl_ref[...]
    vc = vc_ref[...]
    outs = []
    for hh in range(2):
        q = q_ref[:, hh * LANES:(hh + 1) * LANES]
        sc = _dot_nt(q, kc_ref[:, hh * LANES:(hh + 1) * LANES])
        sl = _dot_nt(q, kl_ref[:, hh * LANES:(hh + 1) * LANES])
        outs.append(_softmax_pv([(sc, vc), (sl, vl)], None))
    o_ref[...] = jnp.where(lo, outs[0], outs[1]).astype(BF16)


def _lat_mla(qb, kb, vb, kc, vc):
    nq = DEC_SEQ // TQ_MLA
    npair = B_HEADS // 2
    return pl.pallas_call(
        _lat_mla_kernel,
        grid=(DEC_BATCH, npair, nq),
        in_specs=[pl.BlockSpec((TQ_MLA, 2 * LANES), lambda b, p, i: (b * nq + i, p)),
                  pl.BlockSpec((None, DEC_SEQ, 2 * LANES), lambda b, p, i: (b, 0, p)),
                  pl.BlockSpec((None, DEC_SEQ, LANES), lambda b, p, i: (b, 0, p)),
                  pl.BlockSpec((None, PAST_LEN, 2 * LANES), lambda b, p, i: (b, 0, p)),
                  pl.BlockSpec((None, PAST_LEN, LANES), lambda b, p, i: (b, 0, p))],
        out_specs=pl.BlockSpec((TQ_MLA, LANES), lambda b, p, i: (b * nq + i, p)),
        out_shape=jax.ShapeDtypeStruct((T_LAT, 512), BF16),
        compiler_params=_params("arbitrary", "arbitrary", "arbitrary"),
        name="latent_mla",
    )(qb, kb.reshape(DEC_BATCH, DEC_SEQ, B_HEADS * LANES), vb.reshape(DEC_BATCH, DEC_SEQ, B_HEADS * B_V),
      kc.reshape(DEC_BATCH, PAST_LEN, B_HEADS * LANES), vc.reshape(DEC_BATCH, PAST_LEN, B_HEADS * B_V))


_N_CTX_TILES = T_CTX // TM


def _merge_kernel(xp_ref, xs_ref, oac_ref, oal_ref, obc_ref, obl_ref, mod_ref,
                  gpre_ref, gpost_ref, gffn_ref, woa_ref, wob_ref, wout_ref, wg_ref, wr_ref,
                  x1_o, h2_o, lg_o):
    def body(x_ref, oa_ref, ob_ref):
        x = x_ref[...]
        sh1 = mod_ref[:, 0:D_MODEL]
        sc1 = mod_ref[:, D_MODEL:2 * D_MODEL]
        g1 = mod_ref[:, 2 * D_MODEL:3 * D_MODEL]
        sh2 = mod_ref[:, 3 * D_MODEL:4 * D_MODEL]
        sc2 = mod_ref[:, 4 * D_MODEL:5 * D_MODEL]
        h = (_rms(x, gpre_ref[...]) * (1.0 + sc1) + sh1).astype(BF16)
        gates = _dot(h, wg_ref[...])
        ya = _dot(oa_ref[...], woa_ref[...])
        yb = _dot(ob_ref[...], wob_ref[...])
        y = _sigmoid(gates[:, :D_MODEL]) * ya + _sigmoid(gates[:, D_MODEL:]) * yb
        mix = _dot(y.astype(BF16), wout_ref[...])
        x1 = x + g1 * _rms(mix, gpost_ref[...])
        x1_o[...] = x1
        h2 = _rms(x1, gffn_ref[...]) * (1.0 + sc2) + sh2
        _store_rows(h2_o, h2, TM)
        lg_o[...] = _dot_nt(wr_ref[...], h2.astype(BF16))

    is_ctx = pl.program_id(0) < _N_CTX_TILES

    @pl.when(is_ctx)
    def _():
        body(xp_ref, oac_ref, obc_ref)

    @pl.when(jnp.logical_not(is_ctx))
    def _():
        body(xs_ref, oal_ref, obl_ref)


def _mod_row_all(i):
    per = DEC_SEQ // TM
    return jnp.where(i < _N_CTX_TILES, 0, 1 + (jnp.maximum(i - _N_CTX_TILES, 0)) // per)


def _merge(xp, xs, oac, oal, obc, obl, mod3, gpre, gpost, gffn, woa, wob, wout, wg, wr):
    ctx = lambda i: (jnp.minimum(i, _N_CTX_TILES - 1), 0)
    lat = lambda i: (jnp.maximum(i - _N_CTX_TILES, 0), 0)
    row = lambda i: (i, 0)
    const = lambda i: (0, 0)
    return pl.pallas_call(
        _merge_kernel,
        grid=(T_ALL // TM,),
        in_specs=[pl.BlockSpec((TM, D_MODEL), ctx), pl.BlockSpec((TM, D_MODEL), lat),
                  pl.BlockSpec((TM, 512), ctx), pl.BlockSpec((TM, 512), lat),
                  pl.BlockSpec((TM, 512), ctx), pl.BlockSpec((TM, 512), lat),
                  pl.BlockSpec((None, 1, 6 * D_MODEL), lambda i: (_mod_row_all(i), 0, 0)),
                  pl.BlockSpec((1, D_MODEL), const), pl.BlockSpec((1, D_MODEL), const),
                  pl.BlockSpec((1, D_MODEL), const),
                  pl.BlockSpec((512, D_MODEL), const), pl.BlockSpec((512, D_MODEL), const),
                  pl.BlockSpec((D_MODEL, D_MODEL), const), pl.BlockSpec((D_MODEL, 2 * D_MODEL), const),
                  pl.BlockSpec((N_EXPERTS, D_MODEL), const)],
        out_specs=[pl.BlockSpec((TM, D_MODEL), row), pl.BlockSpec((TM * SUB, LANES), row),
                   pl.BlockSpec((N_EXPERTS, TM), lambda i: (0, i))],
        out_shape=[jax.ShapeDtypeStruct((T_ALL, D_MODEL), F32), jax.ShapeDtypeStruct((T_ALL * SUB, LANES), F32),
                   jax.ShapeDtypeStruct((N_EXPERTS, T_ALL), F32)],
        compiler_params=_params("arbitrary"),
        name="merge",
    )(xp, xs, oac, oal, obc, obl, mod3, gpre, gpost, gffn, woa, wob, wout, wg, wr)


_PER_GROUP = N_EXPERTS // N_GROUPS


def _router_kernel(lg_ref, bias_ref, topi_o, rank_o, topw_o, cnt_o, cnt_s):
    @pl.when(pl.program_id(0) == 0)
    def _():
        cnt_s[...] = jnp.zeros_like(cnt_s)

    scores = _sigmoid(lg_ref[...])
    sel = scores + bias_ref[...]
    eidx = lax.broadcasted_iota(jnp.int32, (N_EXPERTS, TM), 0).astype(F32)
    ninf = -jnp.inf
    big = float(N_EXPERTS)
    sel_g = [sel[g * _PER_GROUP:(g + 1) * _PER_GROUP, :] for g in range(N_GROUPS)]
    gs = []
    for sg in sel_g:
        m1 = jnp.max(sg, axis=0, keepdims=True)
        top = sg == m1
        n_top = jnp.sum(jnp.where(top, 1.0, 0.0), axis=0, keepdims=True)
        rest = jnp.max(jnp.where(top, ninf, sg), axis=0, keepdims=True)
        gs.append(m1 + jnp.where(n_top > 1.5, m1, rest))
    cur_g = []
    for g in range(N_GROUPS):
        beat = jnp.zeros((1, TM), F32)
        for g2 in range(N_GROUPS):
            if g2 == g:
                continue
            better = (gs[g2] >= gs[g]) if g2 < g else (gs[g2] > gs[g])
            beat = beat + jnp.where(better, 1.0, 0.0)
        cur_g.append(jnp.where(beat < TOPK_GROUPS, sel_g[g], NEG))
    cur = jnp.concatenate(cur_g, axis=0)
    chosen = jnp.zeros((N_EXPERTS, TM), F32)
    idxs, ws = [], []
    for k in range(TOP_K):
        m = jnp.max(cur, axis=0, keepdims=True)
        idx = jnp.min(jnp.where(cur == m, eidx, big), axis=0, keepdims=True)
        hit = eidx == idx
        ws.append(jnp.sum(jnp.where(hit, scores, 0.0), axis=0, keepdims=True))
        cur = jnp.where(hit, ninf, cur)
        chosen = jnp.where(hit, 1.0, chosen)
        idxs.append(idx)
    wsum = ws[0]
    for k in range(1, TOP_K):
        wsum = wsum + ws[k]
    for k in range(TOP_K):
        topw_o[k:k + 1, :] = ws[k] / wsum * ROUTED_SCALE
        topi_o[k:k + 1, :] = idxs[k].astype(jnp.int32)
    r_i = lax.broadcasted_iota(jnp.int32, (TM, TM), 0)
    c_i = lax.broadcasted_iota(jnp.int32, (TM, TM), 1)
    earlier = jnp.where(r_i < c_i, 1.0, 0.0).astype(BF16)
    prefix = _dot(chosen.astype(BF16), earlier) + cnt_s[:, 0:1]
    for k in range(TOP_K):
        r = jnp.sum(jnp.where(eidx == idxs[k], prefix, 0.0), axis=0, keepdims=True)
        rank_o[k:k + 1, :] = r.astype(jnp.int32)
    total = cnt_s[...] + jnp.sum(chosen, axis=1, keepdims=True)
    cnt_s[...] = total
    cnt_o[...] = total


def _router(logits_t, bias_col):
    col = lambda i: (0, i)
    const = lambda i: (0, 0)
    return pl.pallas_call(
        _router_kernel,
        grid=(T_ALL // TM,),
        in_specs=[pl.BlockSpec((N_EXPERTS, TM), col), pl.BlockSpec((N_EXPERTS, 1), const)],
        out_specs=[pl.BlockSpec((TOP_K, TM), col), pl.BlockSpec((TOP_K, TM), col),
                   pl.BlockSpec((TOP_K, TM), col), pl.BlockSpec((N_EXPERTS, LANES), const)],
        out_shape=[jax.ShapeDtypeStruct((TOP_K, T_ALL), jnp.int32),
                   jax.ShapeDtypeStruct((TOP_K, T_ALL), jnp.int32),
                   jax.ShapeDtypeStruct((TOP_K, T_ALL), F32),
                   jax.ShapeDtypeStruct((N_EXPERTS, LANES), F32)],
        scratch_shapes=[pltpu.VMEM((N_EXPERTS, LANES), F32)],
        compiler_params=_params("arbitrary"),
        name="router",
    )(logits_t, bias_col)


def _dest_kernel(topi_ref, rank_ref, ps_ref, o_ref):
    eidx = lax.broadcasted_iota(jnp.int32, (N_EXPERTS, TM), 0)
    ps = jnp.broadcast_to(ps_ref[...], (N_EXPERTS, TM))
    for k in range(TOP_K):
        start = jnp.sum(jnp.where(eidx == topi_ref[k:k + 1, :], ps, 0.0), axis=0, keepdims=True)
        o_ref[k:k + 1, :] = rank_ref[k:k + 1, :] + start.astype(jnp.int32)


def _dest(topi, rank, pstart_col):
    col = lambda i: (0, i)
    return pl.pallas_call(
        _dest_kernel,
        grid=(T_ALL // TM,),
        in_specs=[pl.BlockSpec((TOP_K, TM), col), pl.BlockSpec((TOP_K, TM), col),
                  pl.BlockSpec((N_EXPERTS, 1), lambda i: (0, 0))],
        out_specs=pl.BlockSpec((TOP_K, TM), col),
        out_shape=jax.ShapeDtypeStruct((TOP_K, T_ALL), jnp.int32),
        compiler_params=_params("arbitrary"),
        name="dest",
    )(topi, rank, pstart_col)


TD = 512
DISPATCH_PARTS = 4
ROW_UNROLL = 4


def _dispatch_kernel(dest_ref, h_ref, wgu_ref, wd_ref, xs_hbm, sh_o, sem):
    def row(t, carry):
        src = h_ref.at[pl.ds(pl.multiple_of(t * SUB, SUB), SUB)]
        for k in range(TOP_K):
            d = dest_ref[t * TOP_K + k]
            pltpu.make_async_copy(src, xs_hbm.at[pl.ds(pl.multiple_of(d * SUB, SUB), SUB)],
                                  sem).start(priority=k % 2)
        return carry

    part = TD // DISPATCH_PARTS
    for p in range(DISPATCH_PARTS):
        lax.fori_loop(p * part, (p + 1) * part, row, 0, unroll=ROW_UNROLL)
        x = _load_rows(h_ref.at[pl.ds(p * part * SUB, part * SUB)], part).astype(BF16)
        gu = _dot(x, wgu_ref[...])
        a = _silu(gu[:, :D_SHARED]) * gu[:, D_SHARED:]
        sh_o[p * part:(p + 1) * part, :] = _dot(a.astype(BF16), wd_ref[...])
    for _ in range(TOP_K):
        pltpu.make_async_copy(h_ref, xs_hbm.at[pl.ds(0, TD * SUB)], sem).wait()


def _dispatch(dest_flat, h2, wgu, wd):
    const = lambda i: (0, 0)
    return pl.pallas_call(
        _dispatch_kernel,
        grid=(T_ALL // TD,),
        in_specs=[pl.BlockSpec((TD * TOP_K,), lambda i: (i,), memory_space=pltpu.SMEM),
                  pl.BlockSpec((TD * SUB, LANES), lambda i: (i, 0)),
                  pl.BlockSpec((D_MODEL, 2 * D_SHARED), const), pl.BlockSpec((D_SHARED, D_MODEL), const)],
        out_specs=[pl.BlockSpec(memory_space=pl.ANY), pl.BlockSpec((TD, D_MODEL), lambda i: (i, 0))],
        out_shape=[jax.ShapeDtypeStruct((N_SLOTS * SUB, LANES), F32),
                   jax.ShapeDtypeStruct((T_ALL, D_MODEL), F32)],
        scratch_shapes=[pltpu.SemaphoreType.DMA(())],
        compiler_params=_params("arbitrary"),
        name="dispatch",
    )(dest_flat, h2, wgu, wd)


_BLK_TILES = MOE_BLK * SUB
NBX = 5
NBO = 3
BLK_QUANT = 32


def _expert_kernel(bs_ref, vb_ref, x_hbm, wg_ref, wu_ref, wd_ref, o_hbm, wgu_s, wd_s, xbuf, obuf, xsem, osem):
    e = pl.program_id(0)
    b0 = bs_ref[e]
    b1 = bs_ref[e + 1]
    n_used = bs_ref[N_EXPERTS]

    def pieces(g, make, act):
        v = vb_ref[g]
        base = pl.multiple_of(g * _BLK_TILES, _BLK_TILES)

        @pl.when(v == MOE_BLK)
        def _():
            act(make(base, 0, _BLK_TILES))

        p = MOE_BLK // 2
        while p >= BLK_QUANT:
            @pl.when((v < MOE_BLK) & ((v & p) != 0))
            def _(p=p):
                off = pl.multiple_of((v & (MOE_BLK - 2 * p)) * SUB, p * SUB)
                act(make(base, off, p * SUB))
            p //= 2

    def xmake(g):
        s = lax.rem(g, NBX)
        return lambda base, off, n: pltpu.make_async_copy(
            x_hbm.at[pl.ds(base + off, n)], xbuf.at[s, pl.ds(off, n)], xsem.at[s])

    def omake(g):
        s = lax.rem(g, NBO)
        return lambda base, off, n: pltpu.make_async_copy(
            obuf.at[s, pl.ds(off, n)], o_hbm.at[pl.ds(base + off, n)], osem.at[s])

    start_x = lambda c: c.start()
    start_o = lambda c: c.start(priority=1)
    wait = lambda c: c.wait()

    @pl.when(e == 0)
    def _():
        xbuf[...] = jnp.zeros_like(xbuf)
        for j in range(NBX - 1):
            @pl.when(j < n_used)
            def _():
                pieces(j, xmake(j), start_x)

    @pl.when(b1 > b0)
    def _():
        wgu_s[:, 0:D_EXPERT] = wg_ref[...].astype(BF16)
        wgu_s[:, D_EXPERT:2 * D_EXPERT] = wu_ref[...].astype(BF16)
        wd_s[...] = wd_ref[...].astype(BF16)

    def block(g, carry):
        @pl.when(g + NBX - 1 < n_used)
        def _():
            pieces(g + NBX - 1, xmake(g + NBX - 1), start_x)

        pieces(g, xmake(g), wait)

        @pl.when(g >= NBO)
        def _():
            pieces(g - NBO, omake(g - NBO), wait)

        gu = _dot(_load_rows(xbuf.at[lax.rem(g, NBX)], MOE_BLK).astype(BF16), wgu_s[...])
        a = _silu(gu[:, :D_EXPERT]) * gu[:, D_EXPERT:]
        _store_rows(obuf.at[lax.rem(g, NBO)], _dot(a.astype(BF16), wd_s[...]), MOE_BLK)
        pieces(g, omake(g), start_o)
        return carry

    lax.fori_loop(b0, b1, block, 0)

    @pl.when(e == N_EXPERTS - 1)
    def _():
        for j in range(NBO, 0, -1):
            @pl.when(n_used >= j)
            def _():
                pieces(n_used - j, omake(n_used - j), wait)


def _experts(blk_start, blk_rows, x_sorted, w_gate, w_up, w_down):
    wsel = lambda e, bs, vb: (e, 0, 0)
    grid_spec = pltpu.PrefetchScalarGridSpec(
        num_scalar_prefetch=2,
        grid=(N_EXPERTS,),
        in_specs=[pl.BlockSpec(memory_space=pl.ANY),
                  pl.BlockSpec((None, D_MODEL, D_EXPERT), wsel),
                  pl.BlockSpec((None, D_MODEL, D_EXPERT), wsel),
                  pl.BlockSpec((None, D_EXPERT, D_MODEL), wsel)],
        out_specs=pl.BlockSpec(memory_space=pl.ANY),
        scratch_shapes=[pltpu.VMEM((D_MODEL, 2 * D_EXPERT), BF16), pltpu.VMEM((D_EXPERT, D_MODEL), BF16),
                        pltpu.VMEM((NBX, _BLK_TILES, LANES), F32), pltpu.VMEM((NBO, _BLK_TILES, LANES), F32),
                        pltpu.SemaphoreType.DMA((NBX,)), pltpu.SemaphoreType.DMA((NBO,))],
    )
    return pl.pallas_call(
        _expert_kernel,
        grid_spec=grid_spec,
        out_shape=jax.ShapeDtypeStruct((N_SLOTS * SUB, LANES), F32),
        compiler_params=_params("arbitrary"),
        name="experts",
    )(blk_start, blk_rows, x_sorted, w_gate, w_up, w_down)


TC = 256
CR = 16


def _combine_kernel(dcur_ref, dnext_ref, eo_hbm, x1_ref, sh_ref, w_ref, mod_ref, g_ref, o_ref, buf, sem, wb_s):
    i = pl.program_id(0)
    n = pl.num_programs(0)
    slot = lax.rem(i, 2)

    def gather(d_ref, s, t0, t1):
        def row(t, carry):
            for k in range(TOP_K):
                d = d_ref[t * TOP_K + k]
                pltpu.make_async_copy(eo_hbm.at[pl.ds(pl.multiple_of(d * SUB, SUB), SUB)],
                                      buf.at[s, k, pl.ds(pl.multiple_of(t * SUB, SUB), SUB)],
                                      sem.at[s]).start(priority=k % 2)
            return carry
        lax.fori_loop(t0, t1, row, 0, unroll=ROW_UNROLL)

    @pl.when(i == 0)
    def _():
        gather(dcur_ref, 0, 0, TC)

    @pl.when(i + 1 < n)
    def _():
        gather(dnext_ref, 1 - slot, 0, TC)

    gate = mod_ref[:, 5 * D_MODEL:6 * D_MODEL]
    for k in range(TOP_K):
        wb_s[k] = jnp.broadcast_to(w_ref[:, k:k + 1], (TC, LANES))
    for k in range(TOP_K):
        pltpu.make_async_copy(eo_hbm.at[pl.ds(0, TC * SUB)], buf.at[slot, k], sem.at[slot]).wait()

    lanes = [slice(j * LANES, (j + 1) * LANES) for j in range(SUB)]

    def chunk(r, carry):
        r0 = pl.multiple_of(r * CR, CR)
        rows = pl.ds(r0, CR)
        wk = [wb_s[k, rows, :] for k in range(TOP_K)]
        pieces = []
        sq = jnp.zeros((CR, LANES), F32)
        for j in range(SUB):
            acc = sh_ref[rows, lanes[j]]
            for k in range(TOP_K):
                acc = acc + buf[slot, k, pl.ds(r0 * SUB + j, CR, stride=SUB), :] * wk[k]
            pieces.append(acc)
            sq = sq + acc * acc
        inv = lax.rsqrt(jnp.sum(sq, axis=-1, keepdims=True) / D_MODEL + EPS)
        for j in range(SUB):
            o_ref[rows, lanes[j]] = (x1_ref[rows, lanes[j]]
                                     + gate[:, lanes[j]] * (pieces[j] * inv * g_ref[:, lanes[j]]))
        return carry

    lax.fori_loop(0, TC // CR, chunk, 0, unroll=8)


def _combine(dest_flat, eo, x1, shared, topw, mod3, mod_row, g, tile0, n_tiles):
    row = lambda i: (tile0 + i, 0)
    const = lambda i: (0, 0)
    blk = TC * TOP_K
    return pl.pallas_call(
        _combine_kernel,
        grid=(n_tiles,),
        in_specs=[pl.BlockSpec((blk,), lambda i: (tile0 + i,), memory_space=pltpu.SMEM),
                  pl.BlockSpec((blk,), lambda i: (tile0 + jnp.minimum(i + 1, n_tiles - 1),),
                               memory_space=pltpu.SMEM),
                  pl.BlockSpec(memory_space=pl.ANY),
                  pl.BlockSpec((TC, D_MODEL), row), pl.BlockSpec((TC, D_MODEL), row),
                  pl.BlockSpec((TC, LANES), row),
                  pl.BlockSpec((None, 1, 6 * D_MODEL), lambda i: (mod_row(i), 0, 0)),
                  pl.BlockSpec((1, D_MODEL), const)],
        out_specs=pl.BlockSpec((TC, D_MODEL), lambda i: (i, 0)),
        out_shape=jax.ShapeDtypeStruct((n_tiles * TC, D_MODEL), F32),
        scratch_shapes=[pltpu.VMEM((2, TOP_K, TC * SUB, LANES), F32), pltpu.SemaphoreType.DMA((2,)),
                        pltpu.VMEM((TOP_K, TC, LANES), F32)],
        compiler_params=_params("arbitrary"),
        name="combine",
    )(dest_flat, dest_flat, eo, x1, shared, topw, mod3, g)


def _qa_perm():
    cols = []
    for j in range(A_GROUP):
        for g in range(A_KV_HEADS):
            hd = g * A_GROUP + j
            cols.extend(range(hd * A_HEAD_DIM, (hd + 1) * A_HEAD_DIM))
    return np.asarray(cols, np.int32)


def _pad_heads(w, width, n_heads, offset, total=LANES):
    k = w.shape[0]
    stride = w.shape[1] // n_heads
    w3 = w.reshape(k, n_heads, stride)[:, :, offset:offset + width]
    w3 = jnp.pad(w3, ((0, 0), (0, 0), (0, total - width)))
    return w3.reshape(k, n_heads * total)


def kernel(x_prompt, x_sample, cache_k_swa, cache_v_swa, cache_ckv_mla, cache_krope_mla, c, c_ctx, w_ada, b_ada, norm_pre_mix, norm_post_mix, norm_pre_ffn, norm_post_ffn, w_in, sink_swa, q_norm_mla, kv_norm_mla, w_uq_mla, w_ukv_mla, w_o_swa, w_o_mla, w_out, router_w, router_bias, w_gate_exp, w_up_exp, w_down_exp, w_gate_sh, w_up_sh, w_down_sh):
    l = 0
    perm = _qa_perm()
    wi = w_in[l]
    kr_blk = jnp.pad(wi[:, _O_KR:_O_GATE], ((0, 0), (B_NOPE, LANES - B_NOPE - B_ROPE)))
    w1 = jnp.concatenate([wi[:, _O_QA:_O_KA][:, perm], wi[:, _O_KA:_O_KR], kr_blk], axis=1).astype(BF16)
    wg = wi[:, _O_GATE:].astype(BF16)
    wuq = _pad_heads(w_uq_mla[l], B_NOPE + B_ROPE, B_HEADS, 0).astype(BF16)
    wk = _pad_heads(w_ukv_mla[l], B_NOPE, B_HEADS, 0).astype(BF16)
    wv = w_ukv_mla[l].reshape(KV_LORA, B_HEADS, B_NOPE + B_V)[:, :, B_NOPE:].reshape(KV_LORA, B_HEADS * B_V).astype(BF16)
    woa = w_o_swa[l][perm, :].astype(BF16)
    wob = w_o_mla[l].astype(BF16)
    wout = w_out[l].astype(BF16)
    wr = router_w[l].T.astype(BF16)
    wgu_sh = jnp.concatenate([w_gate_sh[l], w_up_sh[l]], axis=1).astype(BF16)
    wd_sh = w_down_sh[l].astype(BF16)
    sink = sink_swa[l]
    row2 = lambda v: v.reshape(1, -1)

    cond8 = jnp.concatenate([c_ctx[None, :], c, jnp.zeros((8 - 1 - DEC_BATCH, D_MODEL), F32)], axis=0)
    mod3 = _modulation(cond8, w_ada[l], b_ada[l]).reshape(8, 1, 6 * D_MODEL)

    xp = x_prompt.reshape(T_CTX, D_MODEL)
    xs = x_sample.reshape(T_LAT, D_MODEL)
    per = DEC_SEQ // TM
    shared_in = (row2(norm_pre_mix[l]), w1, row2(q_norm_mla[l]), row2(kv_norm_mla[l]), wuq, wk, wv)
    qa_c, ka_c, va_c, ckv_c, kr_c, qb_c, kb_c, vb_c = _inproj(xp, mod3, lambda i: 0, *shared_in, None)
    qa_l, ka_l, va_l, qb_l, kb_l, vb_l = _inproj(xs, mod3, lambda i: 1 + i // per, *shared_in, _rope_tables())

    oa_c, ob_c = _ctx_attention(sink, qa_c, ka_c, va_c, qb_c, kb_c, vb_c)

    krp_cache = jnp.pad(cache_krope_mla[:, l].reshape(DEC_BATCH * PAST_LEN, B_ROPE),
                        ((0, 0), (B_NOPE, LANES - B_NOPE - B_ROPE)))
    kc_b, vc_b = _kv_expand(cache_ckv_mla[:, l].reshape(DEC_BATCH * PAST_LEN, KV_LORA), krp_cache, wk, wv)
    oa_l = _lat_swa(sink, qa_l, ka_l, va_l,
                    cache_k_swa[:, l].reshape(DEC_BATCH, PAST_LEN, LANES),
                    cache_v_swa[:, l].reshape(DEC_BATCH, PAST_LEN, LANES))
    ob_l = _lat_mla(qb_l, kb_l, vb_l, kc_b, vc_b)

    x1, h2, logits = _merge(xp, xs, oa_c, oa_l, ob_c, ob_l, mod3,
                            row2(norm_pre_mix[l]), row2(norm_post_mix[l]), row2(norm_pre_ffn[l]),
                            woa, wob, wout, wg, wr)

    topi8, rank8, topw8, counts_b = _router(logits, router_bias[l].reshape(N_EXPERTS, 1))
    counts = counts_b[:, 0].astype(jnp.int32)

    padded = (counts + MOE_BLK - 1) // MOE_BLK * MOE_BLK
    pend = jnp.cumsum(padded)
    pstart = pend - padded
    dest = _dest(topi8, rank8, pstart.astype(F32).reshape(N_EXPERTS, 1)).T.reshape(-1)
    topw_p = jnp.pad(topw8.T, ((0, 0), (0, LANES - TOP_K)))
    blk_start = (jnp.concatenate([pstart, pend[-1:]]) // MOE_BLK).astype(jnp.int32)
    blk_id = jnp.arange(N_SLOT_BLK, dtype=jnp.int32)
    blk_owner = jnp.minimum(jnp.sum(blk_id[:, None] >= blk_start[None, 1:], axis=1), N_EXPERTS - 1)
    left = counts[blk_owner] - (blk_id - blk_start[blk_owner]) * MOE_BLK
    blk_rows = jnp.clip((left + BLK_QUANT - 1) // BLK_QUANT * BLK_QUANT, 0, MOE_BLK).astype(jnp.int32)

    x_sorted, shared = _dispatch(dest, h2, wgu_sh, wd_sh)
    eo = _experts(blk_start, blk_rows, x_sorted, w_gate_exp[l], w_up_exp[l], w_down_exp[l])

    g_post = row2(norm_post_ffn[l])
    per_c = DEC_SEQ // TC
    y_p = _combine(dest, eo, x1, shared, topw_p, mod3, lambda i: 0, g_post, 0, T_CTX // TC)
    y_s = _combine(dest, eo, x1, shared, topw_p, mod3, lambda i: 1 + i // per_c, g_post,
                   T_CTX // TC, T_LAT // TC)

    return (y_p.reshape(BATCH, SEQ, D_MODEL), y_s.reshape(DEC_BATCH, DEC_SEQ, D_MODEL),
            ka_c.reshape(BATCH, 1, SEQ, A_KV_HEADS, A_HEAD_DIM),
            va_c.reshape(BATCH, 1, SEQ, A_KV_HEADS, A_HEAD_DIM),
            ckv_c.reshape(BATCH, 1, SEQ, KV_LORA),
            kr_c[:, B_NOPE:B_NOPE + B_ROPE].reshape(BATCH, 1, SEQ, B_ROPE))
```

```python
import functools

import jax
import jax.numpy as jnp
import numpy as np
from jax import lax
from jax.experimental import pallas as pl
from jax.experimental.pallas import tpu as pltpu

D_MODEL = 1024
BATCH = 32
SEQ = 256
DEC_BATCH = 4
DEC_SEQ = 2048
PAST_LEN = 512
GRID_W = 64
WINDOW = 128
ROPE_BASE = 10000.0
EPS = 1e-6
NEG = -1e30
A_HEADS = 8
A_KV_HEADS = 2
A_GROUP = A_HEADS // A_KV_HEADS
A_HEAD_DIM = 64
B_HEADS = 8
B_NOPE = 64
B_ROPE = 32
B_V = 64
KV_LORA = 256
Q_LORA = 384
MLA_SCALE = (B_NOPE + B_ROPE) ** -0.5
LOG2E = 1.4426950408889634
N_EXPERTS = 256
TOP_K = 8
N_GROUPS = 8
TOPK_GROUPS = 4
D_EXPERT = 256
D_SHARED = 256
ROUTED_SCALE = 2.5

T_CTX = BATCH * SEQ
T_LAT = DEC_BATCH * DEC_SEQ
T_ALL = T_CTX + T_LAT

LANES = 128
TM = 512
TQ = 256
MOE_BLK = 256
N_SLOT_BLK = (T_ALL * TOP_K) // MOE_BLK + N_EXPERTS
N_SLOTS = N_SLOT_BLK * MOE_BLK
VMEM_LIMIT = 48 * 1024 * 1024

F32 = jnp.float32
BF16 = jnp.bfloat16

_O_QA = 0
_O_KA = _O_QA + A_HEADS * A_HEAD_DIM
_O_VA = _O_KA + A_KV_HEADS * A_HEAD_DIM
_O_QL = _O_VA + A_KV_HEADS * A_HEAD_DIM
_O_CKV = _O_QL + Q_LORA
_O_KR = _O_CKV + KV_LORA
_O_GATE = _O_KR + B_ROPE


def _params(*sem):
    return pltpu.CompilerParams(dimension_semantics=sem, vmem_limit_bytes=VMEM_LIMIT)


def _dot(a, b):
    return jnp.dot(a, b, preferred_element_type=F32)


def _dot_nt(a, b):
    return lax.dot_general(a, b, (((1,), (1,)), ((), ())), preferred_element_type=F32)


def _rms(x, g):
    return x * lax.rsqrt(jnp.mean(x * x, axis=-1, keepdims=True) + EPS) * g


def _sigmoid(x):
    return 1.0 / (1.0 + jnp.exp(-x))


SUB = 8
assert SUB * LANES == D_MODEL


def _load_rows(ref, n):
    return jnp.concatenate([ref[pl.ds(j, n, stride=SUB), :] for j in range(SUB)], axis=1)


def _store_rows(ref, val, n):
    for j in range(SUB):
        ref[pl.ds(j, n, stride=SUB), :] = val[:, j * LANES:(j + 1) * LANES]


def _silu(x):
    return x * _sigmoid(x)


def _mod_kernel(cond_ref, w_ref, b_ref, o_ref):
    s = _silu(cond_ref[...]).astype(BF16)
    o_ref[...] = _dot(s, w_ref[...].astype(BF16)) + b_ref[...]


def _modulation(cond8, w_ada, b_ada):
    tn = 768
    n = w_ada.shape[1]
    return pl.pallas_call(
        _mod_kernel,
        grid=(n // tn,),
        in_specs=[pl.BlockSpec((8, D_MODEL), lambda j: (0, 0)),
                  pl.BlockSpec((D_MODEL, tn), lambda j: (0, j)),
                  pl.BlockSpec((1, tn), lambda j: (0, j))],
        out_specs=pl.BlockSpec((8, tn), lambda j: (0, j)),
        out_shape=jax.ShapeDtypeStruct((8, n), F32),
        compiler_params=_params("arbitrary"),
        name="modulation",
    )(cond8, w_ada, b_ada.reshape(1, n))


def _rope(x, cos, sin, half):
    lane = lax.broadcasted_iota(jnp.int32, x.shape, 1)
    first = (lane & (2 * half - 1)) < half
    partner = jnp.where(first, pltpu.roll(x, LANES - half, 1), pltpu.roll(x, half, 1))
    return x * cos + partner * sin


def _rope_tables():
    t = np.arange(DEC_SEQ)
    row = (t // GRID_W).astype(np.float32)
    col = (t % GRID_W).astype(np.float32)

    def tables(d_rot):
        nf = d_rot // 4
        inv = np.float32(ROPE_BASE) ** (-np.arange(nf, dtype=np.float32) / np.float32(nf))
        ar = row[:, None] * inv
        ac = col[:, None] * inv
        cos = np.concatenate([np.cos(ar), np.cos(ar), np.cos(ac), np.cos(ac)], axis=1)
        sin = np.concatenate([-np.sin(ar), np.sin(ar), -np.sin(ac), np.sin(ac)], axis=1)
        return cos.astype(np.float32), sin.astype(np.float32)

    cos_a, sin_a = tables(A_HEAD_DIM)
    cos_a = np.concatenate([cos_a, cos_a], axis=1)
    sin_a = np.concatenate([sin_a, sin_a], axis=1)
    cos_b, sin_b = tables(B_ROPE)
    one = np.ones((DEC_SEQ, B_NOPE), np.float32)
    zero = np.zeros((DEC_SEQ, B_NOPE), np.float32)
    pad = LANES - B_NOPE - B_ROPE
    cos_b = np.concatenate([one, cos_b, one[:, :pad]], axis=1)
    sin_b = np.concatenate([zero, sin_b, zero[:, :pad]], axis=1)
    return tuple(jnp.asarray(a) for a in (cos_a, sin_a, cos_b, sin_b))


_N1 = 1536


def _inproj_kernel(latent, *refs):
    if latent:
        (x_ref, mod_ref, g_ref, w1_ref, qn_ref, kvn_ref, wuq_ref, wk_ref, wv_ref,
         ca_ref, sa_ref, cb_ref, sb_ref,
         qa_o, ka_o, va_o, qb_o, kb_o, vb_o) = refs
    else:
        (x_ref, mod_ref, g_ref, w1_ref, qn_ref, kvn_ref, wuq_ref, wk_ref, wv_ref,
         qa_o, ka_o, va_o, ckv_o, kr_o, qb_o, kb_o, vb_o) = refs
    x = x_ref[...]
    shift = mod_ref[:, 0:D_MODEL]
    scale = mod_ref[:, D_MODEL:2 * D_MODEL]
    h = _rms(x, g_ref[...]) * (1.0 + scale) + shift
    p = _dot(h.astype(BF16), w1_ref[...])
    qa = p[:, 0:512]
    ka = p[:, 512:640]
    va = p[:, 640:768]
    ql = p[:, 768:1152]
    ckv = p[:, 1152:1408]
    krp = p[:, 1408:1536]
    qn = _rms(ql, qn_ref[...])
    qb = _dot(qn.astype(BF16), wuq_ref[...])
    cn = _rms(ckv, kvn_ref[...])
    cnb = cn.astype(BF16)
    kn = _dot(cnb, wk_ref[...])
    vb = _dot(cnb, wv_ref[...])
    if latent:
        ca, sa, cb, sb = ca_ref[...], sa_ref[...], cb_ref[...], sb_ref[...]
        ka = _rope(ka, ca, sa, A_HEAD_DIM // 4)
        krp = _rope(krp, cb, sb, B_ROPE // 4)
        for j in range(4):
            blk = _rope(qa[:, j * LANES:(j + 1) * LANES], ca, sa, A_HEAD_DIM // 4)
            qa_o[:, j * LANES:(j + 1) * LANES] = (blk * (A_HEAD_DIM ** -0.5 * LOG2E)).astype(BF16)
        for hd in range(B_HEADS):
            blk = _rope(qb[:, hd * LANES:(hd + 1) * LANES], cb, sb, B_ROPE // 4)
            qb_o[:, hd * LANES:(hd + 1) * LANES] = (blk * (MLA_SCALE * LOG2E)).astype(BF16)
        ka_o[...] = ka.astype(BF16)
        va_o[...] = va.astype(BF16)
    else:
        qa_o[...] = (qa * (A_HEAD_DIM ** -0.5 * LOG2E)).astype(BF16)
        qb_o[...] = (qb * (MLA_SCALE * LOG2E)).astype(BF16)
        ka_o[...] = ka
        va_o[...] = va
        ckv_o[...] = cn
        kr_o[...] = krp
    for hd in range(B_HEADS):
        kb_o[:, hd * LANES:(hd + 1) * LANES] = (kn[:, hd * LANES:(hd + 1) * LANES] + krp).astype(BF16)
    vb_o[...] = vb.astype(BF16)


def _inproj(x, mod3, mod_row, g, w1, qn, kvn, wuq, wk, wv, tables):
    latent = tables is not None
    t = x.shape[0]
    n_tiles = t // TM
    row = lambda i: (i, 0)
    const = lambda i: (0, 0)
    in_specs = [pl.BlockSpec((TM, D_MODEL), row),
                pl.BlockSpec((None, 1, 6 * D_MODEL), lambda i: (mod_row(i), 0, 0)),
                pl.BlockSpec((1, D_MODEL), const),
                pl.BlockSpec((D_MODEL, _N1), const),
                pl.BlockSpec((1, Q_LORA), const),
                pl.BlockSpec((1, KV_LORA), const),
                pl.BlockSpec((Q_LORA, B_HEADS * LANES), const),
                pl.BlockSpec((KV_LORA, B_HEADS * LANES), const),
                pl.BlockSpec((KV_LORA, B_HEADS * B_V), const)]
    args = [x, mod3, g, w1, qn, kvn, wuq, wk, wv]
    sds = jax.ShapeDtypeStruct
    if latent:
        per = DEC_SEQ // TM
        tab = lambda i: (i % per, 0)
        in_specs += [pl.BlockSpec((TM, LANES), tab)] * 4
        args += list(tables)
        out_shape = [sds((t, 512), BF16), sds((t, LANES), BF16), sds((t, LANES), BF16),
                     sds((t, B_HEADS * LANES), BF16), sds((t, B_HEADS * LANES), BF16),
                     sds((t, B_HEADS * B_V), BF16)]
        widths = [512, LANES, LANES, B_HEADS * LANES, B_HEADS * LANES, B_HEADS * B_V]
    else:
        out_shape = [sds((t, 512), BF16), sds((t, LANES), F32), sds((t, LANES), F32),
                     sds((t, KV_LORA), F32), sds((t, LANES), F32),
                     sds((t, B_HEADS * LANES), BF16), sds((t, B_HEADS * LANES), BF16),
                     sds((t, B_HEADS * B_V), BF16)]
        widths = [512, LANES, LANES, KV_LORA, LANES, B_HEADS * LANES, B_HEADS * LANES, B_HEADS * B_V]
    out_specs = [pl.BlockSpec((TM, w), row) for w in widths]
    return pl.pallas_call(
        functools.partial(_inproj_kernel, latent),
        grid=(n_tiles,),
        in_specs=in_specs,
        out_specs=out_specs,
        out_shape=out_shape,
        compiler_params=_params("arbitrary"),
        name="inproj_latent" if latent else "inproj_ctx",
    )(*args)


def _kvexp_kernel(ckv_ref, krp_ref, wk_ref, wv_ref, kb_o, vb_o):
    cb = ckv_ref[...].astype(BF16)
    kn = _dot(cb, wk_ref[...])
    krp = krp_ref[...]
    for hd in range(B_HEADS):
        kb_o[:, hd * LANES:(hd + 1) * LANES] = (kn[:, hd * LANES:(hd + 1) * LANES] + krp).astype(BF16)
    vb_o[...] = _dot(cb, wv_ref[...]).astype(BF16)


def _kv_expand(ckv, krp, wk, wv):
    t = ckv.shape[0]
    row = lambda i: (i, 0)
    const = lambda i: (0, 0)
    return pl.pallas_call(
        _kvexp_kernel,
        grid=(t // TM,),
        in_specs=[pl.BlockSpec((TM, KV_LORA), row), pl.BlockSpec((TM, LANES), row),
                  pl.BlockSpec((KV_LORA, B_HEADS * LANES), const),
                  pl.BlockSpec((KV_LORA, B_HEADS * B_V), const)],
        out_specs=[pl.BlockSpec((TM, B_HEADS * LANES), row), pl.BlockSpec((TM, B_HEADS * B_V), row)],
        out_shape=[jax.ShapeDtypeStruct((t, B_HEADS * LANES), BF16),
                   jax.ShapeDtypeStruct((t, B_HEADS * B_V), BF16)],
        compiler_params=_params("arbitrary"),
        name="kv_expand",
    )(ckv, krp, wk, wv)


def _lo_mask(rows):
    return lax.broadcasted_iota(jnp.int32, (rows, LANES), 1) < (LANES // 2)


def _softmax_pv(parts, sink):
    if sink is not None:
        sink = sink * LOG2E
    m = None
    for s, _ in parts:
        mi = jnp.max(s, axis=-1, keepdims=True)
        m = mi if m is None else jnp.maximum(m, mi)
    if sink is not None:
        m = jnp.maximum(m, sink)
    den = None
    acc = None
    for s, v in parts:
        p = jnp.exp2(s - m)
        di = jnp.sum(p, axis=-1, keepdims=True)
        den = di if den is None else den + di
        oi = _dot(p.astype(BF16), v)
        acc = oi if acc is None else acc + oi
    if sink is not None:
        den = den + jnp.exp2(sink - m)
    return acc / den


def _ctx_attn_kernel(sink_ref, qa_ref, ka_ref, va_ref, qb_ref, kb_ref, vb_ref, oa_ref, ob_ref):
    lo = _lo_mask(SEQ)
    k = ka_ref[...].astype(BF16)
    v = va_ref[...].astype(BF16)
    for j in range(A_GROUP):
        q = qa_ref[:, j * LANES:(j + 1) * LANES]
        zero = jnp.zeros_like(q)
        o0 = _softmax_pv([(_dot_nt(jnp.where(lo, q, zero), k), v)], sink_ref[j])
        o1 = _softmax_pv([(_dot_nt(jnp.where(lo, zero, q), k), v)], sink_ref[A_GROUP + j])
        oa_ref[:, j * LANES:(j + 1) * LANES] = jnp.where(lo, o0, o1).astype(BF16)
    for j in range(B_HEADS // 2):
        vp = vb_ref[:, j * LANES:(j + 1) * LANES]
        outs = []
        for hd in (2 * j, 2 * j + 1):
            s = _dot_nt(qb_ref[:, hd * LANES:(hd + 1) * LANES], kb_ref[:, hd * LANES:(hd + 1) * LANES])
            outs.append(_softmax_pv([(s, vp)], None))
        ob_ref[:, j * LANES:(j + 1) * LANES] = jnp.where(lo, outs[0], outs[1]).astype(BF16)


def _ctx_attention(sink, qa, ka, va, qb, kb, vb):
    blk = lambda w: pl.BlockSpec((SEQ, w), lambda b: (b, 0))
    return pl.pallas_call(
        _ctx_attn_kernel,
        grid=(BATCH,),
        in_specs=[pl.BlockSpec(memory_space=pltpu.SMEM),
                  blk(512), blk(LANES), blk(LANES), blk(B_HEADS * LANES), blk(B_HEADS * LANES),
                  blk(B_HEADS * B_V)],
        out_specs=[blk(512), blk(512)],
        out_shape=[jax.ShapeDtypeStruct((T_CTX, 512), BF16), jax.ShapeDtypeStruct((T_CTX, 512), BF16)],
        compiler_params=_params("arbitrary"),
        name="ctx_attention",
    )(sink, qa, ka, va, qb, kb, vb)


_WIN = TQ + 2 * WINDOW


def _lat_swa_kernel(sink_ref, q_ref, k_ref, v_ref, kc_ref, vc_ref, o_ref):
    qi = pl.program_id(1)
    q0 = qi * TQ
    start = pl.multiple_of(jnp.clip(q0 - WINDOW, 0, DEC_SEQ - _WIN), WINDOW)
    kw = k_ref[pl.ds(start, _WIN), :]
    vw = v_ref[pl.ds(start, _WIN), :]
    kc = kc_ref[...].astype(BF16)
    vc = vc_ref[...].astype(BF16)
    qpos = q0 + lax.broadcasted_iota(jnp.int32, (TQ, _WIN), 0)
    kpos = start + lax.broadcasted_iota(jnp.int32, (TQ, _WIN), 1)
    valid = jnp.abs(qpos - kpos) <= WINDOW
    lo = _lo_mask(TQ)
    for j in range(A_GROUP):
        q = q_ref[:, j * LANES:(j + 1) * LANES]
        zero = jnp.zeros_like(q)
        outs = []
        for g, qh in ((0, jnp.where(lo, q, zero)), (1, jnp.where(lo, zero, q))):
            sw = jnp.where(valid, _dot_nt(qh, kw), NEG)
            sc = _dot_nt(qh, kc)
            outs.append(_softmax_pv([(sw, vw), (sc, vc)], sink_ref[g * A_GROUP + j]))
        o_ref[:, j * LANES:(j + 1) * LANES] = jnp.where(lo, outs[0], outs[1]).astype(BF16)


def _lat_swa(sink, qa, ka, va, kc, vc):
    nq = DEC_SEQ // TQ
    return pl.pallas_call(
        _lat_swa_kernel,
        grid=(DEC_BATCH, nq),
        in_specs=[pl.BlockSpec(memory_space=pltpu.SMEM),
                  pl.BlockSpec((TQ, 512), lambda b, i: (b * nq + i, 0)),
                  pl.BlockSpec((None, DEC_SEQ, LANES), lambda b, i: (b, 0, 0)),
                  pl.BlockSpec((None, DEC_SEQ, LANES), lambda b, i: (b, 0, 0)),
                  pl.BlockSpec((None, PAST_LEN, LANES), lambda b, i: (b, 0, 0)),
                  pl.BlockSpec((None, PAST_LEN, LANES), lambda b, i: (b, 0, 0))],
        out_specs=pl.BlockSpec((TQ, 512), lambda b, i: (b * nq + i, 0)),
        out_shape=jax.ShapeDtypeStruct((T_LAT, 512), BF16),
        compiler_params=_params("arbitrary", "arbitrary"),
        name="latent_swa",
    )(sink, qa, ka.reshape(DEC_BATCH, DEC_SEQ, LANES), va.reshape(DEC_BATCH, DEC_SEQ, LANES), kc, vc)


TQ_MLA = 256


def _lat_mla_kernel(q_ref, kl_ref, vl_ref, kc_ref, vc_ref, o_ref):
    lo = _lo_mask(TQ_MLA)
    vl = vl_ref[...]
    vc = vc_ref[...]
    outs = []
    for hh in range(2):
        q = q_ref[:, hh * LANES:(hh + 1) * LANES]
        sc = _dot_nt(q, kc_ref[:, hh * LANES:(hh + 1) * LANES])
        sl = _dot_nt(q, kl_ref[:, hh * LANES:(hh + 1) * LANES])
        outs.append(_softmax_pv([(sc, vc), (sl, vl)], None))
    o_ref[...] = jnp.where(lo, outs[0], outs[1]).astype(BF16)


def _lat_mla(qb, kb, vb, kc, vc):
    nq = DEC_SEQ // TQ_MLA
    npair = B_HEADS // 2
    return pl.pallas_call(
        _lat_mla_kernel,
        grid=(DEC_BATCH, npair, nq),
        in_specs=[pl.BlockSpec((TQ_MLA, 2 * LANES), lambda b, p, i: (b * nq + i, p)),
                  pl.BlockSpec((None, DEC_SEQ, 2 * LANES), lambda b, p, i: (b, 0, p)),
                  pl.BlockSpec((None, DEC_SEQ, LANES), lambda b, p, i: (b, 0, p)),
                  pl.BlockSpec((None, PAST_LEN, 2 * LANES), lambda b, p, i: (b, 0, p)),
                  pl.BlockSpec((None, PAST_LEN, LANES), lambda b, p, i: (b, 0, p))],
        out_specs=pl.BlockSpec((TQ_MLA, LANES), lambda b, p, i: (b * nq + i, p)),
        out_shape=jax.ShapeDtypeStruct((T_LAT, 512), BF16),
        compiler_params=_params("arbitrary", "arbitrary", "arbitrary"),
        name="latent_mla",
    )(qb, kb.reshape(DEC_BATCH, DEC_SEQ, B_HEADS * LANES), vb.reshape(DEC_BATCH, DEC_SEQ, B_HEADS * B_V),
      kc.reshape(DEC_BATCH, PAST_LEN, B_HEADS * LANES), vc.reshape(DEC_BATCH, PAST_LEN, B_HEADS * B_V))


_N_CTX_TILES = T_CTX // TM


def _merge_kernel(xp_ref, xs_ref, oac_ref, oal_ref, obc_ref, obl_ref, mod_ref,
                  gpre_ref, gpost_ref, gffn_ref, woa_ref, wob_ref, wout_ref, wg_ref, wr_ref,
                  x1_o, h2_o, lg_o):
    def body(x_ref, oa_ref, ob_ref):
        x = x_ref[...]
        sh1 = mod_ref[:, 0:D_MODEL]
        sc1 = mod_ref[:, D_MODEL:2 * D_MODEL]
        g1 = mod_ref[:, 2 * D_MODEL:3 * D_MODEL]
        sh2 = mod_ref[:, 3 * D_MODEL:4 * D_MODEL]
        sc2 = mod_ref[:, 4 * D_MODEL:5 * D_MODEL]
        h = (_rms(x, gpre_ref[...]) * (1.0 + sc1) + sh1).astype(BF16)
        gates = _dot(h, wg_ref[...])
        ya = _dot(oa_ref[...], woa_ref[...])
        yb = _dot(ob_ref[...], wob_ref[...])
        y = _sigmoid(gates[:, :D_MODEL]) * ya + _sigmoid(gates[:, D_MODEL:]) * yb
        mix = _dot(y.astype(BF16), wout_ref[...])
        x1 = x + g1 * _rms(mix, gpost_ref[...])
        x1_o[...] = x1
        h2 = _rms(x1, gffn_ref[...]) * (1.0 + sc2) + sh2
        _store_rows(h2_o, h2, TM)
        lg_o[...] = _dot_nt(wr_ref[...], h2.astype(BF16))

    is_ctx = pl.program_id(0) < _N_CTX_TILES

    @pl.when(is_ctx)
    def _():
        body(xp_ref, oac_ref, obc_ref)

    @pl.when(jnp.logical_not(is_ctx))
    def _():
        body(xs_ref, oal_ref, obl_ref)


def _mod_row_all(i):
    per = DEC_SEQ // TM
    return jnp.where(i < _N_CTX_TILES, 0, 1 + (jnp.maximum(i - _N_CTX_TILES, 0)) // per)


def _merge(xp, xs, oac, oal, obc, obl, mod3, gpre, gpost, gffn, woa, wob, wout, wg, wr):
    ctx = lambda i: (jnp.minimum(i, _N_CTX_TILES - 1), 0)
    lat = lambda i: (jnp.maximum(i - _N_CTX_TILES, 0), 0)
    row = lambda i: (i, 0)
    const = lambda i: (0, 0)
    return pl.pallas_call(
        _merge_kernel,
        grid=(T_ALL // TM,),
        in_specs=[pl.BlockSpec((TM, D_MODEL), ctx), pl.BlockSpec((TM, D_MODEL), lat),
                  pl.BlockSpec((TM, 512), ctx), pl.BlockSpec((TM, 512), lat),
                  pl.BlockSpec((TM, 512), ctx), pl.BlockSpec((TM, 512), lat),
                  pl.BlockSpec((None, 1, 6 * D_MODEL), lambda i: (_mod_row_all(i), 0, 0)),
                  pl.BlockSpec((1, D_MODEL), const), pl.BlockSpec((1, D_MODEL), const),
                  pl.BlockSpec((1, D_MODEL), const),
                  pl.BlockSpec((512, D_MODEL), const), pl.BlockSpec((512, D_MODEL), const),
                  pl.BlockSpec((D_MODEL, D_MODEL), const), pl.BlockSpec((D_MODEL, 2 * D_MODEL), const),
                  pl.BlockSpec((N_EXPERTS, D_MODEL), const)],
        out_specs=[pl.BlockSpec((TM, D_MODEL), row), pl.BlockSpec((TM * SUB, LANES), row),
                   pl.BlockSpec((N_EXPERTS, TM), lambda i: (0, i))],
        out_shape=[jax.ShapeDtypeStruct((T_ALL, D_MODEL), F32), jax.ShapeDtypeStruct((T_ALL * SUB, LANES), F32),
                   jax.ShapeDtypeStruct((N_EXPERTS, T_ALL), F32)],
        compiler_params=_params("arbitrary"),
        name="merge",
    )(xp, xs, oac, oal, obc, obl, mod3, gpre, gpost, gffn, woa, wob, wout, wg, wr)


_PER_GROUP = N_EXPERTS // N_GROUPS


def _router_kernel(lg_ref, bias_ref, topi_o, rank_o, topw_o, cnt_o, cnt_s):
    @pl.when(pl.program_id(0) == 0)
    def _():
        cnt_s[...] = jnp.zeros_like(cnt_s)

    scores = _sigmoid(lg_ref[...])
    sel = scores + bias_ref[...]
    eidx = lax.broadcasted_iota(jnp.int32, (N_EXPERTS, TM), 0).astype(F32)
    ninf = -jnp.inf
    big = float(N_EXPERTS)
    sel_g = [sel[g * _PER_GROUP:(g + 1) * _PER_GROUP, :] for g in range(N_GROUPS)]
    gs = []
    for sg in sel_g:
        m1 = jnp.max(sg, axis=0, keepdims=True)
        top = sg == m1
        n_top = jnp.sum(jnp.where(top, 1.0, 0.0), axis=0, keepdims=True)
        rest = jnp.max(jnp.where(top, ninf, sg), axis=0, keepdims=True)
        gs.append(m1 + jnp.where(n_top > 1.5, m1, rest))
    cur_g = []
    for g in range(N_GROUPS):
        beat = jnp.zeros((1, TM), F32)
        for g2 in range(N_GROUPS):
            if g2 == g:
                continue
            better = (gs[g2] >= gs[g]) if g2 < g else (gs[g2] > gs[g])
            beat = beat + jnp.where(better, 1.0, 0.0)
        cur_g.append(jnp.where(beat < TOPK_GROUPS, sel_g[g], NEG))
    cur = jnp.concatenate(cur_g, axis=0)
    chosen = jnp.zeros((N_EXPERTS, TM), F32)
    idxs, ws = [], []
    for k in range(TOP_K):
        m = jnp.max(cur, axis=0, keepdims=True)
        idx = jnp.min(jnp.where(cur == m, eidx, big), axis=0, keepdims=True)
        hit = eidx == idx
        ws.append(jnp.sum(jnp.where(hit, scores, 0.0), axis=0, keepdims=True))
        cur = jnp.where(hit, ninf, cur)
        chosen = jnp.where(hit, 1.0, chosen)
        idxs.append(idx)
    wsum = ws[0]
    for k in range(1, TOP_K):
        wsum = wsum + ws[k]
    for k in range(TOP_K):
        topw_o[k:k + 1, :] = ws[k] / wsum * ROUTED_SCALE
        topi_o[k:k + 1, :] = idxs[k].astype(jnp.int32)
    r_i = lax.broadcasted_iota(jnp.int32, (TM, TM), 0)
    c_i = lax.broadcasted_iota(jnp.int32, (TM, TM), 1)
    earlier = jnp.where(r_i < c_i, 1.0, 0.0).astype(BF16)
    prefix = _dot(chosen.astype(BF16), earlier) + cnt_s[:, 0:1]
    for k in range(TOP_K):
        r = jnp.sum(jnp.where(eidx == idxs[k], prefix, 0.0), axis=0, keepdims=True)
        rank_o[k:k + 1, :] = r.astype(jnp.int32)
    total = cnt_s[...] + jnp.sum(chosen, axis=1, keepdims=True)
    cnt_s[...] = total
    cnt_o[...] = total


def _router(logits_t, bias_col):
    col = lambda i: (0, i)
    const = lambda i: (0, 0)
    return pl.pallas_call(
        _router_kernel,
        grid=(T_ALL // TM,),
        in_specs=[pl.BlockSpec((N_EXPERTS, TM), col), pl.BlockSpec((N_EXPERTS, 1), const)],
        out_specs=[pl.BlockSpec((TOP_K, TM), col), pl.BlockSpec((TOP_K, TM), col),
                   pl.BlockSpec((TOP_K, TM), col), pl.BlockSpec((N_EXPERTS, LANES), const)],
        out_shape=[jax.ShapeDtypeStruct((TOP_K, T_ALL), jnp.int32),
                   jax.ShapeDtypeStruct((TOP_K, T_ALL), jnp.int32),
                   jax.ShapeDtypeStruct((TOP_K, T_ALL), F32),
                   jax.ShapeDtypeStruct((N_EXPERTS, LANES), F32)],
        scratch_shapes=[pltpu.VMEM((N_EXPERTS, LANES), F32)],
        compiler_params=_params("arbitrary"),
        name="router",
    )(logits_t, bias_col)


def _dest_kernel(topi_ref, rank_ref, ps_ref, o_ref):
    eidx = lax.broadcasted_iota(jnp.int32, (N_EXPERTS, TM), 0)
    ps = jnp.broadcast_to(ps_ref[...], (N_EXPERTS, TM))
    for k in range(TOP_K):
        start = jnp.sum(jnp.where(eidx == topi_ref[k:k + 1, :], ps, 0.0), axis=0, keepdims=True)
        o_ref[k:k + 1, :] = rank_ref[k:k + 1, :] + start.astype(jnp.int32)


def _dest(topi, rank, pstart_col):
    col = lambda i: (0, i)
    return pl.pallas_call(
        _dest_kernel,
        grid=(T_ALL // TM,),
        in_specs=[pl.BlockSpec((TOP_K, TM), col), pl.BlockSpec((TOP_K, TM), col),
                  pl.BlockSpec((N_EXPERTS, 1), lambda i: (0, 0))],
        out_specs=pl.BlockSpec((TOP_K, TM), col),
        out_shape=jax.ShapeDtypeStruct((TOP_K, T_ALL), jnp.int32),
        compiler_params=_params("arbitrary"),
        name="dest",
    )(topi, rank, pstart_col)


TD = 512
DISPATCH_PARTS = 4
ROW_UNROLL = 4


def _dispatch_kernel(dest_ref, h_ref, wgu_ref, wd_ref, xs_hbm, sh_o, sem):
    def row(t, carry):
        src = h_ref.at[pl.ds(pl.multiple_of(t * SUB, SUB), SUB)]
        for k in range(TOP_K):
            d = dest_ref[t * TOP_K + k]
            pltpu.make_async_copy(src, xs_hbm.at[pl.ds(pl.multiple_of(d * SUB, SUB), SUB)],
                                  sem).start(priority=k % 2)
        return carry

    part = TD // DISPATCH_PARTS
    for p in range(DISPATCH_PARTS):
        lax.fori_loop(p * part, (p + 1) * part, row, 0, unroll=ROW_UNROLL)
        x = _load_rows(h_ref.at[pl.ds(p * part * SUB, part * SUB)], part).astype(BF16)
        gu = _dot(x, wgu_ref[...])
        a = _silu(gu[:, :D_SHARED]) * gu[:, D_SHARED:]
        sh_o[p * part:(p + 1) * part, :] = _dot(a.astype(BF16), wd_ref[...])
    for _ in range(TOP_K):
        pltpu.make_async_copy(h_ref, xs_hbm.at[pl.ds(0, TD * SUB)], sem).wait()


def _dispatch(dest_flat, h2, wgu, wd):
    const = lambda i: (0, 0)
    return pl.pallas_call(
        _dispatch_kernel,
        grid=(T_ALL // TD,),
        in_specs=[pl.BlockSpec((TD * TOP_K,), lambda i: (i,), memory_space=pltpu.SMEM),
                  pl.BlockSpec((TD * SUB, LANES), lambda i: (i, 0)),
                  pl.BlockSpec((D_MODEL, 2 * D_SHARED), const), pl.BlockSpec((D_SHARED, D_MODEL), const)],
        out_specs=[pl.BlockSpec(memory_space=pl.ANY), pl.BlockSpec((TD, D_MODEL), lambda i: (i, 0))],
        out_shape=[jax.ShapeDtypeStruct((N_SLOTS * SUB, LANES), F32),
                   jax.ShapeDtypeStruct((T_ALL, D_MODEL), F32)],
        scratch_shapes=[pltpu.SemaphoreType.DMA(())],
        compiler_params=_params("arbitrary"),
        name="dispatch",
    )(dest_flat, h2, wgu, wd)


_BLK_TILES = MOE_BLK * SUB
NBX = 6
PREFETCH = NBX - 2
NBO = 4


def _expert_kernel(bs_ref, x_hbm, wg_ref, wu_ref, wd_ref, o_hbm, wgu_s, wd_s, xbuf, obuf, xsem, osem):
    e = pl.program_id(0)
    b0 = bs_ref[e]
    b1 = bs_ref[e + 1]
    n_used = bs_ref[N_EXPERTS]

    def xcopy(g):
        s = lax.rem(g, NBX)
        return pltpu.make_async_copy(x_hbm.at[pl.ds(pl.multiple_of(g * _BLK_TILES, _BLK_TILES), _BLK_TILES)],
                                     xbuf.at[s], xsem.at[s])

    def ocopy(g):
        s = lax.rem(g, NBO)
        return pltpu.make_async_copy(obuf.at[s],
                                     o_hbm.at[pl.ds(pl.multiple_of(g * _BLK_TILES, _BLK_TILES), _BLK_TILES)],
                                     osem.at[s])

    @pl.when(e == 0)
    def _():
        for j in range(PREFETCH):
            @pl.when(j < n_used)
            def _():
                xcopy(j).start()

    @pl.when(b1 > b0)
    def _():
        wgu_s[:, 0:D_EXPERT] = wg_ref[...].astype(BF16)
        wgu_s[:, D_EXPERT:2 * D_EXPERT] = wu_ref[...].astype(BF16)
        wd_s[...] = wd_ref[...].astype(BF16)

    def acquire(g):
        @pl.when(g + PREFETCH < n_used)
        def _():
            xcopy(g + PREFETCH).start()

        xcopy(g).wait()

        @pl.when(g >= NBO)
        def _():
            ocopy(g - NBO).wait()

    def compute(g):
        gu = _dot(_load_rows(xbuf.at[lax.rem(g, NBX)], MOE_BLK).astype(BF16), wgu_s[...])
        a = _silu(gu[:, :D_EXPERT]) * gu[:, D_EXPERT:]
        _store_rows(obuf.at[lax.rem(g, NBO)], _dot(a.astype(BF16), wd_s[...]), MOE_BLK)

    def pair(i, carry):
        g = b0 + 2 * i
        acquire(g)
        acquire(g + 1)
        compute(g)
        compute(g + 1)
        ocopy(g).start(priority=1)
        ocopy(g + 1).start(priority=1)
        return carry

    lax.fori_loop(0, lax.shift_right_logical(b1 - b0, 1), pair, 0)

    @pl.when(((b1 - b0) & 1) == 1)
    def _():
        acquire(b1 - 1)
        compute(b1 - 1)
        ocopy(b1 - 1).start(priority=1)

    @pl.when(e == N_EXPERTS - 1)
    def _():
        for j in range(NBO, 0, -1):
            @pl.when(n_used >= j)
            def _():
                ocopy(n_used - j).wait()


def _experts(blk_start, x_sorted, w_gate, w_up, w_down):
    wsel = lambda e, bs: (e, 0, 0)
    grid_spec = pltpu.PrefetchScalarGridSpec(
        num_scalar_prefetch=1,
        grid=(N_EXPERTS,),
        in_specs=[pl.BlockSpec(memory_space=pl.ANY),
                  pl.BlockSpec((None, D_MODEL, D_EXPERT), wsel),
                  pl.BlockSpec((None, D_MODEL, D_EXPERT), wsel),
                  pl.BlockSpec((None, D_EXPERT, D_MODEL), wsel)],
        out_specs=pl.BlockSpec(memory_space=pl.ANY),
        scratch_shapes=[pltpu.VMEM((D_MODEL, 2 * D_EXPERT), BF16), pltpu.VMEM((D_EXPERT, D_MODEL), BF16),
                        pltpu.VMEM((NBX, _BLK_TILES, LANES), F32), pltpu.VMEM((NBO, _BLK_TILES, LANES), F32),
                        pltpu.SemaphoreType.DMA((NBX,)), pltpu.SemaphoreType.DMA((NBO,))],
    )
    return pl.pallas_call(
        _expert_kernel,
        grid_spec=grid_spec,
        out_shape=jax.ShapeDtypeStruct((N_SLOTS * SUB, LANES), F32),
        compiler_params=_params("arbitrary"),
        name="experts",
    )(blk_start, x_sorted, w_gate, w_up, w_down)


TC = 256
CR = 16


def _combine_kernel(dcur_ref, dnext_ref, eo_hbm, x1_ref, sh_ref, w_ref, mod_ref, g_ref, o_ref, buf, sem, wb_s):
    i = pl.program_id(0)
    n = pl.num_programs(0)
    slot = lax.rem(i, 2)

    def gather(d_ref, s, t0, t1):
        def row(t, carry):
            for k in range(TOP_K):
                d = d_ref[t * TOP_K + k]
                pltpu.make_async_copy(eo_hbm.at[pl.ds(pl.multiple_of(d * SUB, SUB), SUB)],
                                      buf.at[s, k, pl.ds(pl.multiple_of(t * SUB, SUB), SUB)],
                                      sem.at[s]).start(priority=k % 2)
            return carry
        lax.fori_loop(t0, t1, row, 0, unroll=ROW_UNROLL)

    @pl.when(i == 0)
    def _():
        gather(dcur_ref, 0, 0, TC)

    @pl.when(i + 1 < n)
    def _():
        gather(dnext_ref, 1 - slot, 0, TC)

    gate = mod_ref[:, 5 * D_MODEL:6 * D_MODEL]
    for k in range(TOP_K):
        wb_s[k] = jnp.broadcast_to(w_ref[:, k:k + 1], (TC, LANES))
    for k in range(TOP_K):
        pltpu.make_async_copy(eo_hbm.at[pl.ds(0, TC * SUB)], buf.at[slot, k], sem.at[slot]).wait()

    lanes = [slice(j * LANES, (j + 1) * LANES) for j in range(SUB)]

    def chunk(r, carry):
        r0 = pl.multiple_of(r * CR, CR)
        rows = pl.ds(r0, CR)
        wk = [wb_s[k, rows, :] for k in range(TOP_K)]
        pieces = []
        sq = jnp.zeros((CR, LANES), F32)
        for j in range(SUB):
            acc = sh_ref[rows, lanes[j]]
            for k in range(TOP_K):
                acc = acc + buf[slot, k, pl.ds(r0 * SUB + j, CR, stride=SUB), :] * wk[k]
            pieces.append(acc)
            sq = sq + acc * acc
        inv = lax.rsqrt(jnp.sum(sq, axis=-1, keepdims=True) / D_MODEL + EPS)
        for j in range(SUB):
            o_ref[rows, lanes[j]] = (x1_ref[rows, lanes[j]]
                                     + gate[:, lanes[j]] * (pieces[j] * inv * g_ref[:, lanes[j]]))
        return carry

    lax.fori_loop(0, TC // CR, chunk, 0, unroll=8)


def _combine(dest_flat, eo, x1, shared, topw, mod3, mod_row, g, tile0, n_tiles):
    row = lambda i: (tile0 + i, 0)
    const = lambda i: (0, 0)
    blk = TC * TOP_K
    return pl.pallas_call(
        _combine_kernel,
        grid=(n_tiles,),
        in_specs=[pl.BlockSpec((blk,), lambda i: (tile0 + i,), memory_space=pltpu.SMEM),
                  pl.BlockSpec((blk,), lambda i: (tile0 + jnp.minimum(i + 1, n_tiles - 1),),
                               memory_space=pltpu.SMEM),
                  pl.BlockSpec(memory_space=pl.ANY),
                  pl.BlockSpec((TC, D_MODEL), row), pl.BlockSpec((TC, D_MODEL), row),
                  pl.BlockSpec((TC, LANES), row),
                  pl.BlockSpec((None, 1, 6 * D_MODEL), lambda i: (mod_row(i), 0, 0)),
                  pl.BlockSpec((1, D_MODEL), const)],
        out_specs=pl.BlockSpec((TC, D_MODEL), lambda i: (i, 0)),
        out_shape=jax.ShapeDtypeStruct((n_tiles * TC, D_MODEL), F32),
        scratch_shapes=[pltpu.VMEM((2, TOP_K, TC * SUB, LANES), F32), pltpu.SemaphoreType.DMA((2,)),
                        pltpu.VMEM((TOP_K, TC, LANES), F32)],
        compiler_params=_params("arbitrary"),
        name="combine",
    )(dest_flat, dest_flat, eo, x1, shared, topw, mod3, g)


def _qa_perm():
    cols = []
    for j in range(A_GROUP):
        for g in range(A_KV_HEADS):
            hd = g * A_GROUP + j
            cols.extend(range(hd * A_HEAD_DIM, (hd + 1) * A_HEAD_DIM))
    return np.asarray(cols, np.int32)


def _pad_heads(w, width, n_heads, offset, total=LANES):
    k = w.shape[0]
    stride = w.shape[1] // n_heads
    w3 = w.reshape(k, n_heads, stride)[:, :, offset:offset + width]
    w3 = jnp.pad(w3, ((0, 0), (0, 0), (0, total - width)))
    return w3.reshape(k, n_heads * total)


def kernel(x_prompt, x_sample, cache_k_swa, cache_v_swa, cache_ckv_mla, cache_krope_mla, c, c_ctx, w_ada, b_ada, norm_pre_mix, norm_post_mix, norm_pre_ffn, norm_post_ffn, w_in, sink_swa, q_norm_mla, kv_norm_mla, w_uq_mla, w_ukv_mla, w_o_swa, w_o_mla, w_out, router_w, router_bias, w_gate_exp, w_up_exp, w_down_exp, w_gate_sh, w_up_sh, w_down_sh):
    l = 0
    perm = _qa_perm()
    wi = w_in[l]
    kr_blk = jnp.pad(wi[:, _O_KR:_O_GATE], ((0, 0), (B_NOPE, LANES - B_NOPE - B_ROPE)))
    w1 = jnp.concatenate([wi[:, _O_QA:_O_KA][:, perm], wi[:, _O_KA:_O_KR], kr_blk], axis=1).astype(BF16)
    wg = wi[:, _O_GATE:].astype(BF16)
    wuq = _pad_heads(w_uq_mla[l], B_NOPE + B_ROPE, B_HEADS, 0).astype(BF16)
    wk = _pad_heads(w_ukv_mla[l], B_NOPE, B_HEADS, 0).astype(BF16)
    wv = w_ukv_mla[l].reshape(KV_LORA, B_HEADS, B_NOPE + B_V)[:, :, B_NOPE:].reshape(KV_LORA, B_HEADS * B_V).astype(BF16)
    woa = w_o_swa[l][perm, :].astype(BF16)
    wob = w_o_mla[l].astype(BF16)
    wout = w_out[l].astype(BF16)
    wr = router_w[l].T.astype(BF16)
    wgu_sh = jnp.concatenate([w_gate_sh[l], w_up_sh[l]], axis=1).astype(BF16)
    wd_sh = w_down_sh[l].astype(BF16)
    sink = sink_swa[l]
    row2 = lambda v: v.reshape(1, -1)

    cond8 = jnp.concatenate([c_ctx[None, :], c, jnp.zeros((8 - 1 - DEC_BATCH, D_MODEL), F32)], axis=0)
    mod3 = _modulation(cond8, w_ada[l], b_ada[l]).reshape(8, 1, 6 * D_MODEL)

    xp = x_prompt.reshape(T_CTX, D_MODEL)
    xs = x_sample.reshape(T_LAT, D_MODEL)
    per = DEC_SEQ // TM
    shared_in = (row2(norm_pre_mix[l]), w1, row2(q_norm_mla[l]), row2(kv_norm_mla[l]), wuq, wk, wv)
    qa_c, ka_c, va_c, ckv_c, kr_c, qb_c, kb_c, vb_c = _inproj(xp, mod3, lambda i: 0, *shared_in, None)
    qa_l, ka_l, va_l, qb_l, kb_l, vb_l = _inproj(xs, mod3, lambda i: 1 + i // per, *shared_in, _rope_tables())

    oa_c, ob_c = _ctx_attention(sink, qa_c, ka_c, va_c, qb_c, kb_c, vb_c)

    krp_cache = jnp.pad(cache_krope_mla[:, l].reshape(DEC_BATCH * PAST_LEN, B_ROPE),
                        ((0, 0), (B_NOPE, LANES - B_NOPE - B_ROPE)))
    kc_b, vc_b = _kv_expand(cache_ckv_mla[:, l].reshape(DEC_BATCH * PAST_LEN, KV_LORA), krp_cache, wk, wv)
    oa_l = _lat_swa(sink, qa_l, ka_l, va_l,
                    cache_k_swa[:, l].reshape(DEC_BATCH, PAST_LEN, LANES),
                    cache_v_swa[:, l].reshape(DEC_BATCH, PAST_LEN, LANES))
    ob_l = _lat_mla(qb_l, kb_l, vb_l, kc_b, vc_b)

    x1, h2, logits = _merge(xp, xs, oa_c, oa_l, ob_c, ob_l, mod3,
                            row2(norm_pre_mix[l]), row2(norm_post_mix[l]), row2(norm_pre_ffn[l]),
                            woa, wob, wout, wg, wr)

    topi8, rank8, topw8, counts_b = _router(logits, router_bias[l].reshape(N_EXPERTS, 1))
    counts = counts_b[:, 0].astype(jnp.int32)

    padded = (counts + MOE_BLK - 1) // MOE_BLK * MOE_BLK
    pend = jnp.cumsum(padded)
    pstart = pend - padded
    dest = _dest(topi8, rank8, pstart.astype(F32).reshape(N_EXPERTS, 1)).T.reshape(-1)
    topw_p = jnp.pad(topw8.T, ((0, 0), (0, LANES - TOP_K)))
    blk_start = (jnp.concatenate([pstart, pend[-1:]]) // MOE_BLK).astype(jnp.int32)

    x_sorted, shared = _dispatch(dest, h2, wgu_sh, wd_sh)
    eo = _experts(blk_start, x_sorted, w_gate_exp[l], w_up_exp[l], w_down_exp[l])

    g_post = row2(norm_post_ffn[l])
    per_c = DEC_SEQ // TC
    y_p = _combine(dest, eo, x1, shared, topw_p, mod3, lambda i: 0, g_post, 0, T_CTX // TC)
    y_s = _combine(dest, eo, x1, shared, topw_p, mod3, lambda i: 1 + i // per_c, g_post,
                   T_CTX // TC, T_LAT // TC)

    return (y_p.reshape(BATCH, SEQ, D_MODEL), y_s.reshape(DEC_BATCH, DEC_SEQ, D_MODEL),
            ka_c.reshape(BATCH, 1, SEQ, A_KV_HEADS, A_HEAD_DIM),
            va_c.reshape(BATCH, 1, SEQ, A_KV_HEADS, A_HEAD_DIM),
            ckv_c.reshape(BATCH, 1, SEQ, KV_LORA),
            kr_c[:, B_NOPE:B_NOPE + B_ROPE].reshape(BATCH, 1, SEQ, B_ROPE))
```

```python
import functools

import jax
import jax.numpy as jnp
import numpy as np
from jax import lax
from jax.experimental import pallas as pl
from jax.experimental.pallas import tpu as pltpu

D_MODEL = 1024
BATCH = 32
SEQ = 256
DEC_BATCH = 4
DEC_SEQ = 2048
PAST_LEN = 512
GRID_W = 64
WINDOW = 128
ROPE_BASE = 10000.0
EPS = 1e-6
NEG = -1e30
A_HEADS = 8
A_KV_HEADS = 2
A_GROUP = A_HEADS // A_KV_HEADS
A_HEAD_DIM = 64
B_HEADS = 8
B_NOPE = 64
B_ROPE = 32
B_V = 64
KV_LORA = 256
Q_LORA = 384
MLA_SCALE = (B_NOPE + B_ROPE) ** -0.5
LOG2E = 1.4426950408889634
N_EXPERTS = 256
TOP_K = 8
N_GROUPS = 8
TOPK_GROUPS = 4
D_EXPERT = 256
D_SHARED = 256
ROUTED_SCALE = 2.5

T_CTX = BATCH * SEQ
T_LAT = DEC_BATCH * DEC_SEQ
T_ALL = T_CTX + T_LAT

LANES = 128
TM = 512
TQ = 256
MOE_BLK = 256
N_SLOT_BLK = (T_ALL * TOP_K) // MOE_BLK + N_EXPERTS
N_SLOTS = N_SLOT_BLK * MOE_BLK
VMEM_LIMIT = 48 * 1024 * 1024

F32 = jnp.float32
BF16 = jnp.bfloat16

_O_QA = 0
_O_KA = _O_QA + A_HEADS * A_HEAD_DIM
_O_VA = _O_KA + A_KV_HEADS * A_HEAD_DIM
_O_QL = _O_VA + A_KV_HEADS * A_HEAD_DIM
_O_CKV = _O_QL + Q_LORA
_O_KR = _O_CKV + KV_LORA
_O_GATE = _O_KR + B_ROPE


def _params(*sem):
    return pltpu.CompilerParams(dimension_semantics=sem, vmem_limit_bytes=VMEM_LIMIT)


def _dot(a, b):
    return jnp.dot(a, b, preferred_element_type=F32)


def _dot_nt(a, b):
    return lax.dot_general(a, b, (((1,), (1,)), ((), ())), preferred_element_type=F32)


def _rms(x, g):
    return x * lax.rsqrt(jnp.mean(x * x, axis=-1, keepdims=True) + EPS) * g


def _sigmoid(x):
    return 1.0 / (1.0 + jnp.exp(-x))


SUB = 8
assert SUB * LANES == D_MODEL


def _load_rows(ref, n):
    return jnp.concatenate([ref[pl.ds(j, n, stride=SUB), :] for j in range(SUB)], axis=1)


def _store_rows(ref, val, n):
    for j in range(SUB):
        ref[pl.ds(j, n, stride=SUB), :] = val[:, j * LANES:(j + 1) * LANES]


def _silu(x):
    return x * _sigmoid(x)


def _mod_kernel(cond_ref, w_ref, b_ref, o_ref):
    s = _silu(cond_ref[...]).astype(BF16)
    o_ref[...] = _dot(s, w_ref[...].astype(BF16)) + b_ref[...]


def _modulation(cond8, w_ada, b_ada):
    tn = 768
    n = w_ada.shape[1]
    return pl.pallas_call(
        _mod_kernel,
        grid=(n // tn,),
        in_specs=[pl.BlockSpec((8, D_MODEL), lambda j: (0, 0)),
                  pl.BlockSpec((D_MODEL, tn), lambda j: (0, j)),
                  pl.BlockSpec((1, tn), lambda j: (0, j))],
        out_specs=pl.BlockSpec((8, tn), lambda j: (0, j)),
        out_shape=jax.ShapeDtypeStruct((8, n), F32),
        compiler_params=_params("arbitrary"),
        name="modulation",
    )(cond8, w_ada, b_ada.reshape(1, n))


def _rope(x, cos, sin, half):
    lane = lax.broadcasted_iota(jnp.int32, x.shape, 1)
    first = (lane & (2 * half - 1)) < half
    partner = jnp.where(first, pltpu.roll(x, LANES - half, 1), pltpu.roll(x, half, 1))
    return x * cos + partner * sin


def _rope_tables():
    t = np.arange(DEC_SEQ)
    row = (t // GRID_W).astype(np.float32)
    col = (t % GRID_W).astype(np.float32)

    def tables(d_rot):
        nf = d_rot // 4
        inv = np.float32(ROPE_BASE) ** (-np.arange(nf, dtype=np.float32) / np.float32(nf))
        ar = row[:, None] * inv
        ac = col[:, None] * inv
        cos = np.concatenate([np.cos(ar), np.cos(ar), np.cos(ac), np.cos(ac)], axis=1)
        sin = np.concatenate([-np.sin(ar), np.sin(ar), -np.sin(ac), np.sin(ac)], axis=1)
        return cos.astype(np.float32), sin.astype(np.float32)

    cos_a, sin_a = tables(A_HEAD_DIM)
    cos_a = np.concatenate([cos_a, cos_a], axis=1)
    sin_a = np.concatenate([sin_a, sin_a], axis=1)
    cos_b, sin_b = tables(B_ROPE)
    one = np.ones((DEC_SEQ, B_NOPE), np.float32)
    zero = np.zeros((DEC_SEQ, B_NOPE), np.float32)
    pad = LANES - B_NOPE - B_ROPE
    cos_b = np.concatenate([one, cos_b, one[:, :pad]], axis=1)
    sin_b = np.concatenate([zero, sin_b, zero[:, :pad]], axis=1)
    return tuple(jnp.asarray(a) for a in (cos_a, sin_a, cos_b, sin_b))


_N1 = 1536


def _inproj_kernel(latent, *refs):
    if latent:
        (x_ref, mod_ref, g_ref, w1_ref, qn_ref, kvn_ref, wuq_ref, wk_ref, wv_ref,
         ca_ref, sa_ref, cb_ref, sb_ref,
         qa_o, ka_o, va_o, qb_o, kb_o, vb_o) = refs
    else:
        (x_ref, mod_ref, g_ref, w1_ref, qn_ref, kvn_ref, wuq_ref, wk_ref, wv_ref,
         qa_o, ka_o, va_o, ckv_o, kr_o, qb_o, kb_o, vb_o) = refs
    x = x_ref[...]
    shift = mod_ref[:, 0:D_MODEL]
    scale = mod_ref[:, D_MODEL:2 * D_MODEL]
    h = _rms(x, g_ref[...]) * (1.0 + scale) + shift
    p = _dot(h.astype(BF16), w1_ref[...])
    qa = p[:, 0:512]
    ka = p[:, 512:640]
    va = p[:, 640:768]
    ql = p[:, 768:1152]
    ckv = p[:, 1152:1408]
    krp = p[:, 1408:1536]
    qn = _rms(ql, qn_ref[...])
    qb = _dot(qn.astype(BF16), wuq_ref[...])
    cn = _rms(ckv, kvn_ref[...])
    cnb = cn.astype(BF16)
    kn = _dot(cnb, wk_ref[...])
    vb = _dot(cnb, wv_ref[...])
    if latent:
        ca, sa, cb, sb = ca_ref[...], sa_ref[...], cb_ref[...], sb_ref[...]
        ka = _rope(ka, ca, sa, A_HEAD_DIM // 4)
        krp = _rope(krp, cb, sb, B_ROPE // 4)
        for j in range(4):
            blk = _rope(qa[:, j * LANES:(j + 1) * LANES], ca, sa, A_HEAD_DIM // 4)
            qa_o[:, j * LANES:(j + 1) * LANES] = (blk * (A_HEAD_DIM ** -0.5 * LOG2E)).astype(BF16)
        for hd in range(B_HEADS):
            blk = _rope(qb[:, hd * LANES:(hd + 1) * LANES], cb, sb, B_ROPE // 4)
            qb_o[:, hd * LANES:(hd + 1) * LANES] = (blk * (MLA_SCALE * LOG2E)).astype(BF16)
        ka_o[...] = ka.astype(BF16)
        va_o[...] = va.astype(BF16)
    else:
        qa_o[...] = (qa * (A_HEAD_DIM ** -0.5 * LOG2E)).astype(BF16)
        qb_o[...] = (qb * (MLA_SCALE * LOG2E)).astype(BF16)
        for bb in range(TM // SEQ):
            tok = slice(bb * SEQ, (bb + 1) * SEQ)
            ka_o[bb] = ka[tok, :].T
            va_o[bb] = va[tok, :].T
            kr_o[bb] = krp[tok, :].T[B_NOPE:B_NOPE + B_ROPE, :]
        ckv_o[...] = cn
    for hd in range(B_HEADS):
        kb_o[:, hd * LANES:(hd + 1) * LANES] = (kn[:, hd * LANES:(hd + 1) * LANES] + krp).astype(BF16)
    vb_o[...] = vb.astype(BF16)


def _inproj(x, mod3, mod_row, g, w1, qn, kvn, wuq, wk, wv, tables):
    latent = tables is not None
    t = x.shape[0]
    n_tiles = t // TM
    row = lambda i: (i, 0)
    const = lambda i: (0, 0)
    in_specs = [pl.BlockSpec((TM, D_MODEL), row),
                pl.BlockSpec((None, 1, 6 * D_MODEL), lambda i: (mod_row(i), 0, 0)),
                pl.BlockSpec((1, D_MODEL), const),
                pl.BlockSpec((D_MODEL, _N1), const),
                pl.BlockSpec((1, Q_LORA), const),
                pl.BlockSpec((1, KV_LORA), const),
                pl.BlockSpec((Q_LORA, B_HEADS * LANES), const),
                pl.BlockSpec((KV_LORA, B_HEADS * LANES), const),
                pl.BlockSpec((KV_LORA, B_HEADS * B_V), const)]
    args = [x, mod3, g, w1, qn, kvn, wuq, wk, wv]
    sds = jax.ShapeDtypeStruct
    if latent:
        per = DEC_SEQ // TM
        tab = lambda i: (i % per, 0)
        in_specs += [pl.BlockSpec((TM, LANES), tab)] * 4
        args += list(tables)
        out_shape = [sds((t, 512), BF16), sds((t, LANES), BF16), sds((t, LANES), BF16),
                     sds((t, B_HEADS * LANES), BF16), sds((t, B_HEADS * LANES), BF16),
                     sds((t, B_HEADS * B_V), BF16)]
        widths = [512, LANES, LANES, B_HEADS * LANES, B_HEADS * LANES, B_HEADS * B_V]
    else:
        nb = t // SEQ
        out_shape = [sds((t, 512), BF16), sds((nb, LANES, SEQ), F32), sds((nb, LANES, SEQ), F32),
                     sds((t, KV_LORA), F32), sds((nb, B_ROPE, SEQ), F32),
                     sds((t, B_HEADS * LANES), BF16), sds((t, B_HEADS * LANES), BF16),
                     sds((t, B_HEADS * B_V), BF16)]
        widths = [512, None, None, KV_LORA, None, B_HEADS * LANES, B_HEADS * LANES, B_HEADS * B_V]
    out_specs = [pl.BlockSpec((TM, w), row) if w is not None
                 else pl.BlockSpec((TM // SEQ,) + o.shape[1:], lambda i: (i, 0, 0))
                 for w, o in zip(widths, out_shape)]
    return pl.pallas_call(
        functools.partial(_inproj_kernel, latent),
        grid=(n_tiles,),
        in_specs=in_specs,
        out_specs=out_specs,
        out_shape=out_shape,
        compiler_params=_params("arbitrary"),
        name="inproj_latent" if latent else "inproj_ctx",
    )(*args)


def _kvexp_kernel(ckv_ref, krp_ref, wk_ref, wv_ref, kb_o, vb_o):
    cb = ckv_ref[...].astype(BF16)
    kn = _dot(cb, wk_ref[...])
    krp = krp_ref[...]
    for hd in range(B_HEADS):
        kb_o[:, hd * LANES:(hd + 1) * LANES] = (kn[:, hd * LANES:(hd + 1) * LANES] + krp).astype(BF16)
    vb_o[...] = _dot(cb, wv_ref[...]).astype(BF16)


def _kv_expand(ckv, krp, wk, wv):
    t = ckv.shape[0]
    row = lambda i: (i, 0)
    const = lambda i: (0, 0)
    return pl.pallas_call(
        _kvexp_kernel,
        grid=(t // TM,),
        in_specs=[pl.BlockSpec((TM, KV_LORA), row), pl.BlockSpec((TM, LANES), row),
                  pl.BlockSpec((KV_LORA, B_HEADS * LANES), const),
                  pl.BlockSpec((KV_LORA, B_HEADS * B_V), const)],
        out_specs=[pl.BlockSpec((TM, B_HEADS * LANES), row), pl.BlockSpec((TM, B_HEADS * B_V), row)],
        out_shape=[jax.ShapeDtypeStruct((t, B_HEADS * LANES), BF16),
                   jax.ShapeDtypeStruct((t, B_HEADS * B_V), BF16)],
        compiler_params=_params("arbitrary"),
        name="kv_expand",
    )(ckv, krp, wk, wv)


def _lo_mask(rows):
    return lax.broadcasted_iota(jnp.int32, (rows, LANES), 1) < (LANES // 2)


def _softmax_pv(parts, sink, v_t=False):
    if sink is not None:
        sink = sink * LOG2E
    m = None
    for s, _ in parts:
        mi = jnp.max(s, axis=-1, keepdims=True)
        m = mi if m is None else jnp.maximum(m, mi)
    if sink is not None:
        m = jnp.maximum(m, sink)
    den = None
    acc = None
    for s, v in parts:
        p = jnp.exp2(s - m)
        di = jnp.sum(p, axis=-1, keepdims=True)
        den = di if den is None else den + di
        oi = _dot_nt(p.astype(BF16), v) if v_t else _dot(p.astype(BF16), v)
        acc = oi if acc is None else acc + oi
    if sink is not None:
        den = den + jnp.exp2(sink - m)
    return acc / den


def _ctx_attn_kernel(sink_ref, qa_ref, ka_ref, va_ref, qb_ref, kb_ref, vb_ref, oa_ref, ob_ref):
    lo = _lo_mask(SEQ)
    k = ka_ref[...].astype(BF16)
    v = va_ref[...].astype(BF16)
    for j in range(A_GROUP):
        q = qa_ref[:, j * LANES:(j + 1) * LANES]
        zero = jnp.zeros_like(q)
        o0 = _softmax_pv([(_dot(jnp.where(lo, q, zero), k), v)], sink_ref[j], v_t=True)
        o1 = _softmax_pv([(_dot(jnp.where(lo, zero, q), k), v)], sink_ref[A_GROUP + j], v_t=True)
        oa_ref[:, j * LANES:(j + 1) * LANES] = jnp.where(lo, o0, o1).astype(BF16)
    for j in range(B_HEADS // 2):
        vp = vb_ref[:, j * LANES:(j + 1) * LANES]
        outs = []
        for hd in (2 * j, 2 * j + 1):
            s = _dot_nt(qb_ref[:, hd * LANES:(hd + 1) * LANES], kb_ref[:, hd * LANES:(hd + 1) * LANES])
            outs.append(_softmax_pv([(s, vp)], None))
        ob_ref[:, j * LANES:(j + 1) * LANES] = jnp.where(lo, outs[0], outs[1]).astype(BF16)


def _ctx_attention(sink, qa, ka, va, qb, kb, vb):
    blk = lambda w: pl.BlockSpec((SEQ, w), lambda b: (b, 0))
    kv_t = pl.BlockSpec((None, LANES, SEQ), lambda b: (b, 0, 0))
    return pl.pallas_call(
        _ctx_attn_kernel,
        grid=(BATCH,),
        in_specs=[pl.BlockSpec(memory_space=pltpu.SMEM),
                  blk(512), kv_t, kv_t, blk(B_HEADS * LANES), blk(B_HEADS * LANES),
                  blk(B_HEADS * B_V)],
        out_specs=[blk(512), blk(512)],
        out_shape=[jax.ShapeDtypeStruct((T_CTX, 512), BF16), jax.ShapeDtypeStruct((T_CTX, 512), BF16)],
        compiler_params=_params("arbitrary"),
        name="ctx_attention",
    )(sink, qa, ka, va, qb, kb, vb)


_WIN = TQ + 2 * WINDOW


def _lat_swa_kernel(sink_ref, q_ref, k_ref, v_ref, kc_ref, vc_ref, o_ref):
    qi = pl.program_id(1)
    q0 = qi * TQ
    start = pl.multiple_of(jnp.clip(q0 - WINDOW, 0, DEC_SEQ - _WIN), WINDOW)
    kw = k_ref[pl.ds(start, _WIN), :]
    vw = v_ref[pl.ds(start, _WIN), :]
    kc = kc_ref[...].astype(BF16)
    vc = vc_ref[...].astype(BF16)
    qpos = q0 + lax.broadcasted_iota(jnp.int32, (TQ, _WIN), 0)
    kpos = start + lax.broadcasted_iota(jnp.int32, (TQ, _WIN), 1)
    valid = jnp.abs(qpos - kpos) <= WINDOW
    lo = _lo_mask(TQ)
    for j in range(A_GROUP):
        q = q_ref[:, j * LANES:(j + 1) * LANES]
        zero = jnp.zeros_like(q)
        outs = []
        for g, qh in ((0, jnp.where(lo, q, zero)), (1, jnp.where(lo, zero, q))):
            sw = jnp.where(valid, _dot_nt(qh, kw), NEG)
            sc = _dot_nt(qh, kc)
            outs.append(_softmax_pv([(sw, vw), (sc, vc)], sink_ref[g * A_GROUP + j]))
        o_ref[:, j * LANES:(j + 1) * LANES] = jnp.where(lo, outs[0], outs[1]).astype(BF16)


def _lat_swa(sink, qa, ka, va, kc, vc):
    nq = DEC_SEQ // TQ
    return pl.pallas_call(
        _lat_swa_kernel,
        grid=(DEC_BATCH, nq),
        in_specs=[pl.BlockSpec(memory_space=pltpu.SMEM),
                  pl.BlockSpec((TQ, 512), lambda b, i: (b * nq + i, 0)),
                  pl.BlockSpec((None, DEC_SEQ, LANES), lambda b, i: (b, 0, 0)),
                  pl.BlockSpec((None, DEC_SEQ, LANES), lambda b, i: (b, 0, 0)),
                  pl.BlockSpec((None, PAST_LEN, LANES), lambda b, i: (b, 0, 0)),
                  pl.BlockSpec((None, PAST_LEN, LANES), lambda b, i: (b, 0, 0))],
        out_specs=pl.BlockSpec((TQ, 512), lambda b, i: (b * nq + i, 0)),
        out_shape=jax.ShapeDtypeStruct((T_LAT, 512), BF16),
        compiler_params=_params("arbitrary", "arbitrary"),
        name="latent_swa",
    )(sink, qa, ka.reshape(DEC_BATCH, DEC_SEQ, LANES), va.reshape(DEC_BATCH, DEC_SEQ, LANES), kc, vc)


TQ_MLA = 256


def _lat_mla_kernel(q_ref, kl_ref, vl_ref, kc_ref, vc_ref, o_ref):
    lo = _lo_mask(TQ_MLA)
    vl = vl_ref[...]
    vc = vc_ref[...]
    outs = []
    for hh in range(2):
        q = q_ref[:, hh * LANES:(hh + 1) * LANES]
        sc = _dot_nt(q, kc_ref[:, hh * LANES:(hh + 1) * LANES])
        sl = _dot_nt(q, kl_ref[:, hh * LANES:(hh + 1) * LANES])
        outs.append(_softmax_pv([(sc, vc), (sl, vl)], None))
    o_ref[...] = jnp.where(lo, outs[0], outs[1]).astype(BF16)


def _lat_mla(qb, kb, vb, kc, vc):
    nq = DEC_SEQ // TQ_MLA
    npair = B_HEADS // 2
    return pl.pallas_call(
        _lat_mla_kernel,
        grid=(DEC_BATCH, npair, nq),
        in_specs=[pl.BlockSpec((TQ_MLA, 2 * LANES), lambda b, p, i: (b * nq + i, p)),
                  pl.BlockSpec((None, DEC_SEQ, 2 * LANES), lambda b, p, i: (b, 0, p)),
                  pl.BlockSpec((None, DEC_SEQ, LANES), lambda b, p, i: (b, 0, p)),
                  pl.BlockSpec((None, PAST_LEN, 2 * LANES), lambda b, p, i: (b, 0, p)),
                  pl.BlockSpec((None, PAST_LEN, LANES), lambda b, p, i: (b, 0, p))],
        out_specs=pl.BlockSpec((TQ_MLA, LANES), lambda b, p, i: (b * nq + i, p)),
        out_shape=jax.ShapeDtypeStruct((T_LAT, 512), BF16),
        compiler_params=_params("arbitrary", "arbitrary", "arbitrary"),
        name="latent_mla",
    )(qb, kb.reshape(DEC_BATCH, DEC_SEQ, B_HEADS * LANES), vb.reshape(DEC_BATCH, DEC_SEQ, B_HEADS * B_V),
      kc.reshape(DEC_BATCH, PAST_LEN, B_HEADS * LANES), vc.reshape(DEC_BATCH, PAST_LEN, B_HEADS * B_V))


_N_CTX_TILES = T_CTX // TM


def _merge_kernel(xp_ref, xs_ref, oac_ref, oal_ref, obc_ref, obl_ref, mod_ref,
                  gpre_ref, gpost_ref, gffn_ref, woa_ref, wob_ref, wout_ref, wg_ref, wr_ref,
                  x1_o, h2_o, lg_o):
    def body(x_ref, oa_ref, ob_ref):
        x = x_ref[...]
        sh1 = mod_ref[:, 0:D_MODEL]
        sc1 = mod_ref[:, D_MODEL:2 * D_MODEL]
        g1 = mod_ref[:, 2 * D_MODEL:3 * D_MODEL]
        sh2 = mod_ref[:, 3 * D_MODEL:4 * D_MODEL]
        sc2 = mod_ref[:, 4 * D_MODEL:5 * D_MODEL]
        h = (_rms(x, gpre_ref[...]) * (1.0 + sc1) + sh1).astype(BF16)
        gates = _dot(h, wg_ref[...])
        ya = _dot(oa_ref[...], woa_ref[...])
        yb = _dot(ob_ref[...], wob_ref[...])
        y = _sigmoid(gates[:, :D_MODEL]) * ya + _sigmoid(gates[:, D_MODEL:]) * yb
        mix = _dot(y.astype(BF16), wout_ref[...])
        x1 = x + g1 * _rms(mix, gpost_ref[...])
        x1_o[...] = x1
        h2 = _rms(x1, gffn_ref[...]) * (1.0 + sc2) + sh2
        _store_rows(h2_o, h2, TM)
        lg_o[...] = _dot_nt(wr_ref[...], h2.astype(BF16))

    is_ctx = pl.program_id(0) < _N_CTX_TILES

    @pl.when(is_ctx)
    def _():
        body(xp_ref, oac_ref, obc_ref)

    @pl.when(jnp.logical_not(is_ctx))
    def _():
        body(xs_ref, oal_ref, obl_ref)


def _mod_row_all(i):
    per = DEC_SEQ // TM
    return jnp.where(i < _N_CTX_TILES, 0, 1 + (jnp.maximum(i - _N_CTX_TILES, 0)) // per)


def _merge(xp, xs, oac, oal, obc, obl, mod3, gpre, gpost, gffn, woa, wob, wout, wg, wr):
    ctx = lambda i: (jnp.minimum(i, _N_CTX_TILES - 1), 0)
    lat = lambda i: (jnp.maximum(i - _N_CTX_TILES, 0), 0)
    row = lambda i: (i, 0)
    const = lambda i: (0, 0)
    return pl.pallas_call(
        _merge_kernel,
        grid=(T_ALL // TM,),
        in_specs=[pl.BlockSpec((TM, D_MODEL), ctx), pl.BlockSpec((TM, D_MODEL), lat),
                  pl.BlockSpec((TM, 512), ctx), pl.BlockSpec((TM, 512), lat),
                  pl.BlockSpec((TM, 512), ctx), pl.BlockSpec((TM, 512), lat),
                  pl.BlockSpec((None, 1, 6 * D_MODEL), lambda i: (_mod_row_all(i), 0, 0)),
                  pl.BlockSpec((1, D_MODEL), const), pl.BlockSpec((1, D_MODEL), const),
                  pl.BlockSpec((1, D_MODEL), const),
                  pl.BlockSpec((512, D_MODEL), const), pl.BlockSpec((512, D_MODEL), const),
                  pl.BlockSpec((D_MODEL, D_MODEL), const), pl.BlockSpec((D_MODEL, 2 * D_MODEL), const),
                  pl.BlockSpec((N_EXPERTS, D_MODEL), const)],
        out_specs=[pl.BlockSpec((TM, D_MODEL), row), pl.BlockSpec((TM * SUB, LANES), row),
                   pl.BlockSpec((N_EXPERTS, TM), lambda i: (0, i))],
        out_shape=[jax.ShapeDtypeStruct((T_ALL, D_MODEL), F32), jax.ShapeDtypeStruct((T_ALL * SUB, LANES), F32),
                   jax.ShapeDtypeStruct((N_EXPERTS, T_ALL), F32)],
        compiler_params=_params("arbitrary"),
        name="merge",
    )(xp, xs, oac, oal, obc, obl, mod3, gpre, gpost, gffn, woa, wob, wout, wg, wr)


_PER_GROUP = N_EXPERTS // N_GROUPS


def _router_kernel(lg_ref, bias_ref, topi_o, rank_o, topw_o, cnt_o, cnt_s):
    @pl.when(pl.program_id(0) == 0)
    def _():
        cnt_s[...] = jnp.zeros_like(cnt_s)

    scores = _sigmoid(lg_ref[...])
    sel = scores + bias_ref[...]
    eidx = lax.broadcasted_iota(jnp.int32, (N_EXPERTS, TM), 0).astype(F32)
    ninf = -jnp.inf
    big = float(N_EXPERTS)
    sel_g = [sel[g * _PER_GROUP:(g + 1) * _PER_GROUP, :] for g in range(N_GROUPS)]
    gs = []
    for sg in sel_g:
        m1 = jnp.max(sg, axis=0, keepdims=True)
        top = sg == m1
        n_top = jnp.sum(jnp.where(top, 1.0, 0.0), axis=0, keepdims=True)
        rest = jnp.max(jnp.where(top, ninf, sg), axis=0, keepdims=True)
        gs.append(m1 + jnp.where(n_top > 1.5, m1, rest))
    cur_g = []
    for g in range(N_GROUPS):
        beat = jnp.zeros((1, TM), F32)
        for g2 in range(N_GROUPS):
            if g2 == g:
                continue
            better = (gs[g2] >= gs[g]) if g2 < g else (gs[g2] > gs[g])
            beat = beat + jnp.where(better, 1.0, 0.0)
        cur_g.append(jnp.where(beat < TOPK_GROUPS, sel_g[g], NEG))
    cur = jnp.concatenate(cur_g, axis=0)
    chosen = jnp.zeros((N_EXPERTS, TM), F32)
    idxs, ws = [], []
    for k in range(TOP_K):
        m = jnp.max(cur, axis=0, keepdims=True)
        idx = jnp.min(jnp.where(cur == m, eidx, big), axis=0, keepdims=True)
        hit = eidx == idx
        ws.append(jnp.sum(jnp.where(hit, scores, 0.0), axis=0, keepdims=True))
        cur = jnp.where(hit, ninf, cur)
        chosen = jnp.where(hit, 1.0, chosen)
        idxs.append(idx)
    wsum = ws[0]
    for k in range(1, TOP_K):
        wsum = wsum + ws[k]
    for k in range(TOP_K):
        topw_o[k:k + 1, :] = ws[k] / wsum * ROUTED_SCALE
        topi_o[k:k + 1, :] = idxs[k].astype(jnp.int32)
    r_i = lax.broadcasted_iota(jnp.int32, (TM, TM), 0)
    c_i = lax.broadcasted_iota(jnp.int32, (TM, TM), 1)
    earlier = jnp.where(r_i < c_i, 1.0, 0.0).astype(BF16)
    prefix = _dot(chosen.astype(BF16), earlier) + cnt_s[:, 0:1]
    for k in range(TOP_K):
        r = jnp.sum(jnp.where(eidx == idxs[k], prefix, 0.0), axis=0, keepdims=True)
        rank_o[k:k + 1, :] = r.astype(jnp.int32)
    total = cnt_s[...] + jnp.sum(chosen, axis=1, keepdims=True)
    cnt_s[...] = total
    cnt_o[...] = total


def _router(logits_t, bias_col):
    col = lambda i: (0, i)
    const = lambda i: (0, 0)
    return pl.pallas_call(
        _router_kernel,
        grid=(T_ALL // TM,),
        in_specs=[pl.BlockSpec((N_EXPERTS, TM), col), pl.BlockSpec((N_EXPERTS, 1), const)],
        out_specs=[pl.BlockSpec((TOP_K, TM), col), pl.BlockSpec((TOP_K, TM), col),
                   pl.BlockSpec((TOP_K, TM), col), pl.BlockSpec((N_EXPERTS, LANES), const)],
        out_shape=[jax.ShapeDtypeStruct((TOP_K, T_ALL), jnp.int32),
                   jax.ShapeDtypeStruct((TOP_K, T_ALL), jnp.int32),
                   jax.ShapeDtypeStruct((TOP_K, T_ALL), F32),
                   jax.ShapeDtypeStruct((N_EXPERTS, LANES), F32)],
        scratch_shapes=[pltpu.VMEM((N_EXPERTS, LANES), F32)],
        compiler_params=_params("arbitrary"),
        name="router",
    )(logits_t, bias_col)


def _dest_kernel(topi_ref, rank_ref, ps_ref, o_ref):
    eidx = lax.broadcasted_iota(jnp.int32, (N_EXPERTS, TM), 0)
    ps = jnp.broadcast_to(ps_ref[...], (N_EXPERTS, TM))
    for k in range(TOP_K):
        start = jnp.sum(jnp.where(eidx == topi_ref[k:k + 1, :], ps, 0.0), axis=0, keepdims=True)
        o_ref[k:k + 1, :] = rank_ref[k:k + 1, :] + start.astype(jnp.int32)


def _dest(topi, rank, pstart_col):
    col = lambda i: (0, i)
    return pl.pallas_call(
        _dest_kernel,
        grid=(T_ALL // TM,),
        in_specs=[pl.BlockSpec((TOP_K, TM), col), pl.BlockSpec((TOP_K, TM), col),
                  pl.BlockSpec((N_EXPERTS, 1), lambda i: (0, 0))],
        out_specs=pl.BlockSpec((TOP_K, TM), col),
        out_shape=jax.ShapeDtypeStruct((TOP_K, T_ALL), jnp.int32),
        compiler_params=_params("arbitrary"),
        name="dest",
    )(topi, rank, pstart_col)


TD = 512
DISPATCH_PARTS = 4
ROW_UNROLL = 4


def _dispatch_kernel(dest_ref, h_ref, wgu_ref, wd_ref, xs_hbm, sh_o, sem):
    def row(t, carry):
        src = h_ref.at[pl.ds(pl.multiple_of(t * SUB, SUB), SUB)]
        for k in range(TOP_K):
            d = dest_ref[t * TOP_K + k]
            pltpu.make_async_copy(src, xs_hbm.at[pl.ds(pl.multiple_of(d * SUB, SUB), SUB)],
                                  sem).start(priority=k % 2)
        return carry

    part = TD // DISPATCH_PARTS
    for p in range(DISPATCH_PARTS):
        lax.fori_loop(p * part, (p + 1) * part, row, 0, unroll=ROW_UNROLL)
        x = _load_rows(h_ref.at[pl.ds(p * part * SUB, part * SUB)], part).astype(BF16)
        gu = _dot(x, wgu_ref[...])
        a = _silu(gu[:, :D_SHARED]) * gu[:, D_SHARED:]
        sh_o[p * part:(p + 1) * part, :] = _dot(a.astype(BF16), wd_ref[...])
    for _ in range(TOP_K):
        pltpu.make_async_copy(h_ref, xs_hbm.at[pl.ds(0, TD * SUB)], sem).wait()


def _dispatch(dest_flat, h2, wgu, wd):
    const = lambda i: (0, 0)
    return pl.pallas_call(
        _dispatch_kernel,
        grid=(T_ALL // TD,),
        in_specs=[pl.BlockSpec((TD * TOP_K,), lambda i: (i,), memory_space=pltpu.SMEM),
                  pl.BlockSpec((TD * SUB, LANES), lambda i: (i, 0)),
                  pl.BlockSpec((D_MODEL, 2 * D_SHARED), const), pl.BlockSpec((D_SHARED, D_MODEL), const)],
        out_specs=[pl.BlockSpec(memory_space=pl.ANY), pl.BlockSpec((TD, D_MODEL), lambda i: (i, 0))],
        out_shape=[jax.ShapeDtypeStruct((N_SLOTS * SUB, LANES), F32),
                   jax.ShapeDtypeStruct((T_ALL, D_MODEL), F32)],
        scratch_shapes=[pltpu.SemaphoreType.DMA(())],
        compiler_params=_params("arbitrary"),
        name="dispatch",
    )(dest_flat, h2, wgu, wd)


_BLK_TILES = MOE_BLK * SUB
NBX = 6
PREFETCH = NBX - 2
NBO = 4


def _expert_kernel(bs_ref, x_hbm, wg_ref, wu_ref, wd_ref, o_hbm, wgu_s, wd_s, xbuf, obuf, xsem, osem):
    e = pl.program_id(0)
    b0 = bs_ref[e]
    b1 = bs_ref[e + 1]
    n_used = bs_ref[N_EXPERTS]

    def xcopy(g):
        s = lax.rem(g, NBX)
        return pltpu.make_async_copy(x_hbm.at[pl.ds(pl.multiple_of(g * _BLK_TILES, _BLK_TILES), _BLK_TILES)],
                                     xbuf.at[s], xsem.at[s])

    def ocopy(g):
        s = lax.rem(g, NBO)
        return pltpu.make_async_copy(obuf.at[s],
                                     o_hbm.at[pl.ds(pl.multiple_of(g * _BLK_TILES, _BLK_TILES), _BLK_TILES)],
                                     osem.at[s])

    @pl.when(e == 0)
    def _():
        for j in range(PREFETCH):
            @pl.when(j < n_used)
            def _():
                xcopy(j).start()

    @pl.when(b1 > b0)
    def _():
        wgu_s[:, 0:D_EXPERT] = wg_ref[...].astype(BF16)
        wgu_s[:, D_EXPERT:2 * D_EXPERT] = wu_ref[...].astype(BF16)
        wd_s[...] = wd_ref[...].astype(BF16)

    def acquire(g):
        @pl.when(g + PREFETCH < n_used)
        def _():
            xcopy(g + PREFETCH).start()

        xcopy(g).wait()

        @pl.when(g >= NBO)
        def _():
            ocopy(g - NBO).wait()

    def compute(g):
        gu = _dot(_load_rows(xbuf.at[lax.rem(g, NBX)], MOE_BLK).astype(BF16), wgu_s[...])
        a = _silu(gu[:, :D_EXPERT]) * gu[:, D_EXPERT:]
        _store_rows(obuf.at[lax.rem(g, NBO)], _dot(a.astype(BF16), wd_s[...]), MOE_BLK)

    def pair(i, carry):
        g = b0 + 2 * i
        acquire(g)
        acquire(g + 1)
        compute(g)
        compute(g + 1)
        ocopy(g).start(priority=1)
        ocopy(g + 1).start(priority=1)
        return carry

    lax.fori_loop(0, lax.shift_right_logical(b1 - b0, 1), pair, 0)

    @pl.when(((b1 - b0) & 1) == 1)
    def _():
        acquire(b1 - 1)
        compute(b1 - 1)
        ocopy(b1 - 1).start(priority=1)

    @pl.when(e == N_EXPERTS - 1)
    def _():
        for j in range(NBO, 0, -1):
            @pl.when(n_used >= j)
            def _():
                ocopy(n_used - j).wait()


def _experts(blk_start, x_sorted, w_gate, w_up, w_down):
    wsel = lambda e, bs: (e, 0, 0)
    grid_spec = pltpu.PrefetchScalarGridSpec(
        num_scalar_prefetch=1,
        grid=(N_EXPERTS,),
        in_specs=[pl.BlockSpec(memory_space=pl.ANY),
                  pl.BlockSpec((None, D_MODEL, D_EXPERT), wsel),
                  pl.BlockSpec((None, D_MODEL, D_EXPERT), wsel),
                  pl.BlockSpec((None, D_EXPERT, D_MODEL), wsel)],
        out_specs=pl.BlockSpec(memory_space=pl.ANY),
        scratch_shapes=[pltpu.VMEM((D_MODEL, 2 * D_EXPERT), BF16), pltpu.VMEM((D_EXPERT, D_MODEL), BF16),
                        pltpu.VMEM((NBX, _BLK_TILES, LANES), F32), pltpu.VMEM((NBO, _BLK_TILES, LANES), F32),
                        pltpu.SemaphoreType.DMA((NBX,)), pltpu.SemaphoreType.DMA((NBO,))],
    )
    return pl.pallas_call(
        _expert_kernel,
        grid_spec=grid_spec,
        out_shape=jax.ShapeDtypeStruct((N_SLOTS * SUB, LANES), F32),
        compiler_params=_params("arbitrary"),
        name="experts",
    )(blk_start, x_sorted, w_gate, w_up, w_down)


TC = 256
CR = 16


def _combine_kernel(dcur_ref, dnext_ref, eo_hbm, x1_ref, sh_ref, w_ref, mod_ref, g_ref, o_ref, buf, sem, wb_s):
    i = pl.program_id(0)
    n = pl.num_programs(0)
    slot = lax.rem(i, 2)

    def gather(d_ref, s, t0, t1):
        def row(t, carry):
            for k in range(TOP_K):
                d = d_ref[t * TOP_K + k]
                pltpu.make_async_copy(eo_hbm.at[pl.ds(pl.multiple_of(d * SUB, SUB), SUB)],
                                      buf.at[s, k, pl.ds(pl.multiple_of(t * SUB, SUB), SUB)],
                                      sem.at[s]).start(priority=k % 2)
            return carry
        lax.fori_loop(t0, t1, row, 0, unroll=ROW_UNROLL)

    @pl.when(i == 0)
    def _():
        gather(dcur_ref, 0, 0, TC)

    @pl.when(i + 1 < n)
    def _():
        gather(dnext_ref, 1 - slot, 0, TC)

    gate = mod_ref[:, 5 * D_MODEL:6 * D_MODEL]
    for k in range(TOP_K):
        wb_s[k] = jnp.broadcast_to(w_ref[:, k:k + 1], (TC, LANES))
    for k in range(TOP_K):
        pltpu.make_async_copy(eo_hbm.at[pl.ds(0, TC * SUB)], buf.at[slot, k], sem.at[slot]).wait()

    lanes = [slice(j * LANES, (j + 1) * LANES) for j in range(SUB)]

    def chunk(r, carry):
        r0 = pl.multiple_of(r * CR, CR)
        rows = pl.ds(r0, CR)
        wk = [wb_s[k, rows, :] for k in range(TOP_K)]
        pieces = []
        sq = jnp.zeros((CR, LANES), F32)
        for j in range(SUB):
            acc = sh_ref[rows, lanes[j]]
            for k in range(TOP_K):
                acc = acc + buf[slot, k, pl.ds(r0 * SUB + j, CR, stride=SUB), :] * wk[k]
            pieces.append(acc)
            sq = sq + acc * acc
        inv = lax.rsqrt(jnp.sum(sq, axis=-1, keepdims=True) / D_MODEL + EPS)
        for j in range(SUB):
            o_ref[rows, lanes[j]] = (x1_ref[rows, lanes[j]]
                                     + gate[:, lanes[j]] * (pieces[j] * inv * g_ref[:, lanes[j]]))
        return carry

    lax.fori_loop(0, TC // CR, chunk, 0, unroll=8)


def _combine(dest_flat, eo, x1, shared, topw, mod3, mod_row, g, tile0, n_tiles):
    row = lambda i: (tile0 + i, 0)
    const = lambda i: (0, 0)
    blk = TC * TOP_K
    return pl.pallas_call(
        _combine_kernel,
        grid=(n_tiles,),
        in_specs=[pl.BlockSpec((blk,), lambda i: (tile0 + i,), memory_space=pltpu.SMEM),
                  pl.BlockSpec((blk,), lambda i: (tile0 + jnp.minimum(i + 1, n_tiles - 1),),
                               memory_space=pltpu.SMEM),
                  pl.BlockSpec(memory_space=pl.ANY),
                  pl.BlockSpec((TC, D_MODEL), row), pl.BlockSpec((TC, D_MODEL), row),
                  pl.BlockSpec((TC, LANES), row),
                  pl.BlockSpec((None, 1, 6 * D_MODEL), lambda i: (mod_row(i), 0, 0)),
                  pl.BlockSpec((1, D_MODEL), const)],
        out_specs=pl.BlockSpec((TC, D_MODEL), lambda i: (i, 0)),
        out_shape=jax.ShapeDtypeStruct((n_tiles * TC, D_MODEL), F32),
        scratch_shapes=[pltpu.VMEM((2, TOP_K, TC * SUB, LANES), F32), pltpu.SemaphoreType.DMA((2,)),
                        pltpu.VMEM((TOP_K, TC, LANES), F32)],
        compiler_params=_params("arbitrary"),
        name="combine",
    )(dest_flat, dest_flat, eo, x1, shared, topw, mod3, g)


def _qa_perm():
    cols = []
    for j in range(A_GROUP):
        for g in range(A_KV_HEADS):
            hd = g * A_GROUP + j
            cols.extend(range(hd * A_HEAD_DIM, (hd + 1) * A_HEAD_DIM))
    return np.asarray(cols, np.int32)


def _pad_heads(w, width, n_heads, offset, total=LANES):
    k = w.shape[0]
    stride = w.shape[1] // n_heads
    w3 = w.reshape(k, n_heads, stride)[:, :, offset:offset + width]
    w3 = jnp.pad(w3, ((0, 0), (0, 0), (0, total - width)))
    return w3.reshape(k, n_heads * total)


def kernel(x_prompt, x_sample, cache_k_swa, cache_v_swa, cache_ckv_mla, cache_krope_mla, c, c_ctx, w_ada, b_ada, norm_pre_mix, norm_post_mix, norm_pre_ffn, norm_post_ffn, w_in, sink_swa, q_norm_mla, kv_norm_mla, w_uq_mla, w_ukv_mla, w_o_swa, w_o_mla, w_out, router_w, router_bias, w_gate_exp, w_up_exp, w_down_exp, w_gate_sh, w_up_sh, w_down_sh):
    l = 0
    perm = _qa_perm()
    wi = w_in[l]
    kr_blk = jnp.pad(wi[:, _O_KR:_O_GATE], ((0, 0), (B_NOPE, LANES - B_NOPE - B_ROPE)))
    w1 = jnp.concatenate([wi[:, _O_QA:_O_KA][:, perm], wi[:, _O_KA:_O_KR], kr_blk], axis=1).astype(BF16)
    wg = wi[:, _O_GATE:].astype(BF16)
    wuq = _pad_heads(w_uq_mla[l], B_NOPE + B_ROPE, B_HEADS, 0).astype(BF16)
    wk = _pad_heads(w_ukv_mla[l], B_NOPE, B_HEADS, 0).astype(BF16)
    wv = w_ukv_mla[l].reshape(KV_LORA, B_HEADS, B_NOPE + B_V)[:, :, B_NOPE:].reshape(KV_LORA, B_HEADS * B_V).astype(BF16)
    woa = w_o_swa[l][perm, :].astype(BF16)
    wob = w_o_mla[l].astype(BF16)
    wout = w_out[l].astype(BF16)
    wr = router_w[l].T.astype(BF16)
    wgu_sh = jnp.concatenate([w_gate_sh[l], w_up_sh[l]], axis=1).astype(BF16)
    wd_sh = w_down_sh[l].astype(BF16)
    sink = sink_swa[l]
    row2 = lambda v: v.reshape(1, -1)

    cond8 = jnp.concatenate([c_ctx[None, :], c, jnp.zeros((8 - 1 - DEC_BATCH, D_MODEL), F32)], axis=0)
    mod3 = _modulation(cond8, w_ada[l], b_ada[l]).reshape(8, 1, 6 * D_MODEL)

    xp = x_prompt.reshape(T_CTX, D_MODEL)
    xs = x_sample.reshape(T_LAT, D_MODEL)
    per = DEC_SEQ // TM
    shared_in = (row2(norm_pre_mix[l]), w1, row2(q_norm_mla[l]), row2(kv_norm_mla[l]), wuq, wk, wv)
    qa_c, ka_c, va_c, ckv_c, kr_c, qb_c, kb_c, vb_c = _inproj(xp, mod3, lambda i: 0, *shared_in, None)
    qa_l, ka_l, va_l, qb_l, kb_l, vb_l = _inproj(xs, mod3, lambda i: 1 + i // per, *shared_in, _rope_tables())

    oa_c, ob_c = _ctx_attention(sink, qa_c, ka_c, va_c, qb_c, kb_c, vb_c)

    krp_cache = jnp.pad(cache_krope_mla[:, l].reshape(DEC_BATCH * PAST_LEN, B_ROPE),
                        ((0, 0), (B_NOPE, LANES - B_NOPE - B_ROPE)))
    kc_b, vc_b = _kv_expand(cache_ckv_mla[:, l].reshape(DEC_BATCH * PAST_LEN, KV_LORA), krp_cache, wk, wv)
    oa_l = _lat_swa(sink, qa_l, ka_l, va_l,
                    cache_k_swa[:, l].reshape(DEC_BATCH, PAST_LEN, LANES),
                    cache_v_swa[:, l].reshape(DEC_BATCH, PAST_LEN, LANES))
    ob_l = _lat_mla(qb_l, kb_l, vb_l, kc_b, vc_b)

    x1, h2, logits = _merge(xp, xs, oa_c, oa_l, ob_c, ob_l, mod3,
                            row2(norm_pre_mix[l]), row2(norm_post_mix[l]), row2(norm_pre_ffn[l]),
                            woa, wob, wout, wg, wr)

    topi8, rank8, topw8, counts_b = _router(logits, router_bias[l].reshape(N_EXPERTS, 1))
    counts = counts_b[:, 0].astype(jnp.int32)

    padded = (counts + MOE_BLK - 1) // MOE_BLK * MOE_BLK
    pend = jnp.cumsum(padded)
    pstart = pend - padded
    dest = _dest(topi8, rank8, pstart.astype(F32).reshape(N_EXPERTS, 1)).T.reshape(-1)
    topw_p = jnp.pad(topw8.T, ((0, 0), (0, LANES - TOP_K)))
    blk_start = (jnp.concatenate([pstart, pend[-1:]]) // MOE_BLK).astype(jnp.int32)

    x_sorted, shared = _dispatch(dest, h2, wgu_sh, wd_sh)
    eo = _experts(blk_start, x_sorted, w_gate_exp[l], w_up_exp[l], w_down_exp[l])

    g_post = row2(norm_post_ffn[l])
    per_c = DEC_SEQ // TC
    y_p = _combine(dest, eo, x1, shared, topw_p, mod3, lambda i: 0, g_post, 0, T_CTX // TC)
    y_s = _combine(dest, eo, x1, shared, topw_p, mod3, lambda i: 1 + i // per_c, g_post,
                   T_CTX // TC, T_LAT // TC)

    return (y_p.reshape(BATCH, SEQ, D_MODEL), y_s.reshape(DEC_BATCH, DEC_SEQ, D_MODEL),
            ka_c.reshape(BATCH, 1, A_KV_HEADS, A_HEAD_DIM, SEQ).transpose(0, 1, 4, 2, 3),
            va_c.reshape(BATCH, 1, A_KV_HEADS, A_HEAD_DIM, SEQ).transpose(0, 1, 4, 2, 3),
            ckv_c.reshape(BATCH, 1, SEQ, KV_LORA),
            kr_c.reshape(BATCH, 1, B_ROPE, SEQ).transpose(0, 1, 3, 2))
```

```python
import functools

import jax
import jax.numpy as jnp
import numpy as np
from jax import lax
from jax.experimental import pallas as pl
from jax.experimental.pallas import tpu as pltpu

D_MODEL = 1024
BATCH = 32
SEQ = 256
DEC_BATCH = 4
DEC_SEQ = 2048
PAST_LEN = 512
GRID_W = 64
WINDOW = 128
ROPE_BASE = 10000.0
EPS = 1e-6
NEG = -1e30
A_HEADS = 8
A_KV_HEADS = 2
A_GROUP = A_HEADS // A_KV_HEADS
A_HEAD_DIM = 64
B_HEADS = 8
B_NOPE = 64
B_ROPE = 32
B_V = 64
KV_LORA = 256
Q_LORA = 384
MLA_SCALE = (B_NOPE + B_ROPE) ** -0.5
LOG2E = 1.4426950408889634
N_EXPERTS = 256
TOP_K = 8
N_GROUPS = 8
TOPK_GROUPS = 4
D_EXPERT = 256
D_SHARED = 256
ROUTED_SCALE = 2.5

T_CTX = BATCH * SEQ
T_LAT = DEC_BATCH * DEC_SEQ
T_ALL = T_CTX + T_LAT

LANES = 128
TM = 512
TQ = 256
MOE_BLK = 256
N_SLOT_BLK = (T_ALL * TOP_K) // MOE_BLK + N_EXPERTS
N_SLOTS = N_SLOT_BLK * MOE_BLK
VMEM_LIMIT = 48 * 1024 * 1024

F32 = jnp.float32
BF16 = jnp.bfloat16

_O_QA = 0
_O_KA = _O_QA + A_HEADS * A_HEAD_DIM
_O_VA = _O_KA + A_KV_HEADS * A_HEAD_DIM
_O_QL = _O_VA + A_KV_HEADS * A_HEAD_DIM
_O_CKV = _O_QL + Q_LORA
_O_KR = _O_CKV + KV_LORA
_O_GATE = _O_KR + B_ROPE


def _params(*sem):
    return pltpu.CompilerParams(dimension_semantics=sem, vmem_limit_bytes=VMEM_LIMIT)


def _dot(a, b):
    return jnp.dot(a, b, preferred_element_type=F32)


def _dot_nt(a, b):
    return lax.dot_general(a, b, (((1,), (1,)), ((), ())), preferred_element_type=F32)


def _rms(x, g):
    return x * lax.rsqrt(jnp.mean(x * x, axis=-1, keepdims=True) + EPS) * g


def _sigmoid(x):
    return 1.0 / (1.0 + jnp.exp(-x))


SUB = 8
assert SUB * LANES == D_MODEL


def _load_rows(ref, n):
    return jnp.concatenate([ref[pl.ds(j, n, stride=SUB), :] for j in range(SUB)], axis=1)


def _store_rows(ref, val, n):
    for j in range(SUB):
        ref[pl.ds(j, n, stride=SUB), :] = val[:, j * LANES:(j + 1) * LANES]


def _silu(x):
    return x * _sigmoid(x)


def _mod_kernel(cond_ref, w_ref, b_ref, o_ref):
    s = _silu(cond_ref[...]).astype(BF16)
    o_ref[...] = _dot(s, w_ref[...].astype(BF16)) + b_ref[...]


def _modulation(cond8, w_ada, b_ada):
    tn = 768
    n = w_ada.shape[1]
    return pl.pallas_call(
        _mod_kernel,
        grid=(n // tn,),
        in_specs=[pl.BlockSpec((8, D_MODEL), lambda j: (0, 0)),
                  pl.BlockSpec((D_MODEL, tn), lambda j: (0, j)),
                  pl.BlockSpec((1, tn), lambda j: (0, j))],
        out_specs=pl.BlockSpec((8, tn), lambda j: (0, j)),
        out_shape=jax.ShapeDtypeStruct((8, n), F32),
        compiler_params=_params("arbitrary"),
        name="modulation",
    )(cond8, w_ada, b_ada.reshape(1, n))


def _rope(x, cos, sin, half):
    lane = lax.broadcasted_iota(jnp.int32, x.shape, 1)
    first = (lane & (2 * half - 1)) < half
    partner = jnp.where(first, pltpu.roll(x, LANES - half, 1), pltpu.roll(x, half, 1))
    return x * cos + partner * sin


def _rope_tables():
    t = np.arange(DEC_SEQ)
    row = (t // GRID_W).astype(np.float32)
    col = (t % GRID_W).astype(np.float32)

    def tables(d_rot):
        nf = d_rot // 4
        inv = np.float32(ROPE_BASE) ** (-np.arange(nf, dtype=np.float32) / np.float32(nf))
        ar = row[:, None] * inv
        ac = col[:, None] * inv
        cos = np.concatenate([np.cos(ar), np.cos(ar), np.cos(ac), np.cos(ac)], axis=1)
        sin = np.concatenate([-np.sin(ar), np.sin(ar), -np.sin(ac), np.sin(ac)], axis=1)
        return cos.astype(np.float32), sin.astype(np.float32)

    cos_a, sin_a = tables(A_HEAD_DIM)
    cos_a = np.concatenate([cos_a, cos_a], axis=1)
    sin_a = np.concatenate([sin_a, sin_a], axis=1)
    cos_b, sin_b = tables(B_ROPE)
    one = np.ones((DEC_SEQ, B_NOPE), np.float32)
    zero = np.zeros((DEC_SEQ, B_NOPE), np.float32)
    pad = LANES - B_NOPE - B_ROPE
    cos_b = np.concatenate([one, cos_b, one[:, :pad]], axis=1)
    sin_b = np.concatenate([zero, sin_b, zero[:, :pad]], axis=1)
    return tuple(jnp.asarray(a) for a in (cos_a, sin_a, cos_b, sin_b))


_N1 = 1536


def _inproj_kernel(latent, *refs):
    if latent:
        (x_ref, mod_ref, g_ref, w1_ref, qn_ref, kvn_ref, wuq_ref, wk_ref, wv_ref,
         ca_ref, sa_ref, cb_ref, sb_ref,
         qa_o, ka_o, va_o, qb_o, kb_o, vb_o) = refs
    else:
        (x_ref, mod_ref, g_ref, w1_ref, qn_ref, kvn_ref, wuq_ref, wk_ref, wv_ref,
         qa_o, ka_o, va_o, ckv_o, kr_o, qb_o, kb_o, vb_o) = refs
    x = x_ref[...]
    shift = mod_ref[:, 0:D_MODEL]
    scale = mod_ref[:, D_MODEL:2 * D_MODEL]
    h = _rms(x, g_ref[...]) * (1.0 + scale) + shift
    p = _dot(h.astype(BF16), w1_ref[...])
    qa = p[:, 0:512]
    ka = p[:, 512:640]
    va = p[:, 640:768]
    ql = p[:, 768:1152]
    ckv = p[:, 1152:1408]
    krp = p[:, 1408:1536]
    qn = _rms(ql, qn_ref[...])
    qb = _dot(qn.astype(BF16), wuq_ref[...])
    cn = _rms(ckv, kvn_ref[...])
    cnb = cn.astype(BF16)
    kn = _dot(cnb, wk_ref[...])
    vb = _dot(cnb, wv_ref[...])
    if latent:
        ca, sa, cb, sb = ca_ref[...], sa_ref[...], cb_ref[...], sb_ref[...]
        ka = _rope(ka, ca, sa, A_HEAD_DIM // 4)
        krp = _rope(krp, cb, sb, B_ROPE // 4)
        for j in range(4):
            blk = _rope(qa[:, j * LANES:(j + 1) * LANES], ca, sa, A_HEAD_DIM // 4)
            qa_o[:, j * LANES:(j + 1) * LANES] = (blk * (A_HEAD_DIM ** -0.5 * LOG2E)).astype(BF16)
        for hd in range(B_HEADS):
            blk = _rope(qb[:, hd * LANES:(hd + 1) * LANES], cb, sb, B_ROPE // 4)
            qb_o[:, hd * LANES:(hd + 1) * LANES] = (blk * (MLA_SCALE * LOG2E)).astype(BF16)
        ka_o[...] = ka.astype(BF16)
        va_o[...] = va.astype(BF16)
    else:
        qa_o[...] = (qa * (A_HEAD_DIM ** -0.5 * LOG2E)).astype(BF16)
        qb_o[...] = (qb * (MLA_SCALE * LOG2E)).astype(BF16)
        for bb in range(TM // SEQ):
            tok = slice(bb * SEQ, (bb + 1) * SEQ)
            ka_o[bb] = ka[tok, :].T
            va_o[bb] = va[tok, :].T
            kr_o[bb] = krp[tok, :].T[B_NOPE:B_NOPE + B_ROPE, :]
        ckv_o[...] = cn
    for hd in range(B_HEADS):
        kh = kn[:, hd * LANES:(hd + 1) * LANES] + krp
        if latent:
            kb_o[hd * LANES:(hd + 1) * LANES, :] = kh.T.astype(BF16)
        else:
            kb_o[:, hd * LANES:(hd + 1) * LANES] = kh.astype(BF16)
    vb_o[...] = vb.astype(BF16)


def _inproj(x, mod3, mod_row, g, w1, qn, kvn, wuq, wk, wv, tables):
    latent = tables is not None
    t = x.shape[0]
    n_tiles = t // TM
    row = lambda i: (i, 0)
    const = lambda i: (0, 0)
    in_specs = [pl.BlockSpec((TM, D_MODEL), row),
                pl.BlockSpec((None, 1, 6 * D_MODEL), lambda i: (mod_row(i), 0, 0)),
                pl.BlockSpec((1, D_MODEL), const),
                pl.BlockSpec((D_MODEL, _N1), const),
                pl.BlockSpec((1, Q_LORA), const),
                pl.BlockSpec((1, KV_LORA), const),
                pl.BlockSpec((Q_LORA, B_HEADS * LANES), const),
                pl.BlockSpec((KV_LORA, B_HEADS * LANES), const),
                pl.BlockSpec((KV_LORA, B_HEADS * B_V), const)]
    args = [x, mod3, g, w1, qn, kvn, wuq, wk, wv]
    sds = jax.ShapeDtypeStruct
    if latent:
        per = DEC_SEQ // TM
        tab = lambda i: (i % per, 0)
        in_specs += [pl.BlockSpec((TM, LANES), tab)] * 4
        args += list(tables)
        out_shape = [sds((t, 512), BF16), sds((t, LANES), BF16), sds((t, LANES), BF16),
                     sds((t, B_HEADS * LANES), BF16), sds((t // DEC_SEQ, B_HEADS * LANES, DEC_SEQ), BF16),
                     sds((t, B_HEADS * B_V), BF16)]
        widths = [512, LANES, LANES, B_HEADS * LANES, None, B_HEADS * B_V]
    else:
        nb = t // SEQ
        out_shape = [sds((t, 512), BF16), sds((nb, LANES, SEQ), F32), sds((nb, LANES, SEQ), F32),
                     sds((t, KV_LORA), F32), sds((nb, B_ROPE, SEQ), F32),
                     sds((t, B_HEADS * LANES), BF16), sds((t, B_HEADS * LANES), BF16),
                     sds((t, B_HEADS * B_V), BF16)]
        widths = [512, None, None, KV_LORA, None, B_HEADS * LANES, B_HEADS * LANES, B_HEADS * B_V]
    if latent:
        per = DEC_SEQ // TM
        special = pl.BlockSpec((None, B_HEADS * LANES, TM), lambda i: (i // per, 0, i % per))
        out_specs = [pl.BlockSpec((TM, w), row) if w is not None else special for w in widths]
    else:
        out_specs = [pl.BlockSpec((TM, w), row) if w is not None
                     else pl.BlockSpec((TM // SEQ,) + o.shape[1:], lambda i: (i, 0, 0))
                     for w, o in zip(widths, out_shape)]
    return pl.pallas_call(
        functools.partial(_inproj_kernel, latent),
        grid=(n_tiles,),
        in_specs=in_specs,
        out_specs=out_specs,
        out_shape=out_shape,
        compiler_params=_params("arbitrary"),
        name="inproj_latent" if latent else "inproj_ctx",
    )(*args)


def _kvexp_kernel(ckv_ref, krp_ref, wk_ref, wv_ref, kb_o, vb_o):
    cb = ckv_ref[...].astype(BF16)
    kn = _dot(cb, wk_ref[...])
    krp = krp_ref[...]
    for hd in range(B_HEADS):
        kb_o[hd * LANES:(hd + 1) * LANES, :] = (kn[:, hd * LANES:(hd + 1) * LANES] + krp).T.astype(BF16)
    vb_o[...] = _dot(cb, wv_ref[...]).astype(BF16)


def _kv_expand(ckv, krp, wk, wv):
    t = ckv.shape[0]
    row = lambda i: (i, 0)
    const = lambda i: (0, 0)
    return pl.pallas_call(
        _kvexp_kernel,
        grid=(t // TM,),
        in_specs=[pl.BlockSpec((TM, KV_LORA), row), pl.BlockSpec((TM, LANES), row),
                  pl.BlockSpec((KV_LORA, B_HEADS * LANES), const),
                  pl.BlockSpec((KV_LORA, B_HEADS * B_V), const)],
        out_specs=[pl.BlockSpec((None, B_HEADS * LANES, TM), lambda i: (i, 0, 0)),
                   pl.BlockSpec((TM, B_HEADS * B_V), row)],
        out_shape=[jax.ShapeDtypeStruct((t // TM, B_HEADS * LANES, TM), BF16),
                   jax.ShapeDtypeStruct((t, B_HEADS * B_V), BF16)],
        compiler_params=_params("arbitrary"),
        name="kv_expand",
    )(ckv, krp, wk, wv)


def _lo_mask(rows):
    return lax.broadcasted_iota(jnp.int32, (rows, LANES), 1) < (LANES // 2)


def _softmax_pv(parts, sink, v_t=False):
    if sink is not None:
        sink = sink * LOG2E
    m = None
    for s, _ in parts:
        mi = jnp.max(s, axis=-1, keepdims=True)
        m = mi if m is None else jnp.maximum(m, mi)
    if sink is not None:
        m = jnp.maximum(m, sink)
    den = None
    acc = None
    for s, v in parts:
        p = jnp.exp2(s - m)
        di = jnp.sum(p, axis=-1, keepdims=True)
        den = di if den is None else den + di
        oi = _dot_nt(p.astype(BF16), v) if v_t else _dot(p.astype(BF16), v)
        acc = oi if acc is None else acc + oi
    if sink is not None:
        den = den + jnp.exp2(sink - m)
    return acc / den


def _ctx_attn_kernel(sink_ref, qa_ref, ka_ref, va_ref, qb_ref, kb_ref, vb_ref, oa_ref, ob_ref):
    lo = _lo_mask(SEQ)
    k = ka_ref[...].astype(BF16)
    v = va_ref[...].astype(BF16)
    for j in range(A_GROUP):
        q = qa_ref[:, j * LANES:(j + 1) * LANES]
        zero = jnp.zeros_like(q)
        o0 = _softmax_pv([(_dot(jnp.where(lo, q, zero), k), v)], sink_ref[j], v_t=True)
        o1 = _softmax_pv([(_dot(jnp.where(lo, zero, q), k), v)], sink_ref[A_GROUP + j], v_t=True)
        oa_ref[:, j * LANES:(j + 1) * LANES] = jnp.where(lo, o0, o1).astype(BF16)
    for j in range(B_HEADS // 2):
        vp = vb_ref[:, j * LANES:(j + 1) * LANES]
        outs = []
        for hd in (2 * j, 2 * j + 1):
            s = _dot_nt(qb_ref[:, hd * LANES:(hd + 1) * LANES], kb_ref[:, hd * LANES:(hd + 1) * LANES])
            outs.append(_softmax_pv([(s, vp)], None))
        ob_ref[:, j * LANES:(j + 1) * LANES] = jnp.where(lo, outs[0], outs[1]).astype(BF16)


def _ctx_attention(sink, qa, ka, va, qb, kb, vb):
    blk = lambda w: pl.BlockSpec((SEQ, w), lambda b: (b, 0))
    kv_t = pl.BlockSpec((None, LANES, SEQ), lambda b: (b, 0, 0))
    return pl.pallas_call(
        _ctx_attn_kernel,
        grid=(BATCH,),
        in_specs=[pl.BlockSpec(memory_space=pltpu.SMEM),
                  blk(512), kv_t, kv_t, blk(B_HEADS * LANES), blk(B_HEADS * LANES),
                  blk(B_HEADS * B_V)],
        out_specs=[blk(512), blk(512)],
        out_shape=[jax.ShapeDtypeStruct((T_CTX, 512), BF16), jax.ShapeDtypeStruct((T_CTX, 512), BF16)],
        compiler_params=_params("arbitrary"),
        name="ctx_attention",
    )(sink, qa, ka, va, qb, kb, vb)


_WIN = TQ + 2 * WINDOW


def _lat_swa_kernel(sink_ref, q_ref, k_ref, v_ref, kc_ref, vc_ref, o_ref):
    qi = pl.program_id(1)
    q0 = qi * TQ
    start = pl.multiple_of(jnp.clip(q0 - WINDOW, 0, DEC_SEQ - _WIN), WINDOW)
    kw = k_ref[pl.ds(start, _WIN), :]
    vw = v_ref[pl.ds(start, _WIN), :]
    kc = kc_ref[...].astype(BF16)
    vc = vc_ref[...].astype(BF16)
    qpos = q0 + lax.broadcasted_iota(jnp.int32, (TQ, _WIN), 0)
    kpos = start + lax.broadcasted_iota(jnp.int32, (TQ, _WIN), 1)
    valid = jnp.abs(qpos - kpos) <= WINDOW
    lo = _lo_mask(TQ)
    for j in range(A_GROUP):
        q = q_ref[:, j * LANES:(j + 1) * LANES]
        zero = jnp.zeros_like(q)
        outs = []
        for g, qh in ((0, jnp.where(lo, q, zero)), (1, jnp.where(lo, zero, q))):
            sw = jnp.where(valid, _dot_nt(qh, kw), NEG)
            sc = _dot_nt(qh, kc)
            outs.append(_softmax_pv([(sw, vw), (sc, vc)], sink_ref[g * A_GROUP + j]))
        o_ref[:, j * LANES:(j + 1) * LANES] = jnp.where(lo, outs[0], outs[1]).astype(BF16)


def _lat_swa(sink, qa, ka, va, kc, vc):
    nq = DEC_SEQ // TQ
    return pl.pallas_call(
        _lat_swa_kernel,
        grid=(DEC_BATCH, nq),
        in_specs=[pl.BlockSpec(memory_space=pltpu.SMEM),
                  pl.BlockSpec((TQ, 512), lambda b, i: (b * nq + i, 0)),
                  pl.BlockSpec((None, DEC_SEQ, LANES), lambda b, i: (b, 0, 0)),
                  pl.BlockSpec((None, DEC_SEQ, LANES), lambda b, i: (b, 0, 0)),
                  pl.BlockSpec((None, PAST_LEN, LANES), lambda b, i: (b, 0, 0)),
                  pl.BlockSpec((None, PAST_LEN, LANES), lambda b, i: (b, 0, 0))],
        out_specs=pl.BlockSpec((TQ, 512), lambda b, i: (b * nq + i, 0)),
        out_shape=jax.ShapeDtypeStruct((T_LAT, 512), BF16),
        compiler_params=_params("arbitrary", "arbitrary"),
        name="latent_swa",
    )(sink, qa, ka.reshape(DEC_BATCH, DEC_SEQ, LANES), va.reshape(DEC_BATCH, DEC_SEQ, LANES), kc, vc)


TQ_MLA = 256


def _lat_mla_kernel(q_ref, kl_ref, vl_ref, kc_ref, vc_ref, o_ref):
    lo = _lo_mask(TQ_MLA)
    vl = vl_ref[...]
    vc = vc_ref[...]
    outs = []
    for hh in range(2):
        q = q_ref[:, hh * LANES:(hh + 1) * LANES]
        sc = _dot(q, kc_ref[hh * LANES:(hh + 1) * LANES, :])
        sl = _dot(q, kl_ref[hh * LANES:(hh + 1) * LANES, :])
        outs.append(_softmax_pv([(sc, vc), (sl, vl)], None))
    o_ref[...] = jnp.where(lo, outs[0], outs[1]).astype(BF16)


def _lat_mla(qb, kb, vb, kc, vc):
    nq = DEC_SEQ // TQ_MLA
    npair = B_HEADS // 2
    return pl.pallas_call(
        _lat_mla_kernel,
        grid=(DEC_BATCH, npair, nq),
        in_specs=[pl.BlockSpec((TQ_MLA, 2 * LANES), lambda b, p, i: (b * nq + i, p)),
                  pl.BlockSpec((None, 2 * LANES, DEC_SEQ), lambda b, p, i: (b, p, 0)),
                  pl.BlockSpec((None, DEC_SEQ, LANES), lambda b, p, i: (b, 0, p)),
                  pl.BlockSpec((None, 2 * LANES, PAST_LEN), lambda b, p, i: (b, p, 0)),
                  pl.BlockSpec((None, PAST_LEN, LANES), lambda b, p, i: (b, 0, p))],
        out_specs=pl.BlockSpec((TQ_MLA, LANES), lambda b, p, i: (b * nq + i, p)),
        out_shape=jax.ShapeDtypeStruct((T_LAT, 512), BF16),
        compiler_params=_params("arbitrary", "arbitrary", "arbitrary"),
        name="latent_mla",
    )(qb, kb, vb.reshape(DEC_BATCH, DEC_SEQ, B_HEADS * B_V), kc, vc.reshape(DEC_BATCH, PAST_LEN, B_HEADS * B_V))


_N_CTX_TILES = T_CTX // TM


def _merge_kernel(xp_ref, xs_ref, oac_ref, oal_ref, obc_ref, obl_ref, mod_ref,
                  gpre_ref, gpost_ref, gffn_ref, woa_ref, wob_ref, wout_ref, wg_ref, wr_ref,
                  x1_o, h2_o, lg_o):
    def body(x_ref, oa_ref, ob_ref):
        x = x_ref[...]
        sh1 = mod_ref[:, 0:D_MODEL]
        sc1 = mod_ref[:, D_MODEL:2 * D_MODEL]
        g1 = mod_ref[:, 2 * D_MODEL:3 * D_MODEL]
        sh2 = mod_ref[:, 3 * D_MODEL:4 * D_MODEL]
        sc2 = mod_ref[:, 4 * D_MODEL:5 * D_MODEL]
        h = (_rms(x, gpre_ref[...]) * (1.0 + sc1) + sh1).astype(BF16)
        gates = _dot(h, wg_ref[...])
        ya = _dot(oa_ref[...], woa_ref[...])
        yb = _dot(ob_ref[...], wob_ref[...])
        y = _sigmoid(gates[:, :D_MODEL]) * ya + _sigmoid(gates[:, D_MODEL:]) * yb
        mix = _dot(y.astype(BF16), wout_ref[...])
        x1 = x + g1 * _rms(mix, gpost_ref[...])
        x1_o[...] = x1
        h2 = _rms(x1, gffn_ref[...]) * (1.0 + sc2) + sh2
        _store_rows(h2_o, h2, TM)
        lg_o[...] = _dot_nt(wr_ref[...], h2.astype(BF16))

    is_ctx = pl.program_id(0) < _N_CTX_TILES

    @pl.when(is_ctx)
    def _():
        body(xp_ref, oac_ref, obc_ref)

    @pl.when(jnp.logical_not(is_ctx))
    def _():
        body(xs_ref, oal_ref, obl_ref)


def _mod_row_all(i):
    per = DEC_SEQ // TM
    return jnp.where(i < _N_CTX_TILES, 0, 1 + (jnp.maximum(i - _N_CTX_TILES, 0)) // per)


def _merge(xp, xs, oac, oal, obc, obl, mod3, gpre, gpost, gffn, woa, wob, wout, wg, wr):
    ctx = lambda i: (jnp.minimum(i, _N_CTX_TILES - 1), 0)
    lat = lambda i: (jnp.maximum(i - _N_CTX_TILES, 0), 0)
    row = lambda i: (i, 0)
    const = lambda i: (0, 0)
    return pl.pallas_call(
        _merge_kernel,
        grid=(T_ALL // TM,),
        in_specs=[pl.BlockSpec((TM, D_MODEL), ctx), pl.BlockSpec((TM, D_MODEL), lat),
                  pl.BlockSpec((TM, 512), ctx), pl.BlockSpec((TM, 512), lat),
                  pl.BlockSpec((TM, 512), ctx), pl.BlockSpec((TM, 512), lat),
                  pl.BlockSpec((None, 1, 6 * D_MODEL), lambda i: (_mod_row_all(i), 0, 0)),
                  pl.BlockSpec((1, D_MODEL), const), pl.BlockSpec((1, D_MODEL), const),
                  pl.BlockSpec((1, D_MODEL), const),
                  pl.BlockSpec((512, D_MODEL), const), pl.BlockSpec((512, D_MODEL), const),
                  pl.BlockSpec((D_MODEL, D_MODEL), const), pl.BlockSpec((D_MODEL, 2 * D_MODEL), const),
                  pl.BlockSpec((N_EXPERTS, D_MODEL), const)],
        out_specs=[pl.BlockSpec((TM, D_MODEL), row), pl.BlockSpec((TM * SUB, LANES), row),
                   pl.BlockSpec((N_EXPERTS, TM), lambda i: (0, i))],
        out_shape=[jax.ShapeDtypeStruct((T_ALL, D_MODEL), F32), jax.ShapeDtypeStruct((T_ALL * SUB, LANES), F32),
                   jax.ShapeDtypeStruct((N_EXPERTS, T_ALL), F32)],
        compiler_params=_params("arbitrary"),
        name="merge",
    )(xp, xs, oac, oal, obc, obl, mod3, gpre, gpost, gffn, woa, wob, wout, wg, wr)


_PER_GROUP = N_EXPERTS // N_GROUPS


def _router_kernel(lg_ref, bias_ref, topi_o, rank_o, topw_o, cnt_o, cnt_s):
    @pl.when(pl.program_id(0) == 0)
    def _():
        cnt_s[...] = jnp.zeros_like(cnt_s)

    scores = _sigmoid(lg_ref[...])
    sel = scores + bias_ref[...]
    eidx = lax.broadcasted_iota(jnp.int32, (N_EXPERTS, TM), 0).astype(F32)
    ninf = -jnp.inf
    big = float(N_EXPERTS)
    sel_g = [sel[g * _PER_GROUP:(g + 1) * _PER_GROUP, :] for g in range(N_GROUPS)]
    gs = []
    for sg in sel_g:
        m1 = jnp.max(sg, axis=0, keepdims=True)
        top = sg == m1
        n_top = jnp.sum(jnp.where(top, 1.0, 0.0), axis=0, keepdims=True)
        rest = jnp.max(jnp.where(top, ninf, sg), axis=0, keepdims=True)
        gs.append(m1 + jnp.where(n_top > 1.5, m1, rest))
    cur_g = []
    for g in range(N_GROUPS):
        beat = jnp.zeros((1, TM), F32)
        for g2 in range(N_GROUPS):
            if g2 == g:
                continue
            better = (gs[g2] >= gs[g]) if g2 < g else (gs[g2] > gs[g])
            beat = beat + jnp.where(better, 1.0, 0.0)
        cur_g.append(jnp.where(beat < TOPK_GROUPS, sel_g[g], NEG))
    cur = jnp.concatenate(cur_g, axis=0)
    chosen = jnp.zeros((N_EXPERTS, TM), F32)
    idxs, ws = [], []
    for k in range(TOP_K):
        m = jnp.max(cur, axis=0, keepdims=True)
        idx = jnp.min(jnp.where(cur == m, eidx, big), axis=0, keepdims=True)
        hit = eidx == idx
        ws.append(jnp.sum(jnp.where(hit, scores, 0.0), axis=0, keepdims=True))
        cur = jnp.where(hit, ninf, cur)
        chosen = jnp.where(hit, 1.0, chosen)
        idxs.append(idx)
    wsum = ws[0]
    for k in range(1, TOP_K):
        wsum = wsum + ws[k]
    for k in range(TOP_K):
        topw_o[k:k + 1, :] = ws[k] / wsum * ROUTED_SCALE
        topi_o[k:k + 1, :] = idxs[k].astype(jnp.int32)
    r_i = lax.broadcasted_iota(jnp.int32, (TM, TM), 0)
    c_i = lax.broadcasted_iota(jnp.int32, (TM, TM), 1)
    earlier = jnp.where(r_i < c_i, 1.0, 0.0).astype(BF16)
    prefix = _dot(chosen.astype(BF16), earlier) + cnt_s[:, 0:1]
    for k in range(TOP_K):
        r = jnp.sum(jnp.where(eidx == idxs[k], prefix, 0.0), axis=0, keepdims=True)
        rank_o[k:k + 1, :] = r.astype(jnp.int32)
    total = cnt_s[...] + jnp.sum(chosen, axis=1, keepdims=True)
    cnt_s[...] = total
    cnt_o[...] = total


def _router(logits_t, bias_col):
    col = lambda i: (0, i)
    const = lambda i: (0, 0)
    return pl.pallas_call(
        _router_kernel,
        grid=(T_ALL // TM,),
        in_specs=[pl.BlockSpec((N_EXPERTS, TM), col), pl.BlockSpec((N_EXPERTS, 1), const)],
        out_specs=[pl.BlockSpec((TOP_K, TM), col), pl.BlockSpec((TOP_K, TM), col),
                   pl.BlockSpec((TOP_K, TM), col), pl.BlockSpec((N_EXPERTS, LANES), const)],
        out_shape=[jax.ShapeDtypeStruct((TOP_K, T_ALL), jnp.int32),
                   jax.ShapeDtypeStruct((TOP_K, T_ALL), jnp.int32),
                   jax.ShapeDtypeStruct((TOP_K, T_ALL), F32),
                   jax.ShapeDtypeStruct((N_EXPERTS, LANES), F32)],
        scratch_shapes=[pltpu.VMEM((N_EXPERTS, LANES), F32)],
        compiler_params=_params("arbitrary"),
        name="router",
    )(logits_t, bias_col)


def _dest_kernel(topi_ref, rank_ref, ps_ref, o_ref):
    eidx = lax.broadcasted_iota(jnp.int32, (N_EXPERTS, TM), 0)
    ps = jnp.broadcast_to(ps_ref[...], (N_EXPERTS, TM))
    for k in range(TOP_K):
        start = jnp.sum(jnp.where(eidx == topi_ref[k:k + 1, :], ps, 0.0), axis=0, keepdims=True)
        o_ref[k:k + 1, :] = rank_ref[k:k + 1, :] + start.astype(jnp.int32)


def _dest(topi, rank, pstart_col):
    col = lambda i: (0, i)
    return pl.pallas_call(
        _dest_kernel,
        grid=(T_ALL // TM,),
        in_specs=[pl.BlockSpec((TOP_K, TM), col), pl.BlockSpec((TOP_K, TM), col),
                  pl.BlockSpec((N_EXPERTS, 1), lambda i: (0, 0))],
        out_specs=pl.BlockSpec((TOP_K, TM), col),
        out_shape=jax.ShapeDtypeStruct((TOP_K, T_ALL), jnp.int32),
        compiler_params=_params("arbitrary"),
        name="dest",
    )(topi, rank, pstart_col)


TD = 512
DISPATCH_PARTS = 4
ROW_UNROLL = 4


def _dispatch_kernel(dest_ref, h_ref, wgu_ref, wd_ref, xs_hbm, sh_o, sem):
    def row(t, carry):
        src = h_ref.at[pl.ds(pl.multiple_of(t * SUB, SUB), SUB)]
        for k in range(TOP_K):
            d = dest_ref[t * TOP_K + k]
            pltpu.make_async_copy(src, xs_hbm.at[pl.ds(pl.multiple_of(d * SUB, SUB), SUB)],
                                  sem).start(priority=k % 2)
        return carry

    part = TD // DISPATCH_PARTS
    for p in range(DISPATCH_PARTS):
        lax.fori_loop(p * part, (p + 1) * part, row, 0, unroll=ROW_UNROLL)
        x = _load_rows(h_ref.at[pl.ds(p * part * SUB, part * SUB)], part).astype(BF16)
        gu = _dot(x, wgu_ref[...])
        a = _silu(gu[:, :D_SHARED]) * gu[:, D_SHARED:]
        sh_o[p * part:(p + 1) * part, :] = _dot(a.astype(BF16), wd_ref[...])
    for _ in range(TOP_K):
        pltpu.make_async_copy(h_ref, xs_hbm.at[pl.ds(0, TD * SUB)], sem).wait()


def _dispatch(dest_flat, h2, wgu, wd):
    const = lambda i: (0, 0)
    return pl.pallas_call(
        _dispatch_kernel,
        grid=(T_ALL // TD,),
        in_specs=[pl.BlockSpec((TD * TOP_K,), lambda i: (i,), memory_space=pltpu.SMEM),
                  pl.BlockSpec((TD * SUB, LANES), lambda i: (i, 0)),
                  pl.BlockSpec((D_MODEL, 2 * D_SHARED), const), pl.BlockSpec((D_SHARED, D_MODEL), const)],
        out_specs=[pl.BlockSpec(memory_space=pl.ANY), pl.BlockSpec((TD, D_MODEL), lambda i: (i, 0))],
        out_shape=[jax.ShapeDtypeStruct((N_SLOTS * SUB, LANES), F32),
                   jax.ShapeDtypeStruct((T_ALL, D_MODEL), F32)],
        scratch_shapes=[pltpu.SemaphoreType.DMA(())],
        compiler_params=_params("arbitrary"),
        name="dispatch",
    )(dest_flat, h2, wgu, wd)


_BLK_TILES = MOE_BLK * SUB
NBX = 6
PREFETCH = NBX - 2
NBO = 4


def _expert_kernel(bs_ref, x_hbm, wg_ref, wu_ref, wd_ref, o_hbm, wgu_s, wd_s, xbuf, obuf, xsem, osem):
    e = pl.program_id(0)
    b0 = bs_ref[e]
    b1 = bs_ref[e + 1]
    n_used = bs_ref[N_EXPERTS]

    def xcopy(g):
        s = lax.rem(g, NBX)
        return pltpu.make_async_copy(x_hbm.at[pl.ds(pl.multiple_of(g * _BLK_TILES, _BLK_TILES), _BLK_TILES)],
                                     xbuf.at[s], xsem.at[s])

    def ocopy(g):
        s = lax.rem(g, NBO)
        return pltpu.make_async_copy(obuf.at[s],
                                     o_hbm.at[pl.ds(pl.multiple_of(g * _BLK_TILES, _BLK_TILES), _BLK_TILES)],
                                     osem.at[s])

    @pl.when(e == 0)
    def _():
        for j in range(PREFETCH):
            @pl.when(j < n_used)
            def _():
                xcopy(j).start()

    @pl.when(b1 > b0)
    def _():
        wgu_s[:, 0:D_EXPERT] = wg_ref[...].astype(BF16)
        wgu_s[:, D_EXPERT:2 * D_EXPERT] = wu_ref[...].astype(BF16)
        wd_s[...] = wd_ref[...].astype(BF16)

    def acquire(g):
        @pl.when(g + PREFETCH < n_used)
        def _():
            xcopy(g + PREFETCH).start()

        xcopy(g).wait()

        @pl.when(g >= NBO)
        def _():
            ocopy(g - NBO).wait()

    def compute(g):
        gu = _dot(_load_rows(xbuf.at[lax.rem(g, NBX)], MOE_BLK).astype(BF16), wgu_s[...])
        a = _silu(gu[:, :D_EXPERT]) * gu[:, D_EXPERT:]
        _store_rows(obuf.at[lax.rem(g, NBO)], _dot(a.astype(BF16), wd_s[...]), MOE_BLK)

    def pair(i, carry):
        g = b0 + 2 * i
        acquire(g)
        acquire(g + 1)
        compute(g)
        compute(g + 1)
        ocopy(g).start(priority=1)
        ocopy(g + 1).start(priority=1)
        return carry

    lax.fori_loop(0, lax.shift_right_logical(b1 - b0, 1), pair, 0)

    @pl.when(((b1 - b0) & 1) == 1)
    def _():
        acquire(b1 - 1)
        compute(b1 - 1)
        ocopy(b1 - 1).start(priority=1)

    @pl.when(e == N_EXPERTS - 1)
    def _():
        for j in range(NBO, 0, -1):
            @pl.when(n_used >= j)
            def _():
                ocopy(n_used - j).wait()


def _experts(blk_start, x_sorted, w_gate, w_up, w_down):
    wsel = lambda e, bs: (e, 0, 0)
    grid_spec = pltpu.PrefetchScalarGridSpec(
        num_scalar_prefetch=1,
        grid=(N_EXPERTS,),
        in_specs=[pl.BlockSpec(memory_space=pl.ANY),
                  pl.BlockSpec((None, D_MODEL, D_EXPERT), wsel),
                  pl.BlockSpec((None, D_MODEL, D_EXPERT), wsel),
                  pl.BlockSpec((None, D_EXPERT, D_MODEL), wsel)],
        out_specs=pl.BlockSpec(memory_space=pl.ANY),
        scratch_shapes=[pltpu.VMEM((D_MODEL, 2 * D_EXPERT), BF16), pltpu.VMEM((D_EXPERT, D_MODEL), BF16),
                        pltpu.VMEM((NBX, _BLK_TILES, LANES), F32), pltpu.VMEM((NBO, _BLK_TILES, LANES), F32),
                        pltpu.SemaphoreType.DMA((NBX,)), pltpu.SemaphoreType.DMA((NBO,))],
    )
    return pl.pallas_call(
        _expert_kernel,
        grid_spec=grid_spec,
        out_shape=jax.ShapeDtypeStruct((N_SLOTS * SUB, LANES), F32),
        compiler_params=_params("arbitrary"),
        name="experts",
    )(blk_start, x_sorted, w_gate, w_up, w_down)


TC = 256
CR = 16


def _combine_kernel(dcur_ref, dnext_ref, eo_hbm, x1_ref, sh_ref, w_ref, mod_ref, g_ref, o_ref, buf, sem, wb_s):
    i = pl.program_id(0)
    n = pl.num_programs(0)
    slot = lax.rem(i, 2)

    def gather(d_ref, s, t0, t1):
        def row(t, carry):
            for k in range(TOP_K):
                d = d_ref[t * TOP_K + k]
                pltpu.make_async_copy(eo_hbm.at[pl.ds(pl.multiple_of(d * SUB, SUB), SUB)],
                                      buf.at[s, k, pl.ds(pl.multiple_of(t * SUB, SUB), SUB)],
                                      sem.at[s]).start(priority=k % 2)
            return carry
        lax.fori_loop(t0, t1, row, 0, unroll=ROW_UNROLL)

    @pl.when(i == 0)
    def _():
        gather(dcur_ref, 0, 0, TC)

    @pl.when(i + 1 < n)
    def _():
        gather(dnext_ref, 1 - slot, 0, TC)

    gate = mod_ref[:, 5 * D_MODEL:6 * D_MODEL]
    for k in range(TOP_K):
        wb_s[k] = jnp.broadcast_to(w_ref[:, k:k + 1], (TC, LANES))
    for k in range(TOP_K):
        pltpu.make_async_copy(eo_hbm.at[pl.ds(0, TC * SUB)], buf.at[slot, k], sem.at[slot]).wait()

    lanes = [slice(j * LANES, (j + 1) * LANES) for j in range(SUB)]

    def chunk(r, carry):
        r0 = pl.multiple_of(r * CR, CR)
        rows = pl.ds(r0, CR)
        wk = [wb_s[k, rows, :] for k in range(TOP_K)]
        pieces = []
        sq = jnp.zeros((CR, LANES), F32)
        for j in range(SUB):
            acc = sh_ref[rows, lanes[j]]
            for k in range(TOP_K):
                acc = acc + buf[slot, k, pl.ds(r0 * SUB + j, CR, stride=SUB), :] * wk[k]
            pieces.append(acc)
            sq = sq + acc * acc
        inv = lax.rsqrt(jnp.sum(sq, axis=-1, keepdims=True) / D_MODEL + EPS)
        for j in range(SUB):
            o_ref[rows, lanes[j]] = (x1_ref[rows, lanes[j]]
                                     + gate[:, lanes[j]] * (pieces[j] * inv * g_ref[:, lanes[j]]))
        return carry

    lax.fori_loop(0, TC // CR, chunk, 0, unroll=8)


def _combine(dest_flat, eo, x1, shared, topw, mod3, mod_row, g, tile0, n_tiles):
    row = lambda i: (tile0 + i, 0)
    const = lambda i: (0, 0)
    blk = TC * TOP_K
    return pl.pallas_call(
        _combine_kernel,
        grid=(n_tiles,),
        in_specs=[pl.BlockSpec((blk,), lambda i: (tile0 + i,), memory_space=pltpu.SMEM),
                  pl.BlockSpec((blk,), lambda i: (tile0 + jnp.minimum(i + 1, n_tiles - 1),),
                               memory_space=pltpu.SMEM),
                  pl.BlockSpec(memory_space=pl.ANY),
                  pl.BlockSpec((TC, D_MODEL), row), pl.BlockSpec((TC, D_MODEL), row),
                  pl.BlockSpec((TC, LANES), row),
                  pl.BlockSpec((None, 1, 6 * D_MODEL), lambda i: (mod_row(i), 0, 0)),
                  pl.BlockSpec((1, D_MODEL), const)],
        out_specs=pl.BlockSpec((TC, D_MODEL), lambda i: (i, 0)),
        out_shape=jax.ShapeDtypeStruct((n_tiles * TC, D_MODEL), F32),
        scratch_shapes=[pltpu.VMEM((2, TOP_K, TC * SUB, LANES), F32), pltpu.SemaphoreType.DMA((2,)),
                        pltpu.VMEM((TOP_K, TC, LANES), F32)],
        compiler_params=_params("arbitrary"),
        name="combine",
    )(dest_flat, dest_flat, eo, x1, shared, topw, mod3, g)


def _qa_perm():
    cols = []
    for j in range(A_GROUP):
        for g in range(A_KV_HEADS):
            hd = g * A_GROUP + j
            cols.extend(range(hd * A_HEAD_DIM, (hd + 1) * A_HEAD_DIM))
    return np.asarray(cols, np.int32)


def _pad_heads(w, width, n_heads, offset, total=LANES):
    k = w.shape[0]
    stride = w.shape[1] // n_heads
    w3 = w.reshape(k, n_heads, stride)[:, :, offset:offset + width]
    w3 = jnp.pad(w3, ((0, 0), (0, 0), (0, total - width)))
    return w3.reshape(k, n_heads * total)


def kernel(x_prompt, x_sample, cache_k_swa, cache_v_swa, cache_ckv_mla, cache_krope_mla, c, c_ctx, w_ada, b_ada, norm_pre_mix, norm_post_mix, norm_pre_ffn, norm_post_ffn, w_in, sink_swa, q_norm_mla, kv_norm_mla, w_uq_mla, w_ukv_mla, w_o_swa, w_o_mla, w_out, router_w, router_bias, w_gate_exp, w_up_exp, w_down_exp, w_gate_sh, w_up_sh, w_down_sh):
    l = 0
    perm = _qa_perm()
    wi = w_in[l]
    kr_blk = jnp.pad(wi[:, _O_KR:_O_GATE], ((0, 0), (B_NOPE, LANES - B_NOPE - B_ROPE)))
    w1 = jnp.concatenate([wi[:, _O_QA:_O_KA][:, perm], wi[:, _O_KA:_O_KR], kr_blk], axis=1).astype(BF16)
    wg = wi[:, _O_GATE:].astype(BF16)
    wuq = _pad_heads(w_uq_mla[l], B_NOPE + B_ROPE, B_HEADS, 0).astype(BF16)
    wk = _pad_heads(w_ukv_mla[l], B_NOPE, B_HEADS, 0).astype(BF16)
    wv = w_ukv_mla[l].reshape(KV_LORA, B_HEADS, B_NOPE + B_V)[:, :, B_NOPE:].reshape(KV_LORA, B_HEADS * B_V).astype(BF16)
    woa = w_o_swa[l][perm, :].astype(BF16)
    wob = w_o_mla[l].astype(BF16)
    wout = w_out[l].astype(BF16)
    wr = router_w[l].T.astype(BF16)
    wgu_sh = jnp.concatenate([w_gate_sh[l], w_up_sh[l]], axis=1).astype(BF16)
    wd_sh = w_down_sh[l].astype(BF16)
    sink = sink_swa[l]
    row2 = lambda v: v.reshape(1, -1)

    cond8 = jnp.concatenate([c_ctx[None, :], c, jnp.zeros((8 - 1 - DEC_BATCH, D_MODEL), F32)], axis=0)
    mod3 = _modulation(cond8, w_ada[l], b_ada[l]).reshape(8, 1, 6 * D_MODEL)

    xp = x_prompt.reshape(T_CTX, D_MODEL)
    xs = x_sample.reshape(T_LAT, D_MODEL)
    per = DEC_SEQ // TM
    shared_in = (row2(norm_pre_mix[l]), w1, row2(q_norm_mla[l]), row2(kv_norm_mla[l]), wuq, wk, wv)
    qa_c, ka_c, va_c, ckv_c, kr_c, qb_c, kb_c, vb_c = _inproj(xp, mod3, lambda i: 0, *shared_in, None)
    qa_l, ka_l, va_l, qb_l, kb_l, vb_l = _inproj(xs, mod3, lambda i: 1 + i // per, *shared_in, _rope_tables())

    oa_c, ob_c = _ctx_attention(sink, qa_c, ka_c, va_c, qb_c, kb_c, vb_c)

    krp_cache = jnp.pad(cache_krope_mla[:, l].reshape(DEC_BATCH * PAST_LEN, B_ROPE),
                        ((0, 0), (B_NOPE, LANES - B_NOPE - B_ROPE)))
    kc_b, vc_b = _kv_expand(cache_ckv_mla[:, l].reshape(DEC_BATCH * PAST_LEN, KV_LORA), krp_cache, wk, wv)
    oa_l = _lat_swa(sink, qa_l, ka_l, va_l,
                    cache_k_swa[:, l].reshape(DEC_BATCH, PAST_LEN, LANES),
                    cache_v_swa[:, l].reshape(DEC_BATCH, PAST_LEN, LANES))
    ob_l = _lat_mla(qb_l, kb_l, vb_l, kc_b, vc_b)

    x1, h2, logits = _merge(xp, xs, oa_c, oa_l, ob_c, ob_l, mod3,
                            row2(norm_pre_mix[l]), row2(norm_post_mix[l]), row2(norm_pre_ffn[l]),
                            woa, wob, wout, wg, wr)

    topi8, rank8, topw8, counts_b = _router(logits, router_bias[l].reshape(N_EXPERTS, 1))
    counts = counts_b[:, 0].astype(jnp.int32)

    padded = (counts + MOE_BLK - 1) // MOE_BLK * MOE_BLK
    pend = jnp.cumsum(padded)
    pstart = pend - padded
    dest = _dest(topi8, rank8, pstart.astype(F32).reshape(N_EXPERTS, 1)).T.reshape(-1)
    topw_p = jnp.pad(topw8.T, ((0, 0), (0, LANES - TOP_K)))
    blk_start = (jnp.concatenate([pstart, pend[-1:]]) // MOE_BLK).astype(jnp.int32)

    x_sorted, shared = _dispatch(dest, h2, wgu_sh, wd_sh)
    eo = _experts(blk_start, x_sorted, w_gate_exp[l], w_up_exp[l], w_down_exp[l])

    g_post = row2(norm_post_ffn[l])
    per_c = DEC_SEQ // TC
    y_p = _combine(dest, eo, x1, shared, topw_p, mod3, lambda i: 0, g_post, 0, T_CTX // TC)
    y_s = _combine(dest, eo, x1, shared, topw_p, mod3, lambda i: 1 + i // per_c, g_post,
                   T_CTX // TC, T_LAT // TC)

    return (y_p.reshape(BATCH, SEQ, D_MODEL), y_s.reshape(DEC_BATCH, DEC_SEQ, D_MODEL),
            ka_c.reshape(BATCH, 1, A_KV_HEADS, A_HEAD_DIM, SEQ).transpose(0, 1, 4, 2, 3),
            va_c.reshape(BATCH, 1, A_KV_HEADS, A_HEAD_DIM, SEQ).transpose(0, 1, 4, 2, 3),
            ckv_c.reshape(BATCH, 1, SEQ, KV_LORA),
            kr_c.reshape(BATCH, 1, B_ROPE, SEQ).transpose(0, 1, 3, 2))
```

```python
import functools

import jax
import jax.numpy as jnp
import numpy as np
from jax import lax
from jax.experimental import pallas as pl
from jax.experimental.pallas import tpu as pltpu

D_MODEL = 1024
BATCH = 32
SEQ = 256
DEC_BATCH = 4
DEC_SEQ = 2048
PAST_LEN = 512
GRID_W = 64
WINDOW = 128
ROPE_BASE = 10000.0
EPS = 1e-6
NEG = -1e30
A_HEADS = 8
A_KV_HEADS = 2
A_GROUP = A_HEADS // A_KV_HEADS
A_HEAD_DIM = 64
B_HEADS = 8
B_NOPE = 64
B_ROPE = 32
B_V = 64
KV_LORA = 256
Q_LORA = 384
MLA_SCALE = (B_NOPE + B_ROPE) ** -0.5
LOG2E = 1.4426950408889634
N_EXPERTS = 256
TOP_K = 8
N_GROUPS = 8
TOPK_GROUPS = 4
D_EXPERT = 256
D_SHARED = 256
ROUTED_SCALE = 2.5

T_CTX = BATCH * SEQ
T_LAT = DEC_BATCH * DEC_SEQ
T_ALL = T_CTX + T_LAT

LANES = 128
TM = 512
TQ = 256
MOE_BLK = 256
N_SLOT_BLK = (T_ALL * TOP_K) // MOE_BLK + N_EXPERTS
N_SLOTS = N_SLOT_BLK * MOE_BLK
VMEM_LIMIT = 48 * 1024 * 1024

F32 = jnp.float32
BF16 = jnp.bfloat16

_O_QA = 0
_O_KA = _O_QA + A_HEADS * A_HEAD_DIM
_O_VA = _O_KA + A_KV_HEADS * A_HEAD_DIM
_O_QL = _O_VA + A_KV_HEADS * A_HEAD_DIM
_O_CKV = _O_QL + Q_LORA
_O_KR = _O_CKV + KV_LORA
_O_GATE = _O_KR + B_ROPE


def _params(*sem):
    return pltpu.CompilerParams(dimension_semantics=sem, vmem_limit_bytes=VMEM_LIMIT)


def _dot(a, b):
    return jnp.dot(a, b, preferred_element_type=F32)


def _dot_nt(a, b):
    return lax.dot_general(a, b, (((1,), (1,)), ((), ())), preferred_element_type=F32)


def _rms(x, g):
    return x * lax.rsqrt(jnp.mean(x * x, axis=-1, keepdims=True) + EPS) * g


def _sigmoid(x):
    return 1.0 / (1.0 + jnp.exp(-x))


SUB = 8
assert SUB * LANES == D_MODEL


def _load_rows(ref, n):
    return jnp.concatenate([ref[pl.ds(j, n, stride=SUB), :] for j in range(SUB)], axis=1)


def _store_rows(ref, val, n):
    for j in range(SUB):
        ref[pl.ds(j, n, stride=SUB), :] = val[:, j * LANES:(j + 1) * LANES]


def _silu(x):
    return x * _sigmoid(x)


def _mod_kernel(cond_ref, w_ref, b_ref, o_ref):
    s = _silu(cond_ref[...]).astype(BF16)
    o_ref[...] = _dot(s, w_ref[...].astype(BF16)) + b_ref[...]


def _modulation(cond8, w_ada, b_ada):
    tn = 768
    n = w_ada.shape[1]
    return pl.pallas_call(
        _mod_kernel,
        grid=(n // tn,),
        in_specs=[pl.BlockSpec((8, D_MODEL), lambda j: (0, 0)),
                  pl.BlockSpec((D_MODEL, tn), lambda j: (0, j)),
                  pl.BlockSpec((1, tn), lambda j: (0, j))],
        out_specs=pl.BlockSpec((8, tn), lambda j: (0, j)),
        out_shape=jax.ShapeDtypeStruct((8, n), F32),
        compiler_params=_params("arbitrary"),
        name="modulation",
    )(cond8, w_ada, b_ada.reshape(1, n))


def _rope(x, cos, sin, half):
    lane = lax.broadcasted_iota(jnp.int32, x.shape, 1)
    first = (lane & (2 * half - 1)) < half
    partner = jnp.where(first, pltpu.roll(x, LANES - half, 1), pltpu.roll(x, half, 1))
    return x * cos + partner * sin


def _rope_tables():
    t = np.arange(DEC_SEQ)
    row = (t // GRID_W).astype(np.float32)
    col = (t % GRID_W).astype(np.float32)

    def tables(d_rot):
        nf = d_rot // 4
        inv = np.float32(ROPE_BASE) ** (-np.arange(nf, dtype=np.float32) / np.float32(nf))
        ar = row[:, None] * inv
        ac = col[:, None] * inv
        cos = np.concatenate([np.cos(ar), np.cos(ar), np.cos(ac), np.cos(ac)], axis=1)
        sin = np.concatenate([-np.sin(ar), np.sin(ar), -np.sin(ac), np.sin(ac)], axis=1)
        return cos.astype(np.float32), sin.astype(np.float32)

    cos_a, sin_a = tables(A_HEAD_DIM)
    cos_a = np.concatenate([cos_a, cos_a], axis=1)
    sin_a = np.concatenate([sin_a, sin_a], axis=1)
    cos_b, sin_b = tables(B_ROPE)
    one = np.ones((DEC_SEQ, B_NOPE), np.float32)
    zero = np.zeros((DEC_SEQ, B_NOPE), np.float32)
    pad = LANES - B_NOPE - B_ROPE
    cos_b = np.concatenate([one, cos_b, one[:, :pad]], axis=1)
    sin_b = np.concatenate([zero, sin_b, zero[:, :pad]], axis=1)
    return tuple(jnp.asarray(a) for a in (cos_a, sin_a, cos_b, sin_b))


_N1 = 1536


def _inproj_kernel(latent, *refs):
    if latent:
        (x_ref, mod_ref, g_ref, w1_ref, qn_ref, kvn_ref, wuq_ref, wk_ref, wv_ref,
         ca_ref, sa_ref, cb_ref, sb_ref,
         qa_o, ka_o, va_o, qb_o, kb_o, vb_o) = refs
    else:
        (x_ref, mod_ref, g_ref, w1_ref, qn_ref, kvn_ref, wuq_ref, wk_ref, wv_ref,
         qa_o, ka_o, va_o, ckv_o, kr_o, qb_o, kb_o, vb_o) = refs
    x = x_ref[...]
    shift = mod_ref[:, 0:D_MODEL]
    scale = mod_ref[:, D_MODEL:2 * D_MODEL]
    h = _rms(x, g_ref[...]) * (1.0 + scale) + shift
    p = _dot(h.astype(BF16), w1_ref[...])
    qa = p[:, 0:512]
    ka = p[:, 512:640]
    va = p[:, 640:768]
    ql = p[:, 768:1152]
    ckv = p[:, 1152:1408]
    krp = p[:, 1408:1536]
    qn = _rms(ql, qn_ref[...])
    qb = _dot(qn.astype(BF16), wuq_ref[...])
    cn = _rms(ckv, kvn_ref[...])
    cnb = cn.astype(BF16)
    kn = _dot(cnb, wk_ref[...])
    vb = _dot(cnb, wv_ref[...])
    if latent:
        ca, sa, cb, sb = ca_ref[...], sa_ref[...], cb_ref[...], sb_ref[...]
        ka = _rope(ka, ca, sa, A_HEAD_DIM // 4)
        krp = _rope(krp, cb, sb, B_ROPE // 4)
        for j in range(4):
            blk = _rope(qa[:, j * LANES:(j + 1) * LANES], ca, sa, A_HEAD_DIM // 4)
            qa_o[:, j * LANES:(j + 1) * LANES] = (blk * (A_HEAD_DIM ** -0.5 * LOG2E)).astype(BF16)
        for hd in range(B_HEADS):
            blk = _rope(qb[:, hd * LANES:(hd + 1) * LANES], cb, sb, B_ROPE // 4)
            qb_o[:, hd * LANES:(hd + 1) * LANES] = (blk * (MLA_SCALE * LOG2E)).astype(BF16)
        ka_o[...] = ka.astype(BF16)
        va_o[...] = va.astype(BF16)
    else:
        qa_o[...] = (qa * (A_HEAD_DIM ** -0.5 * LOG2E)).astype(BF16)
        qb_o[...] = (qb * (MLA_SCALE * LOG2E)).astype(BF16)
        for bb in range(TM // SEQ):
            tok = slice(bb * SEQ, (bb + 1) * SEQ)
            ka_o[bb] = ka[tok, :].T
            va_o[bb] = va[tok, :].T
            kr_o[bb] = krp[tok, :].T[B_NOPE:B_NOPE + B_ROPE, :]
        ckv_o[...] = cn
    for hd in range(B_HEADS):
        kb_o[:, hd * LANES:(hd + 1) * LANES] = (kn[:, hd * LANES:(hd + 1) * LANES] + krp).astype(BF16)
    vb_o[...] = vb.astype(BF16)


def _inproj(x, mod3, mod_row, g, w1, qn, kvn, wuq, wk, wv, tables):
    latent = tables is not None
    t = x.shape[0]
    n_tiles = t // TM
    row = lambda i: (i, 0)
    const = lambda i: (0, 0)
    in_specs = [pl.BlockSpec((TM, D_MODEL), row),
                pl.BlockSpec((None, 1, 6 * D_MODEL), lambda i: (mod_row(i), 0, 0)),
                pl.BlockSpec((1, D_MODEL), const),
                pl.BlockSpec((D_MODEL, _N1), const),
                pl.BlockSpec((1, Q_LORA), const),
                pl.BlockSpec((1, KV_LORA), const),
                pl.BlockSpec((Q_LORA, B_HEADS * LANES), const),
                pl.BlockSpec((KV_LORA, B_HEADS * LANES), const),
                pl.BlockSpec((KV_LORA, B_HEADS * B_V), const)]
    args = [x, mod3, g, w1, qn, kvn, wuq, wk, wv]
    sds = jax.ShapeDtypeStruct
    if latent:
        per = DEC_SEQ // TM
        tab = lambda i: (i % per, 0)
        in_specs += [pl.BlockSpec((TM, LANES), tab)] * 4
        args += list(tables)
        out_shape = [sds((t, 512), BF16), sds((t, LANES), BF16), sds((t, LANES), BF16),
                     sds((t, B_HEADS * LANES), BF16), sds((t, B_HEADS * LANES), BF16),
                     sds((t, B_HEADS * B_V), BF16)]
        widths = [512, LANES, LANES, B_HEADS * LANES, B_HEADS * LANES, B_HEADS * B_V]
    else:
        nb = t // SEQ
        out_shape = [sds((t, 512), BF16), sds((nb, LANES, SEQ), F32), sds((nb, LANES, SEQ), F32),
                     sds((t, KV_LORA), F32), sds((nb, B_ROPE, SEQ), F32),
                     sds((t, B_HEADS * LANES), BF16), sds((t, B_HEADS * LANES), BF16),
                     sds((t, B_HEADS * B_V), BF16)]
        widths = [512, None, None, KV_LORA, None, B_HEADS * LANES, B_HEADS * LANES, B_HEADS * B_V]
    out_specs = [pl.BlockSpec((TM, w), row) if w is not None
                 else pl.BlockSpec((TM // SEQ,) + o.shape[1:], lambda i: (i, 0, 0))
                 for w, o in zip(widths, out_shape)]
    return pl.pallas_call(
        functools.partial(_inproj_kernel, latent),
        grid=(n_tiles,),
        in_specs=in_specs,
        out_specs=out_specs,
        out_shape=out_shape,
        compiler_params=_params("arbitrary"),
        name="inproj_latent" if latent else "inproj_ctx",
    )(*args)


def _kvexp_kernel(ckv_ref, krp_ref, wk_ref, wv_ref, kb_o, vb_o):
    cb = ckv_ref[...].astype(BF16)
    kn = _dot(cb, wk_ref[...])
    krp = krp_ref[...]
    for hd in range(B_HEADS):
        kb_o[:, hd * LANES:(hd + 1) * LANES] = (kn[:, hd * LANES:(hd + 1) * LANES] + krp).astype(BF16)
    vb_o[...] = _dot(cb, wv_ref[...]).astype(BF16)


def _kv_expand(ckv, krp, wk, wv):
    t = ckv.shape[0]
    row = lambda i: (i, 0)
    const = lambda i: (0, 0)
    return pl.pallas_call(
        _kvexp_kernel,
        grid=(t // TM,),
        in_specs=[pl.BlockSpec((TM, KV_LORA), row), pl.BlockSpec((TM, LANES), row),
                  pl.BlockSpec((KV_LORA, B_HEADS * LANES), const),
                  pl.BlockSpec((KV_LORA, B_HEADS * B_V), const)],
        out_specs=[pl.BlockSpec((TM, B_HEADS * LANES), row), pl.BlockSpec((TM, B_HEADS * B_V), row)],
        out_shape=[jax.ShapeDtypeStruct((t, B_HEADS * LANES), BF16),
                   jax.ShapeDtypeStruct((t, B_HEADS * B_V), BF16)],
        compiler_params=_params("arbitrary"),
        name="kv_expand",
    )(ckv, krp, wk, wv)


def _lo_mask(rows):
    return lax.broadcasted_iota(jnp.int32, (rows, LANES), 1) < (LANES // 2)


def _softmax_pv(parts, sink, v_t=False):
    if sink is not None:
        sink = sink * LOG2E
    m = None
    for s, _ in parts:
        mi = jnp.max(s, axis=-1, keepdims=True)
        m = mi if m is None else jnp.maximum(m, mi)
    if sink is not None:
        m = jnp.maximum(m, sink)
    den = None
    acc = None
    for s, v in parts:
        p = jnp.exp2(s - m)
        di = jnp.sum(p, axis=-1, keepdims=True)
        den = di if den is None else den + di
        oi = _dot_nt(p.astype(BF16), v) if v_t else _dot(p.astype(BF16), v)
        acc = oi if acc is None else acc + oi
    if sink is not None:
        den = den + jnp.exp2(sink - m)
    return acc / den


def _ctx_attn_kernel(sink_ref, qa_ref, ka_ref, va_ref, qb_ref, kb_ref, vb_ref, oa_ref, ob_ref):
    lo = _lo_mask(SEQ)
    k = ka_ref[...].astype(BF16)
    v = va_ref[...].astype(BF16)
    for j in range(A_GROUP):
        q = qa_ref[:, j * LANES:(j + 1) * LANES]
        zero = jnp.zeros_like(q)
        o0 = _softmax_pv([(_dot(jnp.where(lo, q, zero), k), v)], sink_ref[j], v_t=True)
        o1 = _softmax_pv([(_dot(jnp.where(lo, zero, q), k), v)], sink_ref[A_GROUP + j], v_t=True)
        oa_ref[:, j * LANES:(j + 1) * LANES] = jnp.where(lo, o0, o1).astype(BF16)
    for j in range(B_HEADS // 2):
        vp = vb_ref[:, j * LANES:(j + 1) * LANES]
        outs = []
        for hd in (2 * j, 2 * j + 1):
            s = _dot_nt(qb_ref[:, hd * LANES:(hd + 1) * LANES], kb_ref[:, hd * LANES:(hd + 1) * LANES])
            outs.append(_softmax_pv([(s, vp)], None))
        ob_ref[:, j * LANES:(j + 1) * LANES] = jnp.where(lo, outs[0], outs[1]).astype(BF16)


def _ctx_attention(sink, qa, ka, va, qb, kb, vb):
    blk = lambda w: pl.BlockSpec((SEQ, w), lambda b: (b, 0))
    kv_t = pl.BlockSpec((None, LANES, SEQ), lambda b: (b, 0, 0))
    return pl.pallas_call(
        _ctx_attn_kernel,
        grid=(BATCH,),
        in_specs=[pl.BlockSpec(memory_space=pltpu.SMEM),
                  blk(512), kv_t, kv_t, blk(B_HEADS * LANES), blk(B_HEADS * LANES),
                  blk(B_HEADS * B_V)],
        out_specs=[blk(512), blk(512)],
        out_shape=[jax.ShapeDtypeStruct((T_CTX, 512), BF16), jax.ShapeDtypeStruct((T_CTX, 512), BF16)],
        compiler_params=_params("arbitrary"),
        name="ctx_attention",
    )(sink, qa, ka, va, qb, kb, vb)


_WIN = TQ + 2 * WINDOW


def _lat_swa_kernel(sink_ref, q_ref, k_ref, v_ref, kc_ref, vc_ref, o_ref):
    qi = pl.program_id(1)
    q0 = qi * TQ
    start = pl.multiple_of(jnp.clip(q0 - WINDOW, 0, DEC_SEQ - _WIN), WINDOW)
    kw = k_ref[pl.ds(start, _WIN), :]
    vw = v_ref[pl.ds(start, _WIN), :]
    kc = kc_ref[...].astype(BF16)
    vc = vc_ref[...].astype(BF16)
    qpos = q0 + lax.broadcasted_iota(jnp.int32, (TQ, _WIN), 0)
    kpos = start + lax.broadcasted_iota(jnp.int32, (TQ, _WIN), 1)
    valid = jnp.abs(qpos - kpos) <= WINDOW
    lo = _lo_mask(TQ)
    for j in range(A_GROUP):
        q = q_ref[:, j * LANES:(j + 1) * LANES]
        zero = jnp.zeros_like(q)
        outs = []
        for g, qh in ((0, jnp.where(lo, q, zero)), (1, jnp.where(lo, zero, q))):
            sw = jnp.where(valid, _dot_nt(qh, kw), NEG)
            sc = _dot_nt(qh, kc)
            outs.append(_softmax_pv([(sw, vw), (sc, vc)], sink_ref[g * A_GROUP + j]))
        o_ref[:, j * LANES:(j + 1) * LANES] = jnp.where(lo, outs[0], outs[1]).astype(BF16)


def _lat_swa(sink, qa, ka, va, kc, vc):
    nq = DEC_SEQ // TQ
    return pl.pallas_call(
        _lat_swa_kernel,
        grid=(DEC_BATCH, nq),
        in_specs=[pl.BlockSpec(memory_space=pltpu.SMEM),
                  pl.BlockSpec((TQ, 512), lambda b, i: (b * nq + i, 0)),
                  pl.BlockSpec((None, DEC_SEQ, LANES), lambda b, i: (b, 0, 0)),
                  pl.BlockSpec((None, DEC_SEQ, LANES), lambda b, i: (b, 0, 0)),
                  pl.BlockSpec((None, PAST_LEN, LANES), lambda b, i: (b, 0, 0)),
                  pl.BlockSpec((None, PAST_LEN, LANES), lambda b, i: (b, 0, 0))],
        out_specs=pl.BlockSpec((TQ, 512), lambda b, i: (b * nq + i, 0)),
        out_shape=jax.ShapeDtypeStruct((T_LAT, 512), BF16),
        compiler_params=_params("arbitrary", "arbitrary"),
        name="latent_swa",
    )(sink, qa, ka.reshape(DEC_BATCH, DEC_SEQ, LANES), va.reshape(DEC_BATCH, DEC_SEQ, LANES), kc, vc)


TQ_MLA = 256


def _lat_mla_kernel(q_ref, kl_ref, vl_ref, kc_ref, vc_ref, o_ref):
    lo = _lo_mask(TQ_MLA)
    vl = vl_ref[...]
    vc = vc_ref[...]
    outs = []
    for hh in range(2):
        q = q_ref[:, hh * LANES:(hh + 1) * LANES]
        sc = _dot_nt(q, kc_ref[:, hh * LANES:(hh + 1) * LANES])
        sl = _dot_nt(q, kl_ref[:, hh * LANES:(hh + 1) * LANES])
        outs.append(_softmax_pv([(sc, vc), (sl, vl)], None))
    o_ref[...] = jnp.where(lo, outs[0], outs[1]).astype(BF16)


def _lat_mla(qb, kb, vb, kc, vc):
    nq = DEC_SEQ // TQ_MLA
    npair = B_HEADS // 2
    return pl.pallas_call(
        _lat_mla_kernel,
        grid=(DEC_BATCH, npair, nq),
        in_specs=[pl.BlockSpec((TQ_MLA, 2 * LANES), lambda b, p, i: (b * nq + i, p)),
                  pl.BlockSpec((None, DEC_SEQ, 2 * LANES), lambda b, p, i: (b, 0, p)),
                  pl.BlockSpec((None, DEC_SEQ, LANES), lambda b, p, i: (b, 0, p)),
                  pl.BlockSpec((None, PAST_LEN, 2 * LANES), lambda b, p, i: (b, 0, p)),
                  pl.BlockSpec((None, PAST_LEN, LANES), lambda b, p, i: (b, 0, p))],
        out_specs=pl.BlockSpec((TQ_MLA, LANES), lambda b, p, i: (b * nq + i, p)),
        out_shape=jax.ShapeDtypeStruct((T_LAT, 512), BF16),
        compiler_params=_params("arbitrary", "arbitrary", "arbitrary"),
        name="latent_mla",
    )(qb, kb.reshape(DEC_BATCH, DEC_SEQ, B_HEADS * LANES), vb.reshape(DEC_BATCH, DEC_SEQ, B_HEADS * B_V),
      kc.reshape(DEC_BATCH, PAST_LEN, B_HEADS * LANES), vc.reshape(DEC_BATCH, PAST_LEN, B_HEADS * B_V))


_N_CTX_TILES = T_CTX // TM


def _merge_kernel(xp_ref, xs_ref, oac_ref, oal_ref, obc_ref, obl_ref, mod_ref,
                  gpre_ref, gpost_ref, gffn_ref, woa_ref, wob_ref, wout_ref, wg_ref, wr_ref,
                  x1_o, h2_o, lg_o):
    def body(x_ref, oa_ref, ob_ref):
        x = x_ref[...]
        sh1 = mod_ref[:, 0:D_MODEL]
        sc1 = mod_ref[:, D_MODEL:2 * D_MODEL]
        g1 = mod_ref[:, 2 * D_MODEL:3 * D_MODEL]
        sh2 = mod_ref[:, 3 * D_MODEL:4 * D_MODEL]
        sc2 = mod_ref[:, 4 * D_MODEL:5 * D_MODEL]
        h = (_rms(x, gpre_ref[...]) * (1.0 + sc1) + sh1).astype(BF16)
        gates = _dot(h, wg_ref[...])
        ya = _dot(oa_ref[...], woa_ref[...])
        yb = _dot(ob_ref[...], wob_ref[...])
        y = _sigmoid(gates[:, :D_MODEL]) * ya + _sigmoid(gates[:, D_MODEL:]) * yb
        mix = _dot(y.astype(BF16), wout_ref[...])
        x1 = x + g1 * _rms(mix, gpost_ref[...])
        x1_o[...] = x1
        h2 = _rms(x1, gffn_ref[...]) * (1.0 + sc2) + sh2
        _store_rows(h2_o, h2, TM)
        lg_o[...] = _dot_nt(wr_ref[...], h2.astype(BF16))

    is_ctx = pl.program_id(0) < _N_CTX_TILES

    @pl.when(is_ctx)
    def _():
        body(xp_ref, oac_ref, obc_ref)

    @pl.when(jnp.logical_not(is_ctx))
    def _():
        body(xs_ref, oal_ref, obl_ref)


def _mod_row_all(i):
    per = DEC_SEQ // TM
    return jnp.where(i < _N_CTX_TILES, 0, 1 + (jnp.maximum(i - _N_CTX_TILES, 0)) // per)


def _merge(xp, xs, oac, oal, obc, obl, mod3, gpre, gpost, gffn, woa, wob, wout, wg, wr):
    ctx = lambda i: (jnp.minimum(i, _N_CTX_TILES - 1), 0)
    lat = lambda i: (jnp.maximum(i - _N_CTX_TILES, 0), 0)
    row = lambda i: (i, 0)
    const = lambda i: (0, 0)
    return pl.pallas_call(
        _merge_kernel,
        grid=(T_ALL // TM,),
        in_specs=[pl.BlockSpec((TM, D_MODEL), ctx), pl.BlockSpec((TM, D_MODEL), lat),
                  pl.BlockSpec((TM, 512), ctx), pl.BlockSpec((TM, 512), lat),
                  pl.BlockSpec((TM, 512), ctx), pl.BlockSpec((TM, 512), lat),
                  pl.BlockSpec((None, 1, 6 * D_MODEL), lambda i: (_mod_row_all(i), 0, 0)),
                  pl.BlockSpec((1, D_MODEL), const), pl.BlockSpec((1, D_MODEL), const),
                  pl.BlockSpec((1, D_MODEL), const),
                  pl.BlockSpec((512, D_MODEL), const), pl.BlockSpec((512, D_MODEL), const),
                  pl.BlockSpec((D_MODEL, D_MODEL), const), pl.BlockSpec((D_MODEL, 2 * D_MODEL), const),
                  pl.BlockSpec((N_EXPERTS, D_MODEL), const)],
        out_specs=[pl.BlockSpec((TM, D_MODEL), row), pl.BlockSpec((TM * SUB, LANES), row),
                   pl.BlockSpec((N_EXPERTS, TM), lambda i: (0, i))],
        out_shape=[jax.ShapeDtypeStruct((T_ALL, D_MODEL), F32), jax.ShapeDtypeStruct((T_ALL * SUB, LANES), F32),
                   jax.ShapeDtypeStruct((N_EXPERTS, T_ALL), F32)],
        compiler_params=_params("arbitrary"),
        name="merge",
    )(xp, xs, oac, oal, obc, obl, mod3, gpre, gpost, gffn, woa, wob, wout, wg, wr)


_PER_GROUP = N_EXPERTS // N_GROUPS


def _router_kernel(lg_ref, bias_ref, topi_o, rank_o, topw_o, cnt_o, cnt_s):
    @pl.when(pl.program_id(0) == 0)
    def _():
        cnt_s[...] = jnp.zeros_like(cnt_s)

    scores = _sigmoid(lg_ref[...])
    sel = scores + bias_ref[...]
    eidx = lax.broadcasted_iota(jnp.int32, (N_EXPERTS, TM), 0).astype(F32)
    ninf = -jnp.inf
    big = float(N_EXPERTS)
    sel_g = [sel[g * _PER_GROUP:(g + 1) * _PER_GROUP, :] for g in range(N_GROUPS)]
    gs = []
    for sg in sel_g:
        m1 = jnp.max(sg, axis=0, keepdims=True)
        top = sg == m1
        n_top = jnp.sum(jnp.where(top, 1.0, 0.0), axis=0, keepdims=True)
        rest = jnp.max(jnp.where(top, ninf, sg), axis=0, keepdims=True)
        gs.append(m1 + jnp.where(n_top > 1.5, m1, rest))
    cur_g = []
    for g in range(N_GROUPS):
        beat = jnp.zeros((1, TM), F32)
        for g2 in range(N_GROUPS):
            if g2 == g:
                continue
            better = (gs[g2] >= gs[g]) if g2 < g else (gs[g2] > gs[g])
            beat = beat + jnp.where(better, 1.0, 0.0)
        cur_g.append(jnp.where(beat < TOPK_GROUPS, sel_g[g], NEG))
    cur = jnp.concatenate(cur_g, axis=0)
    chosen = jnp.zeros((N_EXPERTS, TM), F32)
    idxs, ws = [], []
    for k in range(TOP_K):
        m = jnp.max(cur, axis=0, keepdims=True)
        idx = jnp.min(jnp.where(cur == m, eidx, big), axis=0, keepdims=True)
        hit = eidx == idx
        ws.append(jnp.sum(jnp.where(hit, scores, 0.0), axis=0, keepdims=True))
        cur = jnp.where(hit, ninf, cur)
        chosen = jnp.where(hit, 1.0, chosen)
        idxs.append(idx)
    wsum = ws[0]
    for k in range(1, TOP_K):
        wsum = wsum + ws[k]
    for k in range(TOP_K):
        topw_o[k:k + 1, :] = ws[k] / wsum * ROUTED_SCALE
        topi_o[k:k + 1, :] = idxs[k].astype(jnp.int32)
    r_i = lax.broadcasted_iota(jnp.int32, (TM, TM), 0)
    c_i = lax.broadcasted_iota(jnp.int32, (TM, TM), 1)
    earlier = jnp.where(r_i < c_i, 1.0, 0.0).astype(BF16)
    prefix = _dot(chosen.astype(BF16), earlier) + cnt_s[:, 0:1]
    for k in range(TOP_K):
        r = jnp.sum(jnp.where(eidx == idxs[k], prefix, 0.0), axis=0, keepdims=True)
        rank_o[k:k + 1, :] = r.astype(jnp.int32)
    total = cnt_s[...] + jnp.sum(chosen, axis=1, keepdims=True)
    cnt_s[...] = total
    cnt_o[...] = total


def _router(logits_t, bias_col):
    col = lambda i: (0, i)
    const = lambda i: (0, 0)
    return pl.pallas_call(
        _router_kernel,
        grid=(T_ALL // TM,),
        in_specs=[pl.BlockSpec((N_EXPERTS, TM), col), pl.BlockSpec((N_EXPERTS, 1), const)],
        out_specs=[pl.BlockSpec((TOP_K, TM), col), pl.BlockSpec((TOP_K, TM), col),
                   pl.BlockSpec((TOP_K, TM), col), pl.BlockSpec((N_EXPERTS, LANES), const)],
        out_shape=[jax.ShapeDtypeStruct((TOP_K, T_ALL), jnp.int32),
                   jax.ShapeDtypeStruct((TOP_K, T_ALL), jnp.int32),
                   jax.ShapeDtypeStruct((TOP_K, T_ALL), F32),
                   jax.ShapeDtypeStruct((N_EXPERTS, LANES), F32)],
        scratch_shapes=[pltpu.VMEM((N_EXPERTS, LANES), F32)],
        compiler_params=_params("arbitrary"),
        name="router",
    )(logits_t, bias_col)


def _dest_kernel(topi_ref, rank_ref, ps_ref, o_ref):
    eidx = lax.broadcasted_iota(jnp.int32, (N_EXPERTS, TM), 0)
    ps = jnp.broadcast_to(ps_ref[...], (N_EXPERTS, TM))
    for k in range(TOP_K):
        start = jnp.sum(jnp.where(eidx == topi_ref[k:k + 1, :], ps, 0.0), axis=0, keepdims=True)
        o_ref[k:k + 1, :] = rank_ref[k:k + 1, :] + start.astype(jnp.int32)


def _dest(topi, rank, pstart_col):
    col = lambda i: (0, i)
    return pl.pallas_call(
        _dest_kernel,
        grid=(T_ALL // TM,),
        in_specs=[pl.BlockSpec((TOP_K, TM), col), pl.BlockSpec((TOP_K, TM), col),
                  pl.BlockSpec((N_EXPERTS, 1), lambda i: (0, 0))],
        out_specs=pl.BlockSpec((TOP_K, TM), col),
        out_shape=jax.ShapeDtypeStruct((TOP_K, T_ALL), jnp.int32),
        compiler_params=_params("arbitrary"),
        name="dest",
    )(topi, rank, pstart_col)


TD = 512
DISPATCH_PARTS = 4
ROW_UNROLL = 8


def _dispatch_kernel(dest_ref, h_ref, wgu_ref, wd_ref, xs_hbm, sh_o, sem):
    def row(t, carry):
        src = h_ref.at[pl.ds(pl.multiple_of(t * SUB, SUB), SUB)]
        for k in range(TOP_K):
            d = dest_ref[t * TOP_K + k]
            pltpu.make_async_copy(src, xs_hbm.at[pl.ds(pl.multiple_of(d * SUB, SUB), SUB)],
                                  sem).start(priority=k % 2)
        return carry

    part = TD // DISPATCH_PARTS
    for p in range(DISPATCH_PARTS):
        lax.fori_loop(p * part, (p + 1) * part, row, 0, unroll=ROW_UNROLL)
        x = _load_rows(h_ref.at[pl.ds(p * part * SUB, part * SUB)], part).astype(BF16)
        gu = _dot(x, wgu_ref[...])
        a = _silu(gu[:, :D_SHARED]) * gu[:, D_SHARED:]
        sh_o[p * part:(p + 1) * part, :] = _dot(a.astype(BF16), wd_ref[...])
    for _ in range(TOP_K):
        pltpu.make_async_copy(h_ref, xs_hbm.at[pl.ds(0, TD * SUB)], sem).wait()


def _dispatch(dest_flat, h2, wgu, wd):
    const = lambda i: (0, 0)
    return pl.pallas_call(
        _dispatch_kernel,
        grid=(T_ALL // TD,),
        in_specs=[pl.BlockSpec((TD * TOP_K,), lambda i: (i,), memory_space=pltpu.SMEM),
                  pl.BlockSpec((TD * SUB, LANES), lambda i: (i, 0)),
                  pl.BlockSpec((D_MODEL, 2 * D_SHARED), const), pl.BlockSpec((D_SHARED, D_MODEL), const)],
        out_specs=[pl.BlockSpec(memory_space=pl.ANY), pl.BlockSpec((TD, D_MODEL), lambda i: (i, 0))],
        out_shape=[jax.ShapeDtypeStruct((N_SLOTS * SUB, LANES), F32),
                   jax.ShapeDtypeStruct((T_ALL, D_MODEL), F32)],
        scratch_shapes=[pltpu.SemaphoreType.DMA(())],
        compiler_params=_params("arbitrary"),
        name="dispatch",
    )(dest_flat, h2, wgu, wd)


_BLK_TILES = MOE_BLK * SUB
NBX = 6
PREFETCH = NBX - 2
NBO = 4
BLK_QUANT = 32


def _expert_kernel(bs_ref, vb_ref, x_hbm, wg_ref, wu_ref, wd_ref, o_hbm, wgu_s, wd_s, xbuf, obuf, xsem, osem):
    e = pl.program_id(0)
    b0 = bs_ref[e]
    b1 = bs_ref[e + 1]
    n_used = bs_ref[N_EXPERTS]

    def pieces(g, make, act):
        v = vb_ref[g]
        base = pl.multiple_of(g * _BLK_TILES, _BLK_TILES)

        @pl.when(v == MOE_BLK)
        def _():
            act(make(base, 0, _BLK_TILES))

        p = MOE_BLK // 2
        while p >= BLK_QUANT:
            @pl.when((v < MOE_BLK) & ((v & p) != 0))
            def _(p=p):
                off = pl.multiple_of((v & (MOE_BLK - 2 * p)) * SUB, p * SUB)
                act(make(base, off, p * SUB))
            p //= 2

    def xmake(g):
        s = lax.rem(g, NBX)
        return lambda base, off, n: pltpu.make_async_copy(
            x_hbm.at[pl.ds(base + off, n)], xbuf.at[s, pl.ds(off, n)], xsem.at[s])

    def omake(g):
        s = lax.rem(g, NBO)
        return lambda base, off, n: pltpu.make_async_copy(
            obuf.at[s, pl.ds(off, n)], o_hbm.at[pl.ds(base + off, n)], osem.at[s])

    start_x = lambda c: c.start()
    start_o = lambda c: c.start(priority=1)
    wait = lambda c: c.wait()

    @pl.when(e == 0)
    def _():
        xbuf[...] = jnp.zeros_like(xbuf)
        for j in range(PREFETCH):
            @pl.when(j < n_used)
            def _():
                pieces(j, xmake(j), start_x)

    @pl.when(b1 > b0)
    def _():
        wgu_s[:, 0:D_EXPERT] = wg_ref[...].astype(BF16)
        wgu_s[:, D_EXPERT:2 * D_EXPERT] = wu_ref[...].astype(BF16)
        wd_s[...] = wd_ref[...].astype(BF16)

    def acquire(g):
        @pl.when(g + PREFETCH < n_used)
        def _():
            pieces(g + PREFETCH, xmake(g + PREFETCH), start_x)

        pieces(g, xmake(g), wait)

        @pl.when(g >= NBO)
        def _():
            pieces(g - NBO, omake(g - NBO), wait)

    def compute(g):
        gu = _dot(_load_rows(xbuf.at[lax.rem(g, NBX)], MOE_BLK).astype(BF16), wgu_s[...])
        a = _silu(gu[:, :D_EXPERT]) * gu[:, D_EXPERT:]
        _store_rows(obuf.at[lax.rem(g, NBO)], _dot(a.astype(BF16), wd_s[...]), MOE_BLK)

    def pair(i, carry):
        g = b0 + 2 * i
        acquire(g)
        acquire(g + 1)
        compute(g)
        compute(g + 1)
        pieces(g, omake(g), start_o)
        pieces(g + 1, omake(g + 1), start_o)
        return carry

    lax.fori_loop(0, lax.shift_right_logical(b1 - b0, 1), pair, 0)

    @pl.when(((b1 - b0) & 1) == 1)
    def _():
        acquire(b1 - 1)
        compute(b1 - 1)
        pieces(b1 - 1, omake(b1 - 1), start_o)

    @pl.when(e == N_EXPERTS - 1)
    def _():
        for j in range(NBO, 0, -1):
            @pl.when(n_used >= j)
            def _():
                pieces(n_used - j, omake(n_used - j), wait)


def _experts(blk_start, blk_rows, x_sorted, w_gate, w_up, w_down):
    wsel = lambda e, bs, vb: (e, 0, 0)
    grid_spec = pltpu.PrefetchScalarGridSpec(
        num_scalar_prefetch=2,
        grid=(N_EXPERTS,),
        in_specs=[pl.BlockSpec(memory_space=pl.ANY),
                  pl.BlockSpec((None, D_MODEL, D_EXPERT), wsel),
                  pl.BlockSpec((None, D_MODEL, D_EXPERT), wsel),
                  pl.BlockSpec((None, D_EXPERT, D_MODEL), wsel)],
        out_specs=pl.BlockSpec(memory_space=pl.ANY),
        scratch_shapes=[pltpu.VMEM((D_MODEL, 2 * D_EXPERT), BF16), pltpu.VMEM((D_EXPERT, D_MODEL), BF16),
                        pltpu.VMEM((NBX, _BLK_TILES, LANES), F32), pltpu.VMEM((NBO, _BLK_TILES, LANES), F32),
                        pltpu.SemaphoreType.DMA((NBX,)), pltpu.SemaphoreType.DMA((NBO,))],
    )
    return pl.pallas_call(
        _expert_kernel,
        grid_spec=grid_spec,
        out_shape=jax.ShapeDtypeStruct((N_SLOTS * SUB, LANES), F32),
        compiler_params=_params("arbitrary"),
        name="experts",
    )(blk_start, blk_rows, x_sorted, w_gate, w_up, w_down)


TC = 256
CR = 16


def _combine_kernel(dcur_ref, dnext_ref, eo_hbm, x1_ref, sh_ref, w_ref, mod_ref, g_ref, o_ref, buf, sem, wb_s):
    i = pl.program_id(0)
    n = pl.num_programs(0)
    slot = lax.rem(i, 2)

    def gather(d_ref, s, t0, t1):
        def row(t, carry):
            for k in range(TOP_K):
                d = d_ref[t * TOP_K + k]
                pltpu.make_async_copy(eo_hbm.at[pl.ds(pl.multiple_of(d * SUB, SUB), SUB)],
                                      buf.at[s, k, pl.ds(pl.multiple_of(t * SUB, SUB), SUB)],
                                      sem.at[s]).start(priority=k % 2)
            return carry
        lax.fori_loop(t0, t1, row, 0, unroll=ROW_UNROLL)

    @pl.when(i == 0)
    def _():
        gather(dcur_ref, 0, 0, TC)

    @pl.when(i + 1 < n)
    def _():
        gather(dnext_ref, 1 - slot, 0, TC)

    gate = mod_ref[:, 5 * D_MODEL:6 * D_MODEL]
    for k in range(TOP_K):
        wb_s[k] = jnp.broadcast_to(w_ref[:, k:k + 1], (TC, LANES))
    for k in range(TOP_K):
        pltpu.make_async_copy(eo_hbm.at[pl.ds(0, TC * SUB)], buf.at[slot, k], sem.at[slot]).wait()

    lanes = [slice(j * LANES, (j + 1) * LANES) for j in range(SUB)]

    def chunk(r, carry):
        r0 = pl.multiple_of(r * CR, CR)
        rows = pl.ds(r0, CR)
        wk = [wb_s[k, rows, :] for k in range(TOP_K)]
        pieces = []
        sq = jnp.zeros((CR, LANES), F32)
        for j in range(SUB):
            acc = sh_ref[rows, lanes[j]]
            for k in range(TOP_K):
                acc = acc + buf[slot, k, pl.ds(r0 * SUB + j, CR, stride=SUB), :] * wk[k]
            pieces.append(acc)
            sq = sq + acc * acc
        inv = lax.rsqrt(jnp.sum(sq, axis=-1, keepdims=True) / D_MODEL + EPS)
        for j in range(SUB):
            o_ref[rows, lanes[j]] = (x1_ref[rows, lanes[j]]
                                     + gate[:, lanes[j]] * (pieces[j] * inv * g_ref[:, lanes[j]]))
        return carry

    lax.fori_loop(0, TC // CR, chunk, 0, unroll=8)


def _combine(dest_flat, eo, x1, shared, topw, mod3, mod_row, g, tile0, n_tiles):
    row = lambda i: (tile0 + i, 0)
    const = lambda i: (0, 0)
    blk = TC * TOP_K
    return pl.pallas_call(
        _combine_kernel,
        grid=(n_tiles,),
        in_specs=[pl.BlockSpec((blk,), lambda i: (tile0 + i,), memory_space=pltpu.SMEM),
                  pl.BlockSpec((blk,), lambda i: (tile0 + jnp.minimum(i + 1, n_tiles - 1),),
                               memory_space=pltpu.SMEM),
                  pl.BlockSpec(memory_space=pl.ANY),
                  pl.BlockSpec((TC, D_MODEL), row), pl.BlockSpec((TC, D_MODEL), row),
                  pl.BlockSpec((TC, LANES), row),
                  pl.BlockSpec((None, 1, 6 * D_MODEL), lambda i: (mod_row(i), 0, 0)),
                  pl.BlockSpec((1, D_MODEL), const)],
        out_specs=pl.BlockSpec((TC, D_MODEL), lambda i: (i, 0)),
        out_shape=jax.ShapeDtypeStruct((n_tiles * TC, D_MODEL), F32),
        scratch_shapes=[pltpu.VMEM((2, TOP_K, TC * SUB, LANES), F32), pltpu.SemaphoreType.DMA((2,)),
                        pltpu.VMEM((TOP_K, TC, LANES), F32)],
        compiler_params=_params("arbitrary"),
        name="combine",
    )(dest_flat, dest_flat, eo, x1, shared, topw, mod3, g)


def _qa_perm():
    cols = []
    for j in range(A_GROUP):
        for g in range(A_KV_HEADS):
            hd = g * A_GROUP + j
            cols.extend(range(hd * A_HEAD_DIM, (hd + 1) * A_HEAD_DIM))
    return np.asarray(cols, np.int32)


def _pad_heads(w, width, n_heads, offset, total=LANES):
    k = w.shape[0]
    stride = w.shape[1] // n_heads
    w3 = w.reshape(k, n_heads, stride)[:, :, offset:offset + width]
    w3 = jnp.pad(w3, ((0, 0), (0, 0), (0, total - width)))
    return w3.reshape(k, n_heads * total)


def kernel(x_prompt, x_sample, cache_k_swa, cache_v_swa, cache_ckv_mla, cache_krope_mla, c, c_ctx, w_ada, b_ada, norm_pre_mix, norm_post_mix, norm_pre_ffn, norm_post_ffn, w_in, sink_swa, q_norm_mla, kv_norm_mla, w_uq_mla, w_ukv_mla, w_o_swa, w_o_mla, w_out, router_w, router_bias, w_gate_exp, w_up_exp, w_down_exp, w_gate_sh, w_up_sh, w_down_sh):
    l = 0
    perm = _qa_perm()
    wi = w_in[l]
    kr_blk = jnp.pad(wi[:, _O_KR:_O_GATE], ((0, 0), (B_NOPE, LANES - B_NOPE - B_ROPE)))
    w1 = jnp.concatenate([wi[:, _O_QA:_O_KA][:, perm], wi[:, _O_KA:_O_KR], kr_blk], axis=1).astype(BF16)
    wg = wi[:, _O_GATE:].astype(BF16)
    wuq = _pad_heads(w_uq_mla[l], B_NOPE + B_ROPE, B_HEADS, 0).astype(BF16)
    wk = _pad_heads(w_ukv_mla[l], B_NOPE, B_HEADS, 0).astype(BF16)
    wv = w_ukv_mla[l].reshape(KV_LORA, B_HEADS, B_NOPE + B_V)[:, :, B_NOPE:].reshape(KV_LORA, B_HEADS * B_V).astype(BF16)
    woa = w_o_swa[l][perm, :].astype(BF16)
    wob = w_o_mla[l].astype(BF16)
    wout = w_out[l].astype(BF16)
    wr = router_w[l].T.astype(BF16)
    wgu_sh = jnp.concatenate([w_gate_sh[l], w_up_sh[l]], axis=1).astype(BF16)
    wd_sh = w_down_sh[l].astype(BF16)
    sink = sink_swa[l]
    row2 = lambda v: v.reshape(1, -1)

    cond8 = jnp.concatenate([c_ctx[None, :], c, jnp.zeros((8 - 1 - DEC_BATCH, D_MODEL), F32)], axis=0)
    mod3 = _modulation(cond8, w_ada[l], b_ada[l]).reshape(8, 1, 6 * D_MODEL)

    xp = x_prompt.reshape(T_CTX, D_MODEL)
    xs = x_sample.reshape(T_LAT, D_MODEL)
    per = DEC_SEQ // TM
    shared_in = (row2(norm_pre_mix[l]), w1, row2(q_norm_mla[l]), row2(kv_norm_mla[l]), wuq, wk, wv)
    qa_c, ka_c, va_c, ckv_c, kr_c, qb_c, kb_c, vb_c = _inproj(xp, mod3, lambda i: 0, *shared_in, None)
    qa_l, ka_l, va_l, qb_l, kb_l, vb_l = _inproj(xs, mod3, lambda i: 1 + i // per, *shared_in, _rope_tables())

    oa_c, ob_c = _ctx_attention(sink, qa_c, ka_c, va_c, qb_c, kb_c, vb_c)

    krp_cache = jnp.pad(cache_krope_mla[:, l].reshape(DEC_BATCH * PAST_LEN, B_ROPE),
                        ((0, 0), (B_NOPE, LANES - B_NOPE - B_ROPE)))
    kc_b, vc_b = _kv_expand(cache_ckv_mla[:, l].reshape(DEC_BATCH * PAST_LEN, KV_LORA), krp_cache, wk, wv)
    oa_l = _lat_swa(sink, qa_l, ka_l, va_l,
                    cache_k_swa[:, l].reshape(DEC_BATCH, PAST_LEN, LANES),
                    cache_v_swa[:, l].reshape(DEC_BATCH, PAST_LEN, LANES))
    ob_l = _lat_mla(qb_l, kb_l, vb_l, kc_b, vc_b)

    x1, h2, logits = _merge(xp, xs, oa_c, oa_l, ob_c, ob_l, mod3,
                            row2(norm_pre_mix[l]), row2(norm_post_mix[l]), row2(norm_pre_ffn[l]),
                            woa, wob, wout, wg, wr)

    topi8, rank8, topw8, counts_b = _router(logits, router_bias[l].reshape(N_EXPERTS, 1))
    counts = counts_b[:, 0].astype(jnp.int32)

    padded = (counts + MOE_BLK - 1) // MOE_BLK * MOE_BLK
    pend = jnp.cumsum(padded)
    pstart = pend - padded
    dest = _dest(topi8, rank8, pstart.astype(F32).reshape(N_EXPERTS, 1)).T.reshape(-1)
    topw_p = jnp.pad(topw8.T, ((0, 0), (0, LANES - TOP_K)))
    blk_start = (jnp.concatenate([pstart, pend[-1:]]) // MOE_BLK).astype(jnp.int32)
    blk_id = jnp.arange(N_SLOT_BLK, dtype=jnp.int32)
    blk_owner = jnp.minimum(jnp.sum(blk_id[:, None] >= blk_start[None, 1:], axis=1), N_EXPERTS - 1)
    left = counts[blk_owner] - (blk_id - blk_start[blk_owner]) * MOE_BLK
    blk_rows = jnp.clip((left + BLK_QUANT - 1) // BLK_QUANT * BLK_QUANT, 0, MOE_BLK).astype(jnp.int32)

    x_sorted, shared = _dispatch(dest, h2, wgu_sh, wd_sh)
    eo = _experts(blk_start, blk_rows, x_sorted, w_gate_exp[l], w_up_exp[l], w_down_exp[l])

    g_post = row2(norm_post_ffn[l])
    per_c = DEC_SEQ // TC
    y_p = _combine(dest, eo, x1, shared, topw_p, mod3, lambda i: 0, g_post, 0, T_CTX // TC)
    y_s = _combine(dest, eo, x1, shared, topw_p, mod3, lambda i: 1 + i // per_c, g_post,
                   T_CTX // TC, T_LAT // TC)

    return (y_p.reshape(BATCH, SEQ, D_MODEL), y_s.reshape(DEC_BATCH, DEC_SEQ, D_MODEL),
            ka_c.reshape(BATCH, 1, A_KV_HEADS, A_HEAD_DIM, SEQ).transpose(0, 1, 4, 2, 3),
            va_c.reshape(BATCH, 1, A_KV_HEADS, A_HEAD_DIM, SEQ).transpose(0, 1, 4, 2, 3),
            ckv_c.reshape(BATCH, 1, SEQ, KV_LORA),
            kr_c.reshape(BATCH, 1, B_ROPE, SEQ).transpose(0, 1, 3, 2))
```

```python
import functools

import jax
import jax.numpy as jnp
import numpy as np
from jax import lax
from jax.experimental import pallas as pl
from jax.experimental.pallas import tpu as pltpu

D_MODEL = 1024
BATCH = 32
SEQ = 256
DEC_BATCH = 4
DEC_SEQ = 2048
PAST_LEN = 512
GRID_W = 64
WINDOW = 128
ROPE_BASE = 10000.0
EPS = 1e-6
NEG = -1e30
A_HEADS = 8
A_KV_HEADS = 2
A_GROUP = A_HEADS // A_KV_HEADS
A_HEAD_DIM = 64
B_HEADS = 8
B_NOPE = 64
B_ROPE = 32
B_V = 64
KV_LORA = 256
Q_LORA = 384
MLA_SCALE = (B_NOPE + B_ROPE) ** -0.5
LOG2E = 1.4426950408889634
N_EXPERTS = 256
TOP_K = 8
N_GROUPS = 8
TOPK_GROUPS = 4
D_EXPERT = 256
D_SHARED = 256
ROUTED_SCALE = 2.5

T_CTX = BATCH * SEQ
T_LAT = DEC_BATCH * DEC_SEQ
T_ALL = T_CTX + T_LAT

LANES = 128
TM = 512
TQ = 256
MOE_BLK = 256
N_SLOT_BLK = (T_ALL * TOP_K) // MOE_BLK + N_EXPERTS
N_SLOTS = N_SLOT_BLK * MOE_BLK
VMEM_LIMIT = 48 * 1024 * 1024

F32 = jnp.float32
BF16 = jnp.bfloat16

_O_QA = 0
_O_KA = _O_QA + A_HEADS * A_HEAD_DIM
_O_VA = _O_KA + A_KV_HEADS * A_HEAD_DIM
_O_QL = _O_VA + A_KV_HEADS * A_HEAD_DIM
_O_CKV = _O_QL + Q_LORA
_O_KR = _O_CKV + KV_LORA
_O_GATE = _O_KR + B_ROPE


def _params(*sem):
    return pltpu.CompilerParams(dimension_semantics=sem, vmem_limit_bytes=VMEM_LIMIT)


def _dot(a, b):
    return jnp.dot(a, b, preferred_element_type=F32)


def _dot_nt(a, b):
    return lax.dot_general(a, b, (((1,), (1,)), ((), ())), preferred_element_type=F32)


def _rms(x, g):
    return x * lax.rsqrt(jnp.mean(x * x, axis=-1, keepdims=True) + EPS) * g


def _sigmoid(x):
    return 1.0 / (1.0 + jnp.exp(-x))


SUB = 8
assert SUB * LANES == D_MODEL


def _load_rows(ref, n):
    return jnp.concatenate([ref[pl.ds(j, n, stride=SUB), :] for j in range(SUB)], axis=1)


def _store_rows(ref, val, n):
    for j in range(SUB):
        ref[pl.ds(j, n, stride=SUB), :] = val[:, j * LANES:(j + 1) * LANES]


def _silu(x):
    return x * _sigmoid(x)


def _mod_kernel(cond_ref, w_ref, b_ref, o_ref):
    s = _silu(cond_ref[...]).astype(BF16)
    o_ref[...] = _dot(s, w_ref[...].astype(BF16)) + b_ref[...]


def _modulation(cond8, w_ada, b_ada):
    tn = 768
    n = w_ada.shape[1]
    return pl.pallas_call(
        _mod_kernel,
        grid=(n // tn,),
        in_specs=[pl.BlockSpec((8, D_MODEL), lambda j: (0, 0)),
                  pl.BlockSpec((D_MODEL, tn), lambda j: (0, j)),
                  pl.BlockSpec((1, tn), lambda j: (0, j))],
        out_specs=pl.BlockSpec((8, tn), lambda j: (0, j)),
        out_shape=jax.ShapeDtypeStruct((8, n), F32),
        compiler_params=_params("arbitrary"),
        name="modulation",
    )(cond8, w_ada, b_ada.reshape(1, n))


def _rope(x, cos, sin, half):
    lane = lax.broadcasted_iota(jnp.int32, x.shape, 1)
    first = (lane & (2 * half - 1)) < half
    partner = jnp.where(first, pltpu.roll(x, LANES - half, 1), pltpu.roll(x, half, 1))
    return x * cos + partner * sin


def _rope_tables():
    t = np.arange(DEC_SEQ)
    row = (t // GRID_W).astype(np.float32)
    col = (t % GRID_W).astype(np.float32)

    def tables(d_rot):
        nf = d_rot // 4
        inv = np.float32(ROPE_BASE) ** (-np.arange(nf, dtype=np.float32) / np.float32(nf))
        ar = row[:, None] * inv
        ac = col[:, None] * inv
        cos = np.concatenate([np.cos(ar), np.cos(ar), np.cos(ac), np.cos(ac)], axis=1)
        sin = np.concatenate([-np.sin(ar), np.sin(ar), -np.sin(ac), np.sin(ac)], axis=1)
        return cos.astype(np.float32), sin.astype(np.float32)

    cos_a, sin_a = tables(A_HEAD_DIM)
    cos_a = np.concatenate([cos_a, cos_a], axis=1)
    sin_a = np.concatenate([sin_a, sin_a], axis=1)
    cos_b, sin_b = tables(B_ROPE)
    one = np.ones((DEC_SEQ, B_NOPE), np.float32)
    zero = np.zeros((DEC_SEQ, B_NOPE), np.float32)
    pad = LANES - B_NOPE - B_ROPE
    cos_b = np.concatenate([one, cos_b, one[:, :pad]], axis=1)
    sin_b = np.concatenate([zero, sin_b, zero[:, :pad]], axis=1)
    return tuple(jnp.asarray(a) for a in (cos_a, sin_a, cos_b, sin_b))


_N1 = 1536


def _inproj_kernel(latent, *refs):
    if latent:
        (x_ref, mod_ref, g_ref, w1_ref, qn_ref, kvn_ref, wuq_ref, wk_ref, wv_ref,
         ca_ref, sa_ref, cb_ref, sb_ref,
         qa_o, ka_o, va_o, qb_o, kb_o, vb_o) = refs
    else:
        (x_ref, mod_ref, g_ref, w1_ref, qn_ref, kvn_ref, wuq_ref, wk_ref, wv_ref,
         qa_o, ka_o, va_o, ckv_o, kr_o, qb_o, kb_o, vb_o) = refs
    x = x_ref[...]
    shift = mod_ref[:, 0:D_MODEL]
    scale = mod_ref[:, D_MODEL:2 * D_MODEL]
    h = _rms(x, g_ref[...]) * (1.0 + scale) + shift
    p = _dot(h.astype(BF16), w1_ref[...])
    qa = p[:, 0:512]
    ka = p[:, 512:640]
    va = p[:, 640:768]
    ql = p[:, 768:1152]
    ckv = p[:, 1152:1408]
    krp = p[:, 1408:1536]
    qn = _rms(ql, qn_ref[...])
    qb = _dot(qn.astype(BF16), wuq_ref[...])
    cn = _rms(ckv, kvn_ref[...])
    cnb = cn.astype(BF16)
    kn = _dot(cnb, wk_ref[...])
    vb = _dot(cnb, wv_ref[...])
    if latent:
        ca, sa, cb, sb = ca_ref[...], sa_ref[...], cb_ref[...], sb_ref[...]
        ka = _rope(ka, ca, sa, A_HEAD_DIM // 4)
        krp = _rope(krp, cb, sb, B_ROPE // 4)
        for j in range(4):
            blk = _rope(qa[:, j * LANES:(j + 1) * LANES], ca, sa, A_HEAD_DIM // 4)
            qa_o[:, j * LANES:(j + 1) * LANES] = (blk * (A_HEAD_DIM ** -0.5 * LOG2E)).astype(BF16)
        for hd in range(B_HEADS):
            blk = _rope(qb[:, hd * LANES:(hd + 1) * LANES], cb, sb, B_ROPE // 4)
            qb_o[:, hd * LANES:(hd + 1) * LANES] = (blk * (MLA_SCALE * LOG2E)).astype(BF16)
        ka_o[...] = ka.astype(BF16)
        va_o[...] = va.astype(BF16)
    else:
        qa_o[...] = (qa * (A_HEAD_DIM ** -0.5 * LOG2E)).astype(BF16)
        qb_o[...] = (qb * (MLA_SCALE * LOG2E)).astype(BF16)
        for bb in range(TM // SEQ):
            tok = slice(bb * SEQ, (bb + 1) * SEQ)
            ka_o[bb] = ka[tok, :].T
            va_o[bb] = va[tok, :].T
            kr_o[bb] = krp[tok, :].T[B_NOPE:B_NOPE + B_ROPE, :]
        ckv_o[...] = cn
    for hd in range(B_HEADS):
        kb_o[:, hd * LANES:(hd + 1) * LANES] = (kn[:, hd * LANES:(hd + 1) * LANES] + krp).astype(BF16)
    vb_o[...] = vb.astype(BF16)


def _inproj(x, mod3, mod_row, g, w1, qn, kvn, wuq, wk, wv, tables):
    latent = tables is not None
    t = x.shape[0]
    n_tiles = t // TM
    row = lambda i: (i, 0)
    const = lambda i: (0, 0)
    in_specs = [pl.BlockSpec((TM, D_MODEL), row),
                pl.BlockSpec((None, 1, 6 * D_MODEL), lambda i: (mod_row(i), 0, 0)),
                pl.BlockSpec((1, D_MODEL), const),
                pl.BlockSpec((D_MODEL, _N1), const),
                pl.BlockSpec((1, Q_LORA), const),
                pl.BlockSpec((1, KV_LORA), const),
                pl.BlockSpec((Q_LORA, B_HEADS * LANES), const),
                pl.BlockSpec((KV_LORA, B_HEADS * LANES), const),
                pl.BlockSpec((KV_LORA, B_HEADS * B_V), const)]
    args = [x, mod3, g, w1, qn, kvn, wuq, wk, wv]
    sds = jax.ShapeDtypeStruct
    if latent:
        per = DEC_SEQ // TM
        tab = lambda i: (i % per, 0)
        in_specs += [pl.BlockSpec((TM, LANES), tab)] * 4
        args += list(tables)
        out_shape = [sds((t, 512), BF16), sds((t, LANES), BF16), sds((t, LANES), BF16),
                     sds((t, B_HEADS * LANES), BF16), sds((t, B_HEADS * LANES), BF16),
                     sds((t, B_HEADS * B_V), BF16)]
        widths = [512, LANES, LANES, B_HEADS * LANES, B_HEADS * LANES, B_HEADS * B_V]
    else:
        nb = t // SEQ
        out_shape = [sds((t, 512), BF16), sds((nb, LANES, SEQ), F32), sds((nb, LANES, SEQ), F32),
                     sds((t, KV_LORA), F32), sds((nb, B_ROPE, SEQ), F32),
                     sds((t, B_HEADS * LANES), BF16), sds((t, B_HEADS * LANES), BF16),
                     sds((t, B_HEADS * B_V), BF16)]
        widths = [512, None, None, KV_LORA, None, B_HEADS * LANES, B_HEADS * LANES, B_HEADS * B_V]
    out_specs = [pl.BlockSpec((TM, w), row) if w is not None
                 else pl.BlockSpec((TM // SEQ,) + o.shape[1:], lambda i: (i, 0, 0))
                 for w, o in zip(widths, out_shape)]
    return pl.pallas_call(
        functools.partial(_inproj_kernel, latent),
        grid=(n_tiles,),
        in_specs=in_specs,
        out_specs=out_specs,
        out_shape=out_shape,
        compiler_params=_params("arbitrary"),
        name="inproj_latent" if latent else "inproj_ctx",
    )(*args)


def _kvexp_kernel(ckv_ref, krp_ref, wk_ref, wv_ref, kb_o, vb_o):
    cb = ckv_ref[...].astype(BF16)
    kn = _dot(cb, wk_ref[...])
    krp = krp_ref[...]
    for hd in range(B_HEADS):
        kb_o[:, hd * LANES:(hd + 1) * LANES] = (kn[:, hd * LANES:(hd + 1) * LANES] + krp).astype(BF16)
    vb_o[...] = _dot(cb, wv_ref[...]).astype(BF16)


def _kv_expand(ckv, krp, wk, wv):
    t = ckv.shape[0]
    row = lambda i: (i, 0)
    const = lambda i: (0, 0)
    return pl.pallas_call(
        _kvexp_kernel,
        grid=(t // TM,),
        in_specs=[pl.BlockSpec((TM, KV_LORA), row), pl.BlockSpec((TM, LANES), row),
                  pl.BlockSpec((KV_LORA, B_HEADS * LANES), const),
                  pl.BlockSpec((KV_LORA, B_HEADS * B_V), const)],
        out_specs=[pl.BlockSpec((TM, B_HEADS * LANES), row), pl.BlockSpec((TM, B_HEADS * B_V), row)],
        out_shape=[jax.ShapeDtypeStruct((t, B_HEADS * LANES), BF16),
                   jax.ShapeDtypeStruct((t, B_HEADS * B_V), BF16)],
        compiler_params=_params("arbitrary"),
        name="kv_expand",
    )(ckv, krp, wk, wv)


def _lo_mask(rows):
    return lax.broadcasted_iota(jnp.int32, (rows, LANES), 1) < (LANES // 2)


def _softmax_pv(parts, sink, v_t=False):
    if sink is not None:
        sink = sink * LOG2E
    m = None
    for s, _ in parts:
        mi = jnp.max(s, axis=-1, keepdims=True)
        m = mi if m is None else jnp.maximum(m, mi)
    if sink is not None:
        m = jnp.maximum(m, sink)
    den = None
    acc = None
    for s, v in parts:
        p = jnp.exp2(s - m)
        di = jnp.sum(p, axis=-1, keepdims=True)
        den = di if den is None else den + di
        oi = _dot_nt(p.astype(BF16), v) if v_t else _dot(p.astype(BF16), v)
        acc = oi if acc is None else acc + oi
    if sink is not None:
        den = den + jnp.exp2(sink - m)
    return acc / den


def _ctx_attn_kernel(sink_ref, qa_ref, ka_ref, va_ref, qb_ref, kb_ref, vb_ref, oa_ref, ob_ref):
    lo = _lo_mask(SEQ)
    k = ka_ref[...].astype(BF16)
    v = va_ref[...].astype(BF16)
    for j in range(A_GROUP):
        q = qa_ref[:, j * LANES:(j + 1) * LANES]
        zero = jnp.zeros_like(q)
        o0 = _softmax_pv([(_dot(jnp.where(lo, q, zero), k), v)], sink_ref[j], v_t=True)
        o1 = _softmax_pv([(_dot(jnp.where(lo, zero, q), k), v)], sink_ref[A_GROUP + j], v_t=True)
        oa_ref[:, j * LANES:(j + 1) * LANES] = jnp.where(lo, o0, o1).astype(BF16)
    for j in range(B_HEADS // 2):
        vp = vb_ref[:, j * LANES:(j + 1) * LANES]
        outs = []
        for hd in (2 * j, 2 * j + 1):
            s = _dot_nt(qb_ref[:, hd * LANES:(hd + 1) * LANES], kb_ref[:, hd * LANES:(hd + 1) * LANES])
            outs.append(_softmax_pv([(s, vp)], None))
        ob_ref[:, j * LANES:(j + 1) * LANES] = jnp.where(lo, outs[0], outs[1]).astype(BF16)


def _ctx_attention(sink, qa, ka, va, qb, kb, vb):
    blk = lambda w: pl.BlockSpec((SEQ, w), lambda b: (b, 0))
    kv_t = pl.BlockSpec((None, LANES, SEQ), lambda b: (b, 0, 0))
    return pl.pallas_call(
        _ctx_attn_kernel,
        grid=(BATCH,),
        in_specs=[pl.BlockSpec(memory_space=pltpu.SMEM),
                  blk(512), kv_t, kv_t, blk(B_HEADS * LANES), blk(B_HEADS * LANES),
                  blk(B_HEADS * B_V)],
        out_specs=[blk(512), blk(512)],
        out_shape=[jax.ShapeDtypeStruct((T_CTX, 512), BF16), jax.ShapeDtypeStruct((T_CTX, 512), BF16)],
        compiler_params=_params("arbitrary"),
        name="ctx_attention",
    )(sink, qa, ka, va, qb, kb, vb)


_WIN = TQ + 2 * WINDOW


def _lat_swa_kernel(sink_ref, q_ref, k_ref, v_ref, kc_ref, vc_ref, o_ref):
    qi = pl.program_id(1)
    q0 = qi * TQ
    start = pl.multiple_of(jnp.clip(q0 - WINDOW, 0, DEC_SEQ - _WIN), WINDOW)
    kw = k_ref[pl.ds(start, _WIN), :]
    vw = v_ref[pl.ds(start, _WIN), :]
    kc = kc_ref[...].astype(BF16)
    vc = vc_ref[...].astype(BF16)
    qpos = q0 + lax.broadcasted_iota(jnp.int32, (TQ, _WIN), 0)
    kpos = start + lax.broadcasted_iota(jnp.int32, (TQ, _WIN), 1)
    valid = jnp.abs(qpos - kpos) <= WINDOW
    lo = _lo_mask(TQ)
    for j in range(A_GROUP):
        q = q_ref[:, j * LANES:(j + 1) * LANES]
        zero = jnp.zeros_like(q)
        outs = []
        for g, qh in ((0, jnp.where(lo, q, zero)), (1, jnp.where(lo, zero, q))):
            sw = jnp.where(valid, _dot_nt(qh, kw), NEG)
            sc = _dot_nt(qh, kc)
            outs.append(_softmax_pv([(sw, vw), (sc, vc)], sink_ref[g * A_GROUP + j]))
        o_ref[:, j * LANES:(j + 1) * LANES] = jnp.where(lo, outs[0], outs[1]).astype(BF16)


def _lat_swa(sink, qa, ka, va, kc, vc):
    nq = DEC_SEQ // TQ
    return pl.pallas_call(
        _lat_swa_kernel,
        grid=(DEC_BATCH, nq),
        in_specs=[pl.BlockSpec(memory_space=pltpu.SMEM),
                  pl.BlockSpec((TQ, 512), lambda b, i: (b * nq + i, 0)),
                  pl.BlockSpec((None, DEC_SEQ, LANES), lambda b, i: (b, 0, 0)),
                  pl.BlockSpec((None, DEC_SEQ, LANES), lambda b, i: (b, 0, 0)),
                  pl.BlockSpec((None, PAST_LEN, LANES), lambda b, i: (b, 0, 0)),
                  pl.BlockSpec((None, PAST_LEN, LANES), lambda b, i: (b, 0, 0))],
        out_specs=pl.BlockSpec((TQ, 512), lambda b, i: (b * nq + i, 0)),
        out_shape=jax.ShapeDtypeStruct((T_LAT, 512), BF16),
        compiler_params=_params("arbitrary", "arbitrary"),
        name="latent_swa",
    )(sink, qa, ka.reshape(DEC_BATCH, DEC_SEQ, LANES), va.reshape(DEC_BATCH, DEC_SEQ, LANES), kc, vc)


TQ_MLA = 256


def _lat_mla_kernel(q_ref, kl_ref, vl_ref, kc_ref, vc_ref, o_ref):
    lo = _lo_mask(TQ_MLA)
    vl = vl_ref[...]
    vc = vc_ref[...]
    outs = []
    for hh in range(2):
        q = q_ref[:, hh * LANES:(hh + 1) * LANES]
        sc = _dot_nt(q, kc_ref[:, hh * LANES:(hh + 1) * LANES])
        sl = _dot_nt(q, kl_ref[:, hh * LANES:(hh + 1) * LANES])
        outs.append(_softmax_pv([(sc, vc), (sl, vl)], None))
    o_ref[...] = jnp.where(lo, outs[0], outs[1]).astype(BF16)


def _lat_mla(qb, kb, vb, kc, vc):
    nq = DEC_SEQ // TQ_MLA
    npair = B_HEADS // 2
    return pl.pallas_call(
        _lat_mla_kernel,
        grid=(DEC_BATCH, npair, nq),
        in_specs=[pl.BlockSpec((TQ_MLA, 2 * LANES), lambda b, p, i: (b * nq + i, p)),
                  pl.BlockSpec((None, DEC_SEQ, 2 * LANES), lambda b, p, i: (b, 0, p)),
                  pl.BlockSpec((None, DEC_SEQ, LANES), lambda b, p, i: (b, 0, p)),
                  pl.BlockSpec((None, PAST_LEN, 2 * LANES), lambda b, p, i: (b, 0, p)),
                  pl.BlockSpec((None, PAST_LEN, LANES), lambda b, p, i: (b, 0, p))],
        out_specs=pl.BlockSpec((TQ_MLA, LANES), lambda b, p, i: (b * nq + i, p)),
        out_shape=jax.ShapeDtypeStruct((T_LAT, 512), BF16),
        compiler_params=_params("arbitrary", "arbitrary", "arbitrary"),
        name="latent_mla",
    )(qb, kb.reshape(DEC_BATCH, DEC_SEQ, B_HEADS * LANES), vb.reshape(DEC_BATCH, DEC_SEQ, B_HEADS * B_V),
      kc.reshape(DEC_BATCH, PAST_LEN, B_HEADS * LANES), vc.reshape(DEC_BATCH, PAST_LEN, B_HEADS * B_V))


_N_CTX_TILES = T_CTX // TM


def _merge_kernel(xp_ref, xs_ref, oac_ref, oal_ref, obc_ref, obl_ref, mod_ref,
                  gpre_ref, gpost_ref, gffn_ref, woa_ref, wob_ref, wout_ref, wg_ref, wr_ref,
                  x1_o, h2_o, lg_o):
    def body(x_ref, oa_ref, ob_ref):
        x = x_ref[...]
        sh1 = mod_ref[:, 0:D_MODEL]
        sc1 = mod_ref[:, D_MODEL:2 * D_MODEL]
        g1 = mod_ref[:, 2 * D_MODEL:3 * D_MODEL]
        sh2 = mod_ref[:, 3 * D_MODEL:4 * D_MODEL]
        sc2 = mod_ref[:, 4 * D_MODEL:5 * D_MODEL]
        h = (_rms(x, gpre_ref[...]) * (1.0 + sc1) + sh1).astype(BF16)
        gates = _dot(h, wg_ref[...])
        ya = _dot(oa_ref[...], woa_ref[...])
        yb = _dot(ob_ref[...], wob_ref[...])
        y = _sigmoid(gates[:, :D_MODEL]) * ya + _sigmoid(gates[:, D_MODEL:]) * yb
        mix = _dot(y.astype(BF16), wout_ref[...])
        x1 = x + g1 * _rms(mix, gpost_ref[...])
        x1_o[...] = x1
        h2 = _rms(x1, gffn_ref[...]) * (1.0 + sc2) + sh2
        _store_rows(h2_o, h2, TM)
        lg_o[...] = _dot_nt(wr_ref[...], h2.astype(BF16))

    is_ctx = pl.program_id(0) < _N_CTX_TILES

    @pl.when(is_ctx)
    def _():
        body(xp_ref, oac_ref, obc_ref)

    @pl.when(jnp.logical_not(is_ctx))
    def _():
        body(xs_ref, oal_ref, obl_ref)


def _mod_row_all(i):
    per = DEC_SEQ // TM
    return jnp.where(i < _N_CTX_TILES, 0, 1 + (jnp.maximum(i - _N_CTX_TILES, 0)) // per)


def _merge(xp, xs, oac, oal, obc, obl, mod3, gpre, gpost, gffn, woa, wob, wout, wg, wr):
    ctx = lambda i: (jnp.minimum(i, _N_CTX_TILES - 1), 0)
    lat = lambda i: (jnp.maximum(i - _N_CTX_TILES, 0), 0)
    row = lambda i: (i, 0)
    const = lambda i: (0, 0)
    return pl.pallas_call(
        _merge_kernel,
        grid=(T_ALL // TM,),
        in_specs=[pl.BlockSpec((TM, D_MODEL), ctx), pl.BlockSpec((TM, D_MODEL), lat),
                  pl.BlockSpec((TM, 512), ctx), pl.BlockSpec((TM, 512), lat),
                  pl.BlockSpec((TM, 512), ctx), pl.BlockSpec((TM, 512), lat),
                  pl.BlockSpec((None, 1, 6 * D_MODEL), lambda i: (_mod_row_all(i), 0, 0)),
                  pl.BlockSpec((1, D_MODEL), const), pl.BlockSpec((1, D_MODEL), const),
                  pl.BlockSpec((1, D_MODEL), const),
                  pl.BlockSpec((512, D_MODEL), const), pl.BlockSpec((512, D_MODEL), const),
                  pl.BlockSpec((D_MODEL, D_MODEL), const), pl.BlockSpec((D_MODEL, 2 * D_MODEL), const),
                  pl.BlockSpec((N_EXPERTS, D_MODEL), const)],
        out_specs=[pl.BlockSpec((TM, D_MODEL), row), pl.BlockSpec((TM * SUB, LANES), row),
                   pl.BlockSpec((N_EXPERTS, TM), lambda i: (0, i))],
        out_shape=[jax.ShapeDtypeStruct((T_ALL, D_MODEL), F32), jax.ShapeDtypeStruct((T_ALL * SUB, LANES), F32),
                   jax.ShapeDtypeStruct((N_EXPERTS, T_ALL), F32)],
        compiler_params=_params("arbitrary"),
        name="merge",
    )(xp, xs, oac, oal, obc, obl, mod3, gpre, gpost, gffn, woa, wob, wout, wg, wr)


_PER_GROUP = N_EXPERTS // N_GROUPS


def _router_kernel(lg_ref, bias_ref, topi_o, rank_o, topw_o, cnt_o, cnt_s):
    @pl.when(pl.program_id(0) == 0)
    def _():
        cnt_s[...] = jnp.zeros_like(cnt_s)

    scores = _sigmoid(lg_ref[...])
    sel = scores + bias_ref[...]
    eidx = lax.broadcasted_iota(jnp.int32, (N_EXPERTS, TM), 0).astype(F32)
    ninf = -jnp.inf
    big = float(N_EXPERTS)
    sel_g = [sel[g * _PER_GROUP:(g + 1) * _PER_GROUP, :] for g in range(N_GROUPS)]
    gs = []
    for sg in sel_g:
        m1 = jnp.max(sg, axis=0, keepdims=True)
        top = sg == m1
        n_top = jnp.sum(jnp.where(top, 1.0, 0.0), axis=0, keepdims=True)
        rest = jnp.max(jnp.where(top, ninf, sg), axis=0, keepdims=True)
        gs.append(m1 + jnp.where(n_top > 1.5, m1, rest))
    cur_g = []
    for g in range(N_GROUPS):
        beat = jnp.zeros((1, TM), F32)
        for g2 in range(N_GROUPS):
            if g2 == g:
                continue
            better = (gs[g2] >= gs[g]) if g2 < g else (gs[g2] > gs[g])
            beat = beat + jnp.where(better, 1.0, 0.0)
        cur_g.append(jnp.where(beat < TOPK_GROUPS, sel_g[g], NEG))
    cur = jnp.concatenate(cur_g, axis=0)
    chosen = jnp.zeros((N_EXPERTS, TM), F32)
    idxs, ws = [], []
    for k in range(TOP_K):
        m = jnp.max(cur, axis=0, keepdims=True)
        idx = jnp.min(jnp.where(cur == m, eidx, big), axis=0, keepdims=True)
        hit = eidx == idx
        ws.append(jnp.sum(jnp.where(hit, scores, 0.0), axis=0, keepdims=True))
        cur = jnp.where(hit, ninf, cur)
        chosen = jnp.where(hit, 1.0, chosen)
        idxs.append(idx)
    wsum = ws[0]
    for k in range(1, TOP_K):
        wsum = wsum + ws[k]
    for k in range(TOP_K):
        topw_o[k:k + 1, :] = ws[k] / wsum * ROUTED_SCALE
        topi_o[k:k + 1, :] = idxs[k].astype(jnp.int32)
    r_i = lax.broadcasted_iota(jnp.int32, (TM, TM), 0)
    c_i = lax.broadcasted_iota(jnp.int32, (TM, TM), 1)
    earlier = jnp.where(r_i < c_i, 1.0, 0.0).astype(BF16)
    prefix = _dot(chosen.astype(BF16), earlier) + cnt_s[:, 0:1]
    for k in range(TOP_K):
        r = jnp.sum(jnp.where(eidx == idxs[k], prefix, 0.0), axis=0, keepdims=True)
        rank_o[k:k + 1, :] = r.astype(jnp.int32)
    total = cnt_s[...] + jnp.sum(chosen, axis=1, keepdims=True)
    cnt_s[...] = total
    cnt_o[...] = total


def _router(logits_t, bias_col):
    col = lambda i: (0, i)
    const = lambda i: (0, 0)
    return pl.pallas_call(
        _router_kernel,
        grid=(T_ALL // TM,),
        in_specs=[pl.BlockSpec((N_EXPERTS, TM), col), pl.BlockSpec((N_EXPERTS, 1), const)],
        out_specs=[pl.BlockSpec((TOP_K, TM), col), pl.BlockSpec((TOP_K, TM), col),
                   pl.BlockSpec((TOP_K, TM), col), pl.BlockSpec((N_EXPERTS, LANES), const)],
        out_shape=[jax.ShapeDtypeStruct((TOP_K, T_ALL), jnp.int32),
                   jax.ShapeDtypeStruct((TOP_K, T_ALL), jnp.int32),
                   jax.ShapeDtypeStruct((TOP_K, T_ALL), F32),
                   jax.ShapeDtypeStruct((N_EXPERTS, LANES), F32)],
        scratch_shapes=[pltpu.VMEM((N_EXPERTS, LANES), F32)],
        compiler_params=_params("arbitrary"),
        name="router",
    )(logits_t, bias_col)


def _dest_kernel(topi_ref, rank_ref, ps_ref, o_ref):
    eidx = lax.broadcasted_iota(jnp.int32, (N_EXPERTS, TM), 0)
    ps = jnp.broadcast_to(ps_ref[...], (N_EXPERTS, TM))
    for k in range(TOP_K):
        start = jnp.sum(jnp.where(eidx == topi_ref[k:k + 1, :], ps, 0.0), axis=0, keepdims=True)
        o_ref[k:k + 1, :] = rank_ref[k:k + 1, :] + start.astype(jnp.int32)


def _dest(topi, rank, pstart_col):
    col = lambda i: (0, i)
    return pl.pallas_call(
        _dest_kernel,
        grid=(T_ALL // TM,),
        in_specs=[pl.BlockSpec((TOP_K, TM), col), pl.BlockSpec((TOP_K, TM), col),
                  pl.BlockSpec((N_EXPERTS, 1), lambda i: (0, 0))],
        out_specs=pl.BlockSpec((TOP_K, TM), col),
        out_shape=jax.ShapeDtypeStruct((TOP_K, T_ALL), jnp.int32),
        compiler_params=_params("arbitrary"),
        name="dest",
    )(topi, rank, pstart_col)


TD = 512
DISPATCH_PARTS = 4
ROW_UNROLL = 8


def _dispatch_kernel(dest_ref, h_ref, wgu_ref, wd_ref, xs_hbm, sh_o, sem):
    def row(t, carry):
        src = h_ref.at[pl.ds(pl.multiple_of(t * SUB, SUB), SUB)]
        for k in range(TOP_K):
            d = dest_ref[k, t]
            pltpu.make_async_copy(src, xs_hbm.at[pl.ds(pl.multiple_of(d * SUB, SUB), SUB)],
                                  sem).start(priority=k % 2)
        return carry

    part = TD // DISPATCH_PARTS
    for p in range(DISPATCH_PARTS):
        lax.fori_loop(p * part, (p + 1) * part, row, 0, unroll=ROW_UNROLL)
        x = _load_rows(h_ref.at[pl.ds(p * part * SUB, part * SUB)], part).astype(BF16)
        gu = _dot(x, wgu_ref[...])
        a = _silu(gu[:, :D_SHARED]) * gu[:, D_SHARED:]
        sh_o[p * part:(p + 1) * part, :] = _dot(a.astype(BF16), wd_ref[...])
    for _ in range(TOP_K):
        pltpu.make_async_copy(h_ref, xs_hbm.at[pl.ds(0, TD * SUB)], sem).wait()


def _dispatch(dest_flat, h2, wgu, wd):
    const = lambda i: (0, 0)
    return pl.pallas_call(
        _dispatch_kernel,
        grid=(T_ALL // TD,),
        in_specs=[pl.BlockSpec((TOP_K, TD), lambda i: (0, i), memory_space=pltpu.SMEM),
                  pl.BlockSpec((TD * SUB, LANES), lambda i: (i, 0)),
                  pl.BlockSpec((D_MODEL, 2 * D_SHARED), const), pl.BlockSpec((D_SHARED, D_MODEL), const)],
        out_specs=[pl.BlockSpec(memory_space=pl.ANY), pl.BlockSpec((TD, D_MODEL), lambda i: (i, 0))],
        out_shape=[jax.ShapeDtypeStruct((N_SLOTS * SUB, LANES), F32),
                   jax.ShapeDtypeStruct((T_ALL, D_MODEL), F32)],
        scratch_shapes=[pltpu.SemaphoreType.DMA(())],
        compiler_params=_params("arbitrary"),
        name="dispatch",
    )(dest_flat, h2, wgu, wd)


_BLK_TILES = MOE_BLK * SUB
NBX = 6
PREFETCH = NBX - 2
NBO = 4
BLK_QUANT = 32


def _expert_kernel(bs_ref, vb_ref, x_hbm, wg_ref, wu_ref, wd_ref, o_hbm, wgu_s, wd_s, xbuf, obuf, xsem, osem):
    e = pl.program_id(0)
    b0 = bs_ref[e]
    b1 = bs_ref[e + 1]
    n_used = bs_ref[N_EXPERTS]

    def pieces(g, make, act):
        v = vb_ref[g]
        base = pl.multiple_of(g * _BLK_TILES, _BLK_TILES)

        @pl.when(v == MOE_BLK)
        def _():
            act(make(base, 0, _BLK_TILES))

        p = MOE_BLK // 2
        while p >= BLK_QUANT:
            @pl.when((v < MOE_BLK) & ((v & p) != 0))
            def _(p=p):
                off = pl.multiple_of((v & (MOE_BLK - 2 * p)) * SUB, p * SUB)
                act(make(base, off, p * SUB))
            p //= 2

    def xmake(g):
        s = lax.rem(g, NBX)
        return lambda base, off, n: pltpu.make_async_copy(
            x_hbm.at[pl.ds(base + off, n)], xbuf.at[s, pl.ds(off, n)], xsem.at[s])

    def omake(g):
        s = lax.rem(g, NBO)
        return lambda base, off, n: pltpu.make_async_copy(
            obuf.at[s, pl.ds(off, n)], o_hbm.at[pl.ds(base + off, n)], osem.at[s])

    start_x = lambda c: c.start()
    start_o = lambda c: c.start(priority=1)
    wait = lambda c: c.wait()

    @pl.when(e == 0)
    def _():
        xbuf[...] = jnp.zeros_like(xbuf)
        for j in range(PREFETCH):
            @pl.when(j < n_used)
            def _():
                pieces(j, xmake(j), start_x)

    @pl.when(b1 > b0)
    def _():
        wgu_s[:, 0:D_EXPERT] = wg_ref[...].astype(BF16)
        wgu_s[:, D_EXPERT:2 * D_EXPERT] = wu_ref[...].astype(BF16)
        wd_s[...] = wd_ref[...].astype(BF16)

    def acquire(g):
        @pl.when(g + PREFETCH < n_used)
        def _():
            pieces(g + PREFETCH, xmake(g + PREFETCH), start_x)

        pieces(g, xmake(g), wait)

        @pl.when(g >= NBO)
        def _():
            pieces(g - NBO, omake(g - NBO), wait)

    def compute(g):
        gu = _dot(_load_rows(xbuf.at[lax.rem(g, NBX)], MOE_BLK).astype(BF16), wgu_s[...])
        a = _silu(gu[:, :D_EXPERT]) * gu[:, D_EXPERT:]
        _store_rows(obuf.at[lax.rem(g, NBO)], _dot(a.astype(BF16), wd_s[...]), MOE_BLK)

    def pair(i, carry):
        g = b0 + 2 * i
        acquire(g)
        acquire(g + 1)
        compute(g)
        compute(g + 1)
        pieces(g, omake(g), start_o)
        pieces(g + 1, omake(g + 1), start_o)
        return carry

    lax.fori_loop(0, lax.shift_right_logical(b1 - b0, 1), pair, 0)

    @pl.when(((b1 - b0) & 1) == 1)
    def _():
        acquire(b1 - 1)
        compute(b1 - 1)
        pieces(b1 - 1, omake(b1 - 1), start_o)

    @pl.when(e == N_EXPERTS - 1)
    def _():
        for j in range(NBO, 0, -1):
            @pl.when(n_used >= j)
            def _():
                pieces(n_used - j, omake(n_used - j), wait)


def _experts(blk_start, blk_rows, x_sorted, w_gate, w_up, w_down):
    wsel = lambda e, bs, vb: (e, 0, 0)
    grid_spec = pltpu.PrefetchScalarGridSpec(
        num_scalar_prefetch=2,
        grid=(N_EXPERTS,),
        in_specs=[pl.BlockSpec(memory_space=pl.ANY),
                  pl.BlockSpec((None, D_MODEL, D_EXPERT), wsel),
                  pl.BlockSpec((None, D_MODEL, D_EXPERT), wsel),
                  pl.BlockSpec((None, D_EXPERT, D_MODEL), wsel)],
        out_specs=pl.BlockSpec(memory_space=pl.ANY),
        scratch_shapes=[pltpu.VMEM((D_MODEL, 2 * D_EXPERT), BF16), pltpu.VMEM((D_EXPERT, D_MODEL), BF16),
                        pltpu.VMEM((NBX, _BLK_TILES, LANES), F32), pltpu.VMEM((NBO, _BLK_TILES, LANES), F32),
                        pltpu.SemaphoreType.DMA((NBX,)), pltpu.SemaphoreType.DMA((NBO,))],
    )
    return pl.pallas_call(
        _expert_kernel,
        grid_spec=grid_spec,
        out_shape=jax.ShapeDtypeStruct((N_SLOTS * SUB, LANES), F32),
        compiler_params=_params("arbitrary"),
        name="experts",
    )(blk_start, blk_rows, x_sorted, w_gate, w_up, w_down)


TC = 256
CR = 16


def _combine_kernel(dcur_ref, dnext_ref, eo_hbm, x1_ref, sh_ref, w_ref, mod_ref, g_ref, o_ref, buf, sem, wb_s):
    i = pl.program_id(0)
    n = pl.num_programs(0)
    slot = lax.rem(i, 2)

    def gather(d_ref, s, t0, t1):
        def row(t, carry):
            for k in range(TOP_K):
                d = d_ref[k, t]
                pltpu.make_async_copy(eo_hbm.at[pl.ds(pl.multiple_of(d * SUB, SUB), SUB)],
                                      buf.at[s, k, pl.ds(pl.multiple_of(t * SUB, SUB), SUB)],
                                      sem.at[s]).start(priority=k % 2)
            return carry
        lax.fori_loop(t0, t1, row, 0, unroll=ROW_UNROLL)

    @pl.when(i == 0)
    def _():
        gather(dcur_ref, 0, 0, TC)

    @pl.when(i + 1 < n)
    def _():
        gather(dnext_ref, 1 - slot, 0, TC)

    gate = mod_ref[:, 5 * D_MODEL:6 * D_MODEL]
    for k in range(TOP_K):
        wb_s[k] = jnp.broadcast_to(w_ref[:, k:k + 1], (TC, LANES))
    for k in range(TOP_K):
        pltpu.make_async_copy(eo_hbm.at[pl.ds(0, TC * SUB)], buf.at[slot, k], sem.at[slot]).wait()

    lanes = [slice(j * LANES, (j + 1) * LANES) for j in range(SUB)]

    def chunk(r, carry):
        r0 = pl.multiple_of(r * CR, CR)
        rows = pl.ds(r0, CR)
        wk = [wb_s[k, rows, :] for k in range(TOP_K)]
        pieces = []
        sq = jnp.zeros((CR, LANES), F32)
        for j in range(SUB):
            acc = sh_ref[rows, lanes[j]]
            for k in range(TOP_K):
                acc = acc + buf[slot, k, pl.ds(r0 * SUB + j, CR, stride=SUB), :] * wk[k]
            pieces.append(acc)
            sq = sq + acc * acc
        inv = lax.rsqrt(jnp.sum(sq, axis=-1, keepdims=True) / D_MODEL + EPS)
        for j in range(SUB):
            o_ref[rows, lanes[j]] = (x1_ref[rows, lanes[j]]
                                     + gate[:, lanes[j]] * (pieces[j] * inv * g_ref[:, lanes[j]]))
        return carry

    lax.fori_loop(0, TC // CR, chunk, 0, unroll=8)


def _combine(dest_flat, eo, x1, shared, topw, mod3, mod_row, g, tile0, n_tiles):
    row = lambda i: (tile0 + i, 0)
    const = lambda i: (0, 0)
    blk = TC * TOP_K
    return pl.pallas_call(
        _combine_kernel,
        grid=(n_tiles,),
        in_specs=[pl.BlockSpec((TOP_K, TC), lambda i: (0, tile0 + i), memory_space=pltpu.SMEM),
                  pl.BlockSpec((TOP_K, TC), lambda i: (0, tile0 + jnp.minimum(i + 1, n_tiles - 1)),
                               memory_space=pltpu.SMEM),
                  pl.BlockSpec(memory_space=pl.ANY),
                  pl.BlockSpec((TC, D_MODEL), row), pl.BlockSpec((TC, D_MODEL), row),
                  pl.BlockSpec((TC, LANES), row),
                  pl.BlockSpec((None, 1, 6 * D_MODEL), lambda i: (mod_row(i), 0, 0)),
                  pl.BlockSpec((1, D_MODEL), const)],
        out_specs=pl.BlockSpec((TC, D_MODEL), lambda i: (i, 0)),
        out_shape=jax.ShapeDtypeStruct((n_tiles * TC, D_MODEL), F32),
        scratch_shapes=[pltpu.VMEM((2, TOP_K, TC * SUB, LANES), F32), pltpu.SemaphoreType.DMA((2,)),
                        pltpu.VMEM((TOP_K, TC, LANES), F32)],
        compiler_params=_params("arbitrary"),
        name="combine",
    )(dest_flat, dest_flat, eo, x1, shared, topw, mod3, g)


def _qa_perm():
    cols = []
    for j in range(A_GROUP):
        for g in range(A_KV_HEADS):
            hd = g * A_GROUP + j
            cols.extend(range(hd * A_HEAD_DIM, (hd + 1) * A_HEAD_DIM))
    return np.asarray(cols, np.int32)


def _pad_heads(w, width, n_heads, offset, total=LANES):
    k = w.shape[0]
    stride = w.shape[1] // n_heads
    w3 = w.reshape(k, n_heads, stride)[:, :, offset:offset + width]
    w3 = jnp.pad(w3, ((0, 0), (0, 0), (0, total - width)))
    return w3.reshape(k, n_heads * total)


def kernel(x_prompt, x_sample, cache_k_swa, cache_v_swa, cache_ckv_mla, cache_krope_mla, c, c_ctx, w_ada, b_ada, norm_pre_mix, norm_post_mix, norm_pre_ffn, norm_post_ffn, w_in, sink_swa, q_norm_mla, kv_norm_mla, w_uq_mla, w_ukv_mla, w_o_swa, w_o_mla, w_out, router_w, router_bias, w_gate_exp, w_up_exp, w_down_exp, w_gate_sh, w_up_sh, w_down_sh):
    l = 0
    perm = _qa_perm()
    wi = w_in[l]
    kr_blk = jnp.pad(wi[:, _O_KR:_O_GATE], ((0, 0), (B_NOPE, LANES - B_NOPE - B_ROPE)))
    w1 = jnp.concatenate([wi[:, _O_QA:_O_KA][:, perm], wi[:, _O_KA:_O_KR], kr_blk], axis=1).astype(BF16)
    wg = wi[:, _O_GATE:].astype(BF16)
    wuq = _pad_heads(w_uq_mla[l], B_NOPE + B_ROPE, B_HEADS, 0).astype(BF16)
    wk = _pad_heads(w_ukv_mla[l], B_NOPE, B_HEADS, 0).astype(BF16)
    wv = w_ukv_mla[l].reshape(KV_LORA, B_HEADS, B_NOPE + B_V)[:, :, B_NOPE:].reshape(KV_LORA, B_HEADS * B_V).astype(BF16)
    woa = w_o_swa[l][perm, :].astype(BF16)
    wob = w_o_mla[l].astype(BF16)
    wout = w_out[l].astype(BF16)
    wr = router_w[l].T.astype(BF16)
    wgu_sh = jnp.concatenate([w_gate_sh[l], w_up_sh[l]], axis=1).astype(BF16)
    wd_sh = w_down_sh[l].astype(BF16)
    sink = sink_swa[l]
    row2 = lambda v: v.reshape(1, -1)

    cond8 = jnp.concatenate([c_ctx[None, :], c, jnp.zeros((8 - 1 - DEC_BATCH, D_MODEL), F32)], axis=0)
    mod3 = _modulation(cond8, w_ada[l], b_ada[l]).reshape(8, 1, 6 * D_MODEL)

    xp = x_prompt.reshape(T_CTX, D_MODEL)
    xs = x_sample.reshape(T_LAT, D_MODEL)
    per = DEC_SEQ // TM
    shared_in = (row2(norm_pre_mix[l]), w1, row2(q_norm_mla[l]), row2(kv_norm_mla[l]), wuq, wk, wv)
    qa_c, ka_c, va_c, ckv_c, kr_c, qb_c, kb_c, vb_c = _inproj(xp, mod3, lambda i: 0, *shared_in, None)
    qa_l, ka_l, va_l, qb_l, kb_l, vb_l = _inproj(xs, mod3, lambda i: 1 + i // per, *shared_in, _rope_tables())

    oa_c, ob_c = _ctx_attention(sink, qa_c, ka_c, va_c, qb_c, kb_c, vb_c)

    krp_cache = jnp.pad(cache_krope_mla[:, l].reshape(DEC_BATCH * PAST_LEN, B_ROPE),
                        ((0, 0), (B_NOPE, LANES - B_NOPE - B_ROPE)))
    kc_b, vc_b = _kv_expand(cache_ckv_mla[:, l].reshape(DEC_BATCH * PAST_LEN, KV_LORA), krp_cache, wk, wv)
    oa_l = _lat_swa(sink, qa_l, ka_l, va_l,
                    cache_k_swa[:, l].reshape(DEC_BATCH, PAST_LEN, LANES),
                    cache_v_swa[:, l].reshape(DEC_BATCH, PAST_LEN, LANES))
    ob_l = _lat_mla(qb_l, kb_l, vb_l, kc_b, vc_b)

    x1, h2, logits = _merge(xp, xs, oa_c, oa_l, ob_c, ob_l, mod3,
                            row2(norm_pre_mix[l]), row2(norm_post_mix[l]), row2(norm_pre_ffn[l]),
                            woa, wob, wout, wg, wr)

    topi8, rank8, topw8, counts_b = _router(logits, router_bias[l].reshape(N_EXPERTS, 1))
    counts = counts_b[:, 0].astype(jnp.int32)

    padded = (counts + MOE_BLK - 1) // MOE_BLK * MOE_BLK
    pend = jnp.cumsum(padded)
    pstart = pend - padded
    dest = _dest(topi8, rank8, pstart.astype(F32).reshape(N_EXPERTS, 1))
    topw_p = jnp.pad(topw8.T, ((0, 0), (0, LANES - TOP_K)))
    blk_start = (jnp.concatenate([pstart, pend[-1:]]) // MOE_BLK).astype(jnp.int32)
    blk_id = jnp.arange(N_SLOT_BLK, dtype=jnp.int32)
    blk_owner = jnp.minimum(jnp.sum(blk_id[:, None] >= blk_start[None, 1:], axis=1), N_EXPERTS - 1)
    left = counts[blk_owner] - (blk_id - blk_start[blk_owner]) * MOE_BLK
    blk_rows = jnp.clip((left + BLK_QUANT - 1) // BLK_QUANT * BLK_QUANT, 0, MOE_BLK).astype(jnp.int32)

    x_sorted, shared = _dispatch(dest, h2, wgu_sh, wd_sh)
    eo = _experts(blk_start, blk_rows, x_sorted, w_gate_exp[l], w_up_exp[l], w_down_exp[l])

    g_post = row2(norm_post_ffn[l])
    per_c = DEC_SEQ // TC
    y_p = _combine(dest, eo, x1, shared, topw_p, mod3, lambda i: 0, g_post, 0, T_CTX // TC)
    y_s = _combine(dest, eo, x1, shared, topw_p, mod3, lambda i: 1 + i // per_c, g_post,
                   T_CTX // TC, T_LAT // TC)

    return (y_p.reshape(BATCH, SEQ, D_MODEL), y_s.reshape(DEC_BATCH, DEC_SEQ, D_MODEL),
            ka_c.reshape(BATCH, 1, A_KV_HEADS, A_HEAD_DIM, SEQ).transpose(0, 1, 4, 2, 3),
            va_c.reshape(BATCH, 1, A_KV_HEADS, A_HEAD_DIM, SEQ).transpose(0, 1, 4, 2, 3),
            ckv_c.reshape(BATCH, 1, SEQ, KV_LORA),
            kr_c.reshape(BATCH, 1, B_ROPE, SEQ).transpose(0, 1, 3, 2))
```
